```python
import jax, jax.numpy as jnp
from jax import lax
import numpy as np

D_MODEL = 1024
BATCH = 8
SEQ = 2048
DEPTH = 1
DEC_BATCH = 32
DEC_SEQ = 1
PAST_LEN = 8192
PAGE_SIZE = 128

POOL_WINDOWS = (2, 4, 8, 16)
N_POOL_GROUPS = len(POOL_WINDOWS)
POOL_GROUP_DIM = D_MODEL // 8
D_POOL = N_POOL_GROUPS * POOL_GROUP_DIM
POOL_STATE_LEN = max(POOL_WINDOWS) - 1
HEAD_DIM = 64
N_HEADS = D_MODEL // 128
D_ATT = N_HEADS * HEAD_DIM
Q_BLOCK = 128
FORGET_BIAS_INIT = 3.0
N_BRANCHES = 2
D_IN = D_POOL + 3 * D_ATT + N_HEADS + N_BRANCHES * D_MODEL
N_EXPERT_GROUPS = 4
EXPERTS_PER_GROUP = 8
N_EXPERTS = N_EXPERT_GROUPS * EXPERTS_PER_GROUP
TOP_K_IN_GROUP = 2
D_EXPERT = D_MODEL // 2
RMS_EPS = 1e-6

kernel_name = 'hybrid_pool_fox_hmoe_step'


def rmsnorm(x, g):
    xf = x.astype(jnp.float32)
    y = xf * lax.rsqrt(jnp.mean(xf * xf, axis=-1, keepdims=True) + RMS_EPS)
    return (y * g.astype(jnp.float32)).astype(x.dtype)


def split_projection(h, w_in, b_forget):
    z = h @ w_in
    o1 = D_POOL
    o2 = o1 + D_ATT
    o3 = o2 + D_ATT
    o4 = o3 + D_ATT
    o5 = o4 + N_HEADS
    u, q, k, v, f_logit, g_logit = jnp.split(z, [o1, o2, o3, o4, o5], axis=-1)
    head_shape = z.shape[:-1] + (N_HEADS, HEAD_DIM)
    logf = jax.nn.log_sigmoid((f_logit + b_forget).astype(jnp.float32))
    gates = g_logit.reshape(z.shape[:-1] + (N_BRANCHES, D_MODEL))
    return u, q.reshape(head_shape), k.reshape(head_shape), v.reshape(head_shape), logf, gates


def pool_mixer(u, prefix, start_pos, w_pool, pool_scale):
    B, T, _ = u.shape
    P = POOL_STATE_LEN
    ext = jnp.concatenate([prefix.astype(u.dtype), u], axis=1)
    cs = jnp.cumsum(ext.astype(jnp.float32), axis=1)
    cs = jnp.pad(cs, ((0, 0), (1, 0), (0, 0)))
    pos = start_pos + jnp.arange(T)
    uf = u.astype(jnp.float32)
    outs = []
    for g, w in enumerate(POOL_WINDOWS):
        lo, hi = g * POOL_GROUP_DIM, (g + 1) * POOL_GROUP_DIM
        win_sum = cs[:, P + 1:P + 1 + T, lo:hi] - cs[:, P + 1 - w:P + 1 - w + T, lo:hi]
        count = jnp.minimum(pos + 1, w).astype(jnp.float32)
        outs.append(win_sum / count[None, :, None] - uf[..., lo:hi])
    pooled = jnp.stack(outs, axis=2)
    mixed = jnp.einsum('btgc,gcd->btgd', pooled, w_pool.astype(jnp.float32)).reshape(B, T, D_POOL)
    mixed = mixed * pool_scale.astype(jnp.float32)
    return mixed.astype(u.dtype), ext[:, -P:]


def fox_block(q, c_q, q_pos, k, v, c_k, k_pos):
    s = jnp.einsum('bqhd,bkhd->bhqk', q.astype(jnp.float32), k.astype(jnp.float32)) * (HEAD_DIM ** -0.5)
    bias = jnp.transpose(c_q, (0, 2, 1))[:, :, :, None] - jnp.transpose(c_k, (0, 2, 1))[:, :, None, :]
    mask = k_pos[None, :] <= q_pos[:, None]
    s = jnp.where(mask[None, None], s + bias, -jnp.inf)
    p = jax.nn.softmax(s, axis=-1)
    return jnp.einsum('bhqk,bkhd->bqhd', p, v.astype(jnp.float32)).astype(v.dtype)


def fox_prompt(q, k, v, logf):
    B, T, H, Dh = q.shape
    c = jnp.cumsum(logf, axis=1)
    pos = jnp.arange(T)
    nb = T // Q_BLOCK
    qb = q.reshape(B, nb, Q_BLOCK, H, Dh).swapaxes(0, 1)
    cb = c.reshape(B, nb, Q_BLOCK, H).swapaxes(0, 1)
    pb = pos.reshape(nb, Q_BLOCK)
    out = lax.map(lambda a: fox_block(a[0], a[1], a[2], k, v, c, pos), (qb, cb, pb))
    return out.swapaxes(0, 1).reshape(B, T, H, Dh)


def fox_sample(q, k, v, logf, cache_k, cache_v, cache_logf, page_table):
    DB, T, H, Dh = q.shape
    past = page_table.shape[1] * PAGE_SIZE
    k_past = cache_k[page_table].reshape(DB, past, H, Dh)
    v_past = cache_v[page_table].reshape(DB, past, H, Dh)
    lf_past = cache_logf[page_table].reshape(DB, past, H)
    k_all = jnp.concatenate([k_past, k.astype(k_past.dtype)], axis=1)
    v_all = jnp.concatenate([v_past, v.astype(v_past.dtype)], axis=1)
    lf_all = jnp.concatenate([lf_past.astype(jnp.float32), logf], axis=1)
    c = jnp.cumsum(lf_all, axis=1)
    k_pos = jnp.arange(past + T)
    q_pos = past + jnp.arange(T)
    return fox_block(q, c[:, past:], q_pos, k_all, v_all, c, k_pos)


def merge_branches(pool_out, att_out, gate_logits, w_up_pool, w_up_att, w_out):
    B, T = pool_out.shape[:2]
    gates = jax.nn.sigmoid(gate_logits.astype(jnp.float32))
    y = gates[..., 0, :] * (pool_out @ w_up_pool).astype(jnp.float32) \
        + gates[..., 1, :] * (att_out.reshape(B, T, D_ATT) @ w_up_att).astype(jnp.float32)
    return y.astype(pool_out.dtype) @ w_out


def hier_moe(h, w_rg, b_rg, w_re, b_re, w_gate, w_up, w_down):
    B, T, D = h.shape
    N = B * T
    xf = h.reshape(N, D)
    gl = (xf @ w_rg).astype(jnp.float32) + b_rg.astype(jnp.float32)
    gprob = jax.nn.softmax(gl, axis=-1)
    g_idx = jnp.argmax(gl, axis=-1)
    g_w = jnp.take_along_axis(gprob, g_idx[:, None], axis=1)[:, 0]
    el = jnp.einsum('nd,gde->nge', xf, w_re).astype(jnp.float32) + b_re.astype(jnp.float32)
    el_sel = jnp.take_along_axis(el, g_idx[:, None, None], axis=1)[:, 0]
    top_v, top_i = lax.top_k(el_sel, TOP_K_IN_GROUP)
    top_w = jax.nn.softmax(top_v, axis=-1)
    within = jnp.sum(top_w[..., None] * jax.nn.one_hot(top_i, EXPERTS_PER_GROUP, dtype=jnp.float32), axis=1)
    gate = (g_w[:, None, None] * jax.nn.one_hot(g_idx, N_EXPERT_GROUPS, dtype=jnp.float32)[:, :, None]
            * within[:, None, :]).reshape(N, N_EXPERTS)

    def expert_step(acc, p):
        w1, w3, w2, ge = p
        hdn = jax.nn.silu(xf @ w1) * (xf @ w3)
        return acc + ge[:, None] * (hdn @ w2).astype(jnp.float32), None

    acc, _ = lax.scan(expert_step, jnp.zeros((N, D), jnp.float32), (w_gate, w_up, w_down, gate.T))
    return acc.astype(h.dtype).reshape(B, T, D)


def setup_inputs(seed: int = 0) -> dict:
    key = jax.random.key(seed)
    ks = jax.random.split(key, 24)
    n_pages = PAST_LEN // PAGE_SIZE
    n_used = DEC_BATCH * n_pages
    n_phys = n_used + max(n_used // 4, 1)
    nrm = jax.random.normal
    f32 = jnp.float32
    page_table = jax.random.permutation(ks[5], n_phys)[:n_used].reshape(DEC_BATCH, n_pages).astype(jnp.int32)
    return {
        'x_prompt': nrm(ks[0], (BATCH, SEQ, D_MODEL), f32),
        'x_sample': nrm(ks[1], (DEC_BATCH, DEC_SEQ, D_MODEL), f32),
        'cache_k': nrm(ks[2], (DEPTH, n_phys, PAGE_SIZE, N_HEADS, HEAD_DIM), f32),
        'cache_v': nrm(ks[3], (DEPTH, n_phys, PAGE_SIZE, N_HEADS, HEAD_DIM), f32),
        'cache_logf': jax.nn.log_sigmoid(FORGET_BIAS_INIT + 0.5 * nrm(ks[4], (DEPTH, n_phys, PAGE_SIZE, N_HEADS), f32)),
        'state_pool': nrm(ks[6], (DEPTH, DEC_BATCH, POOL_STATE_LEN, D_POOL), f32),
        'page_table': page_table,
        'norm_mix': 1.0 + 0.02 * nrm(ks[7], (DEPTH, D_MODEL), f32),
        'w_in': nrm(ks[8], (DEPTH, D_MODEL, D_IN), f32) * D_MODEL ** -0.5,
        'b_forget': FORGET_BIAS_INIT + 0.1 * nrm(ks[9], (DEPTH, N_HEADS), f32),
        'w_pool': nrm(ks[10], (DEPTH, N_POOL_GROUPS, POOL_GROUP_DIM, POOL_GROUP_DIM), f32) * POOL_GROUP_DIM ** -0.5,
        'pool_scale': 1.0 + 0.1 * nrm(ks[11], (DEPTH, D_POOL), f32),
        'w_up_pool': nrm(ks[12], (DEPTH, D_POOL, D_MODEL), f32) * D_POOL ** -0.5,
        'w_up_att': nrm(ks[13], (DEPTH, D_ATT, D_MODEL), f32) * D_ATT ** -0.5,
        'w_out': nrm(ks[14], (DEPTH, D_MODEL, D_MODEL), f32) * D_MODEL ** -0.5,
        'norm_ffn': 1.0 + 0.02 * nrm(ks[15], (DEPTH, D_MODEL), f32),
        'w_router_group': nrm(ks[16], (DEPTH, D_MODEL, N_EXPERT_GROUPS), f32) * D_MODEL ** -0.5,
        'b_router_group': 0.01 * nrm(ks[17], (DEPTH, N_EXPERT_GROUPS), f32),
        'w_router_expert': nrm(ks[18], (DEPTH, N_EXPERT_GROUPS, D_MODEL, EXPERTS_PER_GROUP), f32) * D_MODEL ** -0.5,
        'b_router_expert': 0.01 * nrm(ks[19], (DEPTH, N_EXPERT_GROUPS, EXPERTS_PER_GROUP), f32),
        'w_gate': nrm(ks[20], (DEPTH, N_EXPERTS, D_MODEL, D_EXPERT), f32) * D_MODEL ** -0.5,
        'w_up': nrm(ks[21], (DEPTH, N_EXPERTS, D_MODEL, D_EXPERT), f32) * D_MODEL ** -0.5,
        'w_down': nrm(ks[22], (DEPTH, N_EXPERTS, D_EXPERT, D_MODEL), f32) * D_EXPERT ** -0.5,
        'norm_final': 1.0 + 0.02 * nrm(ks[23], (D_MODEL,), f32),
    }


def reference(x_prompt, x_sample, cache_k, cache_v, cache_logf, state_pool, page_table,
              norm_mix, w_in, b_forget, w_pool, pool_scale, w_up_pool, w_up_att, w_out,
              norm_ffn, w_router_group, b_router_group, w_router_expert, b_router_expert,
              w_gate, w_up, w_down, norm_final):
    xp, xs = x_prompt, x_sample
    past = page_table.shape[1] * PAGE_SIZE
    kp_l, vp_l, lfp_l, poolp_l = [], [], [], []
    ks_l, vs_l, lfs_l, pools_l = [], [], [], []
    for l in range(DEPTH):
        h = rmsnorm(xp, norm_mix[l])
        u, q, k, v, logf, gl = split_projection(h, w_in[l], b_forget[l])
        zero_prefix = jnp.zeros((u.shape[0], POOL_STATE_LEN, D_POOL), u.dtype)
        pool_out, pool_new = pool_mixer(u, zero_prefix, 0, w_pool[l], pool_scale[l])
        att = fox_prompt(q, k, v, logf)
        xp = xp + merge_branches(pool_out, att, gl, w_up_pool[l], w_up_att[l], w_out[l])
        xp = xp + hier_moe(rmsnorm(xp, norm_ffn[l]), w_router_group[l], b_router_group[l],
                           w_router_expert[l], b_router_expert[l], w_gate[l], w_up[l], w_down[l])
        kp_l.append(k); vp_l.append(v); lfp_l.append(logf); poolp_l.append(pool_new)
        h = rmsnorm(xs, norm_mix[l])
        u, q, k, v, logf, gl = split_projection(h, w_in[l], b_forget[l])
        pool_out, pool_new = pool_mixer(u, state_pool[l], past, w_pool[l], pool_scale[l])
        att = fox_sample(q, k, v, logf, cache_k[l], cache_v[l], cache_logf[l], page_table)
        xs = xs + merge_branches(pool_out, att, gl, w_up_pool[l], w_up_att[l], w_out[l])
        xs = xs + hier_moe(rmsnorm(xs, norm_ffn[l]), w_router_group[l], b_router_group[l],
                           w_router_expert[l], b_router_expert[l], w_gate[l], w_up[l], w_down[l])
        ks_l.append(k); vs_l.append(v); lfs_l.append(logf); pools_l.append(pool_new)
    y_prompt = rmsnorm(xp, norm_final)
    y_sample = rmsnorm(xs, norm_final)
    return (y_prompt, y_sample,
            jnp.stack(kp_l), jnp.stack(vp_l), jnp.stack(lfp_l), jnp.stack(poolp_l),
            jnp.stack(ks_l), jnp.stack(vs_l), jnp.stack(lfs_l), jnp.stack(pools_l))
```

```python
import functools

import jax
import jax.numpy as jnp
from jax import lax
from jax.experimental import pallas as pl
from jax.experimental.pallas import tpu as pltpu

F32 = jnp.float32
BF16 = jnp.bfloat16
I32 = jnp.int32
HIGHEST = lax.Precision.HIGHEST

RMS_EPS = 1e-6
POOL_WINDOWS = (2, 4, 8, 16)
POOL_HALO = 16
LANES = 128
VMEM_LIMIT_BYTES = 56 * 1024 * 1024

TOKEN_TILE = 512
ATTN_TILE = 512
MOE_ROW_TILE = 256
PAGES_PER_STEP = 8


def _params(*sem):
    return pltpu.CompilerParams(dimension_semantics=sem, vmem_limit_bytes=VMEM_LIMIT_BYTES)


def _rmsnorm(x, g):
    return x * lax.rsqrt(jnp.mean(x * x, axis=-1, keepdims=True) + RMS_EPS) * g


def _log_sigmoid(x):
    return jnp.minimum(x, 0.0) - jnp.log1p(jnp.exp(-jnp.abs(x)))


def _sigmoid(x):
    return 1.0 / (1.0 + jnp.exp(-x))


def _dot(a, b, precise):
    if precise:
        return jnp.dot(a.astype(F32), b.astype(F32), precision=HIGHEST, preferred_element_type=F32)
    return jnp.dot(a.astype(BF16), b.astype(BF16), preferred_element_type=F32)


def _split3(x):
    hi = x.astype(BF16)
    r = x - hi.astype(F32)
    mid = r.astype(BF16)
    lo = (r - mid.astype(F32)).astype(BF16)
    return hi, mid, lo


def _dot_exact_rhs(x, w_bf16):
    hi, mid, lo = _split3(x)
    d = lambda a: jnp.dot(a, w_bf16, preferred_element_type=F32)
    return d(hi) + d(mid) + d(lo)


def _dot_exact_lhs(w_bf16, x):
    hi, mid, lo = _split3(x)
    d = lambda a: jnp.dot(w_bf16, a, preferred_element_type=F32)
    return d(hi) + d(mid) + d(lo)


def _proj_body(x_ref, g_ref, wm_ref, wf_ref, wg_ref, bf_ref,
               u_ref, q_ref, k_ref, v_ref, kb_ref, vb_ref, lf_ref, gate_ref, *, d_pool, d_att, q_scale):
    h = _rmsnorm(x_ref[...], g_ref[...]).astype(BF16)
    z = jnp.dot(h, wm_ref[...], preferred_element_type=F32)
    o1, o2, o3 = d_pool, d_pool + d_att, d_pool + 2 * d_att
    u_ref[...] = z[:, :o1]
    q_ref[...] = (z[:, o1:o2] * q_scale).astype(BF16)
    k = z[:, o2:o3]
    v = z[:, o3:]
    k_ref[...] = k
    v_ref[...] = v
    kb_ref[...] = k.astype(BF16)
    vb_ref[...] = v.astype(BF16)
    lf_ref[...] = _log_sigmoid(jnp.dot(h, wf_ref[...], preferred_element_type=F32) + bf_ref[...])
    gate_ref[...] = _sigmoid(jnp.dot(h, wg_ref[...], preferred_element_type=F32)).astype(BF16)


def _proj_prompt(x, g, wm, wf, wg, bfp, *, tm, d_pool, d_att, q_scale):
    n, d = x.shape
    row = lambda i: (i, 0)
    const = lambda i: (0, 0)
    dg = wg.shape[1]
    out_shape = [
        jax.ShapeDtypeStruct((n, d_pool), F32), jax.ShapeDtypeStruct((n, d_att), BF16),
        jax.ShapeDtypeStruct((n, d_att), F32), jax.ShapeDtypeStruct((n, d_att), F32),
        jax.ShapeDtypeStruct((n, d_att), BF16), jax.ShapeDtypeStruct((n, d_att), BF16),
        jax.ShapeDtypeStruct((n, LANES), F32), jax.ShapeDtypeStruct((n, dg), BF16),
    ]
    return pl.pallas_call(
        functools.partial(_proj_body, d_pool=d_pool, d_att=d_att, q_scale=q_scale),
        grid=(n // tm,),
        in_specs=[pl.BlockSpec((tm, d), row), pl.BlockSpec((1, d), const),
                  pl.BlockSpec(wm.shape, const), pl.BlockSpec(wf.shape, const),
                  pl.BlockSpec(wg.shape, const), pl.BlockSpec((1, LANES), const)],
        out_specs=[pl.BlockSpec((tm, d_pool), row), pl.BlockSpec((tm, d_att), row),
                   pl.BlockSpec((tm, d_att), row), pl.BlockSpec((tm, d_att), row),
                   pl.BlockSpec((tm, d_att), row), pl.BlockSpec((tm, d_att), row),
                   pl.BlockSpec((tm, LANES), row), pl.BlockSpec((tm, dg), row)],
        out_shape=out_shape,
        compiler_params=_params("arbitrary"),
        name="proj_prompt",
    )(x, g, wm, wf, wg, bfp)


def _proj_sample_body(x_ref, g_ref, wm_ref, wf_ref, wg_ref, bf_ref, z_ref, lf_ref, gate_ref):
    h = _rmsnorm(x_ref[...], g_ref[...])
    z_ref[...] = _dot(h, wm_ref[...], True)
    lf_ref[...] = _log_sigmoid(_dot(h, wf_ref[...], True) + bf_ref[...])
    gate_ref[...] = _sigmoid(_dot(h, wg_ref[...], True))


def _proj_sample(x, g, wm, wf, wg, bfp, *, tn):
    n, d = x.shape
    dm, dg = wm.shape[1], wg.shape[1]
    assert dm == dg
    const = lambda j: (0, 0)
    col = lambda j: (0, j)
    return pl.pallas_call(
        _proj_sample_body,
        grid=(dm // tn,),
        in_specs=[pl.BlockSpec((n, d), const), pl.BlockSpec((1, d), const),
                  pl.BlockSpec((d, tn), col), pl.BlockSpec(wf.shape, const),
                  pl.BlockSpec((d, tn), col), pl.BlockSpec((1, LANES), const)],
        out_specs=[pl.BlockSpec((n, tn), col), pl.BlockSpec((n, LANES), const), pl.BlockSpec((n, tn), col)],
        out_shape=[jax.ShapeDtypeStruct((n, dm), F32), jax.ShapeDtypeStruct((n, LANES), F32),
                   jax.ShapeDtypeStruct((n, dg), F32)],
        compiler_params=_params("arbitrary"),
        name="proj_sample",
    )(x, g, wm, wf, wg, bfp)


def _cumsum_body(x_ref, o_ref):
    c = x_ref[...]
    lane = lax.broadcasted_iota(I32, c.shape, 1)
    s = 1
    while s < c.shape[1]:
        c = c + jnp.where(lane >= s, pltpu.roll(c, s, 1), 0.0)
        s *= 2
    o_ref[...] = c


def _cumsum_lanes(x):
    return pl.pallas_call(_cumsum_body, out_shape=jax.ShapeDtypeStruct(x.shape, F32),
                          compiler_params=_params(), name="cumsum_logf")(x)


def _attn_body(q_ref, k_ref, v_ref, c_ref, o_ref, *, tile, dh):
    qi = pl.program_id(2)
    q = q_ref[0]
    lane = lax.broadcasted_iota(I32, q.shape, 1)
    zero = jnp.zeros_like(q)
    q_heads = (jnp.where(lane < dh, q, zero), jnp.where(lane < dh, zero, q))
    row = lax.broadcasted_iota(I32, (tile, tile), 0)
    col = lax.broadcasted_iota(I32, (tile, tile), 1)
    causal = col <= row

    def step(kj, carry, masked):
        start = pl.multiple_of(kj * tile, tile)
        kt = k_ref[0, pl.ds(start, tile), :]
        vt = v_ref[0, pl.ds(start, tile), :]
        ck = c_ref[0, 0, kj]
        out = []
        for h in range(2):
            m, l, acc = carry[h]
            s = lax.dot_general(q_heads[h], kt, (((1,), (1,)), ((), ())), preferred_element_type=F32)
            s = s - ck[h:h + 1, :]
            if masked:
                s = jnp.where(causal, s, -jnp.inf)
            m_new = jnp.maximum(m, jnp.max(s, axis=-1, keepdims=True))
            alpha = jnp.exp(m - m_new)
            p = jnp.exp(s - m_new)
            l = alpha * l + jnp.sum(p, axis=-1, keepdims=True)
            acc = alpha * acc + jnp.dot(p.astype(BF16), vt, preferred_element_type=F32)
            out.append((m_new, l, acc))
        return tuple(out)

    init_h = (jnp.full((tile, 1), -1e30, F32), jnp.zeros((tile, 1), F32), jnp.zeros((tile, 2 * dh), F32))
    carry = lax.fori_loop(0, qi, lambda kj, c: step(kj, c, False), (init_h, init_h))
    (m0, l0, a0), (m1, l1, a1) = step(qi, carry, True)
    o_ref[0] = jnp.where(lane < dh, a0 / l0, a1 / l1).astype(o_ref.dtype)


def _attn_prompt(q, k, v, c, *, tile, dh):
    b, t, da = q.shape
    hp = da // (2 * dh)
    nt = t // tile
    return pl.pallas_call(
        functools.partial(_attn_body, tile=tile, dh=dh),
        grid=(b, hp, nt),
        in_specs=[pl.BlockSpec((1, tile, 2 * dh), lambda bi, hi, qi: (bi, qi, hi)),
                  pl.BlockSpec((1, t, 2 * dh), lambda bi, hi, qi: (bi, 0, hi)),
                  pl.BlockSpec((1, t, 2 * dh), lambda bi, hi, qi: (bi, 0, hi)),
                  pl.BlockSpec((1, 1, nt, 2, tile), lambda bi, hi, qi: (bi, hi, 0, 0, 0))],
        out_specs=pl.BlockSpec((1, tile, 2 * dh), lambda bi, hi, qi: (bi, qi, hi)),
        out_shape=jax.ShapeDtypeStruct((b, t, da), BF16),
        compiler_params=_params("arbitrary", "arbitrary", "arbitrary"),
        name="attn_prompt",
    )(q, k, v, c)


def _attn_sample_body(pt_ref, q_ref, kn_ref, vn_ref, lfn_ref, e_ref, et_ref, *rest, n_pages_step, page, n_heads):
    g_n = n_pages_step
    k_refs = rest[0:g_n]
    v_refs = rest[g_n:2 * g_n]
    lf_refs = rest[2 * g_n:3 * g_n]
    o_ref = rest[3 * g_n]
    m_ref, l_ref, acc_ref, srun_ref, lfpad_ref = rest[3 * g_n + 1:]
    step = pl.program_id(1)
    q = q_ref[0]
    e = e_ref[...]
    et = et_ref[...]
    d_att = q.shape[1]

    @pl.when(step == 0)
    def _():
        s_new = _dot_exact_rhs(jnp.broadcast_to(q * kn_ref[0], (8, d_att)), e)[0:1]
        m_ref[...] = s_new
        l_ref[...] = jnp.ones_like(l_ref)
        sub = lax.broadcasted_iota(I32, (8, d_att), 0)
        acc_ref[...] = jnp.where(sub == 0, jnp.broadcast_to(vn_ref[0], (8, d_att)), 0.0)
        srun_ref[...] = lfn_ref[0]
        lfpad_ref[...] = jnp.zeros_like(lfpad_ref)

    r = lax.broadcasted_iota(I32, (page, page), 0)
    c = lax.broadcasted_iota(I32, (page, page), 1)
    upper = (c > r).astype(BF16)
    s_run = srun_ref[...]
    scores = []
    for g in range(g_n):
        kq = k_refs[g][0] * q
        s = _dot_exact_rhs(kq, e)
        lfpad_ref[:, 0:n_heads] = lf_refs[g][0]
        lf = lfpad_ref[...]
        scores.append(s + (s_run + _dot_exact_lhs(upper, lf)))
        s_run = s_run + jnp.sum(lf, axis=0, keepdims=True)
    srun_ref[...] = s_run
    m_prev = m_ref[...]
    m_new = m_prev
    for s in scores:
        m_new = jnp.maximum(m_new, jnp.max(s, axis=0, keepdims=True))
    alpha = jnp.exp(m_prev - m_new)
    l = alpha * l_ref[...]
    acc = acc_ref[...] * _dot_exact_rhs(jnp.broadcast_to(alpha, (8, LANES)), et)
    for g in range(g_n):
        p = jnp.exp(scores[g] - m_new)
        l = l + jnp.sum(p, axis=0, keepdims=True)
        pv = _dot_exact_rhs(p, et) * v_refs[g][0]
        acc = acc + jnp.sum(pv.reshape(page // 8, 8, d_att), axis=0)
    m_ref[...] = m_new
    l_ref[...] = l
    acc_ref[...] = acc

    @pl.when(step == pl.num_programs(1) - 1)
    def _():
        l_wide = _dot_exact_rhs(jnp.broadcast_to(l, (8, LANES)), et)[0:1]
        o_ref[0] = jnp.sum(acc, axis=0, keepdims=True) / l_wide


def _attn_sample(page_table, q, k_new, v_new, lf_new, cache_k, cache_v, cache_lf, *, n_heads, dh):
    db, n_pages = page_table.shape
    n_phys, page, d_att = cache_k.shape
    g_n = PAGES_PER_STEP
    while n_pages % g_n:
        g_n //= 2
    n_steps = n_pages // g_n
    head_of_col = jnp.arange(d_att) // dh
    e = (head_of_col[:, None] == jnp.arange(LANES)[None, :]).astype(BF16)
    et = e.T

    def page_map(g):
        return lambda b, s, pt: (pt[b, n_pages - 1 - (s * g_n + g)], 0, 0)

    per_seq = lambda b, s, pt: (b, 0, 0)
    const = lambda b, s, pt: (0, 0)
    in_specs = [pl.BlockSpec((1, 1, d_att), per_seq), pl.BlockSpec((1, 1, d_att), per_seq),
                pl.BlockSpec((1, 1, d_att), per_seq), pl.BlockSpec((1, 1, LANES), per_seq),
                pl.BlockSpec((d_att, LANES), const), pl.BlockSpec((LANES, d_att), const)]
    in_specs += [pl.BlockSpec((1, page, d_att), page_map(g)) for g in range(g_n)]
    in_specs += [pl.BlockSpec((1, page, d_att), page_map(g)) for g in range(g_n)]
    in_specs += [pl.BlockSpec((1, page, n_heads), page_map(g)) for g in range(g_n)]
    grid_spec = pltpu.PrefetchScalarGridSpec(
        num_scalar_prefetch=1, grid=(db, n_steps), in_specs=in_specs,
        out_specs=pl.BlockSpec((1, 1, d_att), per_seq),
        scratch_shapes=[pltpu.VMEM((1, LANES), F32), pltpu.VMEM((1, LANES), F32), pltpu.VMEM((8, d_att), F32),
                        pltpu.VMEM((1, LANES), F32), pltpu.VMEM((page, LANES), F32)])
    return pl.pallas_call(
        functools.partial(_attn_sample_body, n_pages_step=g_n, page=page, n_heads=n_heads),
        grid_spec=grid_spec,
        out_shape=jax.ShapeDtypeStruct((db, 1, d_att), F32),
        compiler_params=_params("arbitrary", "arbitrary"),
        name="attn_sample",
    )(page_table, q, k_new, v_new, lf_new, e, et,
      *([cache_k] * g_n), *([cache_v] * g_n), *([cache_lf] * g_n))


def _merge_and_route(x, pooled, att, gates, wp_ref, ps_ref, wup_ref, wua_ref, wo_ref, nf_ref, wr_ref, br_ref,
                     base_counts, *, precise, n_groups, n_per_group):
    tm, d = x.shape
    gw = pooled[0].shape[1]
    mixed = jnp.concatenate([_dot(pooled[g], wp_ref[g], precise) for g in range(len(pooled))], axis=-1)
    pool_out = mixed * ps_ref[...]
    y = gates[:, :d].astype(F32) * _dot(pool_out, wup_ref[...], precise) \
        + gates[:, d:].astype(F32) * _dot(att, wua_ref[...], precise)
    x2 = x + _dot(y, wo_ref[...], precise)
    h2 = _rmsnorm(x2, nf_ref[...])
    logits = _dot(h2, wr_ref[...], precise) + br_ref[...]
    lane = lax.broadcasted_iota(I32, logits.shape, 1)
    lanef = lane.astype(F32)
    neg = -jnp.inf
    is_g = lane < n_groups
    gmax = jnp.max(jnp.where(is_g, logits, neg), axis=-1, keepdims=True)
    gidx = jnp.min(jnp.where(is_g & (logits == gmax), lanef, float(LANES)), axis=-1, keepdims=True)
    gsum = jnp.sum(jnp.where(is_g, jnp.exp(logits - gmax), 0.0), axis=-1, keepdims=True)
    g_w = 1.0 / gsum
    n_exp = n_groups * n_per_group
    exp_id = lanef - float(n_groups)
    in_sel = (lane >= n_groups) & (lane < n_groups + n_exp) & (jnp.floor(exp_id / n_per_group) == gidx)
    v1 = jnp.max(jnp.where(in_sel, logits, neg), axis=-1, keepdims=True)
    i1 = jnp.min(jnp.where(in_sel & (logits == v1), lanef, float(LANES)), axis=-1, keepdims=True)
    in_sel2 = in_sel & (lanef != i1)
    v2 = jnp.max(jnp.where(in_sel2, logits, neg), axis=-1, keepdims=True)
    i2 = jnp.min(jnp.where(in_sel2 & (logits == v2), lanef, float(LANES)), axis=-1, keepdims=True)
    t = jnp.exp(v2 - v1)
    w1 = g_w * (1.0 / (1.0 + t))
    w2 = g_w * (t / (1.0 + t))
    e1 = i1 - float(n_groups)
    e2 = i2 - float(n_groups)
    hit1 = lanef == e1
    hit2 = lanef == e2
    onehot = (hit1 | hit2).astype(BF16)
    rr = lax.broadcasted_iota(I32, (tm, tm), 0)
    cc = lax.broadcasted_iota(I32, (tm, tm), 1)
    incl = jnp.dot((cc <= rr).astype(BF16), onehot, preferred_element_type=F32)
    seen = incl + base_counts - 1.0
    r1 = jnp.sum(jnp.where(hit1, seen, 0.0), axis=-1, keepdims=True)
    r2 = jnp.sum(jnp.where(hit2, seen, 0.0), axis=-1, keepdims=True)
    counts = base_counts + incl[tm - 1:tm, :]
    slab = jnp.zeros((tm, LANES), F32)
    for i, val in enumerate((e1, e2, r1, r2, w1, w2)):
        slab = jnp.where(lane == i, val, slab)
    return x2, h2, slab, counts


def _merge_prompt_body(x_ref, u_ref, halo_ref, att_ref, gate_ref, wp_ref, ps_ref, wup_ref, wua_ref, wo_ref,
                       nf_ref, wr_ref, br_ref, x2_ref, h2_ref, slab_ref, route_ref, counts_ref, ext_ref, cnt_ref,
                       *, seq_len, n_groups, n_per_group):
    i = pl.program_id(0)
    tm = x_ref.shape[0]
    gw = u_ref.shape[1] // len(POOL_WINDOWS)
    pos0 = (i * tm) % seq_len

    @pl.when(i == 0)
    def _():
        cnt_ref[...] = jnp.zeros_like(cnt_ref)

    u = u_ref[...]
    ext_ref[0:POOL_HALO, :] = jnp.where(pos0 == 0, 0.0, halo_ref[...])
    ext_ref[POOL_HALO:, :] = u
    pos = pos0 + lax.broadcasted_iota(I32, (tm, 1), 0)
    pooled = []
    for g, w in enumerate(POOL_WINDOWS):
        lo = g * gw
        wsum = ext_ref[pl.ds(POOL_HALO, tm), lo:lo + gw]
        for j in range(1, w):
            wsum = wsum + ext_ref[pl.ds(POOL_HALO - j, tm), lo:lo + gw]
        count = jnp.minimum(pos + 1, w).astype(F32)
        pooled.append(wsum / count - u[:, lo:lo + gw])
    x2, h2, slab, counts = _merge_and_route(
        x_ref[...], pooled, att_ref[...], gate_ref[...], wp_ref, ps_ref, wup_ref, wua_ref, wo_ref, nf_ref,
        wr_ref, br_ref, cnt_ref[...], precise=False, n_groups=n_groups, n_per_group=n_per_group)
    x2_ref[...] = x2
    h2_ref[...] = h2
    slab_ref[...] = slab
    route_ref[...] = slab.T[0:8, :]
    cnt_ref[...] = counts
    counts_ref[...] = counts


def _merge_prompt(x, u, att, gates, wp, ps, wup, wua, wo, nf, wr, br, *, tm, seq_len, n_groups, n_per_group):
    n, d = x.shape
    d_pool, d_att = u.shape[1], att.shape[1]
    row = lambda i: (i, 0)
    const = lambda i: (0, 0)
    const3 = lambda i: (0, 0, 0)
    halo = lambda i: (jnp.maximum(i * (tm // POOL_HALO) - 1, 0), 0)
    return pl.pallas_call(
        functools.partial(_merge_prompt_body, seq_len=seq_len, n_groups=n_groups, n_per_group=n_per_group),
        grid=(n // tm,),
        in_specs=[pl.BlockSpec((tm, d), row), pl.BlockSpec((tm, d_pool), row), pl.BlockSpec((POOL_HALO, d_pool), halo),
                  pl.BlockSpec((tm, d_att), row), pl.BlockSpec((tm, 2 * d), row),
                  pl.BlockSpec(wp.shape, const3), pl.BlockSpec((1, d_pool), const),
                  pl.BlockSpec(wup.shape, const), pl.BlockSpec(wua.shape, const), pl.BlockSpec(wo.shape, const),
                  pl.BlockSpec((1, d), const), pl.BlockSpec(wr.shape, const), pl.BlockSpec((1, LANES), const)],
        out_specs=[pl.BlockSpec((tm, d), row), pl.BlockSpec((tm, d), row), pl.BlockSpec((tm, LANES), row),
                   pl.BlockSpec((8, tm), lambda i: (0, i)), pl.BlockSpec((1, LANES), const)],
        out_shape=[jax.ShapeDtypeStruct((n, d), F32), jax.ShapeDtypeStruct((n, d), F32),
                   jax.ShapeDtypeStruct((n, LANES), F32), jax.ShapeDtypeStruct((8, n), F32),
                   jax.ShapeDtypeStruct((1, LANES), F32)],
        scratch_shapes=[pltpu.VMEM((tm + POOL_HALO, d_pool), F32), pltpu.VMEM((1, LANES), F32)],
        compiler_params=_params("arbitrary"),
        name="merge_prompt",
    )(x, u, u, att, gates, wp, ps, wup, wua, wo, nf, wr, br)


def _merge_sample_body(x_ref, u_ref, st_ref, att_ref, gate_ref, wp_ref, ps_ref, wup_ref, wua_ref, wo_ref,
                       nf_ref, wr_ref, br_ref, base_ref, x2_ref, h2_ref, slab_ref, counts_ref,
                       *, start_pos, n_groups, n_per_group):
    u = u_ref[...]
    gw = u.shape[1] // len(POOL_WINDOWS)
    n_state = st_ref.shape[0]
    pooled = []
    for g, w in enumerate(POOL_WINDOWS):
        lo = g * gw
        wsum = u[:, lo:lo + gw]
        for j in range(1, w):
            wsum = wsum + st_ref[n_state - j][:, lo:lo + gw]
        pooled.append(wsum / float(min(start_pos + 1, w)) - u[:, lo:lo + gw])
    x2, h2, slab, counts = _merge_and_route(
        x_ref[...], pooled, att_ref[...], gate_ref[...], wp_ref, ps_ref, wup_ref, wua_ref, wo_ref, nf_ref,
        wr_ref, br_ref, base_ref[...], precise=True, n_groups=n_groups, n_per_group=n_per_group)
    x2_ref[...] = x2
    h2_ref[...] = h2
    slab_ref[...] = slab
    counts_ref[...] = counts


def _merge_sample(x, u, state_t, att, gates, wp, ps, wup, wua, wo, nf, wr, br, base, *, start_pos, n_groups,
                  n_per_group):
    n, d = x.shape
    return pl.pallas_call(
        functools.partial(_merge_sample_body, start_pos=start_pos, n_groups=n_groups, n_per_group=n_per_group),
        out_shape=[jax.ShapeDtypeStruct((n, d), F32), jax.ShapeDtypeStruct((n, d), F32),
                   jax.ShapeDtypeStruct((n, LANES), F32), jax.ShapeDtypeStruct((1, LANES), F32)],
        compiler_params=_params(),
        name="merge_sample",
    )(x, u, state_t, att, gates, wp, ps, wup, wua, wo, nf, wr, br, base)


def _row_copy(src_ref, src_row, dst_ref, dst_row, sem):
    return pltpu.make_async_copy(src_ref.at[pl.ds(src_row, 1)], dst_ref.at[pl.ds(dst_row, 1)], sem)


def _moe_scatter_body(goff_ref, e1_ref, e2_ref, r1_ref, r2_ref, h_ref, xs_in_ref, xs_ref, sem):
    del xs_in_ref
    n = h_ref.shape[0]

    def copies(t):
        d1 = goff_ref[e1_ref[t]] + r1_ref[t]
        d2 = goff_ref[e2_ref[t]] + r2_ref[t]
        return _row_copy(h_ref, t, xs_ref, d1, sem), _row_copy(h_ref, t, xs_ref, d2, sem)

    def start(t, carry):
        for cp in copies(t):
            cp.start()
        return carry

    def wait(t, carry):
        for cp in copies(t):
            cp.wait()
        return carry

    lax.fori_loop(0, n, start, 0)
    lax.fori_loop(0, n, wait, 0)


def _moe_scatter(goff, e1, e2, r1, r2, h, xs, *, ts):
    n, d = h.shape
    smem = lambda: pl.BlockSpec((ts,), lambda i: (i,), memory_space=pltpu.SMEM)
    return pl.pallas_call(
        _moe_scatter_body,
        grid=(n // ts,),
        in_specs=[pl.BlockSpec(memory_space=pltpu.SMEM), smem(), smem(), smem(), smem(),
                  pl.BlockSpec((ts, d), lambda i: (i, 0)), pl.BlockSpec(memory_space=pl.ANY)],
        out_specs=pl.BlockSpec(memory_space=pl.ANY),
        out_shape=jax.ShapeDtypeStruct(xs.shape, xs.dtype),
        scratch_shapes=[pltpu.SemaphoreType.DMA],
        input_output_aliases={6: 0},
        compiler_params=_params("arbitrary"),
        name="moe_scatter",
    )(goff, e1, e2, r1, r2, h, xs)


def _moe_mm_body(te_ref, tw_ref, xs_ref, wg_ref, wu_ref, wd_ref, ys_ref, wgb_ref, wub_ref, wdb_ref):
    i = pl.program_id(0)
    expert = te_ref[i]
    prev = te_ref[jnp.maximum(i - 1, 0)]

    @pl.when((expert >= 0) & ((i == 0) | (expert != prev)))
    def _():
        wgb_ref[...] = wg_ref[0].astype(BF16)
        wub_ref[...] = wu_ref[0].astype(BF16)
        wdb_ref[...] = wd_ref[0].astype(BF16)

    @pl.when(expert >= 0)
    def _():
        x = xs_ref[...].astype(BF16)
        a = jnp.dot(x, wgb_ref[...], preferred_element_type=F32)
        b = jnp.dot(x, wub_ref[...], preferred_element_type=F32)
        hdn = (a * _sigmoid(a)) * b
        ys_ref[...] = jnp.dot(hdn.astype(BF16), wdb_ref[...], preferred_element_type=F32)

    @pl.when(expert < 0)
    def _():
        ys_ref[...] = jnp.zeros_like(ys_ref)


def _moe_mm(tile_expert, tile_weight, xs, w_gate, w_up, w_down, *, tm):
    p, d = xs.shape
    n_exp, _, de = w_gate.shape
    wmap = lambda i, te, tw: (tw[i], 0, 0)
    grid_spec = pltpu.PrefetchScalarGridSpec(
        num_scalar_prefetch=2, grid=(p // tm,),
        in_specs=[pl.BlockSpec((tm, d), lambda i, te, tw: (i, 0)),
                  pl.BlockSpec((1, d, de), wmap), pl.BlockSpec((1, d, de), wmap), pl.BlockSpec((1, de, d), wmap)],
        out_specs=pl.BlockSpec((tm, d), lambda i, te, tw: (i, 0)),
        scratch_shapes=[pltpu.VMEM((d, de), BF16), pltpu.VMEM((d, de), BF16), pltpu.VMEM((de, d), BF16)])
    return pl.pallas_call(
        _moe_mm_body, grid_spec=grid_spec,
        out_shape=jax.ShapeDtypeStruct((p, d), F32),
        compiler_params=_params("arbitrary"),
        name="moe_mm",
    )(tile_expert, tile_weight, xs, w_gate, w_up, w_down)


def _moe_combine_body(goff_ref, e1_ref, e2_ref, r1_ref, r2_ref, x_ref, slab_ref, g_ref, ys_ref, o_ref,
                      ya_ref, yb_ref, sem, *, final_norm):
    n = x_ref.shape[0]

    def copies(t):
        d1 = goff_ref[e1_ref[t]] + r1_ref[t]
        d2 = goff_ref[e2_ref[t]] + r2_ref[t]
        return _row_copy(ys_ref, d1, ya_ref, t, sem), _row_copy(ys_ref, d2, yb_ref, t, sem)

    def start(t, carry):
        for cp in copies(t):
            cp.start()
        return carry

    def wait(t, carry):
        for cp in copies(t):
            cp.wait()
        return carry

    lax.fori_loop(0, n, start, 0)
    lax.fori_loop(0, n, wait, 0)
    slab = slab_ref[...]
    out = x_ref[...] + (slab[:, 4:5] * ya_ref[...] + slab[:, 5:6] * yb_ref[...])
    if final_norm:
        out = _rmsnorm(out, g_ref[...])
    o_ref[...] = out


def _moe_combine(goff, e1, e2, r1, r2, x, slab, g, ys, *, ts, final_norm):
    n, d = x.shape
    smem = lambda: pl.BlockSpec((ts,), lambda i: (i,), memory_space=pltpu.SMEM)
    return pl.pallas_call(
        functools.partial(_moe_combine_body, final_norm=final_norm),
        grid=(n // ts,),
        in_specs=[pl.BlockSpec(memory_space=pltpu.SMEM), smem(), smem(), smem(), smem(),
                  pl.BlockSpec((ts, d), lambda i: (i, 0)), pl.BlockSpec((ts, LANES), lambda i: (i, 0)),
                  pl.BlockSpec((1, d), lambda i: (0, 0)), pl.BlockSpec(memory_space=pl.ANY)],
        out_specs=pl.BlockSpec((ts, d), lambda i: (i, 0)),
        out_shape=jax.ShapeDtypeStruct((n, d), F32),
        scratch_shapes=[pltpu.VMEM((ts, d), F32), pltpu.VMEM((ts, d), F32), pltpu.SemaphoreType.DMA],
        compiler_params=_params("arbitrary"),
        name="moe_combine",
    )(goff, e1, e2, r1, r2, x, slab, g, ys)


def kernel(x_prompt, x_sample, cache_k, cache_v, cache_logf, state_pool, page_table, norm_mix, w_in, b_forget,
           w_pool, pool_scale, w_up_pool, w_up_att, w_out, norm_ffn, w_router_group, b_router_group,
           w_router_expert, b_router_expert, w_gate, w_up, w_down, norm_final):
    depth = norm_mix.shape[0]
    assert depth == 1, "single trunk layer"
    b, t, d = x_prompt.shape
    db, dt, _ = x_sample.shape
    assert dt == 1, "one sample token per sequence"
    _, n_phys, page, n_heads, dh = cache_k.shape
    n_pages = page_table.shape[1]
    past = n_pages * page
    n_state, d_pool = state_pool.shape[2], state_pool.shape[3]
    d_att = n_heads * dh
    n_pool_groups = w_pool.shape[1]
    assert n_pool_groups == len(POOL_WINDOWS) and d_pool // n_pool_groups == LANES
    assert n_state == max(POOL_WINDOWS) - 1 and n_state < POOL_HALO
    n_groups, n_per_group = w_router_expert.shape[1], w_router_expert.shape[3]
    n_exp = n_groups * n_per_group
    assert n_groups + n_exp <= LANES and 2 * dh == LANES and n_heads % 2 == 0
    n = b * t
    q_scale = float(dh) ** -0.5
    tm = min(TOKEN_TILE, t)
    assert t % tm == 0 and t % ATTN_TILE == 0

    o_main = d_pool + 3 * d_att
    wi = w_in[0]
    wm_f, wf_f, wg_f = wi[:, :o_main], wi[:, o_main:o_main + n_heads], wi[:, o_main + n_heads:]
    wf_pad = jnp.pad(wf_f, ((0, 0), (0, LANES - n_heads)))
    bf_pad = jnp.pad(b_forget[0], (0, LANES - n_heads)).reshape(1, LANES)
    g_mix = norm_mix[0].reshape(1, d)
    g_ffn = norm_ffn[0].reshape(1, d)
    g_fin = norm_final.reshape(1, d)
    ps = pool_scale[0].reshape(1, d_pool)
    wr_f = jnp.concatenate([w_router_group[0], jnp.transpose(w_router_expert[0], (1, 0, 2)).reshape(d, n_exp)], axis=1)
    wr_pad = jnp.pad(wr_f, ((0, 0), (0, LANES - n_groups - n_exp)))
    br_pad = jnp.pad(jnp.concatenate([b_router_group[0], b_router_expert[0].reshape(n_exp)]),
                     (0, LANES - n_groups - n_exp)).reshape(1, LANES)
    bf = lambda a: a.astype(BF16)

    xp = x_prompt.reshape(n, d)
    u_p, q_p, k_p, v_p, kb_p, vb_p, lf_p, gate_p = _proj_prompt(
        xp, g_mix, bf(wm_f), bf(wf_pad), bf(wg_f), bf_pad, tm=tm, d_pool=d_pool, d_att=d_att, q_scale=q_scale)
    logf_p = lf_p[:, :n_heads].reshape(b, t, n_heads)
    c = _cumsum_lanes(jnp.transpose(logf_p, (0, 2, 1)).reshape(b * n_heads, t))
    nt = t // ATTN_TILE
    c_blk = jnp.transpose(c.reshape(b, n_heads // 2, 2, nt, ATTN_TILE), (0, 1, 3, 2, 4))
    att_p = _attn_prompt(q_p.reshape(b, t, d_att), kb_p.reshape(b, t, d_att), vb_p.reshape(b, t, d_att), c_blk,
                         tile=ATTN_TILE, dh=dh)
    x2_p, h2_p, slab_p, route_p, counts_p = _merge_prompt(
        xp, u_p, att_p.reshape(n, d_att), gate_p, bf(w_pool[0]), ps, bf(w_up_pool[0]), bf(w_up_att[0]),
        bf(w_out[0]), g_ffn, bf(wr_pad), br_pad, tm=tm, seq_len=t, n_groups=n_groups, n_per_group=n_per_group)

    xs = x_sample.reshape(db, d)
    z_s, lf_s, gate_s = _proj_sample(xs, g_mix, wm_f, wf_pad, wg_f, bf_pad, tn=512)
    u_s = z_s[:, :d_pool]
    q_s = z_s[:, d_pool:d_pool + d_att] * q_scale
    k_s = z_s[:, d_pool + d_att:d_pool + 2 * d_att]
    v_s = z_s[:, d_pool + 2 * d_att:]
    att_s = _attn_sample(page_table, q_s.reshape(db, 1, d_att), k_s.reshape(db, 1, d_att),
                         v_s.reshape(db, 1, d_att), lf_s.reshape(db, 1, LANES),
                         cache_k[0].reshape(n_phys, page, d_att), cache_v[0].reshape(n_phys, page, d_att),
                         cache_logf[0], n_heads=n_heads, dh=dh)
    state_t = jnp.transpose(state_pool[0], (1, 0, 2))
    x2_s, h2_s, slab_s, counts = _merge_sample(
        xs, u_s, state_t, att_s.reshape(db, d_att), gate_s, w_pool[0], ps, w_up_pool[0], w_up_att[0], w_out[0],
        g_ffn, wr_pad, br_pad, counts_p, start_pos=past, n_groups=n_groups, n_per_group=n_per_group)

    tmm = MOE_ROW_TILE
    n_tok = n + db
    n_tiles = -(-(2 * n_tok + n_exp * (tmm - 1)) // tmm)
    cnt = counts[0, :n_exp].astype(I32)
    padded = ((cnt + tmm - 1) // tmm) * tmm
    ends = jnp.cumsum(padded)
    goff = (ends - padded).astype(I32)
    tile_e = jnp.sum((jnp.arange(n_tiles, dtype=I32)[:, None] * tmm >= ends[None, :]).astype(I32), axis=1)
    tile_expert = jnp.where(tile_e < n_exp, tile_e, -1).astype(I32)
    tile_weight = jnp.minimum(tile_e, n_exp - 1).astype(I32)
    fields_p = route_p[:4].astype(I32)
    fields_s = jnp.transpose(slab_s[:, :4]).astype(I32)
    xs_rows = jnp.zeros((n_tiles * tmm, d), F32)
    xs_rows = _moe_scatter(goff, *fields_p, h2_p, xs_rows, ts=tm)
    xs_rows = _moe_scatter(goff, *fields_s, h2_s, xs_rows, ts=db)
    ys_rows = _moe_mm(tile_expert, tile_weight, xs_rows, w_gate[0], w_up[0], w_down[0], tm=tmm)
    y_prompt = _moe_combine(goff, *fields_p, x2_p, slab_p, g_fin, ys_rows, ts=tm, final_norm=True)
    y_sample = _moe_combine(goff, *fields_s, x2_s, slab_s, g_fin, ys_rows, ts=db, final_norm=True)

    new_pool_p = u_p.reshape(b, t, d_pool)[:, t - n_state:, :]
    new_pool_s = jnp.concatenate([state_pool[0][:, 1:, :], u_s[:, None, :]], axis=1)
    return (y_prompt.reshape(b, t, d), y_sample.reshape(db, 1, d),
            k_p.reshape(1, b, t, n_heads, dh), v_p.reshape(1, b, t, n_heads, dh), logf_p[None],
            new_pool_p[None],
            k_s.reshape(1, db, 1, n_heads, dh), v_s.reshape(1, db, 1, n_heads, dh),
            lf_s[:, :n_heads].reshape(1, db, 1, n_heads), new_pool_s[None])
```

```python
import functools

import jax
import jax.numpy as jnp
from jax import lax
from jax.experimental import pallas as pl
from jax.experimental.pallas import tpu as pltpu

F32 = jnp.float32
BF16 = jnp.bfloat16
I32 = jnp.int32
HIGHEST = lax.Precision.HIGHEST

RMS_EPS = 1e-6
POOL_WINDOWS = (2, 4, 8, 16)
POOL_HALO = 16
LANES = 128
VMEM_LIMIT_BYTES = 56 * 1024 * 1024

TOKEN_TILE = 512
ATTN_TILE = 512
MOE_ROW_TILE = 256
PAGES_PER_STEP = 8
ROW_DMA_UNROLL = 8


def _params(*sem):
    return pltpu.CompilerParams(dimension_semantics=sem, vmem_limit_bytes=VMEM_LIMIT_BYTES)


def _rmsnorm(x, g):
    return x * lax.rsqrt(jnp.mean(x * x, axis=-1, keepdims=True) + RMS_EPS) * g


def _log_sigmoid(x):
    return jnp.minimum(x, 0.0) - jnp.log1p(jnp.exp(-jnp.abs(x)))


def _sigmoid(x):
    return 1.0 / (1.0 + jnp.exp(-x))


def _dot(a, b, precise):
    if precise:
        return jnp.dot(a.astype(F32), b.astype(F32), precision=HIGHEST, preferred_element_type=F32)
    return jnp.dot(a.astype(BF16), b.astype(BF16), preferred_element_type=F32)


def _split3(x):
    hi = x.astype(BF16)
    r = x - hi.astype(F32)
    mid = r.astype(BF16)
    lo = (r - mid.astype(F32)).astype(BF16)
    return hi, mid, lo


def _dot_exact_rhs(x, w_bf16):
    hi, mid, lo = _split3(x)
    d = lambda a: jnp.dot(a, w_bf16, preferred_element_type=F32)
    return d(hi) + d(mid) + d(lo)


def _dot_exact_lhs(w_bf16, x):
    hi, mid, lo = _split3(x)
    d = lambda a: jnp.dot(w_bf16, a, preferred_element_type=F32)
    return d(hi) + d(mid) + d(lo)


def _proj_body(x_ref, g_ref, wm_ref, wf_ref, wg_ref, bf_ref,
               u_ref, q_ref, k_ref, v_ref, kb_ref, vb_ref, lf_ref, gate_ref, *, d_pool, d_att, q_scale):
    h = _rmsnorm(x_ref[...], g_ref[...]).astype(BF16)
    z = jnp.dot(h, wm_ref[...], preferred_element_type=F32)
    o1, o2, o3 = d_pool, d_pool + d_att, d_pool + 2 * d_att
    u_ref[...] = z[:, :o1]
    q_ref[...] = (z[:, o1:o2] * q_scale).astype(BF16)
    k = z[:, o2:o3]
    v = z[:, o3:]
    k_ref[...] = k
    v_ref[...] = v
    kb_ref[...] = k.astype(BF16)
    vb_ref[...] = v.astype(BF16)
    lf_ref[...] = _log_sigmoid(jnp.dot(h, wf_ref[...], preferred_element_type=F32) + bf_ref[...])
    gate_ref[...] = _sigmoid(jnp.dot(h, wg_ref[...], preferred_element_type=F32)).astype(BF16)


def _proj_prompt(x, g, wm, wf, wg, bfp, *, tm, d_pool, d_att, q_scale):
    n, d = x.shape
    row = lambda i: (i, 0)
    const = lambda i: (0, 0)
    dg = wg.shape[1]
    out_shape = [
        jax.ShapeDtypeStruct((n, d_pool), F32), jax.ShapeDtypeStruct((n, d_att), BF16),
        jax.ShapeDtypeStruct((n, d_att), F32), jax.ShapeDtypeStruct((n, d_att), F32),
        jax.ShapeDtypeStruct((n, d_att), BF16), jax.ShapeDtypeStruct((n, d_att), BF16),
        jax.ShapeDtypeStruct((n, LANES), F32), jax.ShapeDtypeStruct((n, dg), BF16),
    ]
    return pl.pallas_call(
        functools.partial(_proj_body, d_pool=d_pool, d_att=d_att, q_scale=q_scale),
        grid=(n // tm,),
        in_specs=[pl.BlockSpec((tm, d), row), pl.BlockSpec((1, d), const),
                  pl.BlockSpec(wm.shape, const), pl.BlockSpec(wf.shape, const),
                  pl.BlockSpec(wg.shape, const), pl.BlockSpec((1, LANES), const)],
        out_specs=[pl.BlockSpec((tm, d_pool), row), pl.BlockSpec((tm, d_att), row),
                   pl.BlockSpec((tm, d_att), row), pl.BlockSpec((tm, d_att), row),
                   pl.BlockSpec((tm, d_att), row), pl.BlockSpec((tm, d_att), row),
                   pl.BlockSpec((tm, LANES), row), pl.BlockSpec((tm, dg), row)],
        out_shape=out_shape,
        compiler_params=_params("arbitrary"),
        name="proj_prompt",
    )(x, g, wm, wf, wg, bfp)


def _proj_sample_body(x_ref, g_ref, wm_ref, wf_ref, wg_ref, bf_ref, z_ref, lf_ref, gate_ref):
    h = _rmsnorm(x_ref[...], g_ref[...])
    z_ref[...] = _dot(h, wm_ref[...], True)
    lf_ref[...] = _log_sigmoid(_dot(h, wf_ref[...], True) + bf_ref[...])
    gate_ref[...] = _sigmoid(_dot(h, wg_ref[...], True))


def _proj_sample(x, g, wm, wf, wg, bfp, *, tn):
    n, d = x.shape
    dm, dg = wm.shape[1], wg.shape[1]
    assert dm == dg
    const = lambda j: (0, 0)
    col = lambda j: (0, j)
    return pl.pallas_call(
        _proj_sample_body,
        grid=(dm // tn,),
        in_specs=[pl.BlockSpec((n, d), const), pl.BlockSpec((1, d), const),
                  pl.BlockSpec((d, tn), col), pl.BlockSpec(wf.shape, const),
                  pl.BlockSpec((d, tn), col), pl.BlockSpec((1, LANES), const)],
        out_specs=[pl.BlockSpec((n, tn), col), pl.BlockSpec((n, LANES), const), pl.BlockSpec((n, tn), col)],
        out_shape=[jax.ShapeDtypeStruct((n, dm), F32), jax.ShapeDtypeStruct((n, LANES), F32),
                   jax.ShapeDtypeStruct((n, dg), F32)],
        compiler_params=_params("arbitrary"),
        name="proj_sample",
    )(x, g, wm, wf, wg, bfp)


def _cumsum_body(x_ref, o_ref):
    c = x_ref[...]
    lane = lax.broadcasted_iota(I32, c.shape, 1)
    s = 1
    while s < c.shape[1]:
        c = c + jnp.where(lane >= s, pltpu.roll(c, s, 1), 0.0)
        s *= 2
    o_ref[...] = c


def _cumsum_lanes(x):
    return pl.pallas_call(_cumsum_body, out_shape=jax.ShapeDtypeStruct(x.shape, F32),
                          compiler_params=_params(), name="cumsum_logf")(x)


def _attn_body(q_ref, k_ref, v_ref, c_ref, o_ref, *, tile, dh):
    qi = pl.program_id(2)
    q = q_ref[0]
    lane = lax.broadcasted_iota(I32, q.shape, 1)
    zero = jnp.zeros_like(q)
    q_heads = (jnp.where(lane < dh, q, zero), jnp.where(lane < dh, zero, q))
    row = lax.broadcasted_iota(I32, (tile, tile), 0)
    col = lax.broadcasted_iota(I32, (tile, tile), 1)
    causal = col <= row

    def step(kj, carry, masked):
        start = pl.multiple_of(kj * tile, tile)
        kt = k_ref[0, pl.ds(start, tile), :]
        vt = v_ref[0, pl.ds(start, tile), :]
        ck = c_ref[0, 0, kj]
        out = []
        for h in range(2):
            m, l, acc = carry[h]
            s = lax.dot_general(q_heads[h], kt, (((1,), (1,)), ((), ())), preferred_element_type=F32)
            s = s - ck[h:h + 1, :]
            if masked:
                s = jnp.where(causal, s, -jnp.inf)
            m_new = jnp.maximum(m, jnp.max(s, axis=-1, keepdims=True))
            alpha = jnp.exp(m - m_new)
            p = jnp.exp(s - m_new)
            l = alpha * l + jnp.sum(p, axis=-1, keepdims=True)
            acc = alpha * acc + jnp.dot(p.astype(BF16), vt, preferred_element_type=F32)
            out.append((m_new, l, acc))
        return tuple(out)

    init_h = (jnp.full((tile, 1), -1e30, F32), jnp.zeros((tile, 1), F32), jnp.zeros((tile, 2 * dh), F32))
    carry = lax.fori_loop(0, qi, lambda kj, c: step(kj, c, False), (init_h, init_h))
    (m0, l0, a0), (m1, l1, a1) = step(qi, carry, True)
    o_ref[0] = jnp.where(lane < dh, a0 / l0, a1 / l1).astype(o_ref.dtype)


def _attn_prompt(q, k, v, c, *, tile, dh):
    b, t, da = q.shape
    hp = da // (2 * dh)
    nt = t // tile
    return pl.pallas_call(
        functools.partial(_attn_body, tile=tile, dh=dh),
        grid=(b, hp, nt),
        in_specs=[pl.BlockSpec((1, tile, 2 * dh), lambda bi, hi, qi: (bi, qi, hi)),
                  pl.BlockSpec((1, t, 2 * dh), lambda bi, hi, qi: (bi, 0, hi)),
                  pl.BlockSpec((1, t, 2 * dh), lambda bi, hi, qi: (bi, 0, hi)),
                  pl.BlockSpec((1, 1, nt, 2, tile), lambda bi, hi, qi: (bi, hi, 0, 0, 0))],
        out_specs=pl.BlockSpec((1, tile, 2 * dh), lambda bi, hi, qi: (bi, qi, hi)),
        out_shape=jax.ShapeDtypeStruct((b, t, da), BF16),
        compiler_params=_params("arbitrary", "arbitrary", "arbitrary"),
        name="attn_prompt",
    )(q, k, v, c)


def _tree_reduce(op, x):
    parts = [x[i] for i in range(x.shape[0])]
    while len(parts) > 1:
        nxt = [op(parts[i], parts[i + 1]) for i in range(0, len(parts) - 1, 2)]
        if len(parts) % 2:
            nxt.append(parts[-1])
        parts = nxt
    return parts[0]


def _attn_sample_body(pt_ref, q_ref, kn_ref, vn_ref, lfn_ref, *rest, n_pages_step, n_heads):
    g_n = n_pages_step
    k_refs = rest[0:g_n]
    v_refs = rest[g_n:2 * g_n]
    lf_refs = rest[2 * g_n:3 * g_n]
    o_ref = rest[3 * g_n]
    m_ref, l_ref, acc_ref, srun_ref, lfpad_ref, w_ref, sb_ref = rest[3 * g_n + 1:]
    step = pl.program_id(1)
    q = q_ref[0]
    dh = q.shape[1]
    page = lfpad_ref.shape[0]

    @pl.when(step == 0)
    def _():
        m_ref[...] = jnp.broadcast_to(jnp.sum(q * kn_ref[0], axis=-1, keepdims=True), m_ref.shape)
        l_ref[...] = jnp.ones_like(l_ref)
        acc_ref[...] = vn_ref[0]
        srun_ref[...] = lfn_ref[0]
        lfpad_ref[...] = jnp.zeros_like(lfpad_ref)

    r = lax.broadcasted_iota(I32, (page, page), 0)
    c = lax.broadcasted_iota(I32, (page, page), 1)
    later_in_page = (c > r).astype(BF16)
    rr = lax.broadcasted_iota(I32, (LANES, LANES), 0)
    cc = lax.broadcasted_iota(I32, (LANES, LANES), 1)
    same_head = ((rr % n_heads) == (cc % n_heads)) & (rr < g_n * n_heads) & (cc < g_n * n_heads)
    newer_page = (same_head & (rr // n_heads < cc // n_heads)).astype(BF16)
    any_page = same_head.astype(BF16)
    for g in range(g_n):
        lfpad_ref[:, g * n_heads:(g + 1) * n_heads] = lf_refs[g][0]
    lf = lfpad_ref[...]
    tot = jnp.broadcast_to(jnp.sum(lf, axis=0, keepdims=True), (8, LANES))
    s_run = srun_ref[...]
    w_ref[...] = s_run + _dot_exact_rhs(tot, newer_page)[0:1] + _dot_exact_lhs(later_in_page, lf)
    srun_ref[...] = s_run + _dot_exact_rhs(tot, any_page)[0:1]

    sub = lax.broadcasted_iota(I32, (n_heads, dh), 0)
    lane = lax.broadcasted_iota(I32, (n_heads, dh), 1)
    m_prev = m_ref[...]
    m_new = m_prev
    for g in range(g_n):
        pick = (lane == sub + g * n_heads).astype(F32)
        wb = jnp.stack([jnp.broadcast_to(w_ref[pl.ds(i, 1), :], (n_heads, LANES)) for i in range(page)])
        t = k_refs[g][0] * q[None] + wb[:, :, 0:dh] * pick[None]
        sb = jnp.broadcast_to(jnp.sum(t, axis=-1, keepdims=True), t.shape)
        sb_ref[g] = sb
        m_new = jnp.maximum(m_new, _tree_reduce(jnp.maximum, sb))
    alpha = jnp.exp(m_prev - m_new)
    l = alpha * l_ref[...]
    acc = alpha * acc_ref[...]
    for g in range(g_n):
        p = jnp.exp(sb_ref[g] - m_new[None])
        l = l + _tree_reduce(jnp.add, p)
        acc = acc + _tree_reduce(jnp.add, p * v_refs[g][0])
    m_ref[...] = m_new
    l_ref[...] = l
    acc_ref[...] = acc

    @pl.when(step == pl.num_programs(1) - 1)
    def _():
        o_ref[0] = acc / l


def _attn_sample(page_table, q, k_new, v_new, lf_new, cache_k, cache_v, cache_lf):
    db, n_pages = page_table.shape
    n_phys, page, n_heads, dh = cache_k.shape
    g_n = min(PAGES_PER_STEP, dh // n_heads)
    while n_pages % g_n:
        g_n //= 2
    n_steps = n_pages // g_n

    def page_map(g, nd):
        return lambda b, s, pt: (pt[b, n_pages - 1 - (s * g_n + g)],) + (0,) * nd

    per_seq = lambda b, s, pt: (b, 0, 0)
    in_specs = [pl.BlockSpec((1, n_heads, dh), per_seq), pl.BlockSpec((1, n_heads, dh), per_seq),
                pl.BlockSpec((1, n_heads, dh), per_seq), pl.BlockSpec((1, 1, LANES), per_seq)]
    in_specs += [pl.BlockSpec((1, page, n_heads, dh), page_map(g, 3)) for g in range(g_n)]
    in_specs += [pl.BlockSpec((1, page, n_heads, dh), page_map(g, 3)) for g in range(g_n)]
    in_specs += [pl.BlockSpec((1, page, n_heads), page_map(g, 2)) for g in range(g_n)]
    grid_spec = pltpu.PrefetchScalarGridSpec(
        num_scalar_prefetch=1, grid=(db, n_steps), in_specs=in_specs,
        out_specs=pl.BlockSpec((1, n_heads, dh), per_seq),
        scratch_shapes=[pltpu.VMEM((n_heads, dh), F32), pltpu.VMEM((n_heads, dh), F32), pltpu.VMEM((n_heads, dh), F32),
                        pltpu.VMEM((1, LANES), F32), pltpu.VMEM((page, LANES), F32), pltpu.VMEM((page, LANES), F32),
                        pltpu.VMEM((g_n, page, n_heads, dh), F32)])
    return pl.pallas_call(
        functools.partial(_attn_sample_body, n_pages_step=g_n, n_heads=n_heads),
        grid_spec=grid_spec,
        out_shape=jax.ShapeDtypeStruct((db, n_heads, dh), F32),
        compiler_params=_params("arbitrary", "arbitrary"),
        name="attn_sample",
    )(page_table, q, k_new, v_new, lf_new,
      *([cache_k] * g_n), *([cache_v] * g_n), *([cache_lf] * g_n))


def _merge_and_route(x, pooled, att, gates, wp_ref, ps_ref, wup_ref, wua_ref, wo_ref, nf_ref, wr_ref, br_ref,
                     base_counts, *, precise, n_groups, n_per_group):
    tm, d = x.shape
    gw = pooled[0].shape[1]
    mixed = jnp.concatenate([_dot(pooled[g], wp_ref[g], precise) for g in range(len(pooled))], axis=-1)
    pool_out = mixed * ps_ref[...]
    y = gates[:, :d].astype(F32) * _dot(pool_out, wup_ref[...], precise) \
        + gates[:, d:].astype(F32) * _dot(att, wua_ref[...], precise)
    x2 = x + _dot(y, wo_ref[...], precise)
    h2 = _rmsnorm(x2, nf_ref[...])
    logits = _dot(h2, wr_ref[...], precise) + br_ref[...]
    lane = lax.broadcasted_iota(I32, logits.shape, 1)
    lanef = lane.astype(F32)
    neg = -jnp.inf
    is_g = lane < n_groups
    gmax = jnp.max(jnp.where(is_g, logits, neg), axis=-1, keepdims=True)
    gidx = jnp.min(jnp.where(is_g & (logits == gmax), lanef, float(LANES)), axis=-1, keepdims=True)
    gsum = jnp.sum(jnp.where(is_g, jnp.exp(logits - gmax), 0.0), axis=-1, keepdims=True)
    g_w = 1.0 / gsum
    n_exp = n_groups * n_per_group
    exp_id = lanef - float(n_groups)
    in_sel = (lane >= n_groups) & (lane < n_groups + n_exp) & (jnp.floor(exp_id / n_per_group) == gidx)
    v1 = jnp.max(jnp.where(in_sel, logits, neg), axis=-1, keepdims=True)
    i1 = jnp.min(jnp.where(in_sel & (logits == v1), lanef, float(LANES)), axis=-1, keepdims=True)
    in_sel2 = in_sel & (lanef != i1)
    v2 = jnp.max(jnp.where(in_sel2, logits, neg), axis=-1, keepdims=True)
    i2 = jnp.min(jnp.where(in_sel2 & (logits == v2), lanef, float(LANES)), axis=-1, keepdims=True)
    t = jnp.exp(v2 - v1)
    w1 = g_w * (1.0 / (1.0 + t))
    w2 = g_w * (t / (1.0 + t))
    e1 = i1 - float(n_groups)
    e2 = i2 - float(n_groups)
    hit1 = lanef == e1
    hit2 = lanef == e2
    onehot = (hit1 | hit2).astype(BF16)
    rr = lax.broadcasted_iota(I32, (tm, tm), 0)
    cc = lax.broadcasted_iota(I32, (tm, tm), 1)
    incl = jnp.dot((cc <= rr).astype(BF16), onehot, preferred_element_type=F32)
    seen = incl + base_counts - 1.0
    r1 = jnp.sum(jnp.where(hit1, seen, 0.0), axis=-1, keepdims=True)
    r2 = jnp.sum(jnp.where(hit2, seen, 0.0), axis=-1, keepdims=True)
    counts = base_counts + incl[tm - 1:tm, :]
    slab = jnp.zeros((tm, LANES), F32)
    for i, val in enumerate((e1, e2, r1, r2, w1, w2)):
        slab = jnp.where(lane == i, val, slab)
    return x2, h2, slab, counts


def _merge_prompt_body(x_ref, u_ref, halo_ref, att_ref, gate_ref, wp_ref, ps_ref, wup_ref, wua_ref, wo_ref,
                       nf_ref, wr_ref, br_ref, x2_ref, h2_ref, slab_ref, route_ref, counts_ref, ext_ref, cnt_ref,
                       *, seq_len, n_groups, n_per_group):
    i = pl.program_id(0)
    tm = x_ref.shape[0]
    gw = u_ref.shape[1] // len(POOL_WINDOWS)
    pos0 = (i * tm) % seq_len

    @pl.when(i == 0)
    def _():
        cnt_ref[...] = jnp.zeros_like(cnt_ref)

    u = u_ref[...]
    ext_ref[0:POOL_HALO, :] = jnp.where(pos0 == 0, 0.0, halo_ref[...])
    ext_ref[POOL_HALO:, :] = u
    pos = pos0 + lax.broadcasted_iota(I32, (tm, 1), 0)
    pooled = []
    for g, w in enumerate(POOL_WINDOWS):
        lo = g * gw
        wsum = ext_ref[pl.ds(POOL_HALO, tm), lo:lo + gw]
        for j in range(1, w):
            wsum = wsum + ext_ref[pl.ds(POOL_HALO - j, tm), lo:lo + gw]
        count = jnp.minimum(pos + 1, w).astype(F32)
        pooled.append(wsum / count - u[:, lo:lo + gw])
    x2, h2, slab, counts = _merge_and_route(
        x_ref[...], pooled, att_ref[...], gate_ref[...], wp_ref, ps_ref, wup_ref, wua_ref, wo_ref, nf_ref,
        wr_ref, br_ref, cnt_ref[...], precise=False, n_groups=n_groups, n_per_group=n_per_group)
    x2_ref[...] = x2
    h2_ref[...] = h2
    slab_ref[...] = slab
    route_ref[...] = slab.T[0:8, :]
    cnt_ref[...] = counts
    counts_ref[...] = counts


def _merge_prompt(x, u, att, gates, wp, ps, wup, wua, wo, nf, wr, br, *, tm, seq_len, n_groups, n_per_group):
    n, d = x.shape
    d_pool, d_att = u.shape[1], att.shape[1]
    row = lambda i: (i, 0)
    const = lambda i: (0, 0)
    const3 = lambda i: (0, 0, 0)
    halo = lambda i: (jnp.maximum(i * (tm // POOL_HALO) - 1, 0), 0)
    return pl.pallas_call(
        functools.partial(_merge_prompt_body, seq_len=seq_len, n_groups=n_groups, n_per_group=n_per_group),
        grid=(n // tm,),
        in_specs=[pl.BlockSpec((tm, d), row), pl.BlockSpec((tm, d_pool), row), pl.BlockSpec((POOL_HALO, d_pool), halo),
                  pl.BlockSpec((tm, d_att), row), pl.BlockSpec((tm, 2 * d), row),
                  pl.BlockSpec(wp.shape, const3), pl.BlockSpec((1, d_pool), const),
                  pl.BlockSpec(wup.shape, const), pl.BlockSpec(wua.shape, const), pl.BlockSpec(wo.shape, const),
                  pl.BlockSpec((1, d), const), pl.BlockSpec(wr.shape, const), pl.BlockSpec((1, LANES), const)],
        out_specs=[pl.BlockSpec((tm, d), row), pl.BlockSpec((tm, d), row), pl.BlockSpec((tm, LANES), row),
                   pl.BlockSpec((8, tm), lambda i: (0, i)), pl.BlockSpec((1, LANES), const)],
        out_shape=[jax.ShapeDtypeStruct((n, d), F32), jax.ShapeDtypeStruct((n, d), F32),
                   jax.ShapeDtypeStruct((n, LANES), F32), jax.ShapeDtypeStruct((8, n), F32),
                   jax.ShapeDtypeStruct((1, LANES), F32)],
        scratch_shapes=[pltpu.VMEM((tm + POOL_HALO, d_pool), F32), pltpu.VMEM((1, LANES), F32)],
        compiler_params=_params("arbitrary"),
        name="merge_prompt",
    )(x, u, u, att, gates, wp, ps, wup, wua, wo, nf, wr, br)


def _merge_sample_body(x_ref, u_ref, st_ref, att_ref, gate_ref, wp_ref, ps_ref, wup_ref, wua_ref, wo_ref,
                       nf_ref, wr_ref, br_ref, base_ref, x2_ref, h2_ref, slab_ref, counts_ref,
                       *, start_pos, n_groups, n_per_group):
    u = u_ref[...]
    gw = u.shape[1] // len(POOL_WINDOWS)
    n_state = st_ref.shape[0]
    pooled = []
    for g, w in enumerate(POOL_WINDOWS):
        lo = g * gw
        wsum = u[:, lo:lo + gw]
        for j in range(1, w):
            wsum = wsum + st_ref[n_state - j][:, lo:lo + gw]
        pooled.append(wsum / float(min(start_pos + 1, w)) - u[:, lo:lo + gw])
    x2, h2, slab, counts = _merge_and_route(
        x_ref[...], pooled, att_ref[...], gate_ref[...], wp_ref, ps_ref, wup_ref, wua_ref, wo_ref, nf_ref,
        wr_ref, br_ref, base_ref[...], precise=True, n_groups=n_groups, n_per_group=n_per_group)
    x2_ref[...] = x2
    h2_ref[...] = h2
    slab_ref[...] = slab
    counts_ref[...] = counts


def _merge_sample(x, u, state_t, att, gates, wp, ps, wup, wua, wo, nf, wr, br, base, *, start_pos, n_groups,
                  n_per_group):
    n, d = x.shape
    return pl.pallas_call(
        functools.partial(_merge_sample_body, start_pos=start_pos, n_groups=n_groups, n_per_group=n_per_group),
        out_shape=[jax.ShapeDtypeStruct((n, d), F32), jax.ShapeDtypeStruct((n, d), F32),
                   jax.ShapeDtypeStruct((n, LANES), F32), jax.ShapeDtypeStruct((1, LANES), F32)],
        compiler_params=_params(),
        name="merge_sample",
    )(x, u, state_t, att, gates, wp, ps, wup, wua, wo, nf, wr, br, base)


def _row_copy(src_ref, src_row, dst_ref, dst_row, sem):
    return pltpu.make_async_copy(src_ref.at[pl.ds(src_row, 1)], dst_ref.at[pl.ds(dst_row, 1)], sem)


def _moe_scatter_body(goff_ref, e1_ref, e2_ref, r1_ref, r2_ref, h_ref, xs_in_ref, xs_ref, sem):
    del xs_in_ref
    n = h_ref.shape[0]

    def start(i, carry):
        for j in range(ROW_DMA_UNROLL):
            t = i * ROW_DMA_UNROLL + j
            _row_copy(h_ref, t, xs_ref, goff_ref[e1_ref[t]] + r1_ref[t], sem).start()
            _row_copy(h_ref, t, xs_ref, goff_ref[e2_ref[t]] + r2_ref[t], sem).start()
        return carry

    lax.fori_loop(0, n // ROW_DMA_UNROLL, start, 0)
    all_rows = pltpu.make_async_copy(h_ref, xs_ref.at[pl.ds(0, n)], sem)
    all_rows.wait()
    all_rows.wait()


def _moe_scatter(goff, e1, e2, r1, r2, h, xs, *, ts):
    n, d = h.shape
    smem = lambda: pl.BlockSpec((ts,), lambda i: (i,), memory_space=pltpu.SMEM)
    return pl.pallas_call(
        _moe_scatter_body,
        grid=(n // ts,),
        in_specs=[pl.BlockSpec(memory_space=pltpu.SMEM), smem(), smem(), smem(), smem(),
                  pl.BlockSpec((ts, d), lambda i: (i, 0)), pl.BlockSpec(memory_space=pl.ANY)],
        out_specs=pl.BlockSpec(memory_space=pl.ANY),
        out_shape=jax.ShapeDtypeStruct(xs.shape, xs.dtype),
        scratch_shapes=[pltpu.SemaphoreType.DMA],
        input_output_aliases={6: 0},
        compiler_params=_params("arbitrary"),
        name="moe_scatter",
    )(goff, e1, e2, r1, r2, h, xs)


def _moe_mm_body(te_ref, tw_ref, xs_ref, wg_ref, wu_ref, wd_ref, ys_ref, wgb_ref, wub_ref, wdb_ref):
    i = pl.program_id(0)
    expert = te_ref[i]
    prev = te_ref[jnp.maximum(i - 1, 0)]

    @pl.when((expert >= 0) & ((i == 0) | (expert != prev)))
    def _():
        wgb_ref[...] = wg_ref[0].astype(BF16)
        wub_ref[...] = wu_ref[0].astype(BF16)
        wdb_ref[...] = wd_ref[0].astype(BF16)

    @pl.when(expert >= 0)
    def _():
        x = xs_ref[...].astype(BF16)
        a = jnp.dot(x, wgb_ref[...], preferred_element_type=F32)
        b = jnp.dot(x, wub_ref[...], preferred_element_type=F32)
        hdn = (a * _sigmoid(a)) * b
        ys_ref[...] = jnp.dot(hdn.astype(BF16), wdb_ref[...], preferred_element_type=F32)

    @pl.when(expert < 0)
    def _():
        ys_ref[...] = jnp.zeros_like(ys_ref)


def _moe_mm(tile_expert, tile_weight, xs, w_gate, w_up, w_down, *, tm):
    p, d = xs.shape
    n_exp, _, de = w_gate.shape
    wmap = lambda i, te, tw: (tw[i], 0, 0)
    grid_spec = pltpu.PrefetchScalarGridSpec(
        num_scalar_prefetch=2, grid=(p // tm,),
        in_specs=[pl.BlockSpec((tm, d), lambda i, te, tw: (i, 0)),
                  pl.BlockSpec((1, d, de), wmap), pl.BlockSpec((1, d, de), wmap), pl.BlockSpec((1, de, d), wmap)],
        out_specs=pl.BlockSpec((tm, d), lambda i, te, tw: (i, 0)),
        scratch_shapes=[pltpu.VMEM((d, de), BF16), pltpu.VMEM((d, de), BF16), pltpu.VMEM((de, d), BF16)])
    return pl.pallas_call(
        _moe_mm_body, grid_spec=grid_spec,
        out_shape=jax.ShapeDtypeStruct((p, d), F32),
        compiler_params=_params("arbitrary"),
        name="moe_mm",
    )(tile_expert, tile_weight, xs, w_gate, w_up, w_down)


def _moe_combine_body(goff_ref, e1_ref, e2_ref, r1_ref, r2_ref, x_ref, slab_ref, g_ref, ys_ref, o_ref,
                      ya_ref, yb_ref, sem, *, final_norm):
    n = x_ref.shape[0]

    def start(i, carry):
        for j in range(ROW_DMA_UNROLL):
            t = i * ROW_DMA_UNROLL + j
            _row_copy(ys_ref, goff_ref[e1_ref[t]] + r1_ref[t], ya_ref, t, sem).start()
            _row_copy(ys_ref, goff_ref[e2_ref[t]] + r2_ref[t], yb_ref, t, sem).start()
        return carry

    lax.fori_loop(0, n // ROW_DMA_UNROLL, start, 0)
    pltpu.make_async_copy(ys_ref.at[pl.ds(0, n)], ya_ref, sem).wait()
    pltpu.make_async_copy(ys_ref.at[pl.ds(0, n)], yb_ref, sem).wait()
    slab = slab_ref[...]
    out = x_ref[...] + (slab[:, 4:5] * ya_ref[...] + slab[:, 5:6] * yb_ref[...])
    if final_norm:
        out = _rmsnorm(out, g_ref[...])
    o_ref[...] = out


def _moe_combine(goff, e1, e2, r1, r2, x, slab, g, ys, *, ts, final_norm):
    n, d = x.shape
    smem = lambda: pl.BlockSpec((ts,), lambda i: (i,), memory_space=pltpu.SMEM)
    return pl.pallas_call(
        functools.partial(_moe_combine_body, final_norm=final_norm),
        grid=(n // ts,),
        in_specs=[pl.BlockSpec(memory_space=pltpu.SMEM), smem(), smem(), smem(), smem(),
                  pl.BlockSpec((ts, d), lambda i: (i, 0)), pl.BlockSpec((ts, LANES), lambda i: (i, 0)),
                  pl.BlockSpec((1, d), lambda i: (0, 0)), pl.BlockSpec(memory_space=pl.ANY)],
        out_specs=pl.BlockSpec((ts, d), lambda i: (i, 0)),
        out_shape=jax.ShapeDtypeStruct((n, d), F32),
        scratch_shapes=[pltpu.VMEM((ts, d), F32), pltpu.VMEM((ts, d), F32), pltpu.SemaphoreType.DMA],
        compiler_params=_params("arbitrary"),
        name="moe_combine",
    )(goff, e1, e2, r1, r2, x, slab, g, ys)


def kernel(x_prompt, x_sample, cache_k, cache_v, cache_logf, state_pool, page_table, norm_mix, w_in, b_forget,
           w_pool, pool_scale, w_up_pool, w_up_att, w_out, norm_ffn, w_router_group, b_router_group,
           w_router_expert, b_router_expert, w_gate, w_up, w_down, norm_final):
    depth = norm_mix.shape[0]
    assert depth == 1, "single trunk layer"
    b, t, d = x_prompt.shape
    db, dt, _ = x_sample.shape
    assert dt == 1, "one sample token per sequence"
    _, n_phys, page, n_heads, dh = cache_k.shape
    n_pages = page_table.shape[1]
    past = n_pages * page
    n_state, d_pool = state_pool.shape[2], state_pool.shape[3]
    d_att = n_heads * dh
    n_pool_groups = w_pool.shape[1]
    assert n_pool_groups == len(POOL_WINDOWS) and d_pool // n_pool_groups == LANES
    assert n_state == max(POOL_WINDOWS) - 1 and n_state < POOL_HALO
    n_groups, n_per_group = w_router_expert.shape[1], w_router_expert.shape[3]
    n_exp = n_groups * n_per_group
    assert n_groups + n_exp <= LANES and 2 * dh == LANES and n_heads % 2 == 0
    n = b * t
    q_scale = float(dh) ** -0.5
    tm = min(TOKEN_TILE, t)
    assert t % tm == 0 and t % ATTN_TILE == 0

    o_main = d_pool + 3 * d_att
    wi = w_in[0]
    wm_f, wf_f, wg_f = wi[:, :o_main], wi[:, o_main:o_main + n_heads], wi[:, o_main + n_heads:]
    wf_pad = jnp.pad(wf_f, ((0, 0), (0, LANES - n_heads)))
    bf_pad = jnp.pad(b_forget[0], (0, LANES - n_heads)).reshape(1, LANES)
    g_mix = norm_mix[0].reshape(1, d)
    g_ffn = norm_ffn[0].reshape(1, d)
    g_fin = norm_final.reshape(1, d)
    ps = pool_scale[0].reshape(1, d_pool)
    wr_f = jnp.concatenate([w_router_group[0], jnp.transpose(w_router_expert[0], (1, 0, 2)).reshape(d, n_exp)], axis=1)
    wr_pad = jnp.pad(wr_f, ((0, 0), (0, LANES - n_groups - n_exp)))
    br_pad = jnp.pad(jnp.concatenate([b_router_group[0], b_router_expert[0].reshape(n_exp)]),
                     (0, LANES - n_groups - n_exp)).reshape(1, LANES)
    bf = lambda a: a.astype(BF16)

    xp = x_prompt.reshape(n, d)
    u_p, q_p, k_p, v_p, kb_p, vb_p, lf_p, gate_p = _proj_prompt(
        xp, g_mix, bf(wm_f), bf(wf_pad), bf(wg_f), bf_pad, tm=tm, d_pool=d_pool, d_att=d_att, q_scale=q_scale)
    logf_p = lf_p[:, :n_heads].reshape(b, t, n_heads)
    c = _cumsum_lanes(jnp.transpose(logf_p, (0, 2, 1)).reshape(b * n_heads, t))
    nt = t // ATTN_TILE
    c_blk = jnp.transpose(c.reshape(b, n_heads // 2, 2, nt, ATTN_TILE), (0, 1, 3, 2, 4))
    att_p = _attn_prompt(q_p.reshape(b, t, d_att), kb_p.reshape(b, t, d_att), vb_p.reshape(b, t, d_att), c_blk,
                         tile=ATTN_TILE, dh=dh)
    x2_p, h2_p, slab_p, route_p, counts_p = _merge_prompt(
        xp, u_p, att_p.reshape(n, d_att), gate_p, bf(w_pool[0]), ps, bf(w_up_pool[0]), bf(w_up_att[0]),
        bf(w_out[0]), g_ffn, bf(wr_pad), br_pad, tm=tm, seq_len=t, n_groups=n_groups, n_per_group=n_per_group)

    xs = x_sample.reshape(db, d)
    z_s, lf_s, gate_s = _proj_sample(xs, g_mix, wm_f, wf_pad, wg_f, bf_pad, tn=512)
    u_s = z_s[:, :d_pool]
    q_s = z_s[:, d_pool:d_pool + d_att] * q_scale
    k_s = z_s[:, d_pool + d_att:d_pool + 2 * d_att]
    v_s = z_s[:, d_pool + 2 * d_att:]
    lf_new = jnp.tile(lf_s[:, :n_heads], (1, LANES // n_heads)).reshape(db, 1, LANES)
    att_s = _attn_sample(page_table, q_s.reshape(db, n_heads, dh), k_s.reshape(db, n_heads, dh),
                         v_s.reshape(db, n_heads, dh), lf_new, cache_k[0], cache_v[0], cache_logf[0])
    state_t = jnp.transpose(state_pool[0], (1, 0, 2))
    x2_s, h2_s, slab_s, counts = _merge_sample(
        xs, u_s, state_t, att_s.reshape(db, d_att), gate_s, w_pool[0], ps, w_up_pool[0], w_up_att[0], w_out[0],
        g_ffn, wr_pad, br_pad, counts_p, start_pos=past, n_groups=n_groups, n_per_group=n_per_group)

    tmm = MOE_ROW_TILE
    n_tok = n + db
    n_tiles = -(-(2 * n_tok + n_exp * (tmm - 1)) // tmm)
    cnt = counts[0, :n_exp].astype(I32)
    padded = ((cnt + tmm - 1) // tmm) * tmm
    ends = jnp.cumsum(padded)
    goff = (ends - padded).astype(I32)
    tile_e = jnp.sum((jnp.arange(n_tiles, dtype=I32)[:, None] * tmm >= ends[None, :]).astype(I32), axis=1)
    tile_expert = jnp.where(tile_e < n_exp, tile_e, -1).astype(I32)
    tile_weight = jnp.minimum(tile_e, n_exp - 1).astype(I32)
    fields_p = route_p[:4].astype(I32)
    fields_s = jnp.transpose(slab_s[:, :4]).astype(I32)
    xs_rows = jnp.zeros((n_tiles * tmm, d), F32)
    xs_rows = _moe_scatter(goff, *fields_p, h2_p, xs_rows, ts=tm)
    xs_rows = _moe_scatter(goff, *fields_s, h2_s, xs_rows, ts=db)
    ys_rows = _moe_mm(tile_expert, tile_weight, xs_rows, w_gate[0], w_up[0], w_down[0], tm=tmm)
    y_prompt = _moe_combine(goff, *fields_p, x2_p, slab_p, g_fin, ys_rows, ts=tm, final_norm=True)
    y_sample = _moe_combine(goff, *fields_s, x2_s, slab_s, g_fin, ys_rows, ts=db, final_norm=True)

    new_pool_p = u_p.reshape(b, t, d_pool)[:, t - n_state:, :]
    new_pool_s = jnp.concatenate([state_pool[0][:, 1:, :], u_s[:, None, :]], axis=1)
    return (y_prompt.reshape(b, t, d), y_sample.reshape(db, 1, d),
            k_p.reshape(1, b, t, n_heads, dh), v_p.reshape(1, b, t, n_heads, dh), logf_p[None],
            new_pool_p[None],
            k_s.reshape(1, db, 1, n_heads, dh), v_s.reshape(1, db, 1, n_heads, dh),
            lf_s[:, :n_heads].reshape(1, db, 1, n_heads), new_pool_s[None])
```

```python
import functools

import jax
import jax.numpy as jnp
from jax import lax
from jax.experimental import pallas as pl
from jax.experimental.pallas import tpu as pltpu

F32 = jnp.float32
BF16 = jnp.bfloat16
I32 = jnp.int32
HIGHEST = lax.Precision.HIGHEST

RMS_EPS = 1e-6
POOL_WINDOWS = (2, 4, 8, 16)
POOL_HALO = 16
LANES = 128
VMEM_LIMIT_BYTES = 56 * 1024 * 1024

TOKEN_TILE = 512
ATTN_TILE = 512
MOE_ROW_TILE = 256
PAGES_PER_STEP = 8
ROW_DMA_UNROLL = 8


def _params(*sem):
    return pltpu.CompilerParams(dimension_semantics=sem, vmem_limit_bytes=VMEM_LIMIT_BYTES)


def _rmsnorm(x, g):
    return x * lax.rsqrt(jnp.mean(x * x, axis=-1, keepdims=True) + RMS_EPS) * g


def _log_sigmoid(x):
    return jnp.minimum(x, 0.0) - jnp.log1p(jnp.exp(-jnp.abs(x)))


def _sigmoid(x):
    return 1.0 / (1.0 + jnp.exp(-x))


def _dot(a, b, precise):
    if precise:
        return jnp.dot(a.astype(F32), b.astype(F32), precision=HIGHEST, preferred_element_type=F32)
    return jnp.dot(a.astype(BF16), b.astype(BF16), preferred_element_type=F32)


def _split3(x):
    hi = x.astype(BF16)
    r = x - hi.astype(F32)
    mid = r.astype(BF16)
    lo = (r - mid.astype(F32)).astype(BF16)
    return hi, mid, lo


def _dot_exact_rhs(x, w_bf16):
    hi, mid, lo = _split3(x)
    d = lambda a: jnp.dot(a, w_bf16, preferred_element_type=F32)
    return d(hi) + d(mid) + d(lo)


def _dot_exact_lhs(w_bf16, x):
    hi, mid, lo = _split3(x)
    d = lambda a: jnp.dot(w_bf16, a, preferred_element_type=F32)
    return d(hi) + d(mid) + d(lo)


def _proj_body(x_ref, g_ref, wm_ref, wf_ref, wg_ref, bf_ref,
               u_ref, q_ref, kt_ref, vt_ref, kb_ref, vb_ref, lft_ref, gate_ref, *, d_pool, d_att, n_heads, q_scale):
    h = _rmsnorm(x_ref[...], g_ref[...]).astype(BF16)
    z = jnp.dot(h, wm_ref[...], preferred_element_type=F32)
    o1, o2, o3 = d_pool, d_pool + d_att, d_pool + 2 * d_att
    u_ref[...] = z[:, :o1]
    q_ref[...] = (z[:, o1:o2] * q_scale).astype(BF16)
    k = z[:, o2:o3]
    v = z[:, o3:]
    kt_ref[0] = k.T
    vt_ref[0] = v.T
    kb_ref[...] = k.astype(BF16)
    vb_ref[...] = v.astype(BF16)
    lf = _log_sigmoid(jnp.dot(h, wf_ref[...], preferred_element_type=F32) + bf_ref[...])
    lft_ref[0] = lf.T[0:n_heads, :]
    gate_ref[...] = _sigmoid(jnp.dot(h, wg_ref[...], preferred_element_type=F32)).astype(BF16)


def _proj_prompt(x, g, wm, wf, wg, bfp, *, tm, seq_len, d_pool, d_att, n_heads, q_scale):
    n, d = x.shape
    b = n // seq_len
    tps = seq_len // tm
    row = lambda i: (i, 0)
    const = lambda i: (0, 0)
    tmin = lambda i: (i // tps, 0, i % tps)
    dg = wg.shape[1]
    out_shape = [
        jax.ShapeDtypeStruct((n, d_pool), F32), jax.ShapeDtypeStruct((n, d_att), BF16),
        jax.ShapeDtypeStruct((b, d_att, seq_len), F32), jax.ShapeDtypeStruct((b, d_att, seq_len), F32),
        jax.ShapeDtypeStruct((n, d_att), BF16), jax.ShapeDtypeStruct((n, d_att), BF16),
        jax.ShapeDtypeStruct((b, n_heads, seq_len), F32), jax.ShapeDtypeStruct((n, dg), BF16),
    ]
    return pl.pallas_call(
        functools.partial(_proj_body, d_pool=d_pool, d_att=d_att, n_heads=n_heads, q_scale=q_scale),
        grid=(n // tm,),
        in_specs=[pl.BlockSpec((tm, d), row), pl.BlockSpec((1, d), const),
                  pl.BlockSpec(wm.shape, const), pl.BlockSpec(wf.shape, const),
                  pl.BlockSpec(wg.shape, const), pl.BlockSpec((1, LANES), const)],
        out_specs=[pl.BlockSpec((tm, d_pool), row), pl.BlockSpec((tm, d_att), row),
                   pl.BlockSpec((1, d_att, tm), tmin), pl.BlockSpec((1, d_att, tm), tmin),
                   pl.BlockSpec((tm, d_att), row), pl.BlockSpec((tm, d_att), row),
                   pl.BlockSpec((1, n_heads, tm), tmin), pl.BlockSpec((tm, dg), row)],
        out_shape=out_shape,
        compiler_params=_params("arbitrary"),
        name="proj_prompt",
    )(x, g, wm, wf, wg, bfp)


def _dot_nt(a, bt, precise):
    dims = (((1,), (1,)), ((), ()))
    if precise:
        return lax.dot_general(a.astype(F32), bt.astype(F32), dims, precision=HIGHEST, preferred_element_type=F32)
    return lax.dot_general(a.astype(BF16), bt.astype(BF16), dims, preferred_element_type=F32)


def _proj_sample_body(x_ref, g_ref, wmt_ref, wft_ref, wgt_ref, bf_ref, z_ref, lf_ref, gate_ref):
    h = _rmsnorm(x_ref[...], g_ref[...])
    z_ref[...] = _dot_nt(h, wmt_ref[...], True)
    lf_ref[...] = _log_sigmoid(_dot_nt(h, wft_ref[...], True) + bf_ref[...])
    gate_ref[...] = _sigmoid(_dot_nt(h, wgt_ref[...], True))


def _proj_sample(x, g, wmt, wft, wgt, bfp, *, tn):
    n, d = x.shape
    dm, dg = wmt.shape[0], wgt.shape[0]
    assert dm == dg
    const = lambda j: (0, 0)
    chunk = lambda j: (j, 0)
    col = lambda j: (0, j)
    return pl.pallas_call(
        _proj_sample_body,
        grid=(dm // tn,),
        in_specs=[pl.BlockSpec((n, d), const), pl.BlockSpec((1, d), const),
                  pl.BlockSpec((tn, d), chunk), pl.BlockSpec(wft.shape, const),
                  pl.BlockSpec((tn, d), chunk), pl.BlockSpec((1, LANES), const)],
        out_specs=[pl.BlockSpec((n, tn), col), pl.BlockSpec((n, LANES), const), pl.BlockSpec((n, tn), col)],
        out_shape=[jax.ShapeDtypeStruct((n, dm), F32), jax.ShapeDtypeStruct((n, LANES), F32),
                   jax.ShapeDtypeStruct((n, dg), F32)],
        compiler_params=_params("arbitrary"),
        name="proj_sample",
    )(x, g, wmt, wft, wgt, bfp)


def _cumsum_body(x_ref, o_ref):
    c = x_ref[...]
    lane = lax.broadcasted_iota(I32, c.shape, 1)
    s = 1
    while s < c.shape[1]:
        c = c + jnp.where(lane >= s, pltpu.roll(c, s, 1), 0.0)
        s *= 2
    o_ref[...] = c


def _cumsum_lanes(x):
    return pl.pallas_call(_cumsum_body, out_shape=jax.ShapeDtypeStruct(x.shape, F32),
                          compiler_params=_params(), name="cumsum_logf")(x)


def _attn_body(q_ref, k_ref, v_ref, c_ref, o_ref, *, tile, dh):
    qi = pl.program_id(2)
    q = q_ref[0]
    lane = lax.broadcasted_iota(I32, q.shape, 1)
    zero = jnp.zeros_like(q)
    q_heads = (jnp.where(lane < dh, q, zero), jnp.where(lane < dh, zero, q))
    row = lax.broadcasted_iota(I32, (tile, tile), 0)
    col = lax.broadcasted_iota(I32, (tile, tile), 1)
    causal = col <= row

    def step(kj, carry, masked):
        start = pl.multiple_of(kj * tile, tile)
        kt = k_ref[0, pl.ds(start, tile), :]
        vt = v_ref[0, pl.ds(start, tile), :]
        ck = c_ref[0, 0, kj]
        out = []
        for h in range(2):
            m, l, acc = carry[h]
            s = lax.dot_general(q_heads[h], kt, (((1,), (1,)), ((), ())), preferred_element_type=F32)
            s = s - ck[h:h + 1, :]
            if masked:
                s = jnp.where(causal, s, -jnp.inf)
            m_new = jnp.maximum(m, jnp.max(s, axis=-1, keepdims=True))
            alpha = jnp.exp(m - m_new)
            p = jnp.exp(s - m_new)
            l = alpha * l + jnp.sum(p, axis=-1, keepdims=True)
            acc = alpha * acc + jnp.dot(p.astype(BF16), vt, preferred_element_type=F32)
            out.append((m_new, l, acc))
        return tuple(out)

    init_h = (jnp.full((tile, 1), -1e30, F32), jnp.zeros((tile, 1), F32), jnp.zeros((tile, 2 * dh), F32))
    carry = lax.fori_loop(0, qi, lambda kj, c: step(kj, c, False), (init_h, init_h))
    (m0, l0, a0), (m1, l1, a1) = step(qi, carry, True)
    o_ref[0] = jnp.where(lane < dh, a0 / l0, a1 / l1).astype(o_ref.dtype)


def _attn_prompt(q, k, v, c, *, tile, dh):
    b, t, da = q.shape
    hp = da // (2 * dh)
    nt = t // tile
    return pl.pallas_call(
        functools.partial(_attn_body, tile=tile, dh=dh),
        grid=(b, hp, nt),
        in_specs=[pl.BlockSpec((1, tile, 2 * dh), lambda bi, hi, qi: (bi, qi, hi)),
                  pl.BlockSpec((1, t, 2 * dh), lambda bi, hi, qi: (bi, 0, hi)),
                  pl.BlockSpec((1, t, 2 * dh), lambda bi, hi, qi: (bi, 0, hi)),
                  pl.BlockSpec((1, 1, nt, 2, tile), lambda bi, hi, qi: (bi, hi, 0, 0, 0))],
        out_specs=pl.BlockSpec((1, tile, 2 * dh), lambda bi, hi, qi: (bi, qi, hi)),
        out_shape=jax.ShapeDtypeStruct((b, t, da), BF16),
        compiler_params=_params("arbitrary", "arbitrary", "arbitrary"),
        name="attn_prompt",
    )(q, k, v, c)


def _attn_sample_body(pt_ref, qrep_ref, q_ref, kn_ref, vrep_ref, lfn_ref, *rest, n_pages_step):
    g_n = n_pages_step
    k_refs = rest[0:g_n]
    v_refs = rest[g_n:2 * g_n]
    lf_refs = rest[2 * g_n:3 * g_n]
    o_ref = rest[3 * g_n]
    m_ref, l_ref, acc_ref, srun_ref = rest[3 * g_n + 1:]
    step = pl.program_id(1)
    _, n_heads, dh, page = k_refs[0].shape
    d_att = n_heads * dh

    @pl.when(step == 0)
    def _():
        s_new = jnp.sum(q_ref[0] * kn_ref[0], axis=-1, keepdims=True)
        m_ref[...] = jnp.broadcast_to(s_new, m_ref.shape)
        l_ref[...] = jnp.ones_like(l_ref)
        lane = lax.broadcasted_iota(I32, (d_att, page), 1)
        acc_ref[...] = jnp.where(lane == 0, vrep_ref[0], 0.0)
        srun_ref[...] = lfn_ref[0]

    qrep = qrep_ref[0]
    r = lax.broadcasted_iota(I32, (page, page), 0)
    c = lax.broadcasted_iota(I32, (page, page), 1)
    later = (r > c).astype(BF16)
    ones = jnp.ones((page, page), BF16)
    lf_all = jnp.concatenate([lf_refs[g][0] for g in range(g_n)], axis=0)
    suffix = _dot_exact_rhs(lf_all, later)
    total = _dot_exact_rhs(lf_all, ones)
    s_run = srun_ref[...]
    m_prev = m_ref[...]
    m_new = m_prev
    scores = []
    for g in range(g_n):
        kq = k_refs[g][0].reshape(d_att, page) * qrep
        s = jnp.sum(kq.reshape(n_heads, dh, page), axis=1)
        sb = s + s_run + suffix[g * n_heads:(g + 1) * n_heads]
        s_run = s_run + total[g * n_heads:(g + 1) * n_heads]
        scores.append(sb)
        m_new = jnp.maximum(m_new, jnp.max(sb, axis=-1, keepdims=True))
    srun_ref[...] = s_run
    alpha = jnp.exp(m_prev - m_new)
    l = alpha * l_ref[...]
    acc = acc_ref[...].reshape(n_heads, dh, page) * alpha[:, None, :]
    for g in range(g_n):
        p = jnp.exp(scores[g] - m_new)
        l = l + jnp.sum(p, axis=-1, keepdims=True)
        acc = acc + v_refs[g][0] * p[:, None, :]
    m_ref[...] = m_new
    l_ref[...] = l
    acc_ref[...] = acc.reshape(d_att, page)

    @pl.when(step == pl.num_programs(1) - 1)
    def _():
        o_ref[0] = jnp.sum(acc / l[:, None, :], axis=-1)


def _attn_sample(page_table, q, k_new, v_new, lf_new, cache_kt, cache_vt, cache_lft):
    db, n_pages = page_table.shape
    n_phys, n_heads, dh, page = cache_kt.shape
    d_att = n_heads * dh
    g_n = PAGES_PER_STEP
    while n_pages % g_n:
        g_n //= 2
    n_steps = n_pages // g_n
    lane_rep = lambda a: jnp.broadcast_to(a.reshape(db, -1, 1), (db, a.size // db, page))

    def page_map(g, nd):
        return lambda b, s, pt: (pt[b, n_pages - 1 - (s * g_n + g)],) + (0,) * nd

    per_seq = lambda b, s, pt: (b, 0, 0)
    in_specs = [pl.BlockSpec((1, d_att, page), per_seq), pl.BlockSpec((1, n_heads, dh), per_seq),
                pl.BlockSpec((1, n_heads, dh), per_seq), pl.BlockSpec((1, d_att, page), per_seq),
                pl.BlockSpec((1, n_heads, page), per_seq)]
    in_specs += [pl.BlockSpec((1, n_heads, dh, page), page_map(g, 3)) for g in range(g_n)]
    in_specs += [pl.BlockSpec((1, n_heads, dh, page), page_map(g, 3)) for g in range(g_n)]
    in_specs += [pl.BlockSpec((1, n_heads, page), page_map(g, 2)) for g in range(g_n)]
    grid_spec = pltpu.PrefetchScalarGridSpec(
        num_scalar_prefetch=1, grid=(db, n_steps), in_specs=in_specs,
        out_specs=pl.BlockSpec((1, n_heads, dh), per_seq),
        scratch_shapes=[pltpu.VMEM((n_heads, page), F32), pltpu.VMEM((n_heads, page), F32),
                        pltpu.VMEM((d_att, page), F32), pltpu.VMEM((n_heads, page), F32)])
    return pl.pallas_call(
        functools.partial(_attn_sample_body, n_pages_step=g_n),
        grid_spec=grid_spec,
        out_shape=jax.ShapeDtypeStruct((db, n_heads, dh), F32),
        compiler_params=_params("arbitrary", "arbitrary"),
        name="attn_sample",
    )(page_table, lane_rep(q), q.reshape(db, n_heads, dh), k_new.reshape(db, n_heads, dh), lane_rep(v_new),
      lane_rep(lf_new), *([cache_kt] * g_n), *([cache_vt] * g_n), *([cache_lft] * g_n))


def _merge_and_route(x, pooled, att, gates, wp_ref, ps_ref, wup_ref, wua_ref, wo_ref, nf_ref, wr_ref, br_ref,
                     base_counts, *, precise, n_groups, n_per_group):
    tm, d = x.shape
    gw = pooled[0].shape[1]
    mixed = jnp.concatenate([_dot(pooled[g], wp_ref[g], precise) for g in range(len(pooled))], axis=-1)
    pool_out = mixed * ps_ref[...]
    y = gates[:, :d].astype(F32) * _dot(pool_out, wup_ref[...], precise) \
        + gates[:, d:].astype(F32) * _dot(att, wua_ref[...], precise)
    x2 = x + _dot(y, wo_ref[...], precise)
    h2 = _rmsnorm(x2, nf_ref[...])
    logits = _dot(h2, wr_ref[...], precise) + br_ref[...]
    lane = lax.broadcasted_iota(I32, logits.shape, 1)
    lanef = lane.astype(F32)
    neg = -jnp.inf
    is_g = lane < n_groups
    gmax = jnp.max(jnp.where(is_g, logits, neg), axis=-1, keepdims=True)
    gidx = jnp.min(jnp.where(is_g & (logits == gmax), lanef, float(LANES)), axis=-1, keepdims=True)
    gsum = jnp.sum(jnp.where(is_g, jnp.exp(logits - gmax), 0.0), axis=-1, keepdims=True)
    g_w = 1.0 / gsum
    n_exp = n_groups * n_per_group
    exp_id = lanef - float(n_groups)
    in_sel = (lane >= n_groups) & (lane < n_groups + n_exp) & (jnp.floor(exp_id / n_per_group) == gidx)
    v1 = jnp.max(jnp.where(in_sel, logits, neg), axis=-1, keepdims=True)
    i1 = jnp.min(jnp.where(in_sel & (logits == v1), lanef, float(LANES)), axis=-1, keepdims=True)
    in_sel2 = in_sel & (lanef != i1)
    v2 = jnp.max(jnp.where(in_sel2, logits, neg), axis=-1, keepdims=True)
    i2 = jnp.min(jnp.where(in_sel2 & (logits == v2), lanef, float(LANES)), axis=-1, keepdims=True)
    t = jnp.exp(v2 - v1)
    w1 = g_w * (1.0 / (1.0 + t))
    w2 = g_w * (t / (1.0 + t))
    e1 = i1 - float(n_groups)
    e2 = i2 - float(n_groups)
    hit1 = lanef == e1
    hit2 = lanef == e2
    onehot = (hit1 | hit2).astype(BF16)
    rr = lax.broadcasted_iota(I32, (tm, tm), 0)
    cc = lax.broadcasted_iota(I32, (tm, tm), 1)
    incl = jnp.dot((cc <= rr).astype(BF16), onehot, preferred_element_type=F32)
    seen = incl + base_counts - 1.0
    r1 = jnp.sum(jnp.where(hit1, seen, 0.0), axis=-1, keepdims=True)
    r2 = jnp.sum(jnp.where(hit2, seen, 0.0), axis=-1, keepdims=True)
    counts = base_counts + incl[tm - 1:tm, :]
    slab = jnp.zeros((tm, LANES), F32)
    for i, val in enumerate((e1, e2, r1, r2, w1, w2)):
        slab = jnp.where(lane == i, val, slab)
    return x2, h2, slab, counts


def _merge_prompt_body(x_ref, u_ref, halo_ref, att_ref, gate_ref, wp_ref, ps_ref, wup_ref, wua_ref, wo_ref,
                       nf_ref, wr_ref, br_ref, x2_ref, h2_ref, slab_ref, route_ref, counts_ref, ext_ref, cnt_ref,
                       *, seq_len, n_groups, n_per_group):
    i = pl.program_id(0)
    tm = x_ref.shape[0]
    gw = u_ref.shape[1] // len(POOL_WINDOWS)
    pos0 = (i * tm) % seq_len

    @pl.when(i == 0)
    def _():
        cnt_ref[...] = jnp.zeros_like(cnt_ref)

    u = u_ref[...]
    ext_ref[0:POOL_HALO, :] = jnp.where(pos0 == 0, 0.0, halo_ref[...])
    ext_ref[POOL_HALO:, :] = u
    pos = pos0 + lax.broadcasted_iota(I32, (tm, 1), 0)
    pooled = []
    for g, w in enumerate(POOL_WINDOWS):
        lo = g * gw
        wsum = ext_ref[pl.ds(POOL_HALO, tm), lo:lo + gw]
        for j in range(1, w):
            wsum = wsum + ext_ref[pl.ds(POOL_HALO - j, tm), lo:lo + gw]
        count = jnp.minimum(pos + 1, w).astype(F32)
        pooled.append(wsum / count - u[:, lo:lo + gw])
    x2, h2, slab, counts = _merge_and_route(
        x_ref[...], pooled, att_ref[...], gate_ref[...], wp_ref, ps_ref, wup_ref, wua_ref, wo_ref, nf_ref,
        wr_ref, br_ref, cnt_ref[...], precise=False, n_groups=n_groups, n_per_group=n_per_group)
    x2_ref[...] = x2
    h2_ref[...] = h2
    slab_ref[...] = slab
    route_ref[...] = slab.T[0:8, :]
    cnt_ref[...] = counts
    counts_ref[...] = counts


def _merge_prompt(x, u, att, gates, wp, ps, wup, wua, wo, nf, wr, br, *, tm, seq_len, n_groups, n_per_group):
    n, d = x.shape
    d_pool, d_att = u.shape[1], att.shape[1]
    row = lambda i: (i, 0)
    const = lambda i: (0, 0)
    const3 = lambda i: (0, 0, 0)
    halo = lambda i: (jnp.maximum(i * (tm // POOL_HALO) - 1, 0), 0)
    return pl.pallas_call(
        functools.partial(_merge_prompt_body, seq_len=seq_len, n_groups=n_groups, n_per_group=n_per_group),
        grid=(n // tm,),
        in_specs=[pl.BlockSpec((tm, d), row), pl.BlockSpec((tm, d_pool), row), pl.BlockSpec((POOL_HALO, d_pool), halo),
                  pl.BlockSpec((tm, d_att), row), pl.BlockSpec((tm, 2 * d), row),
                  pl.BlockSpec(wp.shape, const3), pl.BlockSpec((1, d_pool), const),
                  pl.BlockSpec(wup.shape, const), pl.BlockSpec(wua.shape, const), pl.BlockSpec(wo.shape, const),
                  pl.BlockSpec((1, d), const), pl.BlockSpec(wr.shape, const), pl.BlockSpec((1, LANES), const)],
        out_specs=[pl.BlockSpec((tm, d), row), pl.BlockSpec((tm, d), row), pl.BlockSpec((tm, LANES), row),
                   pl.BlockSpec((8, tm), lambda i: (0, i)), pl.BlockSpec((1, LANES), const)],
        out_shape=[jax.ShapeDtypeStruct((n, d), F32), jax.ShapeDtypeStruct((n, d), F32),
                   jax.ShapeDtypeStruct((n, LANES), F32), jax.ShapeDtypeStruct((8, n), F32),
                   jax.ShapeDtypeStruct((1, LANES), F32)],
        scratch_shapes=[pltpu.VMEM((tm + POOL_HALO, d_pool), F32), pltpu.VMEM((1, LANES), F32)],
        compiler_params=_params("arbitrary"),
        name="merge_prompt",
    )(x, u, u, att, gates, wp, ps, wup, wua, wo, nf, wr, br)


def _merge_sample_body(x_ref, u_ref, st_ref, att_ref, gate_ref, wp_ref, ps_ref, wup_ref, wua_ref, wo_ref,
                       nf_ref, wr_ref, br_ref, base_ref, x2_ref, h2_ref, slab_ref, counts_ref,
                       *, start_pos, n_groups, n_per_group):
    u = u_ref[...]
    gw = u.shape[1] // len(POOL_WINDOWS)
    n_state = st_ref.shape[0]
    pooled = []
    for g, w in enumerate(POOL_WINDOWS):
        lo = g * gw
        wsum = u[:, lo:lo + gw]
        for j in range(1, w):
            wsum = wsum + st_ref[n_state - j][:, lo:lo + gw]
        pooled.append(wsum / float(min(start_pos + 1, w)) - u[:, lo:lo + gw])
    x2, h2, slab, counts = _merge_and_route(
        x_ref[...], pooled, att_ref[...], gate_ref[...], wp_ref, ps_ref, wup_ref, wua_ref, wo_ref, nf_ref,
        wr_ref, br_ref, base_ref[...], precise=True, n_groups=n_groups, n_per_group=n_per_group)
    x2_ref[...] = x2
    h2_ref[...] = h2
    slab_ref[...] = slab
    counts_ref[...] = counts


def _merge_sample(x, u, state_t, att, gates, wp, ps, wup, wua, wo, nf, wr, br, base, *, start_pos, n_groups,
                  n_per_group):
    n, d = x.shape
    return pl.pallas_call(
        functools.partial(_merge_sample_body, start_pos=start_pos, n_groups=n_groups, n_per_group=n_per_group),
        out_shape=[jax.ShapeDtypeStruct((n, d), F32), jax.ShapeDtypeStruct((n, d), F32),
                   jax.ShapeDtypeStruct((n, LANES), F32), jax.ShapeDtypeStruct((1, LANES), F32)],
        compiler_params=_params(),
        name="merge_sample",
    )(x, u, state_t, att, gates, wp, ps, wup, wua, wo, nf, wr, br, base)


def _row_copy(src_ref, src_row, dst_ref, dst_row, sem):
    return pltpu.make_async_copy(src_ref.at[pl.ds(src_row, 1)], dst_ref.at[pl.ds(dst_row, 1)], sem)


def _moe_scatter_body(goff_ref, e1_ref, e2_ref, r1_ref, r2_ref, h_ref, xs_in_ref, xs_ref, sem):
    del xs_in_ref
    n = h_ref.shape[0]

    def start(i, carry):
        for j in range(ROW_DMA_UNROLL):
            t = i * ROW_DMA_UNROLL + j
            _row_copy(h_ref, t, xs_ref, goff_ref[e1_ref[t]] + r1_ref[t], sem).start()
            _row_copy(h_ref, t, xs_ref, goff_ref[e2_ref[t]] + r2_ref[t], sem).start()
        return carry

    lax.fori_loop(0, n // ROW_DMA_UNROLL, start, 0)
    all_rows = pltpu.make_async_copy(h_ref, xs_ref.at[pl.ds(0, n)], sem)
    all_rows.wait()
    all_rows.wait()


def _moe_scatter(goff, e1, e2, r1, r2, h, xs, *, ts):
    n, d = h.shape
    smem = lambda: pl.BlockSpec((ts,), lambda i: (i,), memory_space=pltpu.SMEM)
    return pl.pallas_call(
        _moe_scatter_body,
        grid=(n // ts,),
        in_specs=[pl.BlockSpec(memory_space=pltpu.SMEM), smem(), smem(), smem(), smem(),
                  pl.BlockSpec((ts, d), lambda i: (i, 0)), pl.BlockSpec(memory_space=pl.ANY)],
        out_specs=pl.BlockSpec(memory_space=pl.ANY),
        out_shape=jax.ShapeDtypeStruct(xs.shape, xs.dtype),
        scratch_shapes=[pltpu.SemaphoreType.DMA],
        input_output_aliases={6: 0},
        compiler_params=_params("arbitrary"),
        name="moe_scatter",
    )(goff, e1, e2, r1, r2, h, xs)


def _moe_mm_body(te_ref, tw_ref, xs_ref, wg_ref, wu_ref, wd_ref, ys_ref, wgb_ref, wub_ref, wdb_ref):
    i = pl.program_id(0)
    expert = te_ref[i]
    prev = te_ref[jnp.maximum(i - 1, 0)]

    @pl.when((expert >= 0) & ((i == 0) | (expert != prev)))
    def _():
        wgb_ref[...] = wg_ref[0].astype(BF16)
        wub_ref[...] = wu_ref[0].astype(BF16)
        wdb_ref[...] = wd_ref[0].astype(BF16)

    @pl.when(expert >= 0)
    def _():
        x = xs_ref[...].astype(BF16)
        a = jnp.dot(x, wgb_ref[...], preferred_element_type=F32)
        b = jnp.dot(x, wub_ref[...], preferred_element_type=F32)
        hdn = (a * _sigmoid(a)) * b
        ys_ref[...] = jnp.dot(hdn.astype(BF16), wdb_ref[...], preferred_element_type=F32)

    @pl.when(expert < 0)
    def _():
        ys_ref[...] = jnp.zeros_like(ys_ref)


def _moe_mm(tile_expert, tile_weight, xs, w_gate, w_up, w_down, *, tm):
    p, d = xs.shape
    n_exp, _, de = w_gate.shape
    wmap = lambda i, te, tw: (tw[i], 0, 0)
    grid_spec = pltpu.PrefetchScalarGridSpec(
        num_scalar_prefetch=2, grid=(p // tm,),
        in_specs=[pl.BlockSpec((tm, d), lambda i, te, tw: (i, 0)),
                  pl.BlockSpec((1, d, de), wmap), pl.BlockSpec((1, d, de), wmap), pl.BlockSpec((1, de, d), wmap)],
        out_specs=pl.BlockSpec((tm, d), lambda i, te, tw: (i, 0)),
        scratch_shapes=[pltpu.VMEM((d, de), BF16), pltpu.VMEM((d, de), BF16), pltpu.VMEM((de, d), BF16)])
    return pl.pallas_call(
        _moe_mm_body, grid_spec=grid_spec,
        out_shape=jax.ShapeDtypeStruct((p, d), F32),
        compiler_params=_params("arbitrary"),
        name="moe_mm",
    )(tile_expert, tile_weight, xs, w_gate, w_up, w_down)


def _moe_combine_body(goff_ref, e1_ref, e2_ref, r1_ref, r2_ref, x_ref, slab_ref, g_ref, ys_ref, o_ref,
                      ya_ref, yb_ref, sem, *, final_norm):
    n = x_ref.shape[0]

    def start(i, carry):
        for j in range(ROW_DMA_UNROLL):
            t = i * ROW_DMA_UNROLL + j
            _row_copy(ys_ref, goff_ref[e1_ref[t]] + r1_ref[t], ya_ref, t, sem).start()
            _row_copy(ys_ref, goff_ref[e2_ref[t]] + r2_ref[t], yb_ref, t, sem).start()
        return carry

    lax.fori_loop(0, n // ROW_DMA_UNROLL, start, 0)
    pltpu.make_async_copy(ys_ref.at[pl.ds(0, n)], ya_ref, sem).wait()
    pltpu.make_async_copy(ys_ref.at[pl.ds(0, n)], yb_ref, sem).wait()
    slab = slab_ref[...]
    out = x_ref[...] + (slab[:, 4:5] * ya_ref[...] + slab[:, 5:6] * yb_ref[...])
    if final_norm:
        out = _rmsnorm(out, g_ref[...])
    o_ref[...] = out


def _moe_combine(goff, e1, e2, r1, r2, x, slab, g, ys, *, ts, final_norm):
    n, d = x.shape
    smem = lambda: pl.BlockSpec((ts,), lambda i: (i,), memory_space=pltpu.SMEM)
    return pl.pallas_call(
        functools.partial(_moe_combine_body, final_norm=final_norm),
        grid=(n // ts,),
        in_specs=[pl.BlockSpec(memory_space=pltpu.SMEM), smem(), smem(), smem(), smem(),
                  pl.BlockSpec((ts, d), lambda i: (i, 0)), pl.BlockSpec((ts, LANES), lambda i: (i, 0)),
                  pl.BlockSpec((1, d), lambda i: (0, 0)), pl.BlockSpec(memory_space=pl.ANY)],
        out_specs=pl.BlockSpec((ts, d), lambda i: (i, 0)),
        out_shape=jax.ShapeDtypeStruct((n, d), F32),
        scratch_shapes=[pltpu.VMEM((ts, d), F32), pltpu.VMEM((ts, d), F32), pltpu.SemaphoreType.DMA],
        compiler_params=_params("arbitrary"),
        name="moe_combine",
    )(goff, e1, e2, r1, r2, x, slab, g, ys)


def kernel(x_prompt, x_sample, cache_k, cache_v, cache_logf, state_pool, page_table, norm_mix, w_in, b_forget,
           w_pool, pool_scale, w_up_pool, w_up_att, w_out, norm_ffn, w_router_group, b_router_group,
           w_router_expert, b_router_expert, w_gate, w_up, w_down, norm_final):
    depth = norm_mix.shape[0]
    assert depth == 1, "single trunk layer"
    b, t, d = x_prompt.shape
    db, dt, _ = x_sample.shape
    assert dt == 1, "one sample token per sequence"
    _, n_phys, page, n_heads, dh = cache_k.shape
    n_pages = page_table.shape[1]
    past = n_pages * page
    n_state, d_pool = state_pool.shape[2], state_pool.shape[3]
    d_att = n_heads * dh
    n_pool_groups = w_pool.shape[1]
    assert n_pool_groups == len(POOL_WINDOWS) and d_pool // n_pool_groups == LANES
    assert n_state == max(POOL_WINDOWS) - 1 and n_state < POOL_HALO
    n_groups, n_per_group = w_router_expert.shape[1], w_router_expert.shape[3]
    n_exp = n_groups * n_per_group
    assert n_groups + n_exp <= LANES and 2 * dh == LANES and n_heads % 2 == 0
    n = b * t
    q_scale = float(dh) ** -0.5
    tm = min(TOKEN_TILE, t)
    assert t % tm == 0 and t % ATTN_TILE == 0

    o_main = d_pool + 3 * d_att
    wi = w_in[0]
    wm_f, wf_f, wg_f = wi[:, :o_main], wi[:, o_main:o_main + n_heads], wi[:, o_main + n_heads:]
    wf_pad = jnp.pad(wf_f, ((0, 0), (0, LANES - n_heads)))
    wit = jnp.transpose(wi)
    wmt_f, wgt_f = wit[:o_main], wit[o_main + n_heads:]
    wft_pad = jnp.pad(wit[o_main:o_main + n_heads], ((0, LANES - n_heads), (0, 0)))
    bf_pad = jnp.pad(b_forget[0], (0, LANES - n_heads)).reshape(1, LANES)
    g_mix = norm_mix[0].reshape(1, d)
    g_ffn = norm_ffn[0].reshape(1, d)
    g_fin = norm_final.reshape(1, d)
    ps = pool_scale[0].reshape(1, d_pool)
    wr_f = jnp.concatenate([w_router_group[0], jnp.transpose(w_router_expert[0], (1, 0, 2)).reshape(d, n_exp)], axis=1)
    wr_pad = jnp.pad(wr_f, ((0, 0), (0, LANES - n_groups - n_exp)))
    br_pad = jnp.pad(jnp.concatenate([b_router_group[0], b_router_expert[0].reshape(n_exp)]),
                     (0, LANES - n_groups - n_exp)).reshape(1, LANES)
    bf = lambda a: a.astype(BF16)

    xp = x_prompt.reshape(n, d)
    u_p, q_p, kt_p, vt_p, kb_p, vb_p, lft_p, gate_p = _proj_prompt(
        xp, g_mix, bf(wm_f), bf(wf_pad), bf(wg_f), bf_pad, tm=tm, seq_len=t, d_pool=d_pool, d_att=d_att,
        n_heads=n_heads, q_scale=q_scale)
    c = _cumsum_lanes(lft_p.reshape(b * n_heads, t))
    nt = t // ATTN_TILE
    c_blk = jnp.transpose(c.reshape(b, n_heads // 2, 2, nt, ATTN_TILE), (0, 1, 3, 2, 4))
    att_p = _attn_prompt(q_p.reshape(b, t, d_att), kb_p.reshape(b, t, d_att), vb_p.reshape(b, t, d_att), c_blk,
                         tile=ATTN_TILE, dh=dh)
    x2_p, h2_p, slab_p, route_p, counts_p = _merge_prompt(
        xp, u_p, att_p.reshape(n, d_att), gate_p, bf(w_pool[0]), ps, bf(w_up_pool[0]), bf(w_up_att[0]),
        bf(w_out[0]), g_ffn, bf(wr_pad), br_pad, tm=tm, seq_len=t, n_groups=n_groups, n_per_group=n_per_group)

    xs = x_sample.reshape(db, d)
    z_s, lf_s, gate_s = _proj_sample(xs, g_mix, wmt_f, wft_pad, wgt_f, bf_pad, tn=512)
    u_s = z_s[:, :d_pool]
    q_s = z_s[:, d_pool:d_pool + d_att] * q_scale
    k_s = z_s[:, d_pool + d_att:d_pool + 2 * d_att]
    v_s = z_s[:, d_pool + 2 * d_att:]
    att_s = _attn_sample(page_table, q_s, k_s, v_s, lf_s[:, :n_heads],
                         jnp.transpose(cache_k[0], (0, 2, 3, 1)), jnp.transpose(cache_v[0], (0, 2, 3, 1)),
                         jnp.transpose(cache_logf[0], (0, 2, 1)))
    state_t = jnp.transpose(state_pool[0], (1, 0, 2))
    x2_s, h2_s, slab_s, counts = _merge_sample(
        xs, u_s, state_t, att_s.reshape(db, d_att), gate_s, w_pool[0], ps, w_up_pool[0], w_up_att[0], w_out[0],
        g_ffn, wr_pad, br_pad, counts_p, start_pos=past, n_groups=n_groups, n_per_group=n_per_group)

    tmm = MOE_ROW_TILE
    n_tok = n + db
    n_tiles = -(-(2 * n_tok + n_exp * (tmm - 1)) // tmm)
    cnt = counts[0, :n_exp].astype(I32)
    padded = ((cnt + tmm - 1) // tmm) * tmm
    ends = jnp.cumsum(padded)
    goff = (ends - padded).astype(I32)
    tile_e = jnp.sum((jnp.arange(n_tiles, dtype=I32)[:, None] * tmm >= ends[None, :]).astype(I32), axis=1)
    tile_expert = jnp.where(tile_e < n_exp, tile_e, -1).astype(I32)
    tile_weight = jnp.minimum(tile_e, n_exp - 1).astype(I32)
    fields_p = route_p[:4].astype(I32)
    fields_s = jnp.transpose(slab_s[:, :4]).astype(I32)
    xs_rows = jnp.zeros((n_tiles * tmm, d), F32)
    xs_rows = _moe_scatter(goff, *fields_p, h2_p, xs_rows, ts=tm)
    xs_rows = _moe_scatter(goff, *fields_s, h2_s, xs_rows, ts=db)
    ys_rows = _moe_mm(tile_expert, tile_weight, xs_rows, w_gate[0], w_up[0], w_down[0], tm=tmm)
    y_prompt = _moe_combine(goff, *fields_p, x2_p, slab_p, g_fin, ys_rows, ts=tm, final_norm=True)
    y_sample = _moe_combine(goff, *fields_s, x2_s, slab_s, g_fin, ys_rows, ts=db, final_norm=True)

    new_pool_p = u_p.reshape(b, t, d_pool)[:, t - n_state:, :]
    new_pool_s = jnp.concatenate([state_pool[0][:, 1:, :], u_s[:, None, :]], axis=1)
    to_heads = lambda a: jnp.transpose(a.reshape(b, n_heads, dh, t), (0, 3, 1, 2))[None]
    return (y_prompt.reshape(b, t, d), y_sample.reshape(db, 1, d),
            to_heads(kt_p), to_heads(vt_p), jnp.transpose(lft_p, (0, 2, 1))[None],
            new_pool_p[None],
            k_s.reshape(1, db, 1, n_heads, dh), v_s.reshape(1, db, 1, n_heads, dh),
            lf_s[:, :n_heads].reshape(1, db, 1, n_heads), new_pool_s[None])
```

```python
import functools

import jax
import jax.numpy as jnp
from jax import lax
from jax.experimental import pallas as pl
from jax.experimental.pallas import tpu as pltpu

F32 = jnp.float32
BF16 = jnp.bfloat16
I32 = jnp.int32
HIGHEST = lax.Precision.HIGHEST

RMS_EPS = 1e-6
POOL_WINDOWS = (2, 4, 8, 16)
POOL_HALO = 16
LANES = 128
VMEM_LIMIT_BYTES = 56 * 1024 * 1024

TOKEN_TILE = 512
ATTN_TILE = 512
MOE_ROW_TILE = 256
PAGES_PER_STEP = 8
ROW_DMA_UNROLL = 8


def _params(*sem):
    return pltpu.CompilerParams(dimension_semantics=sem, vmem_limit_bytes=VMEM_LIMIT_BYTES)


def _rmsnorm(x, g):
    return x * lax.rsqrt(jnp.mean(x * x, axis=-1, keepdims=True) + RMS_EPS) * g


def _log_sigmoid(x):
    return jnp.minimum(x, 0.0) - jnp.log1p(jnp.exp(-jnp.abs(x)))


def _sigmoid(x):
    return 1.0 / (1.0 + jnp.exp(-x))


def _dot(a, b, precise):
    if precise:
        return jnp.dot(a.astype(F32), b.astype(F32), precision=HIGHEST, preferred_element_type=F32)
    return jnp.dot(a.astype(BF16), b.astype(BF16), preferred_element_type=F32)


def _split3(x):
    hi = x.astype(BF16)
    r = x - hi.astype(F32)
    mid = r.astype(BF16)
    lo = (r - mid.astype(F32)).astype(BF16)
    return hi, mid, lo


def _dot_exact_rhs(x, w_bf16):
    hi, mid, lo = _split3(x)
    d = lambda a: jnp.dot(a, w_bf16, preferred_element_type=F32)
    return d(hi) + d(mid) + d(lo)


def _dot_exact_lhs(w_bf16, x):
    hi, mid, lo = _split3(x)
    d = lambda a: jnp.dot(w_bf16, a, preferred_element_type=F32)
    return d(hi) + d(mid) + d(lo)


def _proj_body(x_ref, g_ref, wm_ref, wf_ref, wg_ref, bf_ref,
               u_ref, q_ref, kt_ref, vt_ref, kb_ref, vb_ref, lft_ref, gate_ref, *, d_pool, d_att, n_heads, q_scale):
    h = _rmsnorm(x_ref[...], g_ref[...]).astype(BF16)
    z = jnp.dot(h, wm_ref[...], preferred_element_type=F32)
    o1, o2, o3 = d_pool, d_pool + d_att, d_pool + 2 * d_att
    u_ref[...] = z[:, :o1]
    q_ref[...] = (z[:, o1:o2] * q_scale).astype(BF16)
    k = z[:, o2:o3]
    v = z[:, o3:]
    kt_ref[0] = k.T
    vt_ref[0] = v.T
    kb_ref[...] = k.astype(BF16)
    vb_ref[...] = v.astype(BF16)
    lf = _log_sigmoid(jnp.dot(h, wf_ref[...], preferred_element_type=F32) + bf_ref[...])
    lft_ref[0] = lf.T[0:n_heads, :]
    gate_ref[...] = _sigmoid(jnp.dot(h, wg_ref[...], preferred_element_type=F32)).astype(BF16)


def _proj_prompt(x, g, wm, wf, wg, bfp, *, tm, seq_len, d_pool, d_att, n_heads, q_scale):
    n, d = x.shape
    b = n // seq_len
    tps = seq_len // tm
    row = lambda i: (i, 0)
    const = lambda i: (0, 0)
    tmin = lambda i: (i // tps, 0, i % tps)
    dg = wg.shape[1]
    out_shape = [
        jax.ShapeDtypeStruct((n, d_pool), F32), jax.ShapeDtypeStruct((n, d_att), BF16),
        jax.ShapeDtypeStruct((b, d_att, seq_len), F32), jax.ShapeDtypeStruct((b, d_att, seq_len), F32),
        jax.ShapeDtypeStruct((n, d_att), BF16), jax.ShapeDtypeStruct((n, d_att), BF16),
        jax.ShapeDtypeStruct((b, n_heads, seq_len), F32), jax.ShapeDtypeStruct((n, dg), BF16),
    ]
    return pl.pallas_call(
        functools.partial(_proj_body, d_pool=d_pool, d_att=d_att, n_heads=n_heads, q_scale=q_scale),
        grid=(n // tm,),
        in_specs=[pl.BlockSpec((tm, d), row), pl.BlockSpec((1, d), const),
                  pl.BlockSpec(wm.shape, const), pl.BlockSpec(wf.shape, const),
                  pl.BlockSpec(wg.shape, const), pl.BlockSpec((1, LANES), const)],
        out_specs=[pl.BlockSpec((tm, d_pool), row), pl.BlockSpec((tm, d_att), row),
                   pl.BlockSpec((1, d_att, tm), tmin), pl.BlockSpec((1, d_att, tm), tmin),
                   pl.BlockSpec((tm, d_att), row), pl.BlockSpec((tm, d_att), row),
                   pl.BlockSpec((1, n_heads, tm), tmin), pl.BlockSpec((tm, dg), row)],
        out_shape=out_shape,
        compiler_params=_params("arbitrary"),
        name="proj_prompt",
    )(x, g, wm, wf, wg, bfp)


def _dot_nt(a, bt, precise):
    dims = (((1,), (1,)), ((), ()))
    if precise:
        return lax.dot_general(a.astype(F32), bt.astype(F32), dims, precision=HIGHEST, preferred_element_type=F32)
    return lax.dot_general(a.astype(BF16), bt.astype(BF16), dims, preferred_element_type=F32)


def _proj_sample_body(x_ref, g_ref, wmt_ref, wft_ref, wgt_ref, bf_ref, z_ref, lf_ref, gate_ref):
    h = _rmsnorm(x_ref[...], g_ref[...])
    z_ref[...] = _dot_nt(h, wmt_ref[...], True)
    lf_ref[...] = _log_sigmoid(_dot_nt(h, wft_ref[...], True) + bf_ref[...])
    gate_ref[...] = _sigmoid(_dot_nt(h, wgt_ref[...], True))


def _proj_sample(x, g, wmt, wft, wgt, bfp, *, tn):
    n, d = x.shape
    dm, dg = wmt.shape[0], wgt.shape[0]
    assert dm == dg
    const = lambda j: (0, 0)
    chunk = lambda j: (j, 0)
    col = lambda j: (0, j)
    return pl.pallas_call(
        _proj_sample_body,
        grid=(dm // tn,),
        in_specs=[pl.BlockSpec((n, d), const), pl.BlockSpec((1, d), const),
                  pl.BlockSpec((tn, d), chunk), pl.BlockSpec(wft.shape, const),
                  pl.BlockSpec((tn, d), chunk), pl.BlockSpec((1, LANES), const)],
        out_specs=[pl.BlockSpec((n, tn), col), pl.BlockSpec((n, LANES), const), pl.BlockSpec((n, tn), col)],
        out_shape=[jax.ShapeDtypeStruct((n, dm), F32), jax.ShapeDtypeStruct((n, LANES), F32),
                   jax.ShapeDtypeStruct((n, dg), F32)],
        compiler_params=_params("arbitrary"),
        name="proj_sample",
    )(x, g, wmt, wft, wgt, bfp)


def _cumsum_body(x_ref, o_ref):
    c = x_ref[...]
    lane = lax.broadcasted_iota(I32, c.shape, 1)
    s = 1
    while s < c.shape[1]:
        c = c + jnp.where(lane >= s, pltpu.roll(c, s, 1), 0.0)
        s *= 2
    o_ref[...] = c


def _cumsum_lanes(x):
    return pl.pallas_call(_cumsum_body, out_shape=jax.ShapeDtypeStruct(x.shape, F32),
                          compiler_params=_params(), name="cumsum_logf")(x)


def _attn_body(q_ref, k_ref, v_ref, c_ref, o_ref, *, tile, dh):
    nt = q_ref.shape[1] // tile
    lane = lax.broadcasted_iota(I32, (tile, 2 * dh), 1)
    first = lane < dh
    row = lax.broadcasted_iota(I32, (tile, tile), 0)
    col = lax.broadcasted_iota(I32, (tile, tile), 1)
    causal = col <= row
    one = jnp.ones((tile, 2 * dh), BF16)
    kts, vhs = [], []
    for kj in range(nt):
        vt = v_ref[0, kj * tile:(kj + 1) * tile, :]
        kts.append(k_ref[0, kj * tile:(kj + 1) * tile, :])
        vhs.append((jnp.where(first, vt, one), jnp.where(first, one, vt)))
    for qi in range(nt):
        q = q_ref[0, qi * tile:(qi + 1) * tile, :]
        zero = jnp.zeros_like(q)
        q_heads = (jnp.where(first, q, zero), jnp.where(first, zero, q))
        res = []
        for h in range(2):
            m = jnp.full((tile, 1), -1e30, F32)
            acc = jnp.zeros((tile, 2 * dh), F32)
            for kj in range(qi + 1):
                s = lax.dot_general(q_heads[h], kts[kj], (((1,), (1,)), ((), ())), preferred_element_type=F32)
                s = s - c_ref[0, 0, kj][h:h + 1, :]
                if kj == qi:
                    s = jnp.where(causal, s, -jnp.inf)
                m_new = jnp.maximum(m, jnp.max(s, axis=-1, keepdims=True))
                alpha = jnp.exp(m - m_new)
                p = jnp.exp(s - m_new)
                acc = alpha * acc + jnp.dot(p.astype(BF16), vhs[kj][h], preferred_element_type=F32)
                m = m_new
            res.append(acc)
        a0, a1 = res
        out = jnp.where(first, a0 / a0[:, dh:dh + 1], a1 / a1[:, 0:1])
        o_ref[0, qi * tile:(qi + 1) * tile, :] = out.astype(o_ref.dtype)


def _attn_prompt(q, k, v, c, *, tile, dh):
    b, t, da = q.shape
    hp = da // (2 * dh)
    nt = t // tile
    pair = pl.BlockSpec((1, t, 2 * dh), lambda bi, hi: (bi, 0, hi))
    return pl.pallas_call(
        functools.partial(_attn_body, tile=tile, dh=dh),
        grid=(b, hp),
        in_specs=[pair, pair, pair, pl.BlockSpec((1, 1, nt, 2, tile), lambda bi, hi: (bi, hi, 0, 0, 0))],
        out_specs=pair,
        out_shape=jax.ShapeDtypeStruct((b, t, da), BF16),
        compiler_params=_params("arbitrary", "arbitrary"),
        name="attn_prompt",
    )(q, k, v, c)


def _attn_sample_body(pt_ref, qrep_ref, q_ref, kn_ref, vrep_ref, lfn_ref, *rest, n_pages_step):
    g_n = n_pages_step
    k_refs = rest[0:g_n]
    v_refs = rest[g_n:2 * g_n]
    lf_refs = rest[2 * g_n:3 * g_n]
    o_ref = rest[3 * g_n]
    m_ref, l_ref, acc_ref, srun_ref = rest[3 * g_n + 1:]
    step = pl.program_id(1)
    _, n_heads, dh, page = k_refs[0].shape
    d_att = n_heads * dh

    @pl.when(step == 0)
    def _():
        s_new = jnp.sum(q_ref[0] * kn_ref[0], axis=-1, keepdims=True)
        m_ref[...] = jnp.broadcast_to(s_new, m_ref.shape)
        l_ref[...] = jnp.ones_like(l_ref)
        lane = lax.broadcasted_iota(I32, (d_att, page), 1)
        acc_ref[...] = jnp.where(lane == 0, vrep_ref[0], 0.0)
        srun_ref[...] = lfn_ref[0]

    qrep = qrep_ref[0]
    r = lax.broadcasted_iota(I32, (page, page), 0)
    c = lax.broadcasted_iota(I32, (page, page), 1)
    later = (r > c).astype(BF16)
    ones = jnp.ones((page, page), BF16)
    lf_all = jnp.concatenate([lf_refs[g][0] for g in range(g_n)], axis=0)
    suffix = _dot_exact_rhs(lf_all, later)
    total = _dot_exact_rhs(lf_all, ones)
    s_run = srun_ref[...]
    m_prev = m_ref[...]
    m_new = m_prev
    scores = []
    for g in range(g_n):
        kq = k_refs[g][0].reshape(d_att, page) * qrep
        s = jnp.sum(kq.reshape(n_heads, dh, page), axis=1)
        sb = s + s_run + suffix[g * n_heads:(g + 1) * n_heads]
        s_run = s_run + total[g * n_heads:(g + 1) * n_heads]
        scores.append(sb)
        m_new = jnp.maximum(m_new, jnp.max(sb, axis=-1, keepdims=True))
    srun_ref[...] = s_run
    alpha = jnp.exp(m_prev - m_new)
    l = alpha * l_ref[...]
    acc = acc_ref[...].reshape(n_heads, dh, page) * alpha[:, None, :]
    for g in range(g_n):
        p = jnp.exp(scores[g] - m_new)
        l = l + jnp.sum(p, axis=-1, keepdims=True)
        acc = acc + v_refs[g][0] * p[:, None, :]
    m_ref[...] = m_new
    l_ref[...] = l
    acc_ref[...] = acc.reshape(d_att, page)

    @pl.when(step == pl.num_programs(1) - 1)
    def _():
        o_ref[0] = jnp.sum(acc / l[:, None, :], axis=-1)


def _attn_sample(page_table, q, k_new, v_new, lf_new, cache_kt, cache_vt, cache_lft):
    db, n_pages = page_table.shape
    n_phys, n_heads, dh, page = cache_kt.shape
    d_att = n_heads * dh
    g_n = PAGES_PER_STEP
    while n_pages % g_n:
        g_n //= 2
    n_steps = n_pages // g_n
    lane_rep = lambda a: jnp.broadcast_to(a.reshape(db, -1, 1), (db, a.size // db, page))

    def page_map(g, nd):
        return lambda b, s, pt: (pt[b, n_pages - 1 - (s * g_n + g)],) + (0,) * nd

    per_seq = lambda b, s, pt: (b, 0, 0)
    in_specs = [pl.BlockSpec((1, d_att, page), per_seq), pl.BlockSpec((1, n_heads, dh), per_seq),
                pl.BlockSpec((1, n_heads, dh), per_seq), pl.BlockSpec((1, d_att, page), per_seq),
                pl.BlockSpec((1, n_heads, page), per_seq)]
    in_specs += [pl.BlockSpec((1, n_heads, dh, page), page_map(g, 3)) for g in range(g_n)]
    in_specs += [pl.BlockSpec((1, n_heads, dh, page), page_map(g, 3)) for g in range(g_n)]
    in_specs += [pl.BlockSpec((1, n_heads, page), page_map(g, 2)) for g in range(g_n)]
    grid_spec = pltpu.PrefetchScalarGridSpec(
        num_scalar_prefetch=1, grid=(db, n_steps), in_specs=in_specs,
        out_specs=pl.BlockSpec((1, n_heads, dh), per_seq),
        scratch_shapes=[pltpu.VMEM((n_heads, page), F32), pltpu.VMEM((n_heads, page), F32),
                        pltpu.VMEM((d_att, page), F32), pltpu.VMEM((n_heads, page), F32)])
    return pl.pallas_call(
        functools.partial(_attn_sample_body, n_pages_step=g_n),
        grid_spec=grid_spec,
        out_shape=jax.ShapeDtypeStruct((db, n_heads, dh), F32),
        compiler_params=_params("arbitrary", "arbitrary"),
        name="attn_sample",
    )(page_table, lane_rep(q), q.reshape(db, n_heads, dh), k_new.reshape(db, n_heads, dh), lane_rep(v_new),
      lane_rep(lf_new), *([cache_kt] * g_n), *([cache_vt] * g_n), *([cache_lft] * g_n))


def _merge_and_route(x, pooled, att, gates, wp_ref, ps_ref, wup_ref, wua_ref, wo_ref, nf_ref, wr_ref, br_ref,
                     base_counts, *, precise, n_groups, n_per_group):
    tm, d = x.shape
    gw = pooled[0].shape[1]
    mixed = jnp.concatenate([_dot(pooled[g], wp_ref[g], precise) for g in range(len(pooled))], axis=-1)
    pool_out = mixed * ps_ref[...]
    y = gates[:, :d].astype(F32) * _dot(pool_out, wup_ref[...], precise) \
        + gates[:, d:].astype(F32) * _dot(att, wua_ref[...], precise)
    x2 = x + _dot(y, wo_ref[...], precise)
    h2 = _rmsnorm(x2, nf_ref[...])
    logits = _dot(h2, wr_ref[...], precise) + br_ref[...]
    lane = lax.broadcasted_iota(I32, logits.shape, 1)
    lanef = lane.astype(F32)
    neg = -jnp.inf
    is_g = lane < n_groups
    gmax = jnp.max(jnp.where(is_g, logits, neg), axis=-1, keepdims=True)
    gidx = jnp.min(jnp.where(is_g & (logits == gmax), lanef, float(LANES)), axis=-1, keepdims=True)
    gsum = jnp.sum(jnp.where(is_g, jnp.exp(logits - gmax), 0.0), axis=-1, keepdims=True)
    g_w = 1.0 / gsum
    n_exp = n_groups * n_per_group
    exp_id = lanef - float(n_groups)
    in_sel = (lane >= n_groups) & (lane < n_groups + n_exp) & (jnp.floor(exp_id / n_per_group) == gidx)
    v1 = jnp.max(jnp.where(in_sel, logits, neg), axis=-1, keepdims=True)
    i1 = jnp.min(jnp.where(in_sel & (logits == v1), lanef, float(LANES)), axis=-1, keepdims=True)
    in_sel2 = in_sel & (lanef != i1)
    v2 = jnp.max(jnp.where(in_sel2, logits, neg), axis=-1, keepdims=True)
    i2 = jnp.min(jnp.where(in_sel2 & (logits == v2), lanef, float(LANES)), axis=-1, keepdims=True)
    t = jnp.exp(v2 - v1)
    w1 = g_w * (1.0 / (1.0 + t))
    w2 = g_w * (t / (1.0 + t))
    e1 = i1 - float(n_groups)
    e2 = i2 - float(n_groups)
    hit1 = lanef == e1
    hit2 = lanef == e2
    onehot = (hit1 | hit2).astype(BF16)
    rr = lax.broadcasted_iota(I32, (tm, tm), 0)
    cc = lax.broadcasted_iota(I32, (tm, tm), 1)
    incl = jnp.dot((cc <= rr).astype(BF16), onehot, preferred_element_type=F32)
    seen = incl + base_counts - 1.0
    r1 = jnp.sum(jnp.where(hit1, seen, 0.0), axis=-1, keepdims=True)
    r2 = jnp.sum(jnp.where(hit2, seen, 0.0), axis=-1, keepdims=True)
    counts = base_counts + incl[tm - 1:tm, :]
    slab = jnp.zeros((tm, LANES), F32)
    for i, val in enumerate((e1, e2, r1, r2, w1, w2)):
        slab = jnp.where(lane == i, val, slab)
    return x2, h2, slab, counts


def _merge_prompt_body(x_ref, u_ref, halo_ref, att_ref, gate_ref, wp_ref, ps_ref, wup_ref, wua_ref, wo_ref,
                       nf_ref, wr_ref, br_ref, x2_ref, h2_ref, slab_ref, route_ref, counts_ref, ext_ref, cnt_ref,
                       *, seq_len, n_groups, n_per_group):
    i = pl.program_id(0)
    tm = x_ref.shape[0]
    gw = u_ref.shape[1] // len(POOL_WINDOWS)
    pos0 = (i * tm) % seq_len

    @pl.when(i == 0)
    def _():
        cnt_ref[...] = jnp.zeros_like(cnt_ref)

    u = u_ref[...]
    ext_ref[0:POOL_HALO, :] = jnp.where(pos0 == 0, 0.0, halo_ref[...])
    ext_ref[POOL_HALO:, :] = u
    pos = pos0 + lax.broadcasted_iota(I32, (tm, 1), 0)
    pooled = []
    for g, w in enumerate(POOL_WINDOWS):
        lo = g * gw
        wsum = ext_ref[pl.ds(POOL_HALO, tm), lo:lo + gw]
        for j in range(1, w):
            wsum = wsum + ext_ref[pl.ds(POOL_HALO - j, tm), lo:lo + gw]
        count = jnp.minimum(pos + 1, w).astype(F32)
        pooled.append(wsum / count - u[:, lo:lo + gw])
    x2, h2, slab, counts = _merge_and_route(
        x_ref[...], pooled, att_ref[...], gate_ref[...], wp_ref, ps_ref, wup_ref, wua_ref, wo_ref, nf_ref,
        wr_ref, br_ref, cnt_ref[...], precise=False, n_groups=n_groups, n_per_group=n_per_group)
    x2_ref[...] = x2
    h2_ref[...] = h2
    slab_ref[...] = slab
    route_ref[...] = slab.T[0:8, :]
    cnt_ref[...] = counts
    counts_ref[...] = counts


def _merge_prompt(x, u, att, gates, wp, ps, wup, wua, wo, nf, wr, br, *, tm, seq_len, n_groups, n_per_group):
    n, d = x.shape
    d_pool, d_att = u.shape[1], att.shape[1]
    row = lambda i: (i, 0)
    const = lambda i: (0, 0)
    const3 = lambda i: (0, 0, 0)
    halo = lambda i: (jnp.maximum(i * (tm // POOL_HALO) - 1, 0), 0)
    return pl.pallas_call(
        functools.partial(_merge_prompt_body, seq_len=seq_len, n_groups=n_groups, n_per_group=n_per_group),
        grid=(n // tm,),
        in_specs=[pl.BlockSpec((tm, d), row), pl.BlockSpec((tm, d_pool), row), pl.BlockSpec((POOL_HALO, d_pool), halo),
                  pl.BlockSpec((tm, d_att), row), pl.BlockSpec((tm, 2 * d), row),
                  pl.BlockSpec(wp.shape, const3), pl.BlockSpec((1, d_pool), const),
                  pl.BlockSpec(wup.shape, const), pl.BlockSpec(wua.shape, const), pl.BlockSpec(wo.shape, const),
                  pl.BlockSpec((1, d), const), pl.BlockSpec(wr.shape, const), pl.BlockSpec((1, LANES), const)],
        out_specs=[pl.BlockSpec((tm, d), row), pl.BlockSpec((tm, d), row), pl.BlockSpec((tm, LANES), row),
                   pl.BlockSpec((8, tm), lambda i: (0, i)), pl.BlockSpec((1, LANES), const)],
        out_shape=[jax.ShapeDtypeStruct((n, d), F32), jax.ShapeDtypeStruct((n, d), F32),
                   jax.ShapeDtypeStruct((n, LANES), F32), jax.ShapeDtypeStruct((8, n), F32),
                   jax.ShapeDtypeStruct((1, LANES), F32)],
        scratch_shapes=[pltpu.VMEM((tm + POOL_HALO, d_pool), F32), pltpu.VMEM((1, LANES), F32)],
        compiler_params=_params("arbitrary"),
        name="merge_prompt",
    )(x, u, u, att, gates, wp, ps, wup, wua, wo, nf, wr, br)


def _merge_sample_body(x_ref, u_ref, st_ref, att_ref, gate_ref, wp_ref, ps_ref, wup_ref, wua_ref, wo_ref,
                       nf_ref, wr_ref, br_ref, base_ref, x2_ref, h2_ref, slab_ref, counts_ref,
                       *, start_pos, n_groups, n_per_group):
    u = u_ref[...]
    gw = u.shape[1] // len(POOL_WINDOWS)
    n_state = st_ref.shape[0]
    pooled = []
    for g, w in enumerate(POOL_WINDOWS):
        lo = g * gw
        wsum = u[:, lo:lo + gw]
        for j in range(1, w):
            wsum = wsum + st_ref[n_state - j][:, lo:lo + gw]
        pooled.append(wsum / float(min(start_pos + 1, w)) - u[:, lo:lo + gw])
    x2, h2, slab, counts = _merge_and_route(
        x_ref[...], pooled, att_ref[...], gate_ref[...], wp_ref, ps_ref, wup_ref, wua_ref, wo_ref, nf_ref,
        wr_ref, br_ref, base_ref[...], precise=True, n_groups=n_groups, n_per_group=n_per_group)
    x2_ref[...] = x2
    h2_ref[...] = h2
    slab_ref[...] = slab
    counts_ref[...] = counts


def _merge_sample(x, u, state_t, att, gates, wp, ps, wup, wua, wo, nf, wr, br, base, *, start_pos, n_groups,
                  n_per_group):
    n, d = x.shape
    return pl.pallas_call(
        functools.partial(_merge_sample_body, start_pos=start_pos, n_groups=n_groups, n_per_group=n_per_group),
        out_shape=[jax.ShapeDtypeStruct((n, d), F32), jax.ShapeDtypeStruct((n, d), F32),
                   jax.ShapeDtypeStruct((n, LANES), F32), jax.ShapeDtypeStruct((1, LANES), F32)],
        compiler_params=_params(),
        name="merge_sample",
    )(x, u, state_t, att, gates, wp, ps, wup, wua, wo, nf, wr, br, base)


def _row_copy(src_ref, src_row, dst_ref, dst_row, sem):
    return pltpu.make_async_copy(src_ref.at[pl.ds(src_row, 1)], dst_ref.at[pl.ds(dst_row, 1)], sem)


def _moe_scatter_body(goff_ref, e1_ref, e2_ref, r1_ref, r2_ref, h_ref, xs_in_ref, xs_ref, sem):
    del xs_in_ref
    n = h_ref.shape[0]

    def start(i, carry):
        for j in range(ROW_DMA_UNROLL):
            t = i * ROW_DMA_UNROLL + j
            _row_copy(h_ref, t, xs_ref, goff_ref[e1_ref[t]] + r1_ref[t], sem).start()
            _row_copy(h_ref, t, xs_ref, goff_ref[e2_ref[t]] + r2_ref[t], sem).start()
        return carry

    lax.fori_loop(0, n // ROW_DMA_UNROLL, start, 0)
    all_rows = pltpu.make_async_copy(h_ref, xs_ref.at[pl.ds(0, n)], sem)
    all_rows.wait()
    all_rows.wait()


def _moe_scatter(goff, e1, e2, r1, r2, h, xs, *, ts):
    n, d = h.shape
    smem = lambda: pl.BlockSpec((ts,), lambda i: (i,), memory_space=pltpu.SMEM)
    return pl.pallas_call(
        _moe_scatter_body,
        grid=(n // ts,),
        in_specs=[pl.BlockSpec(memory_space=pltpu.SMEM), smem(), smem(), smem(), smem(),
                  pl.BlockSpec((ts, d), lambda i: (i, 0)), pl.BlockSpec(memory_space=pl.ANY)],
        out_specs=pl.BlockSpec(memory_space=pl.ANY),
        out_shape=jax.ShapeDtypeStruct(xs.shape, xs.dtype),
        scratch_shapes=[pltpu.SemaphoreType.DMA],
        input_output_aliases={6: 0},
        compiler_params=_params("arbitrary"),
        name="moe_scatter",
    )(goff, e1, e2, r1, r2, h, xs)


def _moe_mm_body(te_ref, tw_ref, xs_ref, wg_ref, wu_ref, wd_ref, ys_ref, wgb_ref, wub_ref, wdb_ref):
    i = pl.program_id(0)
    expert = te_ref[i]
    prev = te_ref[jnp.maximum(i - 1, 0)]

    @pl.when((expert >= 0) & ((i == 0) | (expert != prev)))
    def _():
        wgb_ref[...] = wg_ref[0].astype(BF16)
        wub_ref[...] = wu_ref[0].astype(BF16)
        wdb_ref[...] = wd_ref[0].astype(BF16)

    @pl.when(expert >= 0)
    def _():
        x = xs_ref[...].astype(BF16)
        a = jnp.dot(x, wgb_ref[...], preferred_element_type=F32)
        b = jnp.dot(x, wub_ref[...], preferred_element_type=F32)
        hdn = (a * _sigmoid(a)) * b
        ys_ref[...] = jnp.dot(hdn.astype(BF16), wdb_ref[...], preferred_element_type=F32)

    @pl.when(expert < 0)
    def _():
        ys_ref[...] = jnp.zeros_like(ys_ref)


def _moe_mm(tile_expert, tile_weight, xs, w_gate, w_up, w_down, *, tm):
    p, d = xs.shape
    n_exp, _, de = w_gate.shape
    wmap = lambda i, te, tw: (tw[i], 0, 0)
    grid_spec = pltpu.PrefetchScalarGridSpec(
        num_scalar_prefetch=2, grid=(p // tm,),
        in_specs=[pl.BlockSpec((tm, d), lambda i, te, tw: (i, 0)),
                  pl.BlockSpec((1, d, de), wmap), pl.BlockSpec((1, d, de), wmap), pl.BlockSpec((1, de, d), wmap)],
        out_specs=pl.BlockSpec((tm, d), lambda i, te, tw: (i, 0)),
        scratch_shapes=[pltpu.VMEM((d, de), BF16), pltpu.VMEM((d, de), BF16), pltpu.VMEM((de, d), BF16)])
    return pl.pallas_call(
        _moe_mm_body, grid_spec=grid_spec,
        out_shape=jax.ShapeDtypeStruct((p, d), F32),
        compiler_params=_params("arbitrary"),
        name="moe_mm",
    )(tile_expert, tile_weight, xs, w_gate, w_up, w_down)


def _moe_combine_body(goff_ref, e1_ref, e2_ref, r1_ref, r2_ref, x_ref, slab_ref, g_ref, ys_ref, o_ref,
                      ya_ref, yb_ref, sem, *, final_norm):
    n = x_ref.shape[0]

    def start(i, carry):
        for j in range(ROW_DMA_UNROLL):
            t = i * ROW_DMA_UNROLL + j
            _row_copy(ys_ref, goff_ref[e1_ref[t]] + r1_ref[t], ya_ref, t, sem).start()
            _row_copy(ys_ref, goff_ref[e2_ref[t]] + r2_ref[t], yb_ref, t, sem).start()
        return carry

    lax.fori_loop(0, n // ROW_DMA_UNROLL, start, 0)
    pltpu.make_async_copy(ys_ref.at[pl.ds(0, n)], ya_ref, sem).wait()
    pltpu.make_async_copy(ys_ref.at[pl.ds(0, n)], yb_ref, sem).wait()
    slab = slab_ref[...]
    out = x_ref[...] + (slab[:, 4:5] * ya_ref[...] + slab[:, 5:6] * yb_ref[...])
    if final_norm:
        out = _rmsnorm(out, g_ref[...])
    o_ref[...] = out


def _moe_combine(goff, e1, e2, r1, r2, x, slab, g, ys, *, ts, final_norm):
    n, d = x.shape
    smem = lambda: pl.BlockSpec((ts,), lambda i: (i,), memory_space=pltpu.SMEM)
    return pl.pallas_call(
        functools.partial(_moe_combine_body, final_norm=final_norm),
        grid=(n // ts,),
        in_specs=[pl.BlockSpec(memory_space=pltpu.SMEM), smem(), smem(), smem(), smem(),
                  pl.BlockSpec((ts, d), lambda i: (i, 0)), pl.BlockSpec((ts, LANES), lambda i: (i, 0)),
                  pl.BlockSpec((1, d), lambda i: (0, 0)), pl.BlockSpec(memory_space=pl.ANY)],
        out_specs=pl.BlockSpec((ts, d), lambda i: (i, 0)),
        out_shape=jax.ShapeDtypeStruct((n, d), F32),
        scratch_shapes=[pltpu.VMEM((ts, d), F32), pltpu.VMEM((ts, d), F32), pltpu.SemaphoreType.DMA],
        compiler_params=_params("arbitrary"),
        name="moe_combine",
    )(goff, e1, e2, r1, r2, x, slab, g, ys)


def kernel(x_prompt, x_sample, cache_k, cache_v, cache_logf, state_pool, page_table, norm_mix, w_in, b_forget,
           w_pool, pool_scale, w_up_pool, w_up_att, w_out, norm_ffn, w_router_group, b_router_group,
           w_router_expert, b_router_expert, w_gate, w_up, w_down, norm_final):
    depth = norm_mix.shape[0]
    assert depth == 1, "single trunk layer"
    b, t, d = x_prompt.shape
    db, dt, _ = x_sample.shape
    assert dt == 1, "one sample token per sequence"
    _, n_phys, page, n_heads, dh = cache_k.shape
    n_pages = page_table.shape[1]
    past = n_pages * page
    n_state, d_pool = state_pool.shape[2], state_pool.shape[3]
    d_att = n_heads * dh
    n_pool_groups = w_pool.shape[1]
    assert n_pool_groups == len(POOL_WINDOWS) and d_pool // n_pool_groups == LANES
    assert n_state == max(POOL_WINDOWS) - 1 and n_state < POOL_HALO
    n_groups, n_per_group = w_router_expert.shape[1], w_router_expert.shape[3]
    n_exp = n_groups * n_per_group
    assert n_groups + n_exp <= LANES and 2 * dh == LANES and n_heads % 2 == 0
    n = b * t
    q_scale = float(dh) ** -0.5
    tm = min(TOKEN_TILE, t)
    assert t % tm == 0 and t % ATTN_TILE == 0

    o_main = d_pool + 3 * d_att
    wi = w_in[0]
    wm_f, wf_f, wg_f = wi[:, :o_main], wi[:, o_main:o_main + n_heads], wi[:, o_main + n_heads:]
    wf_pad = jnp.pad(wf_f, ((0, 0), (0, LANES - n_heads)))
    wit = jnp.transpose(wi)
    wmt_f, wgt_f = wit[:o_main], wit[o_main + n_heads:]
    wft_pad = jnp.pad(wit[o_main:o_main + n_heads], ((0, LANES - n_heads), (0, 0)))
    bf_pad = jnp.pad(b_forget[0], (0, LANES - n_heads)).reshape(1, LANES)
    g_mix = norm_mix[0].reshape(1, d)
    g_ffn = norm_ffn[0].reshape(1, d)
    g_fin = norm_final.reshape(1, d)
    ps = pool_scale[0].reshape(1, d_pool)
    wr_f = jnp.concatenate([w_router_group[0], jnp.transpose(w_router_expert[0], (1, 0, 2)).reshape(d, n_exp)], axis=1)
    wr_pad = jnp.pad(wr_f, ((0, 0), (0, LANES - n_groups - n_exp)))
    br_pad = jnp.pad(jnp.concatenate([b_router_group[0], b_router_expert[0].reshape(n_exp)]),
                     (0, LANES - n_groups - n_exp)).reshape(1, LANES)
    bf = lambda a: a.astype(BF16)

    xp = x_prompt.reshape(n, d)
    u_p, q_p, kt_p, vt_p, kb_p, vb_p, lft_p, gate_p = _proj_prompt(
        xp, g_mix, bf(wm_f), bf(wf_pad), bf(wg_f), bf_pad, tm=tm, seq_len=t, d_pool=d_pool, d_att=d_att,
        n_heads=n_heads, q_scale=q_scale)
    c = _cumsum_lanes(lft_p.reshape(b * n_heads, t))
    nt = t // ATTN_TILE
    c_blk = jnp.transpose(c.reshape(b, n_heads // 2, 2, nt, ATTN_TILE), (0, 1, 3, 2, 4))
    att_p = _attn_prompt(q_p.reshape(b, t, d_att), kb_p.reshape(b, t, d_att), vb_p.reshape(b, t, d_att), c_blk,
                         tile=ATTN_TILE, dh=dh)
    x2_p, h2_p, slab_p, route_p, counts_p = _merge_prompt(
        xp, u_p, att_p.reshape(n, d_att), gate_p, bf(w_pool[0]), ps, bf(w_up_pool[0]), bf(w_up_att[0]),
        bf(w_out[0]), g_ffn, bf(wr_pad), br_pad, tm=tm, seq_len=t, n_groups=n_groups, n_per_group=n_per_group)

    xs = x_sample.reshape(db, d)
    z_s, lf_s, gate_s = _proj_sample(xs, g_mix, wmt_f, wft_pad, wgt_f, bf_pad, tn=512)
    u_s = z_s[:, :d_pool]
    q_s = z_s[:, d_pool:d_pool + d_att] * q_scale
    k_s = z_s[:, d_pool + d_att:d_pool + 2 * d_att]
    v_s = z_s[:, d_pool + 2 * d_att:]
    att_s = _attn_sample(page_table, q_s, k_s, v_s, lf_s[:, :n_heads],
                         jnp.transpose(cache_k[0], (0, 2, 3, 1)), jnp.transpose(cache_v[0], (0, 2, 3, 1)),
                         jnp.transpose(cache_logf[0], (0, 2, 1)))
    state_t = jnp.transpose(state_pool[0], (1, 0, 2))
    x2_s, h2_s, slab_s, counts = _merge_sample(
        xs, u_s, state_t, att_s.reshape(db, d_att), gate_s, w_pool[0], ps, w_up_pool[0], w_up_att[0], w_out[0],
        g_ffn, wr_pad, br_pad, counts_p, start_pos=past, n_groups=n_groups, n_per_group=n_per_group)

    tmm = MOE_ROW_TILE
    n_tok = n + db
    n_tiles = -(-(2 * n_tok + n_exp * (tmm - 1)) // tmm)
    cnt = counts[0, :n_exp].astype(I32)
    padded = ((cnt + tmm - 1) // tmm) * tmm
    ends = jnp.cumsum(padded)
    goff = (ends - padded).astype(I32)
    tile_e = jnp.sum((jnp.arange(n_tiles, dtype=I32)[:, None] * tmm >= ends[None, :]).astype(I32), axis=1)
    tile_expert = jnp.where(tile_e < n_exp, tile_e, -1).astype(I32)
    tile_weight = jnp.minimum(tile_e, n_exp - 1).astype(I32)
    fields_p = route_p[:4].astype(I32)
    fields_s = jnp.transpose(slab_s[:, :4]).astype(I32)
    xs_rows = jnp.zeros((n_tiles * tmm, d), F32)
    xs_rows = _moe_scatter(goff, *fields_p, h2_p, xs_rows, ts=tm)
    xs_rows = _moe_scatter(goff, *fields_s, h2_s, xs_rows, ts=db)
    ys_rows = _moe_mm(tile_expert, tile_weight, xs_rows, w_gate[0], w_up[0], w_down[0], tm=tmm)
    y_prompt = _moe_combine(goff, *fields_p, x2_p, slab_p, g_fin, ys_rows, ts=tm, final_norm=True)
    y_sample = _moe_combine(goff, *fields_s, x2_s, slab_s, g_fin, ys_rows, ts=db, final_norm=True)

    new_pool_p = u_p.reshape(b, t, d_pool)[:, t - n_state:, :]
    new_pool_s = jnp.concatenate([state_pool[0][:, 1:, :], u_s[:, None, :]], axis=1)
    to_heads = lambda a: jnp.transpose(a.reshape(b, n_heads, dh, t), (0, 3, 1, 2))[None]
    return (y_prompt.reshape(b, t, d), y_sample.reshape(db, 1, d),
            to_heads(kt_p), to_heads(vt_p), jnp.transpose(lft_p, (0, 2, 1))[None],
            new_pool_p[None],
            k_s.reshape(1, db, 1, n_heads, dh), v_s.reshape(1, db, 1, n_heads, dh),
            lf_s[:, :n_heads].reshape(1, db, 1, n_heads), new_pool_s[None])
```

```python
import functools

import jax
import jax.numpy as jnp
from jax import lax
from jax.experimental import pallas as pl
from jax.experimental.pallas import tpu as pltpu

F32 = jnp.float32
BF16 = jnp.bfloat16
I32 = jnp.int32
HIGHEST = lax.Precision.HIGHEST

RMS_EPS = 1e-6
POOL_WINDOWS = (2, 4, 8, 16)
POOL_HALO = 16
LANES = 128
VMEM_LIMIT_BYTES = 56 * 1024 * 1024

TOKEN_TILE = 512
ATTN_TILE = 512
MOE_ROW_TILE = 256
PAGES_PER_STEP = 16
ROW_DMA_UNROLL = 8


def _params(*sem):
    return pltpu.CompilerParams(dimension_semantics=sem, vmem_limit_bytes=VMEM_LIMIT_BYTES)


def _rmsnorm(x, g):
    return x * lax.rsqrt(jnp.mean(x * x, axis=-1, keepdims=True) + RMS_EPS) * g


def _log_sigmoid(x):
    return jnp.minimum(x, 0.0) - jnp.log1p(jnp.exp(-jnp.abs(x)))


def _sigmoid(x):
    return 1.0 / (1.0 + jnp.exp(-x))


def _dot(a, b, precise):
    if precise:
        return jnp.dot(a.astype(F32), b.astype(F32), precision=HIGHEST, preferred_element_type=F32)
    return jnp.dot(a.astype(BF16), b.astype(BF16), preferred_element_type=F32)


def _split3(x):
    hi = x.astype(BF16)
    r = x - hi.astype(F32)
    mid = r.astype(BF16)
    lo = (r - mid.astype(F32)).astype(BF16)
    return hi, mid, lo


def _dot_exact_rhs(x, w_bf16):
    hi, mid, lo = _split3(x)
    d = lambda a: jnp.dot(a, w_bf16, preferred_element_type=F32)
    return d(hi) + d(mid) + d(lo)


def _dot_exact_lhs(w_bf16, x):
    hi, mid, lo = _split3(x)
    d = lambda a: jnp.dot(w_bf16, a, preferred_element_type=F32)
    return d(hi) + d(mid) + d(lo)


def _proj_body(x_ref, g_ref, wm_ref, wf_ref, wg_ref, bf_ref,
               u_ref, q_ref, kt_ref, vt_ref, kb_ref, vb_ref, lft_ref, gate_ref, *, d_pool, d_att, n_heads, q_scale):
    h = _rmsnorm(x_ref[...], g_ref[...]).astype(BF16)
    z = jnp.dot(h, wm_ref[...], preferred_element_type=F32)
    o1, o2, o3 = d_pool, d_pool + d_att, d_pool + 2 * d_att
    u_ref[...] = z[:, :o1]
    q_ref[...] = (z[:, o1:o2] * q_scale).astype(BF16)
    k = z[:, o2:o3]
    v = z[:, o3:]
    kt_ref[0] = k.T
    vt_ref[0] = v.T
    kb_ref[...] = k.astype(BF16)
    vb_ref[...] = v.astype(BF16)
    lf = _log_sigmoid(jnp.dot(h, wf_ref[...], preferred_element_type=F32) + bf_ref[...])
    lft_ref[0] = lf.T[0:n_heads, :]
    gate_ref[...] = _sigmoid(jnp.dot(h, wg_ref[...], preferred_element_type=F32)).astype(BF16)


def _proj_prompt(x, g, wm, wf, wg, bfp, *, tm, seq_len, d_pool, d_att, n_heads, q_scale):
    n, d = x.shape
    b = n // seq_len
    tps = seq_len // tm
    row = lambda i: (i, 0)
    const = lambda i: (0, 0)
    tmin = lambda i: (i // tps, 0, i % tps)
    dg = wg.shape[1]
    out_shape = [
        jax.ShapeDtypeStruct((n, d_pool), F32), jax.ShapeDtypeStruct((n, d_att), BF16),
        jax.ShapeDtypeStruct((b, d_att, seq_len), F32), jax.ShapeDtypeStruct((b, d_att, seq_len), F32),
        jax.ShapeDtypeStruct((n, d_att), BF16), jax.ShapeDtypeStruct((n, d_att), BF16),
        jax.ShapeDtypeStruct((b, n_heads, seq_len), F32), jax.ShapeDtypeStruct((n, dg), BF16),
    ]
    return pl.pallas_call(
        functools.partial(_proj_body, d_pool=d_pool, d_att=d_att, n_heads=n_heads, q_scale=q_scale),
        grid=(n // tm,),
        in_specs=[pl.BlockSpec((tm, d), row), pl.BlockSpec((1, d), const),
                  pl.BlockSpec(wm.shape, const), pl.BlockSpec(wf.shape, const),
                  pl.BlockSpec(wg.shape, const), pl.BlockSpec((1, LANES), const)],
        out_specs=[pl.BlockSpec((tm, d_pool), row), pl.BlockSpec((tm, d_att), row),
                   pl.BlockSpec((1, d_att, tm), tmin), pl.BlockSpec((1, d_att, tm), tmin),
                   pl.BlockSpec((tm, d_att), row), pl.BlockSpec((tm, d_att), row),
                   pl.BlockSpec((1, n_heads, tm), tmin), pl.BlockSpec((tm, dg), row)],
        out_shape=out_shape,
        compiler_params=_params("arbitrary"),
        name="proj_prompt",
    )(x, g, wm, wf, wg, bfp)


def _dot_nt(a, bt, precise):
    dims = (((1,), (1,)), ((), ()))
    if precise:
        return lax.dot_general(a.astype(F32), bt.astype(F32), dims, precision=HIGHEST, preferred_element_type=F32)
    return lax.dot_general(a.astype(BF16), bt.astype(BF16), dims, preferred_element_type=F32)


def _proj_sample_body(x_ref, g_ref, wmt_ref, wft_ref, wgt_ref, bf_ref, z_ref, lf_ref, gate_ref):
    h = _rmsnorm(x_ref[...], g_ref[...])
    z_ref[...] = _dot_nt(h, wmt_ref[...], True)
    lf_ref[...] = _log_sigmoid(_dot_nt(h, wft_ref[...], True) + bf_ref[...])
    gate_ref[...] = _sigmoid(_dot_nt(h, wgt_ref[...], True))


def _proj_sample(x, g, wmt, wft, wgt, bfp, *, tn):
    n, d = x.shape
    dm, dg = wmt.shape[0], wgt.shape[0]
    assert dm == dg
    const = lambda j: (0, 0)
    chunk = lambda j: (j, 0)
    col = lambda j: (0, j)
    return pl.pallas_call(
        _proj_sample_body,
        grid=(dm // tn,),
        in_specs=[pl.BlockSpec((n, d), const), pl.BlockSpec((1, d), const),
                  pl.BlockSpec((tn, d), chunk), pl.BlockSpec(wft.shape, const),
                  pl.BlockSpec((tn, d), chunk), pl.BlockSpec((1, LANES), const)],
        out_specs=[pl.BlockSpec((n, tn), col), pl.BlockSpec((n, LANES), const), pl.BlockSpec((n, tn), col)],
        out_shape=[jax.ShapeDtypeStruct((n, dm), F32), jax.ShapeDtypeStruct((n, LANES), F32),
                   jax.ShapeDtypeStruct((n, dg), F32)],
        compiler_params=_params("arbitrary"),
        name="proj_sample",
    )(x, g, wmt, wft, wgt, bfp)


def _cumsum_body(x_ref, o_ref):
    c = x_ref[...]
    lane = lax.broadcasted_iota(I32, c.shape, 1)
    s = 1
    while s < c.shape[1]:
        c = c + jnp.where(lane >= s, pltpu.roll(c, s, 1), 0.0)
        s *= 2
    o_ref[...] = c


def _cumsum_lanes(x):
    return pl.pallas_call(_cumsum_body, out_shape=jax.ShapeDtypeStruct(x.shape, F32),
                          compiler_params=_params(), name="cumsum_logf")(x)


def _attn_body(q_ref, k_ref, v_ref, c_ref, o_ref, *, tile, dh):
    nt = q_ref.shape[1] // tile
    lane = lax.broadcasted_iota(I32, (tile, 2 * dh), 1)
    first = lane < dh
    row = lax.broadcasted_iota(I32, (tile, tile), 0)
    col = lax.broadcasted_iota(I32, (tile, tile), 1)
    causal = col <= row
    one = jnp.ones((tile, 2 * dh), BF16)
    kts, vhs = [], []
    for kj in range(nt):
        vt = v_ref[0, kj * tile:(kj + 1) * tile, :]
        kts.append(k_ref[0, kj * tile:(kj + 1) * tile, :])
        vhs.append((jnp.where(first, vt, one), jnp.where(first, one, vt)))
    for qi in range(nt):
        q = q_ref[0, qi * tile:(qi + 1) * tile, :]
        zero = jnp.zeros_like(q)
        q_heads = (jnp.where(first, q, zero), jnp.where(first, zero, q))
        res = []
        for h in range(2):
            m = jnp.full((tile, 1), -1e30, F32)
            acc = jnp.zeros((tile, 2 * dh), F32)
            for kj in range(qi + 1):
                s = lax.dot_general(q_heads[h], kts[kj], (((1,), (1,)), ((), ())), preferred_element_type=F32)
                s = s - c_ref[0, 0, kj][h:h + 1, :]
                if kj == qi:
                    s = jnp.where(causal, s, -jnp.inf)
                m_new = jnp.maximum(m, jnp.max(s, axis=-1, keepdims=True))
                alpha = jnp.exp(m - m_new)
                p = jnp.exp(s - m_new)
                acc = alpha * acc + jnp.dot(p.astype(BF16), vhs[kj][h], preferred_element_type=F32)
                m = m_new
            res.append(acc)
        a0, a1 = res
        out = jnp.where(first, a0 / a0[:, dh:dh + 1], a1 / a1[:, 0:1])
        o_ref[0, qi * tile:(qi + 1) * tile, :] = out.astype(o_ref.dtype)


def _attn_prompt(q, k, v, c, *, tile, dh):
    b, t, da = q.shape
    hp = da // (2 * dh)
    nt = t // tile
    pair = pl.BlockSpec((1, t, 2 * dh), lambda bi, hi: (bi, 0, hi))
    return pl.pallas_call(
        functools.partial(_attn_body, tile=tile, dh=dh),
        grid=(b, hp),
        in_specs=[pair, pair, pair, pl.BlockSpec((1, 1, nt, 2, tile), lambda bi, hi: (bi, hi, 0, 0, 0))],
        out_specs=pair,
        out_shape=jax.ShapeDtypeStruct((b, t, da), BF16),
        compiler_params=_params("arbitrary", "arbitrary"),
        name="attn_prompt",
    )(q, k, v, c)


def _attn_sample_body(pt_ref, qrep_ref, q_ref, kn_ref, vrep_ref, lfn_ref, ck_hbm, cv_hbm, clf_hbm, o_ref,
                      m_ref, l_ref, acc_ref, srun_ref, kbuf, vbuf, lfbuf, sems):
    seq = pl.program_id(0)
    step = pl.program_id(1)
    n_steps = pl.num_programs(1)
    n_pages = pt_ref.shape[1]
    _, g_n, n_heads, dh, page = kbuf.shape
    d_att = n_heads * dh

    def fetch(slot, b, s):
        for g in range(g_n):
            pid = pt_ref[b, n_pages - 1 - (s * g_n + g)]
            pltpu.make_async_copy(ck_hbm.at[pid], kbuf.at[slot, g], sems.at[slot, 0]).start()
            pltpu.make_async_copy(cv_hbm.at[pid], vbuf.at[slot, g], sems.at[slot, 1]).start()
            pltpu.make_async_copy(clf_hbm.at[pid], lfbuf.at[slot, g], sems.at[slot, 2]).start()

    gstep = seq * n_steps + step
    slot = gstep % 2

    @pl.when(gstep == 0)
    def _():
        fetch(0, 0, 0)

    @pl.when(gstep + 1 < pl.num_programs(0) * n_steps)
    def _():
        last = step == n_steps - 1
        fetch(1 - slot, jnp.where(last, seq + 1, seq), jnp.where(last, 0, step + 1))

    pltpu.make_async_copy(ck_hbm.at[pl.ds(0, g_n)], kbuf.at[slot], sems.at[slot, 0]).wait()
    pltpu.make_async_copy(cv_hbm.at[pl.ds(0, g_n)], vbuf.at[slot], sems.at[slot, 1]).wait()
    pltpu.make_async_copy(clf_hbm.at[pl.ds(0, g_n)], lfbuf.at[slot], sems.at[slot, 2]).wait()
    k_refs = [kbuf.at[slot, g] for g in range(g_n)]
    v_refs = [vbuf.at[slot, g] for g in range(g_n)]
    lf_refs = [lfbuf.at[slot, g] for g in range(g_n)]

    @pl.when(step == 0)
    def _():
        s_new = jnp.sum(q_ref[0] * kn_ref[0], axis=-1, keepdims=True)
        m_ref[...] = jnp.broadcast_to(s_new, m_ref.shape)
        l_ref[...] = jnp.ones_like(l_ref)
        lane = lax.broadcasted_iota(I32, (d_att, page), 1)
        acc_ref[...] = jnp.where(lane == 0, vrep_ref[0], 0.0)
        srun_ref[...] = lfn_ref[0]

    qrep = qrep_ref[0]
    r = lax.broadcasted_iota(I32, (page, page), 0)
    c = lax.broadcasted_iota(I32, (page, page), 1)
    later = (r > c).astype(BF16)
    ones = jnp.ones((page, page), BF16)
    lf_all = jnp.concatenate([lf_refs[g][...] for g in range(g_n)], axis=0)
    suffix = _dot_exact_rhs(lf_all, later)
    total = _dot_exact_rhs(lf_all, ones)
    s_run = srun_ref[...]
    m_prev = m_ref[...]
    m_new = m_prev
    scores = []
    for g in range(g_n):
        kq = k_refs[g][...].reshape(d_att, page) * qrep
        s = jnp.sum(kq.reshape(n_heads, dh, page), axis=1)
        sb = s + s_run + suffix[g * n_heads:(g + 1) * n_heads]
        s_run = s_run + total[g * n_heads:(g + 1) * n_heads]
        scores.append(sb)
        m_new = jnp.maximum(m_new, jnp.max(sb, axis=-1, keepdims=True))
    srun_ref[...] = s_run
    alpha = jnp.exp(m_prev - m_new)
    l = alpha * l_ref[...]
    acc = acc_ref[...].reshape(n_heads, dh, page) * alpha[:, None, :]
    for g in range(g_n):
        p = jnp.exp(scores[g] - m_new)
        l = l + jnp.sum(p, axis=-1, keepdims=True)
        acc = acc + v_refs[g][...] * p[:, None, :]
    m_ref[...] = m_new
    l_ref[...] = l
    acc_ref[...] = acc.reshape(d_att, page)

    @pl.when(step == pl.num_programs(1) - 1)
    def _():
        o_ref[0] = jnp.sum(acc / l[:, None, :], axis=-1)


def _attn_sample(page_table, q, k_new, v_new, lf_new, cache_kt, cache_vt, cache_lft):
    db, n_pages = page_table.shape
    n_phys, n_heads, dh, page = cache_kt.shape
    d_att = n_heads * dh
    g_n = PAGES_PER_STEP
    while n_pages % g_n:
        g_n //= 2
    n_steps = n_pages // g_n
    lane_rep = lambda a: jnp.broadcast_to(a.reshape(db, -1, 1), (db, a.size // db, page))

    per_seq = lambda b, s, pt: (b, 0, 0)
    hbm = pl.BlockSpec(memory_space=pl.ANY)
    in_specs = [pl.BlockSpec((1, d_att, page), per_seq), pl.BlockSpec((1, n_heads, dh), per_seq),
                pl.BlockSpec((1, n_heads, dh), per_seq), pl.BlockSpec((1, d_att, page), per_seq),
                pl.BlockSpec((1, n_heads, page), per_seq), hbm, hbm, hbm]
    grid_spec = pltpu.PrefetchScalarGridSpec(
        num_scalar_prefetch=1, grid=(db, n_steps), in_specs=in_specs,
        out_specs=pl.BlockSpec((1, n_heads, dh), per_seq),
        scratch_shapes=[pltpu.VMEM((n_heads, page), F32), pltpu.VMEM((n_heads, page), F32),
                        pltpu.VMEM((d_att, page), F32), pltpu.VMEM((n_heads, page), F32),
                        pltpu.VMEM((2, g_n, n_heads, dh, page), F32), pltpu.VMEM((2, g_n, n_heads, dh, page), F32),
                        pltpu.VMEM((2, g_n, n_heads, page), F32), pltpu.SemaphoreType.DMA((2, 3))])
    return pl.pallas_call(
        _attn_sample_body,
        grid_spec=grid_spec,
        out_shape=jax.ShapeDtypeStruct((db, n_heads, dh), F32),
        compiler_params=_params("arbitrary", "arbitrary"),
        name="attn_sample",
    )(page_table, lane_rep(q), q.reshape(db, n_heads, dh), k_new.reshape(db, n_heads, dh), lane_rep(v_new),
      lane_rep(lf_new), cache_kt, cache_vt, cache_lft)


def _merge_and_route(x, pooled, att, gates, wp_ref, ps_ref, wup_ref, wua_ref, wo_ref, nf_ref, wr_ref, br_ref,
                     base_counts, *, precise, n_groups, n_per_group):
    tm, d = x.shape
    gw = pooled[0].shape[1]
    mixed = jnp.concatenate([_dot(pooled[g], wp_ref[g], precise) for g in range(len(pooled))], axis=-1)
    pool_out = mixed * ps_ref[...]
    y = gates[:, :d].astype(F32) * _dot(pool_out, wup_ref[...], precise) \
        + gates[:, d:].astype(F32) * _dot(att, wua_ref[...], precise)
    x2 = x + _dot(y, wo_ref[...], precise)
    h2 = _rmsnorm(x2, nf_ref[...])
    logits = _dot(h2, wr_ref[...], precise) + br_ref[...]
    lane = lax.broadcasted_iota(I32, logits.shape, 1)
    lanef = lane.astype(F32)
    neg = -jnp.inf
    is_g = lane < n_groups
    gmax = jnp.max(jnp.where(is_g, logits, neg), axis=-1, keepdims=True)
    gidx = jnp.min(jnp.where(is_g & (logits == gmax), lanef, float(LANES)), axis=-1, keepdims=True)
    gsum = jnp.sum(jnp.where(is_g, jnp.exp(logits - gmax), 0.0), axis=-1, keepdims=True)
    g_w = 1.0 / gsum
    n_exp = n_groups * n_per_group
    exp_id = lanef - float(n_groups)
    in_sel = (lane >= n_groups) & (lane < n_groups + n_exp) & (jnp.floor(exp_id / n_per_group) == gidx)
    v1 = jnp.max(jnp.where(in_sel, logits, neg), axis=-1, keepdims=True)
    i1 = jnp.min(jnp.where(in_sel & (logits == v1), lanef, float(LANES)), axis=-1, keepdims=True)
    in_sel2 = in_sel & (lanef != i1)
    v2 = jnp.max(jnp.where(in_sel2, logits, neg), axis=-1, keepdims=True)
    i2 = jnp.min(jnp.where(in_sel2 & (logits == v2), lanef, float(LANES)), axis=-1, keepdims=True)
    t = jnp.exp(v2 - v1)
    w1 = g_w * (1.0 / (1.0 + t))
    w2 = g_w * (t / (1.0 + t))
    e1 = i1 - float(n_groups)
    e2 = i2 - float(n_groups)
    hit1 = lanef == e1
    hit2 = lanef == e2
    onehot = (hit1 | hit2).astype(BF16)
    rr = lax.broadcasted_iota(I32, (tm, tm), 0)
    cc = lax.broadcasted_iota(I32, (tm, tm), 1)
    incl = jnp.dot((cc <= rr).astype(BF16), onehot, preferred_element_type=F32)
    seen = incl + base_counts - 1.0
    r1 = jnp.sum(jnp.where(hit1, seen, 0.0), axis=-1, keepdims=True)
    r2 = jnp.sum(jnp.where(hit2, seen, 0.0), axis=-1, keepdims=True)
    counts = base_counts + incl[tm - 1:tm, :]
    slab = jnp.zeros((tm, LANES), F32)
    for i, val in enumerate((e1, e2, r1, r2, w1, w2)):
        slab = jnp.where(lane == i, val, slab)
    return x2, h2, slab, counts


def _merge_prompt_body(x_ref, u_ref, halo_ref, att_ref, gate_ref, wp_ref, ps_ref, wup_ref, wua_ref, wo_ref,
                       nf_ref, wr_ref, br_ref, x2_ref, h2_ref, slab_ref, route_ref, counts_ref, ext_ref, cnt_ref,
                       *, seq_len, n_groups, n_per_group):
    i = pl.program_id(0)
    tm = x_ref.shape[0]
    gw = u_ref.shape[1] // len(POOL_WINDOWS)
    pos0 = (i * tm) % seq_len

    @pl.when(i == 0)
    def _():
        cnt_ref[...] = jnp.zeros_like(cnt_ref)

    u = u_ref[...]
    ext_ref[0:POOL_HALO, :] = jnp.where(pos0 == 0, 0.0, halo_ref[...])
    ext_ref[POOL_HALO:, :] = u
    pos = pos0 + lax.broadcasted_iota(I32, (tm, 1), 0)
    pooled = []
    for g, w in enumerate(POOL_WINDOWS):
        lo = g * gw
        wsum = ext_ref[pl.ds(POOL_HALO, tm), lo:lo + gw]
        for j in range(1, w):
            wsum = wsum + ext_ref[pl.ds(POOL_HALO - j, tm), lo:lo + gw]
        count = jnp.minimum(pos + 1, w).astype(F32)
        pooled.append(wsum / count - u[:, lo:lo + gw])
    x2, h2, slab, counts = _merge_and_route(
        x_ref[...], pooled, att_ref[...], gate_ref[...], wp_ref, ps_ref, wup_ref, wua_ref, wo_ref, nf_ref,
        wr_ref, br_ref, cnt_ref[...], precise=False, n_groups=n_groups, n_per_group=n_per_group)
    x2_ref[...] = x2
    h2_ref[...] = h2
    slab_ref[...] = slab
    route_ref[...] = slab.T[0:8, :]
    cnt_ref[...] = counts
    counts_ref[...] = counts


def _merge_prompt(x, u, att, gates, wp, ps, wup, wua, wo, nf, wr, br, *, tm, seq_len, n_groups, n_per_group):
    n, d = x.shape
    d_pool, d_att = u.shape[1], att.shape[1]
    row = lambda i: (i, 0)
    const = lambda i: (0, 0)
    const3 = lambda i: (0, 0, 0)
    halo = lambda i: (jnp.maximum(i * (tm // POOL_HALO) - 1, 0), 0)
    return pl.pallas_call(
        functools.partial(_merge_prompt_body, seq_len=seq_len, n_groups=n_groups, n_per_group=n_per_group),
        grid=(n // tm,),
        in_specs=[pl.BlockSpec((tm, d), row), pl.BlockSpec((tm, d_pool), row), pl.BlockSpec((POOL_HALO, d_pool), halo),
                  pl.BlockSpec((tm, d_att), row), pl.BlockSpec((tm, 2 * d), row),
                  pl.BlockSpec(wp.shape, const3), pl.BlockSpec((1, d_pool), const),
                  pl.BlockSpec(wup.shape, const), pl.BlockSpec(wua.shape, const), pl.BlockSpec(wo.shape, const),
                  pl.BlockSpec((1, d), const), pl.BlockSpec(wr.shape, const), pl.BlockSpec((1, LANES), const)],
        out_specs=[pl.BlockSpec((tm, d), row), pl.BlockSpec((tm, d), row), pl.BlockSpec((tm, LANES), row),
                   pl.BlockSpec((8, tm), lambda i: (0, i)), pl.BlockSpec((1, LANES), const)],
        out_shape=[jax.ShapeDtypeStruct((n, d), F32), jax.ShapeDtypeStruct((n, d), F32),
                   jax.ShapeDtypeStruct((n, LANES), F32), jax.ShapeDtypeStruct((8, n), F32),
                   jax.ShapeDtypeStruct((1, LANES), F32)],
        scratch_shapes=[pltpu.VMEM((tm + POOL_HALO, d_pool), F32), pltpu.VMEM((1, LANES), F32)],
        compiler_params=_params("arbitrary"),
        name="merge_prompt",
    )(x, u, u, att, gates, wp, ps, wup, wua, wo, nf, wr, br)


def _merge_sample_body(x_ref, u_ref, st_ref, att_ref, gate_ref, wp_ref, ps_ref, wup_ref, wua_ref, wo_ref,
                       nf_ref, wr_ref, br_ref, base_ref, x2_ref, h2_ref, slab_ref, counts_ref,
                       *, start_pos, n_groups, n_per_group):
    u = u_ref[...]
    gw = u.shape[1] // len(POOL_WINDOWS)
    n_state = st_ref.shape[0]
    pooled = []
    for g, w in enumerate(POOL_WINDOWS):
        lo = g * gw
        wsum = u[:, lo:lo + gw]
        for j in range(1, w):
            wsum = wsum + st_ref[n_state - j][:, lo:lo + gw]
        pooled.append(wsum / float(min(start_pos + 1, w)) - u[:, lo:lo + gw])
    x2, h2, slab, counts = _merge_and_route(
        x_ref[...], pooled, att_ref[...], gate_ref[...], wp_ref, ps_ref, wup_ref, wua_ref, wo_ref, nf_ref,
        wr_ref, br_ref, base_ref[...], precise=True, n_groups=n_groups, n_per_group=n_per_group)
    x2_ref[...] = x2
    h2_ref[...] = h2
    slab_ref[...] = slab
    counts_ref[...] = counts


def _merge_sample(x, u, state_t, att, gates, wp, ps, wup, wua, wo, nf, wr, br, base, *, start_pos, n_groups,
                  n_per_group):
    n, d = x.shape
    return pl.pallas_call(
        functools.partial(_merge_sample_body, start_pos=start_pos, n_groups=n_groups, n_per_group=n_per_group),
        out_shape=[jax.ShapeDtypeStruct((n, d), F32), jax.ShapeDtypeStruct((n, d), F32),
                   jax.ShapeDtypeStruct((n, LANES), F32), jax.ShapeDtypeStruct((1, LANES), F32)],
        compiler_params=_params(),
        name="merge_sample",
    )(x, u, state_t, att, gates, wp, ps, wup, wua, wo, nf, wr, br, base)


def _row_copy(src_ref, src_row, dst_ref, dst_row, sem):
    return pltpu.make_async_copy(src_ref.at[pl.ds(src_row, 1)], dst_ref.at[pl.ds(dst_row, 1)], sem)


def _moe_scatter_body(goff_ref, e1_ref, e2_ref, r1_ref, r2_ref, h_ref, xs_in_ref, xs_ref, sem):
    del xs_in_ref
    n = h_ref.shape[0]

    def start(i, carry):
        for j in range(ROW_DMA_UNROLL):
            t = i * ROW_DMA_UNROLL + j
            _row_copy(h_ref, t, xs_ref, goff_ref[e1_ref[t]] + r1_ref[t], sem).start()
            _row_copy(h_ref, t, xs_ref, goff_ref[e2_ref[t]] + r2_ref[t], sem).start()
        return carry

    lax.fori_loop(0, n // ROW_DMA_UNROLL, start, 0)
    all_rows = pltpu.make_async_copy(h_ref, xs_ref.at[pl.ds(0, n)], sem)
    all_rows.wait()
    all_rows.wait()


def _moe_scatter(goff, e1, e2, r1, r2, h, xs, *, ts):
    n, d = h.shape
    smem = lambda: pl.BlockSpec((ts,), lambda i: (i,), memory_space=pltpu.SMEM)
    return pl.pallas_call(
        _moe_scatter_body,
        grid=(n // ts,),
        in_specs=[pl.BlockSpec(memory_space=pltpu.SMEM), smem(), smem(), smem(), smem(),
                  pl.BlockSpec((ts, d), lambda i: (i, 0)), pl.BlockSpec(memory_space=pl.ANY)],
        out_specs=pl.BlockSpec(memory_space=pl.ANY),
        out_shape=jax.ShapeDtypeStruct(xs.shape, xs.dtype),
        scratch_shapes=[pltpu.SemaphoreType.DMA],
        input_output_aliases={6: 0},
        compiler_params=_params("arbitrary"),
        name="moe_scatter",
    )(goff, e1, e2, r1, r2, h, xs)


def _moe_mm_body(te_ref, tw_ref, xs_ref, wg_ref, wu_ref, wd_ref, ys_ref, wgb_ref, wub_ref, wdb_ref):
    i = pl.program_id(0)
    expert = te_ref[i]
    prev = te_ref[jnp.maximum(i - 1, 0)]

    @pl.when((expert >= 0) & ((i == 0) | (expert != prev)))
    def _():
        wgb_ref[...] = wg_ref[0].astype(BF16)
        wub_ref[...] = wu_ref[0].astype(BF16)
        wdb_ref[...] = wd_ref[0].astype(BF16)

    @pl.when(expert >= 0)
    def _():
        x = xs_ref[...].astype(BF16)
        a = jnp.dot(x, wgb_ref[...], preferred_element_type=F32)
        b = jnp.dot(x, wub_ref[...], preferred_element_type=F32)
        hdn = (a * _sigmoid(a)) * b
        ys_ref[...] = jnp.dot(hdn.astype(BF16), wdb_ref[...], preferred_element_type=F32)

    @pl.when(expert < 0)
    def _():
        ys_ref[...] = jnp.zeros_like(ys_ref)


def _moe_mm(tile_expert, tile_weight, xs, w_gate, w_up, w_down, *, tm):
    p, d = xs.shape
    n_exp, _, de = w_gate.shape
    wmap = lambda i, te, tw: (tw[i], 0, 0)
    grid_spec = pltpu.PrefetchScalarGridSpec(
        num_scalar_prefetch=2, grid=(p // tm,),
        in_specs=[pl.BlockSpec((tm, d), lambda i, te, tw: (i, 0)),
                  pl.BlockSpec((1, d, de), wmap), pl.BlockSpec((1, d, de), wmap), pl.BlockSpec((1, de, d), wmap)],
        out_specs=pl.BlockSpec((tm, d), lambda i, te, tw: (i, 0)),
        scratch_shapes=[pltpu.VMEM((d, de), BF16), pltpu.VMEM((d, de), BF16), pltpu.VMEM((de, d), BF16)])
    return pl.pallas_call(
        _moe_mm_body, grid_spec=grid_spec,
        out_shape=jax.ShapeDtypeStruct((p, d), F32),
        compiler_params=_params("arbitrary"),
        name="moe_mm",
    )(tile_expert, tile_weight, xs, w_gate, w_up, w_down)


def _moe_combine_body(goff_ref, e1_ref, e2_ref, r1_ref, r2_ref, x_ref, slab_ref, g_ref, ys_ref, o_ref,
                      ya_ref, yb_ref, sem, *, final_norm):
    n = x_ref.shape[0]

    def start(i, carry):
        for j in range(ROW_DMA_UNROLL):
            t = i * ROW_DMA_UNROLL + j
            _row_copy(ys_ref, goff_ref[e1_ref[t]] + r1_ref[t], ya_ref, t, sem).start()
            _row_copy(ys_ref, goff_ref[e2_ref[t]] + r2_ref[t], yb_ref, t, sem).start()
        return carry

    lax.fori_loop(0, n // ROW_DMA_UNROLL, start, 0)
    pltpu.make_async_copy(ys_ref.at[pl.ds(0, n)], ya_ref, sem).wait()
    pltpu.make_async_copy(ys_ref.at[pl.ds(0, n)], yb_ref, sem).wait()
    slab = slab_ref[...]
    out = x_ref[...] + (slab[:, 4:5] * ya_ref[...] + slab[:, 5:6] * yb_ref[...])
    if final_norm:
        out = _rmsnorm(out, g_ref[...])
    o_ref[...] = out


def _moe_combine(goff, e1, e2, r1, r2, x, slab, g, ys, *, ts, final_norm):
    n, d = x.shape
    smem = lambda: pl.BlockSpec((ts,), lambda i: (i,), memory_space=pltpu.SMEM)
    return pl.pallas_call(
        functools.partial(_moe_combine_body, final_norm=final_norm),
        grid=(n // ts,),
        in_specs=[pl.BlockSpec(memory_space=pltpu.SMEM), smem(), smem(), smem(), smem(),
                  pl.BlockSpec((ts, d), lambda i: (i, 0)), pl.BlockSpec((ts, LANES), lambda i: (i, 0)),
                  pl.BlockSpec((1, d), lambda i: (0, 0)), pl.BlockSpec(memory_space=pl.ANY)],
        out_specs=pl.BlockSpec((ts, d), lambda i: (i, 0)),
        out_shape=jax.ShapeDtypeStruct((n, d), F32),
        scratch_shapes=[pltpu.VMEM((ts, d), F32), pltpu.VMEM((ts, d), F32), pltpu.SemaphoreType.DMA],
        compiler_params=_params("arbitrary"),
        name="moe_combine",
    )(goff, e1, e2, r1, r2, x, slab, g, ys)


def kernel(x_prompt, x_sample, cache_k, cache_v, cache_logf, state_pool, page_table, norm_mix, w_in, b_forget,
           w_pool, pool_scale, w_up_pool, w_up_att, w_out, norm_ffn, w_router_group, b_router_group,
           w_router_expert, b_router_expert, w_gate, w_up, w_down, norm_final):
    depth = norm_mix.shape[0]
    assert depth == 1, "single trunk layer"
    b, t, d = x_prompt.shape
    db, dt, _ = x_sample.shape
    assert dt == 1, "one sample token per sequence"
    _, n_phys, page, n_heads, dh = cache_k.shape
    n_pages = page_table.shape[1]
    past = n_pages * page
    n_state, d_pool = state_pool.shape[2], state_pool.shape[3]
    d_att = n_heads * dh
    n_pool_groups = w_pool.shape[1]
    assert n_pool_groups == len(POOL_WINDOWS) and d_pool // n_pool_groups == LANES
    assert n_state == max(POOL_WINDOWS) - 1 and n_state < POOL_HALO
    n_groups, n_per_group = w_router_expert.shape[1], w_router_expert.shape[3]
    n_exp = n_groups * n_per_group
    assert n_groups + n_exp <= LANES and 2 * dh == LANES and n_heads % 2 == 0
    n = b * t
    q_scale = float(dh) ** -0.5
    tm = min(TOKEN_TILE, t)
    assert t % tm == 0 and t % ATTN_TILE == 0

    o_main = d_pool + 3 * d_att
    wi = w_in[0]
    wm_f, wf_f, wg_f = wi[:, :o_main], wi[:, o_main:o_main + n_heads], wi[:, o_main + n_heads:]
    wf_pad = jnp.pad(wf_f, ((0, 0), (0, LANES - n_heads)))
    wit = jnp.transpose(wi)
    wmt_f, wgt_f = wit[:o_main], wit[o_main + n_heads:]
    wft_pad = jnp.pad(wit[o_main:o_main + n_heads], ((0, LANES - n_heads), (0, 0)))
    bf_pad = jnp.pad(b_forget[0], (0, LANES - n_heads)).reshape(1, LANES)
    g_mix = norm_mix[0].reshape(1, d)
    g_ffn = norm_ffn[0].reshape(1, d)
    g_fin = norm_final.reshape(1, d)
    ps = pool_scale[0].reshape(1, d_pool)
    wr_f = jnp.concatenate([w_router_group[0], jnp.transpose(w_router_expert[0], (1, 0, 2)).reshape(d, n_exp)], axis=1)
    wr_pad = jnp.pad(wr_f, ((0, 0), (0, LANES - n_groups - n_exp)))
    br_pad = jnp.pad(jnp.concatenate([b_router_group[0], b_router_expert[0].reshape(n_exp)]),
                     (0, LANES - n_groups - n_exp)).reshape(1, LANES)
    bf = lambda a: a.astype(BF16)

    xp = x_prompt.reshape(n, d)
    u_p, q_p, kt_p, vt_p, kb_p, vb_p, lft_p, gate_p = _proj_prompt(
        xp, g_mix, bf(wm_f), bf(wf_pad), bf(wg_f), bf_pad, tm=tm, seq_len=t, d_pool=d_pool, d_att=d_att,
        n_heads=n_heads, q_scale=q_scale)
    c = _cumsum_lanes(lft_p.reshape(b * n_heads, t))
    nt = t // ATTN_TILE
    c_blk = jnp.transpose(c.reshape(b, n_heads // 2, 2, nt, ATTN_TILE), (0, 1, 3, 2, 4))
    att_p = _attn_prompt(q_p.reshape(b, t, d_att), kb_p.reshape(b, t, d_att), vb_p.reshape(b, t, d_att), c_blk,
                         tile=ATTN_TILE, dh=dh)
    x2_p, h2_p, slab_p, route_p, counts_p = _merge_prompt(
        xp, u_p, att_p.reshape(n, d_att), gate_p, bf(w_pool[0]), ps, bf(w_up_pool[0]), bf(w_up_att[0]),
        bf(w_out[0]), g_ffn, bf(wr_pad), br_pad, tm=tm, seq_len=t, n_groups=n_groups, n_per_group=n_per_group)

    xs = x_sample.reshape(db, d)
    z_s, lf_s, gate_s = _proj_sample(xs, g_mix, wmt_f, wft_pad, wgt_f, bf_pad, tn=512)
    u_s = z_s[:, :d_pool]
    q_s = z_s[:, d_pool:d_pool + d_att] * q_scale
    k_s = z_s[:, d_pool + d_att:d_pool + 2 * d_att]
    v_s = z_s[:, d_pool + 2 * d_att:]
    att_s = _attn_sample(page_table, q_s, k_s, v_s, lf_s[:, :n_heads],
                         jnp.transpose(cache_k[0], (0, 2, 3, 1)), jnp.transpose(cache_v[0], (0, 2, 3, 1)),
                         jnp.transpose(cache_logf[0], (0, 2, 1)))
    state_t = jnp.transpose(state_pool[0], (1, 0, 2))
    x2_s, h2_s, slab_s, counts = _merge_sample(
        xs, u_s, state_t, att_s.reshape(db, d_att), gate_s, w_pool[0], ps, w_up_pool[0], w_up_att[0], w_out[0],
        g_ffn, wr_pad, br_pad, counts_p, start_pos=past, n_groups=n_groups, n_per_group=n_per_group)

    tmm = MOE_ROW_TILE
    n_tok = n + db
    n_tiles = -(-(2 * n_tok + n_exp * (tmm - 1)) // tmm)
    cnt = counts[0, :n_exp].astype(I32)
    padded = ((cnt + tmm - 1) // tmm) * tmm
    ends = jnp.cumsum(padded)
    goff = (ends - padded).astype(I32)
    tile_e = jnp.sum((jnp.arange(n_tiles, dtype=I32)[:, None] * tmm >= ends[None, :]).astype(I32), axis=1)
    tile_expert = jnp.where(tile_e < n_exp, tile_e, -1).astype(I32)
    tile_weight = jnp.minimum(tile_e, n_exp - 1).astype(I32)
    fields_p = route_p[:4].astype(I32)
    fields_s = jnp.transpose(slab_s[:, :4]).astype(I32)
    xs_rows = jnp.zeros((n_tiles * tmm, d), F32)
    xs_rows = _moe_scatter(goff, *fields_p, h2_p, xs_rows, ts=tm)
    xs_rows = _moe_scatter(goff, *fields_s, h2_s, xs_rows, ts=db)
    ys_rows = _moe_mm(tile_expert, tile_weight, xs_rows, w_gate[0], w_up[0], w_down[0], tm=tmm)
    y_prompt = _moe_combine(goff, *fields_p, x2_p, slab_p, g_fin, ys_rows, ts=tm, final_norm=True)
    y_sample = _moe_combine(goff, *fields_s, x2_s, slab_s, g_fin, ys_rows, ts=db, final_norm=True)

    new_pool_p = u_p.reshape(b, t, d_pool)[:, t - n_state:, :]
    new_pool_s = jnp.concatenate([state_pool[0][:, 1:, :], u_s[:, None, :]], axis=1)
    to_heads = lambda a: jnp.transpose(a.reshape(b, n_heads, dh, t), (0, 3, 1, 2))[None]
    return (y_prompt.reshape(b, t, d), y_sample.reshape(db, 1, d),
            to_heads(kt_p), to_heads(vt_p), jnp.transpose(lft_p, (0, 2, 1))[None],
            new_pool_p[None],
            k_s.reshape(1, db, 1, n_heads, dh), v_s.reshape(1, db, 1, n_heads, dh),
            lf_s[:, :n_heads].reshape(1, db, 1, n_heads), new_pool_s[None])
```

```python
import functools

import jax
import jax.numpy as jnp
from jax import lax
from jax.experimental import pallas as pl
from jax.experimental.pallas import tpu as pltpu

F32 = jnp.float32
BF16 = jnp.bfloat16
I32 = jnp.int32
HIGHEST = lax.Precision.HIGHEST

RMS_EPS = 1e-6
POOL_WINDOWS = (2, 4, 8, 16)
POOL_HALO = 16
LANES = 128
VMEM_LIMIT_BYTES = 56 * 1024 * 1024

TOKEN_TILE = 512
ATTN_TILE = 512
MOE_ROW_TILE = 256
PAGES_PER_STEP = 16
ROW_GROUP = 8
ROW_DMA_UNROLL = 8


def _params(*sem):
    return pltpu.CompilerParams(dimension_semantics=sem, vmem_limit_bytes=VMEM_LIMIT_BYTES)


def _rmsnorm(x, g):
    return x * lax.rsqrt(jnp.mean(x * x, axis=-1, keepdims=True) + RMS_EPS) * g


def _log_sigmoid(x):
    return jnp.minimum(x, 0.0) - jnp.log1p(jnp.exp(-jnp.abs(x)))


def _sigmoid(x):
    return 1.0 / (1.0 + jnp.exp(-x))


def _dot(a, b, precise):
    if precise:
        return jnp.dot(a.astype(F32), b.astype(F32), precision=HIGHEST, preferred_element_type=F32)
    return jnp.dot(a.astype(BF16), b.astype(BF16), preferred_element_type=F32)


def _split3(x):
    hi = x.astype(BF16)
    r = x - hi.astype(F32)
    mid = r.astype(BF16)
    lo = (r - mid.astype(F32)).astype(BF16)
    return hi, mid, lo


def _dot_exact_rhs(x, w_bf16):
    hi, mid, lo = _split3(x)
    d = lambda a: jnp.dot(a, w_bf16, preferred_element_type=F32)
    return d(hi) + d(mid) + d(lo)


def _dot_exact_lhs(w_bf16, x):
    hi, mid, lo = _split3(x)
    d = lambda a: jnp.dot(w_bf16, a, preferred_element_type=F32)
    return d(hi) + d(mid) + d(lo)


def _proj_body(x_ref, g_ref, wm_ref, wf_ref, wg_ref, bf_ref,
               u_ref, q_ref, kt_ref, vt_ref, kb_ref, vb_ref, lft_ref, gate_ref, *, d_pool, d_att, n_heads, q_scale):
    h = _rmsnorm(x_ref[...], g_ref[...]).astype(BF16)
    z = jnp.dot(h, wm_ref[...], preferred_element_type=F32)
    o1, o2, o3 = d_pool, d_pool + d_att, d_pool + 2 * d_att
    u_ref[...] = z[:, :o1]
    q_ref[...] = (z[:, o1:o2] * q_scale).astype(BF16)
    k = z[:, o2:o3]
    v = z[:, o3:]
    kt_ref[0] = k.T
    vt_ref[0] = v.T
    kb_ref[...] = k.astype(BF16)
    vb_ref[...] = v.astype(BF16)
    lf = _log_sigmoid(jnp.dot(h, wf_ref[...], preferred_element_type=F32) + bf_ref[...])
    lft_ref[0] = lf.T[0:n_heads, :]
    gate_ref[...] = _sigmoid(jnp.dot(h, wg_ref[...], preferred_element_type=F32)).astype(BF16)


def _proj_prompt(x, g, wm, wf, wg, bfp, *, tm, seq_len, d_pool, d_att, n_heads, q_scale):
    n, d = x.shape
    b = n // seq_len
    tps = seq_len // tm
    row = lambda i: (i, 0)
    const = lambda i: (0, 0)
    tmin = lambda i: (i // tps, 0, i % tps)
    dg = wg.shape[1]
    out_shape = [
        jax.ShapeDtypeStruct((n, d_pool), F32), jax.ShapeDtypeStruct((n, d_att), BF16),
        jax.ShapeDtypeStruct((b, d_att, seq_len), F32), jax.ShapeDtypeStruct((b, d_att, seq_len), F32),
        jax.ShapeDtypeStruct((n, d_att), BF16), jax.ShapeDtypeStruct((n, d_att), BF16),
        jax.ShapeDtypeStruct((b, n_heads, seq_len), F32), jax.ShapeDtypeStruct((n, dg), BF16),
    ]
    return pl.pallas_call(
        functools.partial(_proj_body, d_pool=d_pool, d_att=d_att, n_heads=n_heads, q_scale=q_scale),
        grid=(n // tm,),
        in_specs=[pl.BlockSpec((tm, d), row), pl.BlockSpec((1, d), const),
                  pl.BlockSpec(wm.shape, const), pl.BlockSpec(wf.shape, const),
                  pl.BlockSpec(wg.shape, const), pl.BlockSpec((1, LANES), const)],
        out_specs=[pl.BlockSpec((tm, d_pool), row), pl.BlockSpec((tm, d_att), row),
                   pl.BlockSpec((1, d_att, tm), tmin), pl.BlockSpec((1, d_att, tm), tmin),
                   pl.BlockSpec((tm, d_att), row), pl.BlockSpec((tm, d_att), row),
                   pl.BlockSpec((1, n_heads, tm), tmin), pl.BlockSpec((tm, dg), row)],
        out_shape=out_shape,
        compiler_params=_params("arbitrary"),
        name="proj_prompt",
    )(x, g, wm, wf, wg, bfp)


def _dot_nt(a, bt, precise):
    dims = (((1,), (1,)), ((), ()))
    if precise:
        return lax.dot_general(a.astype(F32), bt.astype(F32), dims, precision=HIGHEST, preferred_element_type=F32)
    return lax.dot_general(a.astype(BF16), bt.astype(BF16), dims, preferred_element_type=F32)


def _proj_sample_body(x_ref, g_ref, wmt_ref, wft_ref, wgt_ref, bf_ref, z_ref, lf_ref, gate_ref):
    h = _rmsnorm(x_ref[...], g_ref[...])
    z_ref[...] = _dot_nt(h, wmt_ref[...], True)
    lf_ref[...] = _log_sigmoid(_dot_nt(h, wft_ref[...], True) + bf_ref[...])
    gate_ref[...] = _sigmoid(_dot_nt(h, wgt_ref[...], True))


def _proj_sample(x, g, wmt, wft, wgt, bfp, *, tn):
    n, d = x.shape
    dm, dg = wmt.shape[0], wgt.shape[0]
    assert dm == dg
    const = lambda j: (0, 0)
    chunk = lambda j: (j, 0)
    col = lambda j: (0, j)
    return pl.pallas_call(
        _proj_sample_body,
        grid=(dm // tn,),
        in_specs=[pl.BlockSpec((n, d), const), pl.BlockSpec((1, d), const),
                  pl.BlockSpec((tn, d), chunk), pl.BlockSpec(wft.shape, const),
                  pl.BlockSpec((tn, d), chunk), pl.BlockSpec((1, LANES), const)],
        out_specs=[pl.BlockSpec((n, tn), col), pl.BlockSpec((n, LANES), const), pl.BlockSpec((n, tn), col)],
        out_shape=[jax.ShapeDtypeStruct((n, dm), F32), jax.ShapeDtypeStruct((n, LANES), F32),
                   jax.ShapeDtypeStruct((n, dg), F32)],
        compiler_params=_params("arbitrary"),
        name="proj_sample",
    )(x, g, wmt, wft, wgt, bfp)


def _cumsum_body(x_ref, o_ref):
    c = x_ref[...]
    lane = lax.broadcasted_iota(I32, c.shape, 1)
    s = 1
    while s < c.shape[1]:
        c = c + jnp.where(lane >= s, pltpu.roll(c, s, 1), 0.0)
        s *= 2
    o_ref[...] = c


def _cumsum_lanes(x):
    return pl.pallas_call(_cumsum_body, out_shape=jax.ShapeDtypeStruct(x.shape, F32),
                          compiler_params=_params(), name="cumsum_logf")(x)


def _attn_body(q_ref, k_ref, v_ref, c_ref, o_ref, *, tile, dh):
    nt = q_ref.shape[1] // tile
    lane = lax.broadcasted_iota(I32, (tile, 2 * dh), 1)
    first = lane < dh
    row = lax.broadcasted_iota(I32, (tile, tile), 0)
    col = lax.broadcasted_iota(I32, (tile, tile), 1)
    causal = col <= row
    one = jnp.ones((tile, 2 * dh), BF16)
    kts, vhs = [], []
    for kj in range(nt):
        vt = v_ref[0, kj * tile:(kj + 1) * tile, :]
        kts.append(k_ref[0, kj * tile:(kj + 1) * tile, :])
        vhs.append((jnp.where(first, vt, one), jnp.where(first, one, vt)))
    for qi in range(nt):
        q = q_ref[0, qi * tile:(qi + 1) * tile, :]
        zero = jnp.zeros_like(q)
        q_heads = (jnp.where(first, q, zero), jnp.where(first, zero, q))
        res = []
        for h in range(2):
            m = jnp.full((tile, 1), -1e30, F32)
            acc = jnp.zeros((tile, 2 * dh), F32)
            for kj in range(qi + 1):
                s = lax.dot_general(q_heads[h], kts[kj], (((1,), (1,)), ((), ())), preferred_element_type=F32)
                s = s - c_ref[0, 0, kj][h:h + 1, :]
                if kj == qi:
                    s = jnp.where(causal, s, -jnp.inf)
                m_new = jnp.maximum(m, jnp.max(s, axis=-1, keepdims=True))
                alpha = jnp.exp(m - m_new)
                p = jnp.exp(s - m_new)
                acc = alpha * acc + jnp.dot(p.astype(BF16), vhs[kj][h], preferred_element_type=F32)
                m = m_new
            res.append(acc)
        a0, a1 = res
        out = jnp.where(first, a0 / a0[:, dh:dh + 1], a1 / a1[:, 0:1])
        o_ref[0, qi * tile:(qi + 1) * tile, :] = out.astype(o_ref.dtype)


def _attn_prompt(q, k, v, c, *, tile, dh):
    b, t, da = q.shape
    hp = da // (2 * dh)
    nt = t // tile
    pair = pl.BlockSpec((1, t, 2 * dh), lambda bi, hi: (bi, 0, hi))
    return pl.pallas_call(
        functools.partial(_attn_body, tile=tile, dh=dh),
        grid=(b, hp),
        in_specs=[pair, pair, pair, pl.BlockSpec((1, 1, nt, 2, tile), lambda bi, hi: (bi, hi, 0, 0, 0))],
        out_specs=pair,
        out_shape=jax.ShapeDtypeStruct((b, t, da), BF16),
        compiler_params=_params("arbitrary", "arbitrary"),
        name="attn_prompt",
    )(q, k, v, c)


def _attn_sample_body(pt_ref, qrep_ref, q_ref, kn_ref, vrep_ref, lfn_ref, ck_hbm, cv_hbm, clf_hbm, o_ref,
                      m_ref, l_ref, acc_ref, srun_ref, kbuf, vbuf, lfbuf, sems):
    seq = pl.program_id(0)
    step = pl.program_id(1)
    n_steps = pl.num_programs(1)
    n_pages = pt_ref.shape[1]
    _, g_n, n_heads, dh, page = kbuf.shape
    d_att = n_heads * dh

    def fetch(slot, b, s):
        for g in range(g_n):
            pid = pt_ref[b, n_pages - 1 - (s * g_n + g)]
            pltpu.make_async_copy(ck_hbm.at[pid], kbuf.at[slot, g], sems.at[slot, 0]).start()
            pltpu.make_async_copy(cv_hbm.at[pid], vbuf.at[slot, g], sems.at[slot, 1]).start()
            pltpu.make_async_copy(clf_hbm.at[pid], lfbuf.at[slot, g], sems.at[slot, 2]).start()

    gstep = seq * n_steps + step
    slot = gstep % 2

    @pl.when(gstep == 0)
    def _():
        fetch(0, 0, 0)

    @pl.when(gstep + 1 < pl.num_programs(0) * n_steps)
    def _():
        last = step == n_steps - 1
        fetch(1 - slot, jnp.where(last, seq + 1, seq), jnp.where(last, 0, step + 1))

    pltpu.make_async_copy(ck_hbm.at[pl.ds(0, g_n)], kbuf.at[slot], sems.at[slot, 0]).wait()
    pltpu.make_async_copy(cv_hbm.at[pl.ds(0, g_n)], vbuf.at[slot], sems.at[slot, 1]).wait()
    pltpu.make_async_copy(clf_hbm.at[pl.ds(0, g_n)], lfbuf.at[slot], sems.at[slot, 2]).wait()
    k_refs = [kbuf.at[slot, g] for g in range(g_n)]
    v_refs = [vbuf.at[slot, g] for g in range(g_n)]
    lf_refs = [lfbuf.at[slot, g] for g in range(g_n)]

    @pl.when(step == 0)
    def _():
        s_new = jnp.sum(q_ref[0] * kn_ref[0], axis=-1, keepdims=True)
        m_ref[...] = jnp.broadcast_to(s_new, m_ref.shape)
        l_ref[...] = jnp.ones_like(l_ref)
        lane = lax.broadcasted_iota(I32, (d_att, page), 1)
        acc_ref[...] = jnp.where(lane == 0, vrep_ref[0], 0.0)
        srun_ref[...] = lfn_ref[0]

    qrep = qrep_ref[0]
    r = lax.broadcasted_iota(I32, (page, page), 0)
    c = lax.broadcasted_iota(I32, (page, page), 1)
    later = (r > c).astype(BF16)
    ones = jnp.ones((page, page), BF16)
    lf_all = jnp.concatenate([lf_refs[g][...] for g in range(g_n)], axis=0)
    suffix = _dot_exact_rhs(lf_all, later)
    total = _dot_exact_rhs(lf_all, ones)
    s_run = srun_ref[...]
    m_prev = m_ref[...]
    m_new = m_prev
    scores = []
    for g in range(g_n):
        kq = k_refs[g][...].reshape(d_att, page) * qrep
        s = jnp.sum(kq.reshape(n_heads, dh, page), axis=1)
        sb = s + s_run + suffix[g * n_heads:(g + 1) * n_heads]
        s_run = s_run + total[g * n_heads:(g + 1) * n_heads]
        scores.append(sb)
        m_new = jnp.maximum(m_new, jnp.max(sb, axis=-1, keepdims=True))
    srun_ref[...] = s_run
    alpha = jnp.exp(m_prev - m_new)
    l = alpha * l_ref[...]
    acc = acc_ref[...].reshape(n_heads, dh, page) * alpha[:, None, :]
    for g in range(g_n):
        p = jnp.exp(scores[g] - m_new)
        l = l + jnp.sum(p, axis=-1, keepdims=True)
        acc = acc + v_refs[g][...] * p[:, None, :]
    m_ref[...] = m_new
    l_ref[...] = l
    acc_ref[...] = acc.reshape(d_att, page)

    @pl.when(step == pl.num_programs(1) - 1)
    def _():
        o_ref[0] = jnp.sum(acc / l[:, None, :], axis=-1)


def _attn_sample(page_table, q, k_new, v_new, lf_new, cache_kt, cache_vt, cache_lft):
    db, n_pages = page_table.shape
    n_phys, n_heads, dh, page = cache_kt.shape
    d_att = n_heads * dh
    g_n = PAGES_PER_STEP
    while n_pages % g_n:
        g_n //= 2
    n_steps = n_pages // g_n
    lane_rep = lambda a: jnp.broadcast_to(a.reshape(db, -1, 1), (db, a.size // db, page))

    per_seq = lambda b, s, pt: (b, 0, 0)
    hbm = pl.BlockSpec(memory_space=pl.ANY)
    in_specs = [pl.BlockSpec((1, d_att, page), per_seq), pl.BlockSpec((1, n_heads, dh), per_seq),
                pl.BlockSpec((1, n_heads, dh), per_seq), pl.BlockSpec((1, d_att, page), per_seq),
                pl.BlockSpec((1, n_heads, page), per_seq), hbm, hbm, hbm]
    grid_spec = pltpu.PrefetchScalarGridSpec(
        num_scalar_prefetch=1, grid=(db, n_steps), in_specs=in_specs,
        out_specs=pl.BlockSpec((1, n_heads, dh), per_seq),
        scratch_shapes=[pltpu.VMEM((n_heads, page), F32), pltpu.VMEM((n_heads, page), F32),
                        pltpu.VMEM((d_att, page), F32), pltpu.VMEM((n_heads, page), F32),
                        pltpu.VMEM((2, g_n, n_heads, dh, page), F32), pltpu.VMEM((2, g_n, n_heads, dh, page), F32),
                        pltpu.VMEM((2, g_n, n_heads, page), F32), pltpu.SemaphoreType.DMA((2, 3))])
    return pl.pallas_call(
        _attn_sample_body,
        grid_spec=grid_spec,
        out_shape=jax.ShapeDtypeStruct((db, n_heads, dh), F32),
        compiler_params=_params("arbitrary", "arbitrary"),
        name="attn_sample",
    )(page_table, lane_rep(q), q.reshape(db, n_heads, dh), k_new.reshape(db, n_heads, dh), lane_rep(v_new),
      lane_rep(lf_new), cache_kt, cache_vt, cache_lft)


def _merge_and_route(x, pooled, att, gates, wp_ref, ps_ref, wup_ref, wua_ref, wo_ref, nf_ref, wr_ref, br_ref,
                     *, precise, n_groups, n_per_group):
    tm, d = x.shape
    gw = pooled[0].shape[1]
    mixed = jnp.concatenate([_dot(pooled[g], wp_ref[g], precise) for g in range(len(pooled))], axis=-1)
    pool_out = mixed * ps_ref[...]
    y = gates[:, :d].astype(F32) * _dot(pool_out, wup_ref[...], precise) \
        + gates[:, d:].astype(F32) * _dot(att, wua_ref[...], precise)
    x2 = x + _dot(y, wo_ref[...], precise)
    h2 = _rmsnorm(x2, nf_ref[...])
    logits = _dot(h2, wr_ref[...], precise) + br_ref[...]
    lane = lax.broadcasted_iota(I32, logits.shape, 1)
    lanef = lane.astype(F32)
    neg = -jnp.inf
    is_g = lane < n_groups
    gmax = jnp.max(jnp.where(is_g, logits, neg), axis=-1, keepdims=True)
    gidx = jnp.min(jnp.where(is_g & (logits == gmax), lanef, float(LANES)), axis=-1, keepdims=True)
    gsum = jnp.sum(jnp.where(is_g, jnp.exp(logits - gmax), 0.0), axis=-1, keepdims=True)
    g_w = 1.0 / gsum
    n_exp = n_groups * n_per_group
    exp_id = lanef - float(n_groups)
    in_sel = (lane >= n_groups) & (lane < n_groups + n_exp) & (jnp.floor(exp_id / n_per_group) == gidx)
    v1 = jnp.max(jnp.where(in_sel, logits, neg), axis=-1, keepdims=True)
    i1 = jnp.min(jnp.where(in_sel & (logits == v1), lanef, float(LANES)), axis=-1, keepdims=True)
    in_sel2 = in_sel & (lanef != i1)
    v2 = jnp.max(jnp.where(in_sel2, logits, neg), axis=-1, keepdims=True)
    i2 = jnp.min(jnp.where(in_sel2 & (logits == v2), lanef, float(LANES)), axis=-1, keepdims=True)
    t = jnp.exp(v2 - v1)
    w1 = g_w * (1.0 / (1.0 + t))
    w2 = g_w * (t / (1.0 + t))
    e1 = i1 - float(n_groups)
    e2 = i2 - float(n_groups)
    hit1 = lanef == e1
    hit2 = lanef == e2
    onehot = (hit1 | hit2).astype(BF16)
    rr = lax.broadcasted_iota(I32, (tm, tm), 0)
    cc = lax.broadcasted_iota(I32, (tm, tm), 1)
    incl = jnp.dot((cc <= rr).astype(BF16), onehot, preferred_element_type=F32)
    counts = incl[tm - 1:tm, :]
    groups = jnp.floor((counts + (ROW_GROUP - 1.0)) * (1.0 / ROW_GROUP))
    ur = lax.broadcasted_iota(I32, (LANES, LANES), 0)
    uc = lax.broadcasted_iota(I32, (LANES, LANES), 1)
    before = jnp.dot(jnp.broadcast_to(groups, (8, LANES)).astype(BF16), (ur < uc).astype(BF16),
                     preferred_element_type=F32)[0:1]
    seg_start = before * float(ROW_GROUP)
    pick = lambda hit, tbl: jnp.sum(jnp.where(hit, tbl, 0.0), axis=-1, keepdims=True)
    r1 = pick(hit1, incl) - 1.0
    r2 = pick(hit2, incl) - 1.0
    row1 = pick(hit1, seg_start) + r1
    row2 = pick(hit2, seg_start) + r2
    slab = jnp.zeros((tm, LANES), F32)
    for i, val in enumerate((e1, e2, r1, r2, w1, w2, row1, row2)):
        slab = jnp.where(lane == i, val, slab)
    return x2, h2, slab, counts


def _sorted_copy(h2, row1, row2, n_rows):
    tm = h2.shape[0]
    r = lax.broadcasted_iota(I32, (n_rows, tm), 0).astype(F32)
    place = ((r == row1) | (r == row2)).astype(BF16)
    return jnp.dot(place, h2.astype(BF16), preferred_element_type=F32)


def _merge_prompt_body(x_ref, u_ref, halo_ref, att_ref, gate_ref, wp_ref, ps_ref, wup_ref, wua_ref, wo_ref,
                       nf_ref, wr_ref, br_ref, x2_ref, xs_ref, slab_ref, route_ref, counts_ref, ext_ref,
                       *, seq_len, n_groups, n_per_group):
    i = pl.program_id(0)
    tm = x_ref.shape[0]
    gw = u_ref.shape[1] // len(POOL_WINDOWS)
    pos0 = (i * tm) % seq_len
    u = u_ref[...]
    ext_ref[0:POOL_HALO, :] = jnp.where(pos0 == 0, 0.0, halo_ref[...])
    ext_ref[POOL_HALO:, :] = u
    pos = pos0 + lax.broadcasted_iota(I32, (tm, 1), 0)
    pooled = []
    for g, w in enumerate(POOL_WINDOWS):
        lo = g * gw
        wsum = ext_ref[pl.ds(POOL_HALO, tm), lo:lo + gw]
        for j in range(1, w):
            wsum = wsum + ext_ref[pl.ds(POOL_HALO - j, tm), lo:lo + gw]
        count = jnp.minimum(pos + 1, w).astype(F32)
        pooled.append(wsum / count - u[:, lo:lo + gw])
    x2, h2, slab, counts = _merge_and_route(
        x_ref[...], pooled, att_ref[...], gate_ref[...], wp_ref, ps_ref, wup_ref, wua_ref, wo_ref, nf_ref,
        wr_ref, br_ref, precise=False, n_groups=n_groups, n_per_group=n_per_group)
    fields = slab.T[0:8, :]
    x2_ref[...] = x2
    xs_ref[...] = _sorted_copy(h2, fields[6:7, :], fields[7:8, :], xs_ref.shape[0])
    slab_ref[...] = slab
    route_ref[...] = fields
    counts_ref[0] = counts


def _merge_prompt(x, u, att, gates, wp, ps, wup, wua, wo, nf, wr, br, *, tm, seq_len, n_groups, n_per_group,
                  tile_rows):
    n, d = x.shape
    d_pool, d_att = u.shape[1], att.shape[1]
    row = lambda i: (i, 0)
    const = lambda i: (0, 0)
    const3 = lambda i: (0, 0, 0)
    halo = lambda i: (jnp.maximum(i * (tm // POOL_HALO) - 1, 0), 0)
    return pl.pallas_call(
        functools.partial(_merge_prompt_body, seq_len=seq_len, n_groups=n_groups, n_per_group=n_per_group),
        grid=(n // tm,),
        in_specs=[pl.BlockSpec((tm, d), row), pl.BlockSpec((tm, d_pool), row), pl.BlockSpec((POOL_HALO, d_pool), halo),
                  pl.BlockSpec((tm, d_att), row), pl.BlockSpec((tm, 2 * d), row),
                  pl.BlockSpec(wp.shape, const3), pl.BlockSpec((1, d_pool), const),
                  pl.BlockSpec(wup.shape, const), pl.BlockSpec(wua.shape, const), pl.BlockSpec(wo.shape, const),
                  pl.BlockSpec((1, d), const), pl.BlockSpec(wr.shape, const), pl.BlockSpec((1, LANES), const)],
        out_specs=[pl.BlockSpec((tm, d), row), pl.BlockSpec((tile_rows, d), row), pl.BlockSpec((tm, LANES), row),
                   pl.BlockSpec((8, tm), lambda i: (0, i)), pl.BlockSpec((1, 1, LANES), lambda i: (i, 0, 0))],
        out_shape=[jax.ShapeDtypeStruct((n, d), F32), jax.ShapeDtypeStruct((n // tm * tile_rows, d), F32),
                   jax.ShapeDtypeStruct((n, LANES), F32), jax.ShapeDtypeStruct((8, n), F32),
                   jax.ShapeDtypeStruct((n // tm, 1, LANES), F32)],
        scratch_shapes=[pltpu.VMEM((tm + POOL_HALO, d_pool), F32)],
        compiler_params=_params("arbitrary"),
        name="merge_prompt",
    )(x, u, u, att, gates, wp, ps, wup, wua, wo, nf, wr, br)


def _merge_sample_body(x_ref, u_ref, st_ref, att_ref, gate_ref, wp_ref, ps_ref, wup_ref, wua_ref, wo_ref,
                       nf_ref, wr_ref, br_ref, x2_ref, xs_ref, slab_ref, counts_ref,
                       *, start_pos, n_groups, n_per_group):
    u = u_ref[...]
    gw = u.shape[1] // len(POOL_WINDOWS)
    n_state = st_ref.shape[0]
    pooled = []
    for g, w in enumerate(POOL_WINDOWS):
        lo = g * gw
        wsum = u[:, lo:lo + gw]
        for j in range(1, w):
            wsum = wsum + st_ref[n_state - j][:, lo:lo + gw]
        pooled.append(wsum / float(min(start_pos + 1, w)) - u[:, lo:lo + gw])
    x2, h2, slab, counts = _merge_and_route(
        x_ref[...], pooled, att_ref[...], gate_ref[...], wp_ref, ps_ref, wup_ref, wua_ref, wo_ref, nf_ref,
        wr_ref, br_ref, precise=True, n_groups=n_groups, n_per_group=n_per_group)
    n = slab.shape[0]
    fields = jnp.concatenate([slab, jnp.zeros((LANES - n, LANES), F32)], axis=0).T
    x2_ref[...] = x2
    xs_ref[...] = _sorted_copy(h2, fields[6:7, 0:n], fields[7:8, 0:n], xs_ref.shape[0])
    slab_ref[...] = slab
    counts_ref[0] = counts


def _merge_sample(x, u, state_t, att, gates, wp, ps, wup, wua, wo, nf, wr, br, *, start_pos, n_groups,
                  n_per_group, tile_rows):
    n, d = x.shape
    assert n <= LANES
    return pl.pallas_call(
        functools.partial(_merge_sample_body, start_pos=start_pos, n_groups=n_groups, n_per_group=n_per_group),
        out_shape=[jax.ShapeDtypeStruct((n, d), F32), jax.ShapeDtypeStruct((tile_rows, d), F32),
                   jax.ShapeDtypeStruct((n, LANES), F32), jax.ShapeDtypeStruct((1, 1, LANES), F32)],
        compiler_params=_params(),
        name="merge_sample",
    )(x, u, state_t, att, gates, wp, ps, wup, wua, wo, nf, wr, br)


def _row_copy(src_ref, src_row, dst_ref, dst_row, sem):
    return pltpu.make_async_copy(src_ref.at[pl.ds(src_row, 1)], dst_ref.at[pl.ds(dst_row, 1)], sem)


def _moe_mm_body(te_ref, tw_ref, src_ref, xs_hbm, xs2_hbm, wg_ref, wu_ref, wd_ref, ys_ref, xbuf, wgb_ref, wub_ref,
                 wdb_ref, sems):
    i = pl.program_id(0)
    tm = ys_ref.shape[0]
    groups = tm // ROW_GROUP
    expert = te_ref[i]
    prev = te_ref[jnp.maximum(i - 1, 0)]

    split = xs_hbm.shape[0]

    def fetch(slot, tile):
        for k in range(groups):
            src = pl.multiple_of(src_ref[tile * groups + k], ROW_GROUP)
            dst = xbuf.at[slot, pl.ds(k * ROW_GROUP, ROW_GROUP)]

            @pl.when(src < split)
            def _():
                pltpu.make_async_copy(xs_hbm.at[pl.ds(src, ROW_GROUP)], dst, sems.at[slot]).start()

            @pl.when(src >= split)
            def _():
                pltpu.make_async_copy(xs2_hbm.at[pl.ds(src - split, ROW_GROUP)], dst, sems.at[slot]).start()

    slot = i % 2

    @pl.when(i == 0)
    def _():
        fetch(0, 0)

    @pl.when(i + 1 < pl.num_programs(0))
    def _():
        fetch(1 - slot, i + 1)

    pltpu.make_async_copy(xs_hbm.at[pl.ds(0, tm)], xbuf.at[slot], sems.at[slot]).wait()

    @pl.when((expert >= 0) & ((i == 0) | (expert != prev)))
    def _():
        wgb_ref[...] = wg_ref[0].astype(BF16)
        wub_ref[...] = wu_ref[0].astype(BF16)
        wdb_ref[...] = wd_ref[0].astype(BF16)

    @pl.when(expert >= 0)
    def _():
        x = xbuf[slot].astype(BF16)
        a = jnp.dot(x, wgb_ref[...], preferred_element_type=F32)
        b = jnp.dot(x, wub_ref[...], preferred_element_type=F32)
        hdn = (a * _sigmoid(a)) * b
        ys_ref[...] = jnp.dot(hdn.astype(BF16), wdb_ref[...], preferred_element_type=F32)

    @pl.when(expert < 0)
    def _():
        ys_ref[...] = jnp.zeros_like(ys_ref)


def _moe_mm(tile_expert, tile_weight, group_src, xs, xs2, w_gate, w_up, w_down, *, tm):
    n_tiles = tile_expert.shape[0]
    d = xs.shape[1]
    n_exp, _, de = w_gate.shape
    wmap = lambda i, te, tw, src: (tw[i], 0, 0)
    grid_spec = pltpu.PrefetchScalarGridSpec(
        num_scalar_prefetch=3, grid=(n_tiles,),
        in_specs=[pl.BlockSpec(memory_space=pl.ANY), pl.BlockSpec(memory_space=pl.ANY),
                  pl.BlockSpec((1, d, de), wmap), pl.BlockSpec((1, d, de), wmap), pl.BlockSpec((1, de, d), wmap)],
        out_specs=pl.BlockSpec((tm, d), lambda i, te, tw, src: (i, 0)),
        scratch_shapes=[pltpu.VMEM((2, tm, d), F32), pltpu.VMEM((d, de), BF16), pltpu.VMEM((d, de), BF16),
                        pltpu.VMEM((de, d), BF16), pltpu.SemaphoreType.DMA((2,))])
    return pl.pallas_call(
        _moe_mm_body, grid_spec=grid_spec,
        out_shape=jax.ShapeDtypeStruct((n_tiles * tm, d), F32),
        compiler_params=_params("arbitrary"),
        name="moe_mm",
    )(tile_expert, tile_weight, group_src, xs, xs2, w_gate, w_up, w_down)


def _moe_combine_body(base_ref, e1_ref, e2_ref, r1_ref, r2_ref, x_ref, slab_ref, g_ref, ys_ref, o_ref,
                      ya_ref, yb_ref, sem, *, final_norm, first_tile, n_exp):
    n = x_ref.shape[0]
    seg = (first_tile + pl.program_id(0)) * n_exp

    def start(i, carry):
        for j in range(ROW_DMA_UNROLL):
            t = i * ROW_DMA_UNROLL + j
            _row_copy(ys_ref, base_ref[seg + e1_ref[t]] * ROW_GROUP + r1_ref[t], ya_ref, t, sem).start()
            _row_copy(ys_ref, base_ref[seg + e2_ref[t]] * ROW_GROUP + r2_ref[t], yb_ref, t, sem).start()
        return carry

    lax.fori_loop(0, n // ROW_DMA_UNROLL, start, 0)
    pltpu.make_async_copy(ys_ref.at[pl.ds(0, n)], ya_ref, sem).wait()
    pltpu.make_async_copy(ys_ref.at[pl.ds(0, n)], yb_ref, sem).wait()
    slab = slab_ref[...]
    out = x_ref[...] + (slab[:, 4:5] * ya_ref[...] + slab[:, 5:6] * yb_ref[...])
    if final_norm:
        out = _rmsnorm(out, g_ref[...])
    o_ref[...] = out


def _moe_combine(seg_base, e1, e2, r1, r2, x, slab, g, ys, *, ts, final_norm, first_tile, n_exp):
    n, d = x.shape
    smem = lambda: pl.BlockSpec((ts,), lambda i: (i,), memory_space=pltpu.SMEM)
    return pl.pallas_call(
        functools.partial(_moe_combine_body, final_norm=final_norm, first_tile=first_tile, n_exp=n_exp),
        grid=(n // ts,),
        in_specs=[pl.BlockSpec(memory_space=pltpu.SMEM), smem(), smem(), smem(), smem(),
                  pl.BlockSpec((ts, d), lambda i: (i, 0)), pl.BlockSpec((ts, LANES), lambda i: (i, 0)),
                  pl.BlockSpec((1, d), lambda i: (0, 0)), pl.BlockSpec(memory_space=pl.ANY)],
        out_specs=pl.BlockSpec((ts, d), lambda i: (i, 0)),
        out_shape=jax.ShapeDtypeStruct((n, d), F32),
        scratch_shapes=[pltpu.VMEM((ts, d), F32), pltpu.VMEM((ts, d), F32), pltpu.SemaphoreType.DMA],
        compiler_params=_params("arbitrary"),
        name="moe_combine",
    )(seg_base, e1, e2, r1, r2, x, slab, g, ys)


def kernel(x_prompt, x_sample, cache_k, cache_v, cache_logf, state_pool, page_table, norm_mix, w_in, b_forget,
           w_pool, pool_scale, w_up_pool, w_up_att, w_out, norm_ffn, w_router_group, b_router_group,
           w_router_expert, b_router_expert, w_gate, w_up, w_down, norm_final):
    depth = norm_mix.shape[0]
    assert depth == 1, "single trunk layer"
    b, t, d = x_prompt.shape
    db, dt, _ = x_sample.shape
    assert dt == 1, "one sample token per sequence"
    _, n_phys, page, n_heads, dh = cache_k.shape
    n_pages = page_table.shape[1]
    past = n_pages * page
    n_state, d_pool = state_pool.shape[2], state_pool.shape[3]
    d_att = n_heads * dh
    n_pool_groups = w_pool.shape[1]
    assert n_pool_groups == len(POOL_WINDOWS) and d_pool // n_pool_groups == LANES
    assert n_state == max(POOL_WINDOWS) - 1 and n_state < POOL_HALO
    n_groups, n_per_group = w_router_expert.shape[1], w_router_expert.shape[3]
    n_exp = n_groups * n_per_group
    assert n_groups + n_exp <= LANES and 2 * dh == LANES and n_heads % 2 == 0
    n = b * t
    q_scale = float(dh) ** -0.5
    tm = min(TOKEN_TILE, t)
    assert t % tm == 0 and t % ATTN_TILE == 0

    o_main = d_pool + 3 * d_att
    wi = w_in[0]
    wm_f, wf_f, wg_f = wi[:, :o_main], wi[:, o_main:o_main + n_heads], wi[:, o_main + n_heads:]
    wf_pad = jnp.pad(wf_f, ((0, 0), (0, LANES - n_heads)))
    wit = jnp.transpose(wi)
    wmt_f, wgt_f = wit[:o_main], wit[o_main + n_heads:]
    wft_pad = jnp.pad(wit[o_main:o_main + n_heads], ((0, LANES - n_heads), (0, 0)))
    bf_pad = jnp.pad(b_forget[0], (0, LANES - n_heads)).reshape(1, LANES)
    g_mix = norm_mix[0].reshape(1, d)
    g_ffn = norm_ffn[0].reshape(1, d)
    g_fin = norm_final.reshape(1, d)
    ps = pool_scale[0].reshape(1, d_pool)
    wr_f = jnp.concatenate([w_router_group[0], jnp.transpose(w_router_expert[0], (1, 0, 2)).reshape(d, n_exp)], axis=1)
    wr_pad = jnp.pad(wr_f, ((0, 0), (0, LANES - n_groups - n_exp)))
    br_pad = jnp.pad(jnp.concatenate([b_router_group[0], b_router_expert[0].reshape(n_exp)]),
                     (0, LANES - n_groups - n_exp)).reshape(1, LANES)
    bf = lambda a: a.astype(BF16)

    xp = x_prompt.reshape(n, d)
    u_p, q_p, kt_p, vt_p, kb_p, vb_p, lft_p, gate_p = _proj_prompt(
        xp, g_mix, bf(wm_f), bf(wf_pad), bf(wg_f), bf_pad, tm=tm, seq_len=t, d_pool=d_pool, d_att=d_att,
        n_heads=n_heads, q_scale=q_scale)
    c = _cumsum_lanes(lft_p.reshape(b * n_heads, t))
    nt = t // ATTN_TILE
    c_blk = jnp.transpose(c.reshape(b, n_heads // 2, 2, nt, ATTN_TILE), (0, 1, 3, 2, 4))
    att_p = _attn_prompt(q_p.reshape(b, t, d_att), kb_p.reshape(b, t, d_att), vb_p.reshape(b, t, d_att), c_blk,
                         tile=ATTN_TILE, dh=dh)
    n_ptiles = n // tm
    n_ttiles = n_ptiles + 1
    tile_rows = -(-(2 * tm + n_exp * (ROW_GROUP - 1)) // MOE_ROW_TILE) * MOE_ROW_TILE
    assert 2 * db + n_exp * (ROW_GROUP - 1) <= tile_rows - ROW_GROUP, "the sample tile must end in an unused row group"
    x2_p, xs_rows, slab_p, route_p, counts_p = _merge_prompt(
        xp, u_p, att_p.reshape(n, d_att), gate_p, bf(w_pool[0]), ps, bf(w_up_pool[0]), bf(w_up_att[0]),
        bf(w_out[0]), g_ffn, bf(wr_pad), br_pad, tm=tm, seq_len=t, n_groups=n_groups, n_per_group=n_per_group,
        tile_rows=tile_rows)

    xs = x_sample.reshape(db, d)
    z_s, lf_s, gate_s = _proj_sample(xs, g_mix, wmt_f, wft_pad, wgt_f, bf_pad, tn=512)
    u_s = z_s[:, :d_pool]
    q_s = z_s[:, d_pool:d_pool + d_att] * q_scale
    k_s = z_s[:, d_pool + d_att:d_pool + 2 * d_att]
    v_s = z_s[:, d_pool + 2 * d_att:]
    att_s = _attn_sample(page_table, q_s, k_s, v_s, lf_s[:, :n_heads],
                         jnp.transpose(cache_k[0], (0, 2, 3, 1)), jnp.transpose(cache_v[0], (0, 2, 3, 1)),
                         jnp.transpose(cache_logf[0], (0, 2, 1)))
    state_t = jnp.transpose(state_pool[0], (1, 0, 2))
    x2_s, xs_rows_s, slab_s, counts_s = _merge_sample(
        xs, u_s, state_t, att_s.reshape(db, d_att), gate_s, w_pool[0], ps, w_up_pool[0], w_up_att[0], w_out[0],
        g_ffn, wr_pad, br_pad, start_pos=past, n_groups=n_groups, n_per_group=n_per_group, tile_rows=tile_rows)

    tmm = MOE_ROW_TILE
    gpt = tmm // ROW_GROUP
    n_tok = n + db
    n_groups_max = -(-(2 * n_tok) // ROW_GROUP) + n_ttiles * n_exp + n_exp * (gpt - 1)
    n_mm_tiles = -(-n_groups_max // gpt)
    cnt = jnp.concatenate([counts_p, counts_s], axis=0)[:, 0, :n_exp].astype(I32)
    seg_groups = (cnt + ROW_GROUP - 1) // ROW_GROUP
    seg_local = jnp.cumsum(seg_groups, axis=1) - seg_groups
    seg_end = jnp.cumsum(seg_groups, axis=0)
    exp_groups = seg_end[-1]
    exp_padded = ((exp_groups + gpt - 1) // gpt) * gpt
    exp_end = jnp.cumsum(exp_padded)
    exp_start = exp_end - exp_padded
    seg_base = (exp_start[None, :] + seg_end - seg_groups).reshape(-1).astype(I32)
    g_idx = jnp.arange(n_mm_tiles * gpt, dtype=I32)
    g_exp = jnp.sum((g_idx[:, None] >= exp_end[None, :]).astype(I32), axis=1)
    g_expc = jnp.minimum(g_exp, n_exp - 1)
    g_in_exp = g_idx - exp_start[g_expc]
    g_valid = (g_exp < n_exp) & (g_in_exp < exp_groups[g_expc])
    g_tile = jnp.minimum(jnp.sum((g_in_exp[:, None] >= seg_end.T[g_expc]).astype(I32), axis=1), n_ttiles - 1)
    g_in_seg = g_in_exp - (seg_end - seg_groups)[g_tile, g_expc]
    g_src = g_tile * tile_rows + (seg_local[g_tile, g_expc] + g_in_seg) * ROW_GROUP
    zero_group = n_ttiles * tile_rows - ROW_GROUP
    group_src = jnp.where(g_valid, g_src, zero_group).astype(I32)
    tile_e = g_exp[::gpt]
    tile_expert = jnp.where(tile_e < n_exp, tile_e, -1).astype(I32)
    tile_weight = jnp.minimum(tile_e, n_exp - 1).astype(I32)
    fields_p = route_p[:4].astype(I32)
    fields_s = jnp.transpose(slab_s[:, :4]).astype(I32)
    ys_rows = _moe_mm(tile_expert, tile_weight, group_src, xs_rows, xs_rows_s, w_gate[0], w_up[0], w_down[0], tm=tmm)
    y_prompt = _moe_combine(seg_base, *fields_p, x2_p, slab_p, g_fin, ys_rows, ts=tm, final_norm=True,
                            first_tile=0, n_exp=n_exp)
    y_sample = _moe_combine(seg_base, *fields_s, x2_s, slab_s, g_fin, ys_rows, ts=db, final_norm=True,
                            first_tile=n_ptiles, n_exp=n_exp)

    new_pool_p = u_p.reshape(b, t, d_pool)[:, t - n_state:, :]
    new_pool_s = jnp.concatenate([state_pool[0][:, 1:, :], u_s[:, None, :]], axis=1)
    to_heads = lambda a: jnp.transpose(a.reshape(b, n_heads, dh, t), (0, 3, 1, 2))[None]
    return (y_prompt.reshape(b, t, d), y_sample.reshape(db, 1, d),
            to_heads(kt_p), to_heads(vt_p), jnp.transpose(lft_p, (0, 2, 1))[None],
            new_pool_p[None],
            k_s.reshape(1, db, 1, n_heads, dh), v_s.reshape(1, db, 1, n_heads, dh),
            lf_s[:, :n_heads].reshape(1, db, 1, n_heads), new_pool_s[None])
```

```python
import functools

import jax
import jax.numpy as jnp
from jax import lax
from jax.experimental import pallas as pl
from jax.experimental.pallas import tpu as pltpu

F32 = jnp.float32
BF16 = jnp.bfloat16
I32 = jnp.int32
HIGHEST = lax.Precision.HIGHEST

RMS_EPS = 1e-6
POOL_WINDOWS = (2, 4, 8, 16)
POOL_HALO = 16
LANES = 128
VMEM_LIMIT_BYTES = 56 * 1024 * 1024

TOKEN_TILE = 512
ATTN_TILE = 512
MOE_ROW_TILE = 256
PAGES_PER_STEP = 16
ROW_GROUP = 8
ROW_DMA_UNROLL = 8


def _params(*sem):
    return pltpu.CompilerParams(dimension_semantics=sem, vmem_limit_bytes=VMEM_LIMIT_BYTES)


def _rmsnorm(x, g):
    return x * lax.rsqrt(jnp.mean(x * x, axis=-1, keepdims=True) + RMS_EPS) * g


def _log_sigmoid(x):
    return jnp.minimum(x, 0.0) - jnp.log1p(jnp.exp(-jnp.abs(x)))


def _sigmoid(x):
    return 1.0 / (1.0 + jnp.exp(-x))


def _dot(a, b, precise):
    if precise:
        return jnp.dot(a.astype(F32), b.astype(F32), precision=HIGHEST, preferred_element_type=F32)
    return jnp.dot(a.astype(BF16), b.astype(BF16), preferred_element_type=F32)


def _split3(x):
    hi = x.astype(BF16)
    r = x - hi.astype(F32)
    mid = r.astype(BF16)
    lo = (r - mid.astype(F32)).astype(BF16)
    return hi, mid, lo


def _dot_exact_rhs(x, w_bf16):
    hi, mid, lo = _split3(x)
    d = lambda a: jnp.dot(a, w_bf16, preferred_element_type=F32)
    return d(hi) + d(mid) + d(lo)


def _proj_body(x_ref, g_ref, wm_ref, wf_ref, wg_ref, bf_ref,
               u_ref, q_ref, kt_ref, vt_ref, kb_ref, vb_ref, lft_ref, gate_ref, *, d_pool, d_att, n_heads, q_scale):
    h = _rmsnorm(x_ref[...], g_ref[...]).astype(BF16)
    z = jnp.dot(h, wm_ref[...], preferred_element_type=F32)
    o1, o2, o3 = d_pool, d_pool + d_att, d_pool + 2 * d_att
    u_ref[...] = z[:, :o1]
    q_ref[...] = (z[:, o1:o2] * q_scale).astype(BF16)
    k = z[:, o2:o3]
    v = z[:, o3:]
    kt_ref[0] = k.T
    vt_ref[0] = v.T
    kb_ref[...] = k.astype(BF16)
    vb_ref[...] = v.astype(BF16)
    lf = _log_sigmoid(jnp.dot(h, wf_ref[...], preferred_element_type=F32) + bf_ref[...])
    lft_ref[0] = lf.T[0:n_heads, :]
    gate_ref[...] = _sigmoid(jnp.dot(h, wg_ref[...], preferred_element_type=F32)).astype(BF16)


def _proj_prompt(x, g, wm, wf, wg, bfp, *, tm, seq_len, d_pool, d_att, n_heads, q_scale):
    n, d = x.shape
    b = n // seq_len
    tps = seq_len // tm
    row = lambda i: (i, 0)
    const = lambda i: (0, 0)
    tmin = lambda i: (i // tps, 0, i % tps)
    dg = wg.shape[1]
    out_shape = [
        jax.ShapeDtypeStruct((n, d_pool), F32), jax.ShapeDtypeStruct((n, d_att), BF16),
        jax.ShapeDtypeStruct((b, d_att, seq_len), F32), jax.ShapeDtypeStruct((b, d_att, seq_len), F32),
        jax.ShapeDtypeStruct((n, d_att), BF16), jax.ShapeDtypeStruct((n, d_att), BF16),
        jax.ShapeDtypeStruct((b, n_heads, seq_len), F32), jax.ShapeDtypeStruct((n, dg), BF16),
    ]
    return pl.pallas_call(
        functools.partial(_proj_body, d_pool=d_pool, d_att=d_att, n_heads=n_heads, q_scale=q_scale),
        grid=(n // tm,),
        in_specs=[pl.BlockSpec((tm, d), row), pl.BlockSpec((1, d), const),
                  pl.BlockSpec(wm.shape, const), pl.BlockSpec(wf.shape, const),
                  pl.BlockSpec(wg.shape, const), pl.BlockSpec((1, LANES), const)],
        out_specs=[pl.BlockSpec((tm, d_pool), row), pl.BlockSpec((tm, d_att), row),
                   pl.BlockSpec((1, d_att, tm), tmin), pl.BlockSpec((1, d_att, tm), tmin),
                   pl.BlockSpec((tm, d_att), row), pl.BlockSpec((tm, d_att), row),
                   pl.BlockSpec((1, n_heads, tm), tmin), pl.BlockSpec((tm, dg), row)],
        out_shape=out_shape,
        compiler_params=_params("arbitrary"),
        name="proj_prompt",
    )(x, g, wm, wf, wg, bfp)


def _dot_nt(a, bt, precise):
    dims = (((1,), (1,)), ((), ()))
    if precise:
        return lax.dot_general(a.astype(F32), bt.astype(F32), dims, precision=HIGHEST, preferred_element_type=F32)
    return lax.dot_general(a.astype(BF16), bt.astype(BF16), dims, preferred_element_type=F32)


def _proj_sample_body(x_ref, g_ref, wmt_ref, wft_ref, wgt_ref, bf_ref, z_ref, lf_ref, gate_ref):
    h = _rmsnorm(x_ref[...], g_ref[...])
    z_ref[...] = _dot_nt(h, wmt_ref[...], True)
    lf_ref[...] = _log_sigmoid(_dot_nt(h, wft_ref[...], True) + bf_ref[...])
    gate_ref[...] = _sigmoid(_dot_nt(h, wgt_ref[...], True))


def _proj_sample(x, g, wmt, wft, wgt, bfp, *, tn):
    n, d = x.shape
    dm, dg = wmt.shape[0], wgt.shape[0]
    assert dm == dg
    const = lambda j: (0, 0)
    chunk = lambda j: (j, 0)
    col = lambda j: (0, j)
    return pl.pallas_call(
        _proj_sample_body,
        grid=(dm // tn,),
        in_specs=[pl.BlockSpec((n, d), const), pl.BlockSpec((1, d), const),
                  pl.BlockSpec((tn, d), chunk), pl.BlockSpec(wft.shape, const),
                  pl.BlockSpec((tn, d), chunk), pl.BlockSpec((1, LANES), const)],
        out_specs=[pl.BlockSpec((n, tn), col), pl.BlockSpec((n, LANES), const), pl.BlockSpec((n, tn), col)],
        out_shape=[jax.ShapeDtypeStruct((n, dm), F32), jax.ShapeDtypeStruct((n, LANES), F32),
                   jax.ShapeDtypeStruct((n, dg), F32)],
        compiler_params=_params("arbitrary"),
        name="proj_sample",
    )(x, g, wmt, wft, wgt, bfp)


def _cumsum_body(x_ref, o_ref):
    c = x_ref[...]
    lane = lax.broadcasted_iota(I32, c.shape, 1)
    s = 1
    while s < c.shape[1]:
        c = c + jnp.where(lane >= s, pltpu.roll(c, s, 1), 0.0)
        s *= 2
    o_ref[...] = c


def _cumsum_lanes(x):
    return pl.pallas_call(_cumsum_body, out_shape=jax.ShapeDtypeStruct(x.shape, F32),
                          compiler_params=_params(), name="cumsum_logf")(x)


def _attn_body(q_ref, k_ref, v_ref, c_ref, o_ref, *, tile, dh):
    nt = q_ref.shape[1] // tile
    lane = lax.broadcasted_iota(I32, (tile, 2 * dh), 1)
    first = lane < dh
    row = lax.broadcasted_iota(I32, (tile, tile), 0)
    col = lax.broadcasted_iota(I32, (tile, tile), 1)
    causal = col <= row
    one = jnp.ones((tile, 2 * dh), BF16)
    kts, vhs = [], []
    for kj in range(nt):
        vt = v_ref[0, kj * tile:(kj + 1) * tile, :]
        kts.append(k_ref[0, kj * tile:(kj + 1) * tile, :])
        vhs.append((jnp.where(first, vt, one), jnp.where(first, one, vt)))
    for qi in range(nt):
        q = q_ref[0, qi * tile:(qi + 1) * tile, :]
        zero = jnp.zeros_like(q)
        q_heads = (jnp.where(first, q, zero), jnp.where(first, zero, q))
        res = []
        for h in range(2):
            m = jnp.full((tile, 1), -1e30, F32)
            acc = jnp.zeros((tile, 2 * dh), F32)
            for kj in range(qi + 1):
                s = lax.dot_general(q_heads[h], kts[kj], (((1,), (1,)), ((), ())), preferred_element_type=F32)
                s = s - c_ref[0, 0, kj][h:h + 1, :]
                if kj == qi:
                    s = jnp.where(causal, s, -jnp.inf)
                m_new = jnp.maximum(m, jnp.max(s, axis=-1, keepdims=True))
                alpha = jnp.exp(m - m_new)
                p = jnp.exp(s - m_new)
                acc = alpha * acc + jnp.dot(p.astype(BF16), vhs[kj][h], preferred_element_type=F32)
                m = m_new
            res.append(acc)
        a0, a1 = res
        out = jnp.where(first, a0 / a0[:, dh:dh + 1], a1 / a1[:, 0:1])
        o_ref[0, qi * tile:(qi + 1) * tile, :] = out.astype(o_ref.dtype)


def _attn_prompt(q, k, v, c, *, tile, dh):
    b, t, da = q.shape
    hp = da // (2 * dh)
    nt = t // tile
    pair = pl.BlockSpec((1, t, 2 * dh), lambda bi, hi: (bi, 0, hi))
    return pl.pallas_call(
        functools.partial(_attn_body, tile=tile, dh=dh),
        grid=(b, hp),
        in_specs=[pair, pair, pair, pl.BlockSpec((1, 1, nt, 2, tile), lambda bi, hi: (bi, hi, 0, 0, 0))],
        out_specs=pair,
        out_shape=jax.ShapeDtypeStruct((b, t, da), BF16),
        compiler_params=_params("arbitrary", "arbitrary"),
        name="attn_prompt",
    )(q, k, v, c)


def _attn_sample_body(pt_ref, qrep_ref, q_ref, kn_ref, vrep_ref, lfn_ref, ck_hbm, cv_hbm, clf_hbm, o_ref,
                      m_ref, l_ref, acc_ref, srun_ref, kbuf, vbuf, lfbuf, sems):
    seq = pl.program_id(0)
    step = pl.program_id(1)
    n_steps = pl.num_programs(1)
    n_pages = pt_ref.shape[1]
    _, g_n, n_heads, dh, page = kbuf.shape
    d_att = n_heads * dh

    def fetch(slot, b, s):
        for g in range(g_n):
            pid = pt_ref[b, n_pages - 1 - (s * g_n + g)]
            pltpu.make_async_copy(ck_hbm.at[pid], kbuf.at[slot, g], sems.at[slot, 0]).start()
            pltpu.make_async_copy(cv_hbm.at[pid], vbuf.at[slot, g], sems.at[slot, 1]).start()
            pltpu.make_async_copy(clf_hbm.at[pid], lfbuf.at[slot, g], sems.at[slot, 2]).start()

    gstep = seq * n_steps + step
    slot = gstep % 2

    @pl.when(gstep == 0)
    def _():
        fetch(0, 0, 0)

    @pl.when(gstep + 1 < pl.num_programs(0) * n_steps)
    def _():
        last = step == n_steps - 1
        fetch(1 - slot, jnp.where(last, seq + 1, seq), jnp.where(last, 0, step + 1))

    pltpu.make_async_copy(ck_hbm.at[pl.ds(0, g_n)], kbuf.at[slot], sems.at[slot, 0]).wait()
    pltpu.make_async_copy(cv_hbm.at[pl.ds(0, g_n)], vbuf.at[slot], sems.at[slot, 1]).wait()
    pltpu.make_async_copy(clf_hbm.at[pl.ds(0, g_n)], lfbuf.at[slot], sems.at[slot, 2]).wait()
    k_refs = [kbuf.at[slot, g] for g in range(g_n)]
    v_refs = [vbuf.at[slot, g] for g in range(g_n)]
    lf_refs = [lfbuf.at[slot, g] for g in range(g_n)]

    @pl.when(step == 0)
    def _():
        s_new = jnp.sum(q_ref[0] * kn_ref[0], axis=-1, keepdims=True)
        m_ref[...] = jnp.broadcast_to(s_new, m_ref.shape)
        l_ref[...] = jnp.ones_like(l_ref)
        lane = lax.broadcasted_iota(I32, (d_att, page), 1)
        acc_ref[...] = jnp.where(lane == 0, vrep_ref[0], 0.0)
        srun_ref[...] = lfn_ref[0]

    qrep = qrep_ref[0]
    r = lax.broadcasted_iota(I32, (page, page), 0)
    c = lax.broadcasted_iota(I32, (page, page), 1)
    later = (r > c).astype(BF16)
    ones = jnp.ones((page, page), BF16)
    lf_all = jnp.concatenate([lf_refs[g][...] for g in range(g_n)], axis=0)
    suffix = _dot_exact_rhs(lf_all, later)
    total = _dot_exact_rhs(lf_all, ones)
    s_run = srun_ref[...]
    m_prev = m_ref[...]
    m_new = m_prev
    scores = []
    for g in range(g_n):
        kq = k_refs[g][...].reshape(d_att, page) * qrep
        s = jnp.sum(kq.reshape(n_heads, dh, page), axis=1)
        sb = s + s_run + suffix[g * n_heads:(g + 1) * n_heads]
        s_run = s_run + total[g * n_heads:(g + 1) * n_heads]
        scores.append(sb)
        m_new = jnp.maximum(m_new, jnp.max(sb, axis=-1, keepdims=True))
    srun_ref[...] = s_run
    alpha = jnp.exp(m_prev - m_new)
    l = alpha * l_ref[...]
    acc = acc_ref[...].reshape(n_heads, dh, page) * alpha[:, None, :]
    for g in range(g_n):
        p = jnp.exp(scores[g] - m_new)
        l = l + jnp.sum(p, axis=-1, keepdims=True)
        acc = acc + v_refs[g][...] * p[:, None, :]
    m_ref[...] = m_new
    l_ref[...] = l
    acc_ref[...] = acc.reshape(d_att, page)

    @pl.when(step == pl.num_programs(1) - 1)
    def _():
        o_ref[0] = jnp.sum(acc / l[:, None, :], axis=-1)


def _attn_sample(page_table, q, k_new, v_new, lf_new, cache_kt, cache_vt, cache_lft):
    db, n_pages = page_table.shape
    n_phys, n_heads, dh, page = cache_kt.shape
    d_att = n_heads * dh
    g_n = PAGES_PER_STEP
    while n_pages % g_n:
        g_n //= 2
    n_steps = n_pages // g_n
    lane_rep = lambda a: jnp.broadcast_to(a.reshape(db, -1, 1), (db, a.size // db, page))

    per_seq = lambda b, s, pt: (b, 0, 0)
    hbm = pl.BlockSpec(memory_space=pl.ANY)
    in_specs = [pl.BlockSpec((1, d_att, page), per_seq), pl.BlockSpec((1, n_heads, dh), per_seq),
                pl.BlockSpec((1, n_heads, dh), per_seq), pl.BlockSpec((1, d_att, page), per_seq),
                pl.BlockSpec((1, n_heads, page), per_seq), hbm, hbm, hbm]
    grid_spec = pltpu.PrefetchScalarGridSpec(
        num_scalar_prefetch=1, grid=(db, n_steps), in_specs=in_specs,
        out_specs=pl.BlockSpec((1, n_heads, dh), per_seq),
        scratch_shapes=[pltpu.VMEM((n_heads, page), F32), pltpu.VMEM((n_heads, page), F32),
                        pltpu.VMEM((d_att, page), F32), pltpu.VMEM((n_heads, page), F32),
                        pltpu.VMEM((2, g_n, n_heads, dh, page), F32), pltpu.VMEM((2, g_n, n_heads, dh, page), F32),
                        pltpu.VMEM((2, g_n, n_heads, page), F32), pltpu.SemaphoreType.DMA((2, 3))])
    return pl.pallas_call(
        _attn_sample_body,
        grid_spec=grid_spec,
        out_shape=jax.ShapeDtypeStruct((db, n_heads, dh), F32),
        compiler_params=_params("arbitrary", "arbitrary"),
        name="attn_sample",
    )(page_table, lane_rep(q), q.reshape(db, n_heads, dh), k_new.reshape(db, n_heads, dh), lane_rep(v_new),
      lane_rep(lf_new), cache_kt, cache_vt, cache_lft)


def _merge_and_route(x, pooled, att, gates, wp_ref, ps_ref, wup_ref, wua_ref, wo_ref, nf_ref, wr_ref, br_ref,
                     *, precise, n_groups, n_per_group):
    tm, d = x.shape
    gw = pooled[0].shape[1]
    mixed = jnp.concatenate([_dot(pooled[g], wp_ref[g], precise) for g in range(len(pooled))], axis=-1)
    pool_out = mixed * ps_ref[...]
    y = gates[:, :d].astype(F32) * _dot(pool_out, wup_ref[...], precise) \
        + gates[:, d:].astype(F32) * _dot(att, wua_ref[...], precise)
    x2 = x + _dot(y, wo_ref[...], precise)
    h2 = _rmsnorm(x2, nf_ref[...])
    logits = _dot(h2, wr_ref[...], precise) + br_ref[...]
    lane = lax.broadcasted_iota(I32, logits.shape, 1)
    lanef = lane.astype(F32)
    neg = -jnp.inf
    is_g = lane < n_groups
    gmax = jnp.max(jnp.where(is_g, logits, neg), axis=-1, keepdims=True)
    gidx = jnp.min(jnp.where(is_g & (logits == gmax), lanef, float(LANES)), axis=-1, keepdims=True)
    gsum = jnp.sum(jnp.where(is_g, jnp.exp(logits - gmax), 0.0), axis=-1, keepdims=True)
    g_w = 1.0 / gsum
    n_exp = n_groups * n_per_group
    exp_id = lanef - float(n_groups)
    in_sel = (lane >= n_groups) & (lane < n_groups + n_exp) & (jnp.floor(exp_id / n_per_group) == gidx)
    v1 = jnp.max(jnp.where(in_sel, logits, neg), axis=-1, keepdims=True)
    i1 = jnp.min(jnp.where(in_sel & (logits == v1), lanef, float(LANES)), axis=-1, keepdims=True)
    in_sel2 = in_sel & (lanef != i1)
    v2 = jnp.max(jnp.where(in_sel2, logits, neg), axis=-1, keepdims=True)
    i2 = jnp.min(jnp.where(in_sel2 & (logits == v2), lanef, float(LANES)), axis=-1, keepdims=True)
    t = jnp.exp(v2 - v1)
    w1 = g_w * (1.0 / (1.0 + t))
    w2 = g_w * (t / (1.0 + t))
    e1 = i1 - float(n_groups)
    e2 = i2 - float(n_groups)
    hit1 = lanef == e1
    hit2 = lanef == e2
    onehot = (hit1 | hit2).astype(BF16)
    rr = lax.broadcasted_iota(I32, (tm, tm), 0)
    cc = lax.broadcasted_iota(I32, (tm, tm), 1)
    incl = jnp.dot((cc <= rr).astype(BF16), onehot, preferred_element_type=F32)
    counts = incl[tm - 1:tm, :]
    groups = jnp.floor((counts + (ROW_GROUP - 1.0)) * (1.0 / ROW_GROUP))
    ur = lax.broadcasted_iota(I32, (LANES, LANES), 0)
    uc = lax.broadcasted_iota(I32, (LANES, LANES), 1)
    before = jnp.dot(jnp.broadcast_to(groups, (8, LANES)).astype(BF16), (ur < uc).astype(BF16),
                     preferred_element_type=F32)[0:1]
    seg_start = before * float(ROW_GROUP)
    pick = lambda hit, tbl: jnp.sum(jnp.where(hit, tbl, 0.0), axis=-1, keepdims=True)
    r1 = pick(hit1, incl) - 1.0
    r2 = pick(hit2, incl) - 1.0
    row1 = pick(hit1, seg_start) + r1
    row2 = pick(hit2, seg_start) + r2
    slab = jnp.zeros((tm, LANES), F32)
    for i, val in enumerate((e1, e2, r1, r2, w1, w2, row1, row2)):
        slab = jnp.where(lane == i, val, slab)
    return x2, h2, slab, counts


def _sorted_copy(h2, row1, row2, n_rows):
    tm = h2.shape[0]
    r = lax.broadcasted_iota(I32, (n_rows, tm), 0).astype(F32)
    place = ((r == row1) | (r == row2)).astype(BF16)
    return jnp.dot(place, h2.astype(BF16), preferred_element_type=F32)


def _merge_prompt_body(x_ref, u_ref, halo_ref, att_ref, gate_ref, wp_ref, ps_ref, wup_ref, wua_ref, wo_ref,
                       nf_ref, wr_ref, br_ref, xs_last_hbm, x2_ref, xs_ref, slab_ref, route_ref, counts_ref,
                       ext_ref, sem, *, seq_len, n_groups, n_per_group):
    i = pl.program_id(0)
    n_tiles = pl.num_programs(0) - 1
    tm = x_ref.shape[0]
    gw = u_ref.shape[1] // len(POOL_WINDOWS)

    @pl.when(i < n_tiles)
    def _():
        pos0 = (i * tm) % seq_len
        u = u_ref[...]
        ext_ref[0:POOL_HALO, :] = jnp.where(pos0 == 0, 0.0, halo_ref[...])
        ext_ref[POOL_HALO:, :] = u
        pos = pos0 + lax.broadcasted_iota(I32, (tm, 1), 0)
        pooled = []
        for g, w in enumerate(POOL_WINDOWS):
            lo = g * gw
            wsum = ext_ref[pl.ds(POOL_HALO, tm), lo:lo + gw]
            for j in range(1, w):
                wsum = wsum + ext_ref[pl.ds(POOL_HALO - j, tm), lo:lo + gw]
            count = jnp.minimum(pos + 1, w).astype(F32)
            pooled.append(wsum / count - u[:, lo:lo + gw])
        x2, h2, slab, counts = _merge_and_route(
            x_ref[...], pooled, att_ref[...], gate_ref[...], wp_ref, ps_ref, wup_ref, wua_ref, wo_ref, nf_ref,
            wr_ref, br_ref, precise=False, n_groups=n_groups, n_per_group=n_per_group)
        fields = slab.T[0:8, :]
        x2_ref[...] = x2
        xs_ref[...] = _sorted_copy(h2, fields[6:7, :], fields[7:8, :], xs_ref.shape[0])
        slab_ref[...] = slab
        route_ref[...] = fields
        counts_ref[0] = counts

    @pl.when(i == n_tiles)
    def _():
        copy = pltpu.make_async_copy(xs_last_hbm, xs_ref, sem)
        copy.start()
        copy.wait()


def _merge_prompt(x, u, att, gates, wp, ps, wup, wua, wo, nf, wr, br, xs_last, *, tm, seq_len, n_groups, n_per_group):
    n, d = x.shape
    d_pool, d_att = u.shape[1], att.shape[1]
    tile_rows = xs_last.shape[0]
    nt = n // tm
    clamp = lambda i: jnp.minimum(i, nt - 1)
    row = lambda i: (clamp(i), 0)
    const = lambda i: (0, 0)
    const3 = lambda i: (0, 0, 0)
    halo = lambda i: (jnp.maximum(clamp(i) * (tm // POOL_HALO) - 1, 0), 0)
    return pl.pallas_call(
        functools.partial(_merge_prompt_body, seq_len=seq_len, n_groups=n_groups, n_per_group=n_per_group),
        grid=(nt + 1,),
        in_specs=[pl.BlockSpec((tm, d), row), pl.BlockSpec((tm, d_pool), row), pl.BlockSpec((POOL_HALO, d_pool), halo),
                  pl.BlockSpec((tm, d_att), row), pl.BlockSpec((tm, 2 * d), row),
                  pl.BlockSpec(wp.shape, const3), pl.BlockSpec((1, d_pool), const),
                  pl.BlockSpec(wup.shape, const), pl.BlockSpec(wua.shape, const), pl.BlockSpec(wo.shape, const),
                  pl.BlockSpec((1, d), const), pl.BlockSpec(wr.shape, const), pl.BlockSpec((1, LANES), const),
                  pl.BlockSpec(memory_space=pl.ANY)],
        out_specs=[pl.BlockSpec((tm, d), row), pl.BlockSpec((tile_rows, d), lambda i: (i, 0)),
                   pl.BlockSpec((tm, LANES), row), pl.BlockSpec((8, tm), lambda i: (0, clamp(i))),
                   pl.BlockSpec((1, 1, LANES), lambda i: (clamp(i), 0, 0))],
        out_shape=[jax.ShapeDtypeStruct((n, d), F32), jax.ShapeDtypeStruct(((nt + 1) * tile_rows, d), F32),
                   jax.ShapeDtypeStruct((n, LANES), F32), jax.ShapeDtypeStruct((8, n), F32),
                   jax.ShapeDtypeStruct((nt, 1, LANES), F32)],
        scratch_shapes=[pltpu.VMEM((tm + POOL_HALO, d_pool), F32), pltpu.SemaphoreType.DMA],
        compiler_params=_params("arbitrary"),
        name="merge_prompt",
    )(x, u, u, att, gates, wp, ps, wup, wua, wo, nf, wr, br, xs_last)


def _merge_sample_body(x_ref, u_ref, st_ref, att_ref, gate_ref, wp_ref, ps_ref, wup_ref, wua_ref, wo_ref,
                       nf_ref, wr_ref, br_ref, x2_ref, xs_ref, slab_ref, counts_ref,
                       *, start_pos, n_groups, n_per_group):
    u = u_ref[...]
    gw = u.shape[1] // len(POOL_WINDOWS)
    n_state = st_ref.shape[0]
    pooled = []
    for g, w in enumerate(POOL_WINDOWS):
        lo = g * gw
        wsum = u[:, lo:lo + gw]
        for j in range(1, w):
            wsum = wsum + st_ref[n_state - j][:, lo:lo + gw]
        pooled.append(wsum / float(min(start_pos + 1, w)) - u[:, lo:lo + gw])
    x2, h2, slab, counts = _merge_and_route(
        x_ref[...], pooled, att_ref[...], gate_ref[...], wp_ref, ps_ref, wup_ref, wua_ref, wo_ref, nf_ref,
        wr_ref, br_ref, precise=True, n_groups=n_groups, n_per_group=n_per_group)
    n = slab.shape[0]
    fields = jnp.concatenate([slab, jnp.zeros((LANES - n, LANES), F32)], axis=0).T
    x2_ref[...] = x2
    xs_ref[...] = _sorted_copy(h2, fields[6:7, 0:n], fields[7:8, 0:n], xs_ref.shape[0])
    slab_ref[...] = slab
    counts_ref[0] = counts


def _merge_sample(x, u, state_t, att, gates, wp, ps, wup, wua, wo, nf, wr, br, *, start_pos, n_groups,
                  n_per_group, tile_rows):
    n, d = x.shape
    assert n <= LANES
    return pl.pallas_call(
        functools.partial(_merge_sample_body, start_pos=start_pos, n_groups=n_groups, n_per_group=n_per_group),
        out_shape=[jax.ShapeDtypeStruct((n, d), F32), jax.ShapeDtypeStruct((tile_rows, d), F32),
                   jax.ShapeDtypeStruct((n, LANES), F32), jax.ShapeDtypeStruct((1, 1, LANES), F32)],
        compiler_params=_params(),
        name="merge_sample",
    )(x, u, state_t, att, gates, wp, ps, wup, wua, wo, nf, wr, br)


def _row_copy(src_ref, src_row, dst_ref, dst_row, sem):
    return pltpu.make_async_copy(src_ref.at[pl.ds(src_row, 1)], dst_ref.at[pl.ds(dst_row, 1)], sem)


def _moe_plan_body(cnt_ref, src_ref, base_ref, te_ref, tw_ref, loc_ref, *, n_ttiles, n_exp, tile_rows, gpt,
                   zero_group):
    n_groups, n_mm = src_ref.shape[0], te_ref.shape[0]

    def fill_src(g, c):
        src_ref[g] = zero_group
        return c

    def fill_tiles(t, c):
        te_ref[t] = -1
        tw_ref[t] = n_exp - 1
        return c

    def fill_loc(i, c):
        loc_ref[i] = 0
        return c

    lax.fori_loop(0, n_groups, fill_src, 0)
    lax.fori_loop(0, n_mm, fill_tiles, 0)
    lax.fori_loop(0, n_ttiles, fill_loc, 0)

    def per_expert(e, pos):
        def per_tile(i, p):
            g = (cnt_ref[i * n_exp + e] + (ROW_GROUP - 1)) // ROW_GROUP
            base_ref[i * n_exp + e] = p
            row0 = i * tile_rows + loc_ref[i] * ROW_GROUP

            def per_group(j, c):
                src_ref[p + j] = row0 + j * ROW_GROUP
                return c

            lax.fori_loop(0, g, per_group, 0)
            loc_ref[i] = loc_ref[i] + g
            return p + g

        end = lax.fori_loop(0, n_ttiles, per_tile, pos)
        end_pad = ((end + (gpt - 1)) // gpt) * gpt

        def mark(t, c):
            te_ref[t] = e
            tw_ref[t] = e
            return c

        lax.fori_loop(pos // gpt, end_pad // gpt, mark, 0)
        return end_pad

    lax.fori_loop(0, n_exp, per_expert, 0)


def _moe_plan(cnt, *, n_ttiles, n_exp, tile_rows, gpt, n_mm_tiles):
    smem = pl.BlockSpec(memory_space=pltpu.SMEM)
    zero_group = n_ttiles * tile_rows - ROW_GROUP
    return pl.pallas_call(
        functools.partial(_moe_plan_body, n_ttiles=n_ttiles, n_exp=n_exp, tile_rows=tile_rows, gpt=gpt,
                          zero_group=zero_group),
        in_specs=[smem], out_specs=[smem, smem, smem, smem],
        out_shape=[jax.ShapeDtypeStruct((n_mm_tiles * gpt,), I32), jax.ShapeDtypeStruct((n_ttiles * n_exp,), I32),
                   jax.ShapeDtypeStruct((n_mm_tiles,), I32), jax.ShapeDtypeStruct((n_mm_tiles,), I32)],
        scratch_shapes=[pltpu.SMEM((n_ttiles,), I32)],
        name="moe_plan",
    )(cnt)


def _moe_mm_body(te_ref, tw_ref, src_ref, xs_hbm, wg_ref, wu_ref, wd_ref, ys_ref, xbuf, wgb_ref, wub_ref, wdb_ref,
                 sems):
    i = pl.program_id(0)
    tm = ys_ref.shape[0]
    groups = tm // ROW_GROUP
    expert = te_ref[i]
    prev = te_ref[jnp.maximum(i - 1, 0)]

    def fetch(slot, tile):
        for k in range(groups):
            src = pl.multiple_of(src_ref[tile * groups + k], ROW_GROUP)
            pltpu.make_async_copy(xs_hbm.at[pl.ds(src, ROW_GROUP)],
                                  xbuf.at[slot, pl.ds(k * ROW_GROUP, ROW_GROUP)], sems.at[slot]).start()

    slot = i % 2

    @pl.when(i == 0)
    def _():
        fetch(0, 0)

    @pl.when(i + 1 < pl.num_programs(0))
    def _():
        fetch(1 - slot, i + 1)

    pltpu.make_async_copy(xs_hbm.at[pl.ds(0, tm)], xbuf.at[slot], sems.at[slot]).wait()

    @pl.when((expert >= 0) & ((i == 0) | (expert != prev)))
    def _():
        wgb_ref[...] = wg_ref[0].astype(BF16)
        wub_ref[...] = wu_ref[0].astype(BF16)
        wdb_ref[...] = wd_ref[0].astype(BF16)

    @pl.when(expert >= 0)
    def _():
        x = xbuf[slot].astype(BF16)
        a = jnp.dot(x, wgb_ref[...], preferred_element_type=F32)
        b = jnp.dot(x, wub_ref[...], preferred_element_type=F32)
        hdn = (a * _sigmoid(a)) * b
        ys_ref[...] = jnp.dot(hdn.astype(BF16), wdb_ref[...], preferred_element_type=F32)

    @pl.when(expert < 0)
    def _():
        ys_ref[...] = jnp.zeros_like(ys_ref)


def _moe_mm(tile_expert, tile_weight, group_src, xs, w_gate, w_up, w_down, *, tm):
    n_tiles = tile_expert.shape[0]
    d = xs.shape[1]
    n_exp, _, de = w_gate.shape
    wmap = lambda i, te, tw, src: (tw[i], 0, 0)
    grid_spec = pltpu.PrefetchScalarGridSpec(
        num_scalar_prefetch=3, grid=(n_tiles,),
        in_specs=[pl.BlockSpec(memory_space=pl.ANY),
                  pl.BlockSpec((1, d, de), wmap), pl.BlockSpec((1, d, de), wmap), pl.BlockSpec((1, de, d), wmap)],
        out_specs=pl.BlockSpec((tm, d), lambda i, te, tw, src: (i, 0)),
        scratch_shapes=[pltpu.VMEM((2, tm, d), F32), pltpu.VMEM((d, de), BF16), pltpu.VMEM((d, de), BF16),
                        pltpu.VMEM((de, d), BF16), pltpu.SemaphoreType.DMA((2,))])
    return pl.pallas_call(
        _moe_mm_body, grid_spec=grid_spec,
        out_shape=jax.ShapeDtypeStruct((n_tiles * tm, d), F32),
        compiler_params=_params("arbitrary"),
        name="moe_mm",
    )(tile_expert, tile_weight, group_src, xs, w_gate, w_up, w_down)


def _moe_combine_body(base_ref, e1_ref, e2_ref, r1_ref, r2_ref, x_ref, slab_ref, g_ref, ys_ref, o_ref,
                      ya_ref, yb_ref, sem, *, final_norm, first_tile, n_exp):
    n = x_ref.shape[0]
    seg = (first_tile + pl.program_id(0)) * n_exp

    def start(i, carry):
        for j in range(ROW_DMA_UNROLL):
            t = i * ROW_DMA_UNROLL + j
            _row_copy(ys_ref, base_ref[seg + e1_ref[t]] * ROW_GROUP + r1_ref[t], ya_ref, t, sem).start()
            _row_copy(ys_ref, base_ref[seg + e2_ref[t]] * ROW_GROUP + r2_ref[t], yb_ref, t, sem).start()
        return carry

    lax.fori_loop(0, n // ROW_DMA_UNROLL, start, 0)
    pltpu.make_async_copy(ys_ref.at[pl.ds(0, n)], ya_ref, sem).wait()
    pltpu.make_async_copy(ys_ref.at[pl.ds(0, n)], yb_ref, sem).wait()
    slab = slab_ref[...]
    out = x_ref[...] + (slab[:, 4:5] * ya_ref[...] + slab[:, 5:6] * yb_ref[...])
    if final_norm:
        out = _rmsnorm(out, g_ref[...])
    o_ref[...] = out


def _moe_combine(seg_base, e1, e2, r1, r2, x, slab, g, ys, *, ts, final_norm, first_tile, n_exp):
    n, d = x.shape
    smem = lambda: pl.BlockSpec((ts,), lambda i: (i,), memory_space=pltpu.SMEM)
    return pl.pallas_call(
        functools.partial(_moe_combine_body, final_norm=final_norm, first_tile=first_tile, n_exp=n_exp),
        grid=(n // ts,),
        in_specs=[pl.BlockSpec(memory_space=pltpu.SMEM), smem(), smem(), smem(), smem(),
                  pl.BlockSpec((ts, d), lambda i: (i, 0)), pl.BlockSpec((ts, LANES), lambda i: (i, 0)),
                  pl.BlockSpec((1, d), lambda i: (0, 0)), pl.BlockSpec(memory_space=pl.ANY)],
        out_specs=pl.BlockSpec((ts, d), lambda i: (i, 0)),
        out_shape=jax.ShapeDtypeStruct((n, d), F32),
        scratch_shapes=[pltpu.VMEM((ts, d), F32), pltpu.VMEM((ts, d), F32), pltpu.SemaphoreType.DMA],
        compiler_params=_params("arbitrary"),
        name="moe_combine",
    )(seg_base, e1, e2, r1, r2, x, slab, g, ys)


def kernel(x_prompt, x_sample, cache_k, cache_v, cache_logf, state_pool, page_table, norm_mix, w_in, b_forget,
           w_pool, pool_scale, w_up_pool, w_up_att, w_out, norm_ffn, w_router_group, b_router_group,
           w_router_expert, b_router_expert, w_gate, w_up, w_down, norm_final):
    depth = norm_mix.shape[0]
    assert depth == 1, "single trunk layer"
    b, t, d = x_prompt.shape
    db, dt, _ = x_sample.shape
    assert dt == 1, "one sample token per sequence"
    _, n_phys, page, n_heads, dh = cache_k.shape
    n_pages = page_table.shape[1]
    past = n_pages * page
    n_state, d_pool = state_pool.shape[2], state_pool.shape[3]
    d_att = n_heads * dh
    n_pool_groups = w_pool.shape[1]
    assert n_pool_groups == len(POOL_WINDOWS) and d_pool // n_pool_groups == LANES
    assert n_state == max(POOL_WINDOWS) - 1 and n_state < POOL_HALO
    n_groups, n_per_group = w_router_expert.shape[1], w_router_expert.shape[3]
    n_exp = n_groups * n_per_group
    assert n_groups + n_exp <= LANES and 2 * dh == LANES and n_heads % 2 == 0
    n = b * t
    q_scale = float(dh) ** -0.5
    tm = min(TOKEN_TILE, t)
    assert t % tm == 0 and t % ATTN_TILE == 0

    o_main = d_pool + 3 * d_att
    wi = w_in[0]
    wm_f, wf_f, wg_f = wi[:, :o_main], wi[:, o_main:o_main + n_heads], wi[:, o_main + n_heads:]
    wf_pad = jnp.pad(wf_f, ((0, 0), (0, LANES - n_heads)))
    wit = jnp.transpose(wi)
    wmt_f, wgt_f = wit[:o_main], wit[o_main + n_heads:]
    wft_pad = jnp.pad(wit[o_main:o_main + n_heads], ((0, LANES - n_heads), (0, 0)))
    bf_pad = jnp.pad(b_forget[0], (0, LANES - n_heads)).reshape(1, LANES)
    g_mix = norm_mix[0].reshape(1, d)
    g_ffn = norm_ffn[0].reshape(1, d)
    g_fin = norm_final.reshape(1, d)
    ps = pool_scale[0].reshape(1, d_pool)
    wr_f = jnp.concatenate([w_router_group[0], jnp.transpose(w_router_expert[0], (1, 0, 2)).reshape(d, n_exp)], axis=1)
    wr_pad = jnp.pad(wr_f, ((0, 0), (0, LANES - n_groups - n_exp)))
    br_pad = jnp.pad(jnp.concatenate([b_router_group[0], b_router_expert[0].reshape(n_exp)]),
                     (0, LANES - n_groups - n_exp)).reshape(1, LANES)
    bf = lambda a: a.astype(BF16)

    xp = x_prompt.reshape(n, d)
    u_p, q_p, kt_p, vt_p, kb_p, vb_p, lft_p, gate_p = _proj_prompt(
        xp, g_mix, bf(wm_f), bf(wf_pad), bf(wg_f), bf_pad, tm=tm, seq_len=t, d_pool=d_pool, d_att=d_att,
        n_heads=n_heads, q_scale=q_scale)
    c = _cumsum_lanes(lft_p.reshape(b * n_heads, t))
    nt = t // ATTN_TILE
    c_blk = jnp.transpose(c.reshape(b, n_heads // 2, 2, nt, ATTN_TILE), (0, 1, 3, 2, 4))
    att_p = _attn_prompt(q_p.reshape(b, t, d_att), kb_p.reshape(b, t, d_att), vb_p.reshape(b, t, d_att), c_blk,
                         tile=ATTN_TILE, dh=dh)
    n_ptiles = n // tm
    n_ttiles = n_ptiles + 1
    tile_rows = -(-(2 * tm + n_exp * (ROW_GROUP - 1)) // MOE_ROW_TILE) * MOE_ROW_TILE
    assert 2 * db + n_exp * (ROW_GROUP - 1) <= tile_rows - ROW_GROUP, "the sample tile must end in an unused row group"

    xs = x_sample.reshape(db, d)
    z_s, lf_s, gate_s = _proj_sample(xs, g_mix, wmt_f, wft_pad, wgt_f, bf_pad, tn=512)
    u_s = z_s[:, :d_pool]
    q_s = z_s[:, d_pool:d_pool + d_att] * q_scale
    k_s = z_s[:, d_pool + d_att:d_pool + 2 * d_att]
    v_s = z_s[:, d_pool + 2 * d_att:]
    att_s = _attn_sample(page_table, q_s, k_s, v_s, lf_s[:, :n_heads],
                         jnp.transpose(cache_k[0], (0, 2, 3, 1)), jnp.transpose(cache_v[0], (0, 2, 3, 1)),
                         jnp.transpose(cache_logf[0], (0, 2, 1)))
    state_t = jnp.transpose(state_pool[0], (1, 0, 2))
    x2_s, xs_rows_s, slab_s, counts_s = _merge_sample(
        xs, u_s, state_t, att_s.reshape(db, d_att), gate_s, w_pool[0], ps, w_up_pool[0], w_up_att[0], w_out[0],
        g_ffn, wr_pad, br_pad, start_pos=past, n_groups=n_groups, n_per_group=n_per_group, tile_rows=tile_rows)

    x2_p, xs_rows, slab_p, route_p, counts_p = _merge_prompt(
        xp, u_p, att_p.reshape(n, d_att), gate_p, bf(w_pool[0]), ps, bf(w_up_pool[0]), bf(w_up_att[0]),
        bf(w_out[0]), g_ffn, bf(wr_pad), br_pad, xs_rows_s, tm=tm, seq_len=t, n_groups=n_groups,
        n_per_group=n_per_group)

    tmm = MOE_ROW_TILE
    gpt = tmm // ROW_GROUP
    n_groups_max = -(-(2 * (n + db)) // ROW_GROUP) + n_ttiles * n_exp + n_exp * (gpt - 1)
    cnt = jnp.concatenate([counts_p, counts_s], axis=0)[:, 0, :n_exp].astype(I32).reshape(-1)
    group_src, seg_base, tile_expert, tile_weight = _moe_plan(
        cnt, n_ttiles=n_ttiles, n_exp=n_exp, tile_rows=tile_rows, gpt=gpt, n_mm_tiles=-(-n_groups_max // gpt))
    fields_p = route_p[:4].astype(I32)
    fields_s = jnp.transpose(slab_s[:, :4]).astype(I32)
    ys_rows = _moe_mm(tile_expert, tile_weight, group_src, xs_rows, w_gate[0], w_up[0], w_down[0], tm=tmm)
    y_prompt = _moe_combine(seg_base, *fields_p, x2_p, slab_p, g_fin, ys_rows, ts=tm, final_norm=True,
                            first_tile=0, n_exp=n_exp)
    y_sample = _moe_combine(seg_base, *fields_s, x2_s, slab_s, g_fin, ys_rows, ts=db, final_norm=True,
                            first_tile=n_ptiles, n_exp=n_exp)

    new_pool_p = u_p.reshape(b, t, d_pool)[:, t - n_state:, :]
    new_pool_s = jnp.concatenate([state_pool[0][:, 1:, :], u_s[:, None, :]], axis=1)
    to_heads = lambda a: jnp.transpose(a.reshape(b, n_heads, dh, t), (0, 3, 1, 2))[None]
    return (y_prompt.reshape(b, t, d), y_sample.reshape(db, 1, d),
            to_heads(kt_p), to_heads(vt_p), jnp.transpose(lft_p, (0, 2, 1))[None],
            new_pool_p[None],
            k_s.reshape(1, db, 1, n_heads, dh), v_s.reshape(1, db, 1, n_heads, dh),
            lf_s[:, :n_heads].reshape(1, db, 1, n_heads), new_pool_s[None])
```

```python
import functools

import jax
import jax.numpy as jnp
from jax import lax
from jax.experimental import pallas as pl
from jax.experimental.pallas import tpu as pltpu

F32 = jnp.float32
BF16 = jnp.bfloat16
I32 = jnp.int32
HIGHEST = lax.Precision.HIGHEST

RMS_EPS = 1e-6
POOL_WINDOWS = (2, 4, 8, 16)
POOL_HALO = 16
LANES = 128
VMEM_LIMIT_BYTES = 56 * 1024 * 1024

TOKEN_TILE = 512
ATTN_TILE = 512
MOE_ROW_TILE = 256
PAGES_PER_STEP = 16
ROW_GROUP = 8


def _params(*sem):
    return pltpu.CompilerParams(dimension_semantics=sem, vmem_limit_bytes=VMEM_LIMIT_BYTES)


def _rmsnorm(x, g):
    return x * lax.rsqrt(jnp.mean(x * x, axis=-1, keepdims=True) + RMS_EPS) * g


def _log_sigmoid(x):
    return jnp.minimum(x, 0.0) - jnp.log1p(jnp.exp(-jnp.abs(x)))


def _sigmoid(x):
    return 1.0 / (1.0 + jnp.exp(-x))


def _dot(a, b, precise):
    if precise:
        return jnp.dot(a.astype(F32), b.astype(F32), precision=HIGHEST, preferred_element_type=F32)
    return jnp.dot(a.astype(BF16), b.astype(BF16), preferred_element_type=F32)


def _split3(x):
    hi = x.astype(BF16)
    r = x - hi.astype(F32)
    mid = r.astype(BF16)
    lo = (r - mid.astype(F32)).astype(BF16)
    return hi, mid, lo


def _dot_exact_rhs(x, w_bf16):
    hi, mid, lo = _split3(x)
    d = lambda a: jnp.dot(a, w_bf16, preferred_element_type=F32)
    return d(hi) + d(mid) + d(lo)


def _proj_body(x_ref, g_ref, wm_ref, wf_ref, wg_ref, bf_ref,
               u_ref, q_ref, kt_ref, vt_ref, kb_ref, vb_ref, lft_ref, gate_ref, *, d_pool, d_att, n_heads, q_scale):
    h = _rmsnorm(x_ref[...], g_ref[...]).astype(BF16)
    z = jnp.dot(h, wm_ref[...], preferred_element_type=F32)
    o1, o2, o3 = d_pool, d_pool + d_att, d_pool + 2 * d_att
    u_ref[...] = z[:, :o1]
    q_ref[...] = (z[:, o1:o2] * q_scale).astype(BF16)
    k = z[:, o2:o3]
    v = z[:, o3:]
    kt_ref[0] = k.T
    vt_ref[0] = v.T
    kb_ref[...] = k.astype(BF16)
    vb_ref[...] = v.astype(BF16)
    lf = _log_sigmoid(jnp.dot(h, wf_ref[...], preferred_element_type=F32) + bf_ref[...])
    lft_ref[0] = lf.T[0:n_heads, :]
    gate_ref[...] = _sigmoid(jnp.dot(h, wg_ref[...], preferred_element_type=F32)).astype(BF16)


def _proj_prompt(x, g, wm, wf, wg, bfp, *, tm, seq_len, d_pool, d_att, n_heads, q_scale):
    n, d = x.shape
    b = n // seq_len
    tps = seq_len // tm
    row = lambda i: (i, 0)
    const = lambda i: (0, 0)
    tmin = lambda i: (i // tps, 0, i % tps)
    dg = wg.shape[1]
    out_shape = [
        jax.ShapeDtypeStruct((n, d_pool), F32), jax.ShapeDtypeStruct((n, d_att), BF16),
        jax.ShapeDtypeStruct((b, d_att, seq_len), F32), jax.ShapeDtypeStruct((b, d_att, seq_len), F32),
        jax.ShapeDtypeStruct((n, d_att), BF16), jax.ShapeDtypeStruct((n, d_att), BF16),
        jax.ShapeDtypeStruct((b, n_heads, seq_len), F32), jax.ShapeDtypeStruct((n, dg), BF16),
    ]
    return pl.pallas_call(
        functools.partial(_proj_body, d_pool=d_pool, d_att=d_att, n_heads=n_heads, q_scale=q_scale),
        grid=(n // tm,),
        in_specs=[pl.BlockSpec((tm, d), row), pl.BlockSpec((1, d), const),
                  pl.BlockSpec(wm.shape, const), pl.BlockSpec(wf.shape, const),
                  pl.BlockSpec(wg.shape, const), pl.BlockSpec((1, LANES), const)],
        out_specs=[pl.BlockSpec((tm, d_pool), row), pl.BlockSpec((tm, d_att), row),
                   pl.BlockSpec((1, d_att, tm), tmin), pl.BlockSpec((1, d_att, tm), tmin),
                   pl.BlockSpec((tm, d_att), row), pl.BlockSpec((tm, d_att), row),
                   pl.BlockSpec((1, n_heads, tm), tmin), pl.BlockSpec((tm, dg), row)],
        out_shape=out_shape,
        compiler_params=_params("arbitrary"),
        name="proj_prompt",
    )(x, g, wm, wf, wg, bfp)


def _dot_nt(a, bt, precise):
    dims = (((1,), (1,)), ((), ()))
    if precise:
        return lax.dot_general(a.astype(F32), bt.astype(F32), dims, precision=HIGHEST, preferred_element_type=F32)
    return lax.dot_general(a.astype(BF16), bt.astype(BF16), dims, preferred_element_type=F32)


def _proj_sample_body(x_ref, g_ref, wmt_ref, wft_ref, wgt_ref, bf_ref, z_ref, lf_ref, gate_ref):
    h = _rmsnorm(x_ref[...], g_ref[...])
    z_ref[...] = _dot_nt(h, wmt_ref[...], True)
    lf_ref[...] = _log_sigmoid(_dot_nt(h, wft_ref[...], True) + bf_ref[...])
    gate_ref[...] = _sigmoid(_dot_nt(h, wgt_ref[...], True))


def _proj_sample(x, g, wmt, wft, wgt, bfp, *, tn):
    n, d = x.shape
    dm, dg = wmt.shape[0], wgt.shape[0]
    assert dm == dg
    const = lambda j: (0, 0)
    chunk = lambda j: (j, 0)
    col = lambda j: (0, j)
    return pl.pallas_call(
        _proj_sample_body,
        grid=(dm // tn,),
        in_specs=[pl.BlockSpec((n, d), const), pl.BlockSpec((1, d), const),
                  pl.BlockSpec((tn, d), chunk), pl.BlockSpec(wft.shape, const),
                  pl.BlockSpec((tn, d), chunk), pl.BlockSpec((1, LANES), const)],
        out_specs=[pl.BlockSpec((n, tn), col), pl.BlockSpec((n, LANES), const), pl.BlockSpec((n, tn), col)],
        out_shape=[jax.ShapeDtypeStruct((n, dm), F32), jax.ShapeDtypeStruct((n, LANES), F32),
                   jax.ShapeDtypeStruct((n, dg), F32)],
        compiler_params=_params("arbitrary"),
        name="proj_sample",
    )(x, g, wmt, wft, wgt, bfp)


def _cumsum_body(x_ref, o_ref):
    c = x_ref[...]
    lane = lax.broadcasted_iota(I32, c.shape, 1)
    s = 1
    while s < c.shape[1]:
        c = c + jnp.where(lane >= s, pltpu.roll(c, s, 1), 0.0)
        s *= 2
    o_ref[...] = c


def _cumsum_lanes(x):
    return pl.pallas_call(_cumsum_body, out_shape=jax.ShapeDtypeStruct(x.shape, F32),
                          compiler_params=_params(), name="cumsum_logf")(x)


def _attn_body(q_ref, k_ref, v_ref, c_ref, o_ref, *, tile, dh):
    nt = q_ref.shape[1] // tile
    lane = lax.broadcasted_iota(I32, (tile, 2 * dh), 1)
    first = lane < dh
    row = lax.broadcasted_iota(I32, (tile, tile), 0)
    col = lax.broadcasted_iota(I32, (tile, tile), 1)
    causal = col <= row
    one = jnp.ones((tile, 2 * dh), BF16)
    kts, vhs = [], []
    for kj in range(nt):
        vt = v_ref[0, kj * tile:(kj + 1) * tile, :]
        kts.append(k_ref[0, kj * tile:(kj + 1) * tile, :])
        vhs.append((jnp.where(first, vt, one), jnp.where(first, one, vt)))
    for qi in range(nt):
        q = q_ref[0, qi * tile:(qi + 1) * tile, :]
        zero = jnp.zeros_like(q)
        q_heads = (jnp.where(first, q, zero), jnp.where(first, zero, q))
        res = []
        for h in range(2):
            m = jnp.full((tile, 1), -1e30, F32)
            acc = jnp.zeros((tile, 2 * dh), F32)
            for kj in range(qi + 1):
                s = lax.dot_general(q_heads[h], kts[kj], (((1,), (1,)), ((), ())), preferred_element_type=F32)
                s = s - c_ref[0, 0, kj][h:h + 1, :]
                if kj == qi:
                    s = jnp.where(causal, s, -jnp.inf)
                m_new = jnp.maximum(m, jnp.max(s, axis=-1, keepdims=True))
                alpha = jnp.exp(m - m_new)
                p = jnp.exp(s - m_new)
                acc = alpha * acc + jnp.dot(p.astype(BF16), vhs[kj][h], preferred_element_type=F32)
                m = m_new
            res.append(acc)
        a0, a1 = res
        out = jnp.where(first, a0 / a0[:, dh:dh + 1], a1 / a1[:, 0:1])
        o_ref[0, qi * tile:(qi + 1) * tile, :] = out.astype(o_ref.dtype)


def _attn_prompt(q, k, v, c, *, tile, dh):
    b, t, da = q.shape
    hp = da // (2 * dh)
    nt = t // tile
    pair = pl.BlockSpec((1, t, 2 * dh), lambda bi, hi: (bi, 0, hi))
    return pl.pallas_call(
        functools.partial(_attn_body, tile=tile, dh=dh),
        grid=(b, hp),
        in_specs=[pair, pair, pair, pl.BlockSpec((1, 1, nt, 2, tile), lambda bi, hi: (bi, hi, 0, 0, 0))],
        out_specs=pair,
        out_shape=jax.ShapeDtypeStruct((b, t, da), BF16),
        compiler_params=_params("arbitrary", "arbitrary"),
        name="attn_prompt",
    )(q, k, v, c)


def _attn_sample_body(pt_ref, qrep_ref, q_ref, kn_ref, vrep_ref, lfn_ref, ck_hbm, cv_hbm, clf_hbm, o_ref,
                      m_ref, l_ref, acc_ref, srun_ref, kbuf, vbuf, lfbuf, sems):
    seq = pl.program_id(0)
    step = pl.program_id(1)
    n_steps = pl.num_programs(1)
    n_pages = pt_ref.shape[1]
    _, g_n, n_heads, dh, page = kbuf.shape
    d_att = n_heads * dh

    def fetch(slot, b, s):
        for g in range(g_n):
            pid = pt_ref[b, n_pages - 1 - (s * g_n + g)]
            pltpu.make_async_copy(ck_hbm.at[pid], kbuf.at[slot, g], sems.at[slot, 0]).start()
            pltpu.make_async_copy(cv_hbm.at[pid], vbuf.at[slot, g], sems.at[slot, 1]).start()
            pltpu.make_async_copy(clf_hbm.at[pid], lfbuf.at[slot, g], sems.at[slot, 2]).start()

    gstep = seq * n_steps + step
    slot = gstep % 2

    @pl.when(gstep == 0)
    def _():
        fetch(0, 0, 0)

    @pl.when(gstep + 1 < pl.num_programs(0) * n_steps)
    def _():
        last = step == n_steps - 1
        fetch(1 - slot, jnp.where(last, seq + 1, seq), jnp.where(last, 0, step + 1))

    pltpu.make_async_copy(ck_hbm.at[pl.ds(0, g_n)], kbuf.at[slot], sems.at[slot, 0]).wait()
    pltpu.make_async_copy(cv_hbm.at[pl.ds(0, g_n)], vbuf.at[slot], sems.at[slot, 1]).wait()
    pltpu.make_async_copy(clf_hbm.at[pl.ds(0, g_n)], lfbuf.at[slot], sems.at[slot, 2]).wait()
    k_refs = [kbuf.at[slot, g] for g in range(g_n)]
    v_refs = [vbuf.at[slot, g] for g in range(g_n)]
    lf_refs = [lfbuf.at[slot, g] for g in range(g_n)]

    @pl.when(step == 0)
    def _():
        s_new = jnp.sum(q_ref[0] * kn_ref[0], axis=-1, keepdims=True)
        m_ref[...] = jnp.broadcast_to(s_new, m_ref.shape)
        l_ref[...] = jnp.ones_like(l_ref)
        lane = lax.broadcasted_iota(I32, (d_att, page), 1)
        acc_ref[...] = jnp.where(lane == 0, vrep_ref[0], 0.0)
        srun_ref[...] = lfn_ref[0]

    qrep = qrep_ref[0]
    r = lax.broadcasted_iota(I32, (page, page), 0)
    c = lax.broadcasted_iota(I32, (page, page), 1)
    later = (r > c).astype(BF16)
    ones = jnp.ones((page, page), BF16)
    lf_all = jnp.concatenate([lf_refs[g][...] for g in range(g_n)], axis=0)
    suffix = _dot_exact_rhs(lf_all, later)
    total = _dot_exact_rhs(lf_all, ones)
    s_run = srun_ref[...]
    m_prev = m_ref[...]
    m_new = m_prev
    scores = []
    for g in range(g_n):
        kq = k_refs[g][...].reshape(d_att, page) * qrep
        s = jnp.sum(kq.reshape(n_heads, dh, page), axis=1)
        sb = s + s_run + suffix[g * n_heads:(g + 1) * n_heads]
        s_run = s_run + total[g * n_heads:(g + 1) * n_heads]
        scores.append(sb)
        m_new = jnp.maximum(m_new, jnp.max(sb, axis=-1, keepdims=True))
    srun_ref[...] = s_run
    alpha = jnp.exp(m_prev - m_new)
    l = alpha * l_ref[...]
    acc = acc_ref[...].reshape(n_heads, dh, page) * alpha[:, None, :]
    for g in range(g_n):
        p = jnp.exp(scores[g] - m_new)
        l = l + jnp.sum(p, axis=-1, keepdims=True)
        acc = acc + v_refs[g][...] * p[:, None, :]
    m_ref[...] = m_new
    l_ref[...] = l
    acc_ref[...] = acc.reshape(d_att, page)

    @pl.when(step == pl.num_programs(1) - 1)
    def _():
        o_ref[0] = jnp.sum(acc / l[:, None, :], axis=-1)


def _attn_sample(page_table, q, k_new, v_new, lf_new, cache_kt, cache_vt, cache_lft):
    db, n_pages = page_table.shape
    n_phys, n_heads, dh, page = cache_kt.shape
    d_att = n_heads * dh
    g_n = PAGES_PER_STEP
    while n_pages % g_n:
        g_n //= 2
    n_steps = n_pages // g_n
    lane_rep = lambda a: jnp.broadcast_to(a.reshape(db, -1, 1), (db, a.size // db, page))

    per_seq = lambda b, s, pt: (b, 0, 0)
    hbm = pl.BlockSpec(memory_space=pl.ANY)
    in_specs = [pl.BlockSpec((1, d_att, page), per_seq), pl.BlockSpec((1, n_heads, dh), per_seq),
                pl.BlockSpec((1, n_heads, dh), per_seq), pl.BlockSpec((1, d_att, page), per_seq),
                pl.BlockSpec((1, n_heads, page), per_seq), hbm, hbm, hbm]
    grid_spec = pltpu.PrefetchScalarGridSpec(
        num_scalar_prefetch=1, grid=(db, n_steps), in_specs=in_specs,
        out_specs=pl.BlockSpec((1, n_heads, dh), per_seq),
        scratch_shapes=[pltpu.VMEM((n_heads, page), F32), pltpu.VMEM((n_heads, page), F32),
                        pltpu.VMEM((d_att, page), F32), pltpu.VMEM((n_heads, page), F32),
                        pltpu.VMEM((2, g_n, n_heads, dh, page), F32), pltpu.VMEM((2, g_n, n_heads, dh, page), F32),
                        pltpu.VMEM((2, g_n, n_heads, page), F32), pltpu.SemaphoreType.DMA((2, 3))])
    return pl.pallas_call(
        _attn_sample_body,
        grid_spec=grid_spec,
        out_shape=jax.ShapeDtypeStruct((db, n_heads, dh), F32),
        compiler_params=_params("arbitrary", "arbitrary"),
        name="attn_sample",
    )(page_table, lane_rep(q), q.reshape(db, n_heads, dh), k_new.reshape(db, n_heads, dh), lane_rep(v_new),
      lane_rep(lf_new), cache_kt, cache_vt, cache_lft)


def _merge_and_route(x, pooled, att, gates, wp_ref, ps_ref, wup_ref, wua_ref, wo_ref, nf_ref, wr_ref, br_ref,
                     *, precise, n_groups, n_per_group):
    tm, d = x.shape
    gw = pooled[0].shape[1]
    mixed = jnp.concatenate([_dot(pooled[g], wp_ref[g], precise) for g in range(len(pooled))], axis=-1)
    pool_out = mixed * ps_ref[...]
    y = gates[:, :d].astype(F32) * _dot(pool_out, wup_ref[...], precise) \
        + gates[:, d:].astype(F32) * _dot(att, wua_ref[...], precise)
    x2 = x + _dot(y, wo_ref[...], precise)
    h2 = _rmsnorm(x2, nf_ref[...])
    logits = _dot(h2, wr_ref[...], precise) + br_ref[...]
    lane = lax.broadcasted_iota(I32, logits.shape, 1)
    lanef = lane.astype(F32)
    neg = -jnp.inf
    is_g = lane < n_groups
    gmax = jnp.max(jnp.where(is_g, logits, neg), axis=-1, keepdims=True)
    gidx = jnp.min(jnp.where(is_g & (logits == gmax), lanef, float(LANES)), axis=-1, keepdims=True)
    gsum = jnp.sum(jnp.where(is_g, jnp.exp(logits - gmax), 0.0), axis=-1, keepdims=True)
    g_w = 1.0 / gsum
    n_exp = n_groups * n_per_group
    exp_id = lanef - float(n_groups)
    in_sel = (lane >= n_groups) & (lane < n_groups + n_exp) & (jnp.floor(exp_id / n_per_group) == gidx)
    v1 = jnp.max(jnp.where(in_sel, logits, neg), axis=-1, keepdims=True)
    i1 = jnp.min(jnp.where(in_sel & (logits == v1), lanef, float(LANES)), axis=-1, keepdims=True)
    in_sel2 = in_sel & (lanef != i1)
    v2 = jnp.max(jnp.where(in_sel2, logits, neg), axis=-1, keepdims=True)
    i2 = jnp.min(jnp.where(in_sel2 & (logits == v2), lanef, float(LANES)), axis=-1, keepdims=True)
    t = jnp.exp(v2 - v1)
    w1 = g_w * (1.0 / (1.0 + t))
    w2 = g_w * (t / (1.0 + t))
    e1 = i1 - float(n_groups)
    e2 = i2 - float(n_groups)
    hit1 = lanef == e1
    hit2 = lanef == e2
    onehot = (hit1 | hit2).astype(BF16)
    rr = lax.broadcasted_iota(I32, (tm, tm), 0)
    cc = lax.broadcasted_iota(I32, (tm, tm), 1)
    incl = jnp.dot((cc <= rr).astype(BF16), onehot, preferred_element_type=F32)
    counts = incl[tm - 1:tm, :]
    groups = jnp.floor((counts + (ROW_GROUP - 1.0)) * (1.0 / ROW_GROUP))
    ur = lax.broadcasted_iota(I32, (LANES, LANES), 0)
    uc = lax.broadcasted_iota(I32, (LANES, LANES), 1)
    before = jnp.dot(jnp.broadcast_to(groups, (8, LANES)).astype(BF16), (ur < uc).astype(BF16),
                     preferred_element_type=F32)[0:1]
    seg_start = before * float(ROW_GROUP)
    pick = lambda hit, tbl: jnp.sum(jnp.where(hit, tbl, 0.0), axis=-1, keepdims=True)
    r1 = pick(hit1, incl) - 1.0
    r2 = pick(hit2, incl) - 1.0
    row1 = pick(hit1, seg_start) + r1
    row2 = pick(hit2, seg_start) + r2
    slab = jnp.zeros((tm, LANES), F32)
    for i, val in enumerate((e1, e2, r1, r2, w1, w2, row1, row2)):
        slab = jnp.where(lane == i, val, slab)
    return x2, h2, slab, counts


def _sorted_copy(h2, row1, row2, n_rows):
    tm = h2.shape[0]
    r = lax.broadcasted_iota(I32, (n_rows, tm), 0).astype(F32)
    place = ((r == row1) | (r == row2)).astype(BF16)
    return jnp.dot(place, h2.astype(BF16), preferred_element_type=F32)


def _merge_prompt_body(x_ref, u_ref, halo_ref, att_ref, gate_ref, wp_ref, ps_ref, wup_ref, wua_ref, wo_ref,
                       nf_ref, wr_ref, br_ref, xs_last_hbm, x2_ref, xs_ref, slab_ref, counts_ref,
                       ext_ref, sem, *, seq_len, n_groups, n_per_group):
    i = pl.program_id(0)
    n_tiles = pl.num_programs(0) - 1
    tm = x_ref.shape[0]
    gw = u_ref.shape[1] // len(POOL_WINDOWS)

    @pl.when(i < n_tiles)
    def _():
        pos0 = (i * tm) % seq_len
        u = u_ref[...]
        ext_ref[0:POOL_HALO, :] = jnp.where(pos0 == 0, 0.0, halo_ref[...])
        ext_ref[POOL_HALO:, :] = u
        pos = pos0 + lax.broadcasted_iota(I32, (tm, 1), 0)
        pooled = []
        for g, w in enumerate(POOL_WINDOWS):
            lo = g * gw
            wsum = ext_ref[pl.ds(POOL_HALO, tm), lo:lo + gw]
            for j in range(1, w):
                wsum = wsum + ext_ref[pl.ds(POOL_HALO - j, tm), lo:lo + gw]
            count = jnp.minimum(pos + 1, w).astype(F32)
            pooled.append(wsum / count - u[:, lo:lo + gw])
        x2, h2, slab, counts = _merge_and_route(
            x_ref[...], pooled, att_ref[...], gate_ref[...], wp_ref, ps_ref, wup_ref, wua_ref, wo_ref, nf_ref,
            wr_ref, br_ref, precise=False, n_groups=n_groups, n_per_group=n_per_group)
        fields = slab.T[0:8, :]
        x2_ref[...] = x2
        xs_ref[...] = _sorted_copy(h2, fields[6:7, :], fields[7:8, :], xs_ref.shape[0])
        slab_ref[...] = slab
        counts_ref[0] = counts

    @pl.when(i == n_tiles)
    def _():
        copy = pltpu.make_async_copy(xs_last_hbm, xs_ref, sem)
        copy.start()
        copy.wait()


def _merge_prompt(x, u, att, gates, wp, ps, wup, wua, wo, nf, wr, br, xs_last, *, tm, seq_len, n_groups, n_per_group):
    n, d = x.shape
    d_pool, d_att = u.shape[1], att.shape[1]
    tile_rows = xs_last.shape[0]
    nt = n // tm
    clamp = lambda i: jnp.minimum(i, nt - 1)
    row = lambda i: (clamp(i), 0)
    const = lambda i: (0, 0)
    const3 = lambda i: (0, 0, 0)
    halo = lambda i: (jnp.maximum(clamp(i) * (tm // POOL_HALO) - 1, 0), 0)
    return pl.pallas_call(
        functools.partial(_merge_prompt_body, seq_len=seq_len, n_groups=n_groups, n_per_group=n_per_group),
        grid=(nt + 1,),
        in_specs=[pl.BlockSpec((tm, d), row), pl.BlockSpec((tm, d_pool), row), pl.BlockSpec((POOL_HALO, d_pool), halo),
                  pl.BlockSpec((tm, d_att), row), pl.BlockSpec((tm, 2 * d), row),
                  pl.BlockSpec(wp.shape, const3), pl.BlockSpec((1, d_pool), const),
                  pl.BlockSpec(wup.shape, const), pl.BlockSpec(wua.shape, const), pl.BlockSpec(wo.shape, const),
                  pl.BlockSpec((1, d), const), pl.BlockSpec(wr.shape, const), pl.BlockSpec((1, LANES), const),
                  pl.BlockSpec(memory_space=pl.ANY)],
        out_specs=[pl.BlockSpec((tm, d), row), pl.BlockSpec((tile_rows, d), lambda i: (i, 0)),
                   pl.BlockSpec((tm, LANES), row), pl.BlockSpec((1, 1, LANES), lambda i: (clamp(i), 0, 0))],
        out_shape=[jax.ShapeDtypeStruct((n, d), F32), jax.ShapeDtypeStruct(((nt + 1) * tile_rows, d), F32),
                   jax.ShapeDtypeStruct((n, LANES), F32), jax.ShapeDtypeStruct((nt, 1, LANES), F32)],
        scratch_shapes=[pltpu.VMEM((tm + POOL_HALO, d_pool), F32), pltpu.SemaphoreType.DMA],
        compiler_params=_params("arbitrary"),
        name="merge_prompt",
    )(x, u, u, att, gates, wp, ps, wup, wua, wo, nf, wr, br, xs_last)


def _merge_sample_body(x_ref, u_ref, st_ref, att_ref, gate_ref, wp_ref, ps_ref, wup_ref, wua_ref, wo_ref,
                       nf_ref, wr_ref, br_ref, x2_ref, xs_ref, slab_ref, counts_ref,
                       *, start_pos, n_groups, n_per_group):
    u = u_ref[...]
    gw = u.shape[1] // len(POOL_WINDOWS)
    n_state = st_ref.shape[0]
    pooled = []
    for g, w in enumerate(POOL_WINDOWS):
        lo = g * gw
        wsum = u[:, lo:lo + gw]
        for j in range(1, w):
            wsum = wsum + st_ref[n_state - j][:, lo:lo + gw]
        pooled.append(wsum / float(min(start_pos + 1, w)) - u[:, lo:lo + gw])
    x2, h2, slab, counts = _merge_and_route(
        x_ref[...], pooled, att_ref[...], gate_ref[...], wp_ref, ps_ref, wup_ref, wua_ref, wo_ref, nf_ref,
        wr_ref, br_ref, precise=True, n_groups=n_groups, n_per_group=n_per_group)
    n = slab.shape[0]
    fields = jnp.concatenate([slab, jnp.zeros((LANES - n, LANES), F32)], axis=0).T
    x2_ref[...] = x2
    xs_ref[...] = _sorted_copy(h2, fields[6:7, 0:n], fields[7:8, 0:n], xs_ref.shape[0])
    slab_ref[...] = slab
    counts_ref[0] = counts


def _merge_sample(x, u, state_t, att, gates, wp, ps, wup, wua, wo, nf, wr, br, *, start_pos, n_groups,
                  n_per_group, tile_rows):
    n, d = x.shape
    assert n <= LANES
    return pl.pallas_call(
        functools.partial(_merge_sample_body, start_pos=start_pos, n_groups=n_groups, n_per_group=n_per_group),
        out_shape=[jax.ShapeDtypeStruct((n, d), F32), jax.ShapeDtypeStruct((tile_rows, d), F32),
                   jax.ShapeDtypeStruct((n, LANES), F32), jax.ShapeDtypeStruct((1, 1, LANES), F32)],
        compiler_params=_params(),
        name="merge_sample",
    )(x, u, state_t, att, gates, wp, ps, wup, wua, wo, nf, wr, br)


def _moe_plan_body(cnt_ref, src_ref, base_ref, te_ref, tw_ref, loc_ref, *, n_ttiles, n_exp, tile_rows, gpt,
                   zero_group):
    n_groups, n_mm = src_ref.shape[0], te_ref.shape[0]

    def fill_src(g, c):
        src_ref[g] = zero_group
        return c

    def fill_tiles(t, c):
        te_ref[t] = -1
        tw_ref[t] = n_exp - 1
        return c

    def fill_loc(i, c):
        loc_ref[i] = 0
        return c

    lax.fori_loop(0, n_groups, fill_src, 0)
    lax.fori_loop(0, n_mm, fill_tiles, 0)
    lax.fori_loop(0, n_ttiles, fill_loc, 0)

    def per_expert(e, pos):
        def per_tile(i, p):
            g = (cnt_ref[i * n_exp + e] + (ROW_GROUP - 1)) // ROW_GROUP
            base_ref[i * n_exp + e] = p
            row0 = i * tile_rows + loc_ref[i] * ROW_GROUP

            def per_group(j, c):
                src_ref[p + j] = row0 + j * ROW_GROUP
                return c

            lax.fori_loop(0, g, per_group, 0)
            loc_ref[i] = loc_ref[i] + g
            return p + g

        end = lax.fori_loop(0, n_ttiles, per_tile, pos)
        end_pad = ((end + (gpt - 1)) // gpt) * gpt

        def mark(t, c):
            te_ref[t] = e
            tw_ref[t] = e
            return c

        lax.fori_loop(pos // gpt, end_pad // gpt, mark, 0)
        return end_pad

    lax.fori_loop(0, n_exp, per_expert, 0)


def _moe_plan(cnt, *, n_ttiles, n_exp, tile_rows, gpt, n_mm_tiles):
    smem = pl.BlockSpec(memory_space=pltpu.SMEM)
    zero_group = n_ttiles * tile_rows - ROW_GROUP
    return pl.pallas_call(
        functools.partial(_moe_plan_body, n_ttiles=n_ttiles, n_exp=n_exp, tile_rows=tile_rows, gpt=gpt,
                          zero_group=zero_group),
        in_specs=[smem], out_specs=[smem, smem, smem, smem],
        out_shape=[jax.ShapeDtypeStruct((n_mm_tiles * gpt,), I32), jax.ShapeDtypeStruct((n_ttiles * n_exp,), I32),
                   jax.ShapeDtypeStruct((n_mm_tiles,), I32), jax.ShapeDtypeStruct((n_mm_tiles,), I32)],
        scratch_shapes=[pltpu.SMEM((n_ttiles,), I32)],
        name="moe_plan",
    )(cnt)


def _moe_mm_body(te_ref, tw_ref, src_ref, xs_hbm, wg_ref, wu_ref, wd_ref, ys_ref, xbuf, wgb_ref, wub_ref, wdb_ref,
                 sems):
    i = pl.program_id(0)
    tm = ys_ref.shape[0]
    groups = tm // ROW_GROUP
    expert = te_ref[i]
    prev = te_ref[jnp.maximum(i - 1, 0)]

    def fetch(slot, tile):
        for k in range(groups):
            src = pl.multiple_of(src_ref[tile * groups + k], ROW_GROUP)
            pltpu.make_async_copy(xs_hbm.at[pl.ds(src, ROW_GROUP)],
                                  xbuf.at[slot, pl.ds(k * ROW_GROUP, ROW_GROUP)], sems.at[slot]).start()

    slot = i % 2

    @pl.when(i == 0)
    def _():
        fetch(0, 0)

    @pl.when(i + 1 < pl.num_programs(0))
    def _():
        fetch(1 - slot, i + 1)

    pltpu.make_async_copy(xs_hbm.at[pl.ds(0, tm)], xbuf.at[slot], sems.at[slot]).wait()

    @pl.when((expert >= 0) & ((i == 0) | (expert != prev)))
    def _():
        wgb_ref[...] = wg_ref[0].astype(BF16)
        wub_ref[...] = wu_ref[0].astype(BF16)
        wdb_ref[...] = wd_ref[0].astype(BF16)

    @pl.when(expert >= 0)
    def _():
        x = xbuf[slot].astype(BF16)
        a = jnp.dot(x, wgb_ref[...], preferred_element_type=F32)
        b = jnp.dot(x, wub_ref[...], preferred_element_type=F32)
        hdn = (a * _sigmoid(a)) * b
        ys_ref[...] = jnp.dot(hdn.astype(BF16), wdb_ref[...], preferred_element_type=F32)

    @pl.when(expert < 0)
    def _():
        ys_ref[...] = jnp.zeros_like(ys_ref)


def _moe_mm(tile_expert, tile_weight, group_src, xs, w_gate, w_up, w_down, *, tm):
    n_tiles = tile_expert.shape[0]
    d = xs.shape[1]
    n_exp, _, de = w_gate.shape
    wmap = lambda i, te, tw, src: (tw[i], 0, 0)
    grid_spec = pltpu.PrefetchScalarGridSpec(
        num_scalar_prefetch=3, grid=(n_tiles,),
        in_specs=[pl.BlockSpec(memory_space=pl.ANY),
                  pl.BlockSpec((1, d, de), wmap), pl.BlockSpec((1, d, de), wmap), pl.BlockSpec((1, de, d), wmap)],
        out_specs=pl.BlockSpec((tm, d), lambda i, te, tw, src: (i, 0)),
        scratch_shapes=[pltpu.VMEM((2, tm, d), F32), pltpu.VMEM((d, de), BF16), pltpu.VMEM((d, de), BF16),
                        pltpu.VMEM((de, d), BF16), pltpu.SemaphoreType.DMA((2,))])
    return pl.pallas_call(
        _moe_mm_body, grid_spec=grid_spec,
        out_shape=jax.ShapeDtypeStruct((n_tiles * tm, d), F32),
        compiler_params=_params("arbitrary"),
        name="moe_mm",
    )(tile_expert, tile_weight, group_src, xs, w_gate, w_up, w_down)


def _moe_combine_body(cnt_ref, base_ref, x_ref, slab_ref, g_ref, ys_hbm, o_ref, ybuf, sems,
                      *, final_norm, first_tile, n_exp):
    i = pl.program_id(0)
    tm = x_ref.shape[0]
    tile_rows = ybuf.shape[1]

    def segment_copies(slot, tile, act):
        def per_expert(e, first_group):
            n_groups = (cnt_ref[tile * n_exp + e] + (ROW_GROUP - 1)) // ROW_GROUP
            src_group = base_ref[tile * n_exp + e]

            def per_group(j, c):
                src = pl.multiple_of((src_group + j) * ROW_GROUP, ROW_GROUP)
                dst = pl.multiple_of((first_group + j) * ROW_GROUP, ROW_GROUP)
                act(pltpu.make_async_copy(ys_hbm.at[pl.ds(src, ROW_GROUP)], ybuf.at[slot, pl.ds(dst, ROW_GROUP)],
                                          sems.at[slot]))
                return c

            lax.fori_loop(0, n_groups, per_group, 0)
            return first_group + n_groups

        lax.fori_loop(0, n_exp, per_expert, 0)

    slot = i % 2

    @pl.when(i == 0)
    def _():
        ybuf[...] = jnp.zeros_like(ybuf)
        segment_copies(0, first_tile, lambda cp: cp.start())

    @pl.when(i + 1 < pl.num_programs(0))
    def _():
        segment_copies(1 - slot, first_tile + i + 1, lambda cp: cp.start())

    segment_copies(slot, first_tile + i, lambda cp: cp.wait())
    y = ybuf[slot].astype(BF16)
    slab = slab_ref[...]
    r = lax.broadcasted_iota(I32, (tm, tile_rows), 1).astype(F32)
    ya = jnp.dot((r == slab[:, 6:7]).astype(BF16), y, preferred_element_type=F32)
    yb = jnp.dot((r == slab[:, 7:8]).astype(BF16), y, preferred_element_type=F32)
    out = x_ref[...] + (slab[:, 4:5] * ya + slab[:, 5:6] * yb)
    if final_norm:
        out = _rmsnorm(out, g_ref[...])
    o_ref[...] = out


def _moe_combine(seg_count, seg_base, x, slab, g, ys, *, ts, tile_rows, final_norm, first_tile, n_exp):
    n, d = x.shape
    return pl.pallas_call(
        functools.partial(_moe_combine_body, final_norm=final_norm, first_tile=first_tile, n_exp=n_exp),
        grid=(n // ts,),
        in_specs=[pl.BlockSpec(memory_space=pltpu.SMEM), pl.BlockSpec(memory_space=pltpu.SMEM),
                  pl.BlockSpec((ts, d), lambda i: (i, 0)), pl.BlockSpec((ts, LANES), lambda i: (i, 0)),
                  pl.BlockSpec((1, d), lambda i: (0, 0)), pl.BlockSpec(memory_space=pl.ANY)],
        out_specs=pl.BlockSpec((ts, d), lambda i: (i, 0)),
        out_shape=jax.ShapeDtypeStruct((n, d), F32),
        scratch_shapes=[pltpu.VMEM((2, tile_rows, d), F32), pltpu.SemaphoreType.DMA((2,))],
        compiler_params=_params("arbitrary"),
        name="moe_combine",
    )(seg_count, seg_base, x, slab, g, ys)


def kernel(x_prompt, x_sample, cache_k, cache_v, cache_logf, state_pool, page_table, norm_mix, w_in, b_forget,
           w_pool, pool_scale, w_up_pool, w_up_att, w_out, norm_ffn, w_router_group, b_router_group,
           w_router_expert, b_router_expert, w_gate, w_up, w_down, norm_final):
    depth = norm_mix.shape[0]
    assert depth == 1, "single trunk layer"
    b, t, d = x_prompt.shape
    db, dt, _ = x_sample.shape
    assert dt == 1, "one sample token per sequence"
    _, n_phys, page, n_heads, dh = cache_k.shape
    n_pages = page_table.shape[1]
    past = n_pages * page
    n_state, d_pool = state_pool.shape[2], state_pool.shape[3]
    d_att = n_heads * dh
    n_pool_groups = w_pool.shape[1]
    assert n_pool_groups == len(POOL_WINDOWS) and d_pool // n_pool_groups == LANES
    assert n_state == max(POOL_WINDOWS) - 1 and n_state < POOL_HALO
    n_groups, n_per_group = w_router_expert.shape[1], w_router_expert.shape[3]
    n_exp = n_groups * n_per_group
    assert n_groups + n_exp <= LANES and 2 * dh == LANES and n_heads % 2 == 0
    n = b * t
    q_scale = float(dh) ** -0.5
    tm = min(TOKEN_TILE, t)
    assert t % tm == 0 and t % ATTN_TILE == 0

    o_main = d_pool + 3 * d_att
    wi = w_in[0]
    wm_f, wf_f, wg_f = wi[:, :o_main], wi[:, o_main:o_main + n_heads], wi[:, o_main + n_heads:]
    wf_pad = jnp.pad(wf_f, ((0, 0), (0, LANES - n_heads)))
    wit = jnp.transpose(wi)
    wmt_f, wgt_f = wit[:o_main], wit[o_main + n_heads:]
    wft_pad = jnp.pad(wit[o_main:o_main + n_heads], ((0, LANES - n_heads), (0, 0)))
    bf_pad = jnp.pad(b_forget[0], (0, LANES - n_heads)).reshape(1, LANES)
    g_mix = norm_mix[0].reshape(1, d)
    g_ffn = norm_ffn[0].reshape(1, d)
    g_fin = norm_final.reshape(1, d)
    ps = pool_scale[0].reshape(1, d_pool)
    wr_f = jnp.concatenate([w_router_group[0], jnp.transpose(w_router_expert[0], (1, 0, 2)).reshape(d, n_exp)], axis=1)
    wr_pad = jnp.pad(wr_f, ((0, 0), (0, LANES - n_groups - n_exp)))
    br_pad = jnp.pad(jnp.concatenate([b_router_group[0], b_router_expert[0].reshape(n_exp)]),
                     (0, LANES - n_groups - n_exp)).reshape(1, LANES)
    bf = lambda a: a.astype(BF16)

    xp = x_prompt.reshape(n, d)
    u_p, q_p, kt_p, vt_p, kb_p, vb_p, lft_p, gate_p = _proj_prompt(
        xp, g_mix, bf(wm_f), bf(wf_pad), bf(wg_f), bf_pad, tm=tm, seq_len=t, d_pool=d_pool, d_att=d_att,
        n_heads=n_heads, q_scale=q_scale)
    c = _cumsum_lanes(lft_p.reshape(b * n_heads, t))
    nt = t // ATTN_TILE
    c_blk = jnp.transpose(c.reshape(b, n_heads // 2, 2, nt, ATTN_TILE), (0, 1, 3, 2, 4))
    att_p = _attn_prompt(q_p.reshape(b, t, d_att), kb_p.reshape(b, t, d_att), vb_p.reshape(b, t, d_att), c_blk,
                         tile=ATTN_TILE, dh=dh)
    n_ptiles = n // tm
    n_ttiles = n_ptiles + 1
    tile_rows = -(-(2 * tm + n_exp * (ROW_GROUP - 1)) // MOE_ROW_TILE) * MOE_ROW_TILE
    assert 2 * db + n_exp * (ROW_GROUP - 1) <= tile_rows - ROW_GROUP, "the sample tile must end in an unused row group"

    xs = x_sample.reshape(db, d)
    z_s, lf_s, gate_s = _proj_sample(xs, g_mix, wmt_f, wft_pad, wgt_f, bf_pad, tn=512)
    u_s = z_s[:, :d_pool]
    q_s = z_s[:, d_pool:d_pool + d_att] * q_scale
    k_s = z_s[:, d_pool + d_att:d_pool + 2 * d_att]
    v_s = z_s[:, d_pool + 2 * d_att:]
    att_s = _attn_sample(page_table, q_s, k_s, v_s, lf_s[:, :n_heads],
                         jnp.transpose(cache_k[0], (0, 2, 3, 1)), jnp.transpose(cache_v[0], (0, 2, 3, 1)),
                         jnp.transpose(cache_logf[0], (0, 2, 1)))
    state_t = jnp.transpose(state_pool[0], (1, 0, 2))
    x2_s, xs_rows_s, slab_s, counts_s = _merge_sample(
        xs, u_s, state_t, att_s.reshape(db, d_att), gate_s, w_pool[0], ps, w_up_pool[0], w_up_att[0], w_out[0],
        g_ffn, wr_pad, br_pad, start_pos=past, n_groups=n_groups, n_per_group=n_per_group, tile_rows=tile_rows)

    x2_p, xs_rows, slab_p, counts_p = _merge_prompt(
        xp, u_p, att_p.reshape(n, d_att), gate_p, bf(w_pool[0]), ps, bf(w_up_pool[0]), bf(w_up_att[0]),
        bf(w_out[0]), g_ffn, bf(wr_pad), br_pad, xs_rows_s, tm=tm, seq_len=t, n_groups=n_groups,
        n_per_group=n_per_group)

    tmm = MOE_ROW_TILE
    gpt = tmm // ROW_GROUP
    n_groups_max = -(-(2 * (n + db)) // ROW_GROUP) + n_ttiles * n_exp + n_exp * (gpt - 1)
    cnt = jnp.concatenate([counts_p, counts_s], axis=0)[:, 0, :n_exp].astype(I32).reshape(-1)
    group_src, seg_base, tile_expert, tile_weight = _moe_plan(
        cnt, n_ttiles=n_ttiles, n_exp=n_exp, tile_rows=tile_rows, gpt=gpt, n_mm_tiles=-(-n_groups_max // gpt))
    ys_rows = _moe_mm(tile_expert, tile_weight, group_src, xs_rows, w_gate[0], w_up[0], w_down[0], tm=tmm)
    y_prompt = _moe_combine(cnt, seg_base, x2_p, slab_p, g_fin, ys_rows, ts=tm, tile_rows=tile_rows,
                            final_norm=True, first_tile=0, n_exp=n_exp)
    y_sample = _moe_combine(cnt, seg_base, x2_s, slab_s, g_fin, ys_rows, ts=db, tile_rows=tile_rows,
                            final_norm=True, first_tile=n_ptiles, n_exp=n_exp)

    new_pool_p = u_p.reshape(b, t, d_pool)[:, t - n_state:, :]
    new_pool_s = jnp.concatenate([state_pool[0][:, 1:, :], u_s[:, None, :]], axis=1)
    to_heads = lambda a: jnp.transpose(a.reshape(b, n_heads, dh, t), (0, 3, 1, 2))[None]
    return (y_prompt.reshape(b, t, d), y_sample.reshape(db, 1, d),
            to_heads(kt_p), to_heads(vt_p), jnp.transpose(lft_p, (0, 2, 1))[None],
            new_pool_p[None],
            k_s.reshape(1, db, 1, n_heads, dh), v_s.reshape(1, db, 1, n_heads, dh),
            lf_s[:, :n_heads].reshape(1, db, 1, n_heads), new_pool_s[None])
```

```python
import functools

import jax
import jax.numpy as jnp
from jax import lax
from jax.experimental import pallas as pl
from jax.experimental.pallas import tpu as pltpu

F32 = jnp.float32
BF16 = jnp.bfloat16
I32 = jnp.int32
HIGHEST = lax.Precision.HIGHEST

RMS_EPS = 1e-6
POOL_WINDOWS = (2, 4, 8, 16)
POOL_HALO = 16
LANES = 128
VMEM_LIMIT_BYTES = 56 * 1024 * 1024

TOKEN_TILE = 512
ATTN_TILE = 512
MOE_ROW_TILE = 256
PAGES_PER_STEP = 16
ROW_GROUP = 8


def _params(*sem):
    return pltpu.CompilerParams(dimension_semantics=sem, vmem_limit_bytes=VMEM_LIMIT_BYTES)


def _rmsnorm(x, g):
    return x * lax.rsqrt(jnp.mean(x * x, axis=-1, keepdims=True) + RMS_EPS) * g


def _log_sigmoid(x):
    return jnp.minimum(x, 0.0) - jnp.log1p(jnp.exp(-jnp.abs(x)))


def _sigmoid(x):
    return 1.0 / (1.0 + jnp.exp(-x))


def _dot(a, b, precise):
    if precise:
        return jnp.dot(a.astype(F32), b.astype(F32), precision=HIGHEST, preferred_element_type=F32)
    return jnp.dot(a.astype(BF16), b.astype(BF16), preferred_element_type=F32)


def _split3(x):
    hi = x.astype(BF16)
    r = x - hi.astype(F32)
    mid = r.astype(BF16)
    lo = (r - mid.astype(F32)).astype(BF16)
    return hi, mid, lo


def _dot_exact_rhs(x, w_bf16):
    hi, mid, lo = _split3(x)
    d = lambda a: jnp.dot(a, w_bf16, preferred_element_type=F32)
    return d(hi) + d(mid) + d(lo)


def _proj_body(x_ref, g_ref, wm_ref, wf_ref, wg_ref, bf_ref,
               u_ref, q_ref, kt_ref, vt_ref, kb_ref, vb_ref, lft_ref, gate_ref, *, d_pool, d_att, n_heads, q_scale):
    h = _rmsnorm(x_ref[...], g_ref[...]).astype(BF16)
    z = jnp.dot(h, wm_ref[...], preferred_element_type=F32)
    o1, o2, o3 = d_pool, d_pool + d_att, d_pool + 2 * d_att
    u_ref[...] = z[:, :o1]
    q_ref[...] = (z[:, o1:o2] * q_scale).astype(BF16)
    k = z[:, o2:o3]
    v = z[:, o3:]
    kt_ref[0] = k.T
    vt_ref[0] = v.T
    kb_ref[...] = k.astype(BF16)
    vb_ref[...] = v.astype(BF16)
    lf = _log_sigmoid(jnp.dot(h, wf_ref[...], preferred_element_type=F32) + bf_ref[...])
    lft_ref[0] = lf.T[0:n_heads, :]
    gate_ref[...] = _sigmoid(jnp.dot(h, wg_ref[...], preferred_element_type=F32)).astype(BF16)


def _proj_prompt(x, g, wm, wf, wg, bfp, *, tm, seq_len, d_pool, d_att, n_heads, q_scale):
    n, d = x.shape
    b = n // seq_len
    tps = seq_len // tm
    row = lambda i: (i, 0)
    const = lambda i: (0, 0)
    tmin = lambda i: (i // tps, 0, i % tps)
    dg = wg.shape[1]
    out_shape = [
        jax.ShapeDtypeStruct((n, d_pool), F32), jax.ShapeDtypeStruct((n, d_att), BF16),
        jax.ShapeDtypeStruct((b, d_att, seq_len), F32), jax.ShapeDtypeStruct((b, d_att, seq_len), F32),
        jax.ShapeDtypeStruct((n, d_att), BF16), jax.ShapeDtypeStruct((n, d_att), BF16),
        jax.ShapeDtypeStruct((b, n_heads, seq_len), F32), jax.ShapeDtypeStruct((n, dg), BF16),
    ]
    return pl.pallas_call(
        functools.partial(_proj_body, d_pool=d_pool, d_att=d_att, n_heads=n_heads, q_scale=q_scale),
        grid=(n // tm,),
        in_specs=[pl.BlockSpec((tm, d), row), pl.BlockSpec((1, d), const),
                  pl.BlockSpec(wm.shape, const), pl.BlockSpec(wf.shape, const),
                  pl.BlockSpec(wg.shape, const), pl.BlockSpec((1, LANES), const)],
        out_specs=[pl.BlockSpec((tm, d_pool), row), pl.BlockSpec((tm, d_att), row),
                   pl.BlockSpec((1, d_att, tm), tmin), pl.BlockSpec((1, d_att, tm), tmin),
                   pl.BlockSpec((tm, d_att), row), pl.BlockSpec((tm, d_att), row),
                   pl.BlockSpec((1, n_heads, tm), tmin), pl.BlockSpec((tm, dg), row)],
        out_shape=out_shape,
        compiler_params=_params("arbitrary"),
        name="proj_prompt",
    )(x, g, wm, wf, wg, bfp)


def _dot_nt(a, bt, precise):
    dims = (((1,), (1,)), ((), ()))
    if precise:
        return lax.dot_general(a.astype(F32), bt.astype(F32), dims, precision=HIGHEST, preferred_element_type=F32)
    return lax.dot_general(a.astype(BF16), bt.astype(BF16), dims, preferred_element_type=F32)


def _proj_sample_body(x_ref, g_ref, wmt_ref, wft_ref, wgt_ref, bf_ref, z_ref, lf_ref, gate_ref):
    h = _rmsnorm(x_ref[...], g_ref[...])
    z_ref[...] = _dot_nt(h, wmt_ref[...], True)
    lf_ref[...] = _log_sigmoid(_dot_nt(h, wft_ref[...], True) + bf_ref[...])
    gate_ref[...] = _sigmoid(_dot_nt(h, wgt_ref[...], True))


def _proj_sample(x, g, wmt, wft, wgt, bfp, *, tn):
    n, d = x.shape
    dm, dg = wmt.shape[0], wgt.shape[0]
    assert dm == dg
    const = lambda j: (0, 0)
    chunk = lambda j: (j, 0)
    col = lambda j: (0, j)
    return pl.pallas_call(
        _proj_sample_body,
        grid=(dm // tn,),
        in_specs=[pl.BlockSpec((n, d), const), pl.BlockSpec((1, d), const),
                  pl.BlockSpec((tn, d), chunk), pl.BlockSpec(wft.shape, const),
                  pl.BlockSpec((tn, d), chunk), pl.BlockSpec((1, LANES), const)],
        out_specs=[pl.BlockSpec((n, tn), col), pl.BlockSpec((n, LANES), const), pl.BlockSpec((n, tn), col)],
        out_shape=[jax.ShapeDtypeStruct((n, dm), F32), jax.ShapeDtypeStruct((n, LANES), F32),
                   jax.ShapeDtypeStruct((n, dg), F32)],
        compiler_params=_params("arbitrary"),
        name="proj_sample",
    )(x, g, wmt, wft, wgt, bfp)


def _cumsum_body(x_ref, o_ref):
    c = x_ref[...]
    lane = lax.broadcasted_iota(I32, c.shape, 1)
    s = 1
    while s < c.shape[1]:
        c = c + jnp.where(lane >= s, pltpu.roll(c, s, 1), 0.0)
        s *= 2
    o_ref[...] = c


def _cumsum_lanes(x):
    return pl.pallas_call(_cumsum_body, out_shape=jax.ShapeDtypeStruct(x.shape, F32),
                          compiler_params=_params(), name="cumsum_logf")(x)


def _attn_body(q_ref, k_ref, v_ref, c_ref, o_ref, *, tile, dh):
    nt = q_ref.shape[1] // tile
    lane = lax.broadcasted_iota(I32, (tile, 2 * dh), 1)
    first = lane < dh
    row = lax.broadcasted_iota(I32, (tile, tile), 0)
    col = lax.broadcasted_iota(I32, (tile, tile), 1)
    causal = col <= row
    one = jnp.ones((tile, 2 * dh), BF16)
    kts, vhs = [], []
    for kj in range(nt):
        vt = v_ref[0, kj * tile:(kj + 1) * tile, :]
        kts.append(k_ref[0, kj * tile:(kj + 1) * tile, :])
        vhs.append((jnp.where(first, vt, one), jnp.where(first, one, vt)))
    for qi in range(nt):
        q = q_ref[0, qi * tile:(qi + 1) * tile, :]
        zero = jnp.zeros_like(q)
        q_heads = (jnp.where(first, q, zero), jnp.where(first, zero, q))
        res = []
        for h in range(2):
            m = jnp.full((tile, 1), -1e30, F32)
            acc = jnp.zeros((tile, 2 * dh), F32)
            for kj in range(qi + 1):
                s = lax.dot_general(q_heads[h], kts[kj], (((1,), (1,)), ((), ())), preferred_element_type=F32)
                s = s - c_ref[0, 0, kj][h:h + 1, :]
                if kj == qi:
                    s = jnp.where(causal, s, -jnp.inf)
                m_new = jnp.maximum(m, jnp.max(s, axis=-1, keepdims=True))
                alpha = jnp.exp(m - m_new)
                p = jnp.exp(s - m_new)
                acc = alpha * acc + jnp.dot(p.astype(BF16), vhs[kj][h], preferred_element_type=F32)
                m = m_new
            res.append(acc)
        a0, a1 = res
        out = jnp.where(first, a0 / a0[:, dh:dh + 1], a1 / a1[:, 0:1])
        o_ref[0, qi * tile:(qi + 1) * tile, :] = out.astype(o_ref.dtype)


def _attn_prompt(q, k, v, c, *, tile, dh):
    b, t, da = q.shape
    hp = da // (2 * dh)
    nt = t // tile
    pair = pl.BlockSpec((1, t, 2 * dh), lambda bi, hi: (bi, 0, hi))
    return pl.pallas_call(
        functools.partial(_attn_body, tile=tile, dh=dh),
        grid=(b, hp),
        in_specs=[pair, pair, pair, pl.BlockSpec((1, 1, nt, 2, tile), lambda bi, hi: (bi, hi, 0, 0, 0))],
        out_specs=pair,
        out_shape=jax.ShapeDtypeStruct((b, t, da), BF16),
        compiler_params=_params("arbitrary", "arbitrary"),
        name="attn_prompt",
    )(q, k, v, c)


def _attn_sample_body(pt_ref, qrep_ref, q_ref, kn_ref, vrep_ref, lfn_ref, ck_hbm, cv_hbm, clf_hbm, o_ref,
                      m_ref, l_ref, acc_ref, srun_ref, kbuf, vbuf, lfbuf, sems):
    seq = pl.program_id(0)
    step = pl.program_id(1)
    n_steps = pl.num_programs(1)
    n_pages = pt_ref.shape[1]
    _, g_n, n_heads, dh, page = kbuf.shape
    d_att = n_heads * dh

    def fetch(slot, b, s):
        for g in range(g_n):
            pid = pt_ref[b, n_pages - 1 - (s * g_n + g)]
            pltpu.make_async_copy(ck_hbm.at[pid], kbuf.at[slot, g], sems.at[slot, 0]).start()
            pltpu.make_async_copy(cv_hbm.at[pid], vbuf.at[slot, g], sems.at[slot, 1]).start()
            pltpu.make_async_copy(clf_hbm.at[pid], lfbuf.at[slot, g], sems.at[slot, 2]).start()

    gstep = seq * n_steps + step
    slot = gstep % 2

    @pl.when(gstep == 0)
    def _():
        fetch(0, 0, 0)

    @pl.when(gstep + 1 < pl.num_programs(0) * n_steps)
    def _():
        last = step == n_steps - 1
        fetch(1 - slot, jnp.where(last, seq + 1, seq), jnp.where(last, 0, step + 1))

    pltpu.make_async_copy(ck_hbm.at[pl.ds(0, g_n)], kbuf.at[slot], sems.at[slot, 0]).wait()
    pltpu.make_async_copy(cv_hbm.at[pl.ds(0, g_n)], vbuf.at[slot], sems.at[slot, 1]).wait()
    pltpu.make_async_copy(clf_hbm.at[pl.ds(0, g_n)], lfbuf.at[slot], sems.at[slot, 2]).wait()
    k_refs = [kbuf.at[slot, g] for g in range(g_n)]
    v_refs = [vbuf.at[slot, g] for g in range(g_n)]
    lf_refs = [lfbuf.at[slot, g] for g in range(g_n)]

    @pl.when(step == 0)
    def _():
        s_new = jnp.sum(q_ref[0] * kn_ref[0], axis=-1, keepdims=True)
        m_ref[...] = jnp.broadcast_to(s_new, m_ref.shape)
        l_ref[...] = jnp.ones_like(l_ref)
        lane = lax.broadcasted_iota(I32, (d_att, page), 1)
        acc_ref[...] = jnp.where(lane == 0, vrep_ref[0], 0.0)
        srun_ref[...] = lfn_ref[0]

    qrep = qrep_ref[0]
    r = lax.broadcasted_iota(I32, (page, page), 0)
    c = lax.broadcasted_iota(I32, (page, page), 1)
    later = (r > c).astype(BF16)
    ones = jnp.ones((page, page), BF16)
    lf_all = jnp.concatenate([lf_refs[g][...] for g in range(g_n)], axis=0)
    suffix = _dot_exact_rhs(lf_all, later)
    total = _dot_exact_rhs(lf_all, ones)
    s_run = srun_ref[...]
    m_prev = m_ref[...]
    m_new = m_prev
    scores = []
    for g in range(g_n):
        kq = k_refs[g][...].reshape(d_att, page) * qrep
        s = jnp.sum(kq.reshape(n_heads, dh, page), axis=1)
        sb = s + s_run + suffix[g * n_heads:(g + 1) * n_heads]
        s_run = s_run + total[g * n_heads:(g + 1) * n_heads]
        scores.append(sb)
        m_new = jnp.maximum(m_new, jnp.max(sb, axis=-1, keepdims=True))
    srun_ref[...] = s_run
    alpha = jnp.exp(m_prev - m_new)
    l = alpha * l_ref[...]
    acc = acc_ref[...].reshape(n_heads, dh, page) * alpha[:, None, :]
    for g in range(g_n):
        p = jnp.exp(scores[g] - m_new)
        l = l + jnp.sum(p, axis=-1, keepdims=True)
        acc = acc + v_refs[g][...] * p[:, None, :]
    m_ref[...] = m_new
    l_ref[...] = l
    acc_ref[...] = acc.reshape(d_att, page)

    @pl.when(step == pl.num_programs(1) - 1)
    def _():
        o_ref[0] = jnp.sum(acc / l[:, None, :], axis=-1)


def _attn_sample(page_table, q, k_new, v_new, lf_new, cache_kt, cache_vt, cache_lft):
    db, n_pages = page_table.shape
    n_phys, n_heads, dh, page = cache_kt.shape
    d_att = n_heads * dh
    g_n = PAGES_PER_STEP
    while n_pages % g_n:
        g_n //= 2
    n_steps = n_pages // g_n
    lane_rep = lambda a: jnp.broadcast_to(a.reshape(db, -1, 1), (db, a.size // db, page))

    per_seq = lambda b, s, pt: (b, 0, 0)
    hbm = pl.BlockSpec(memory_space=pl.ANY)
    in_specs = [pl.BlockSpec((1, d_att, page), per_seq), pl.BlockSpec((1, n_heads, dh), per_seq),
                pl.BlockSpec((1, n_heads, dh), per_seq), pl.BlockSpec((1, d_att, page), per_seq),
                pl.BlockSpec((1, n_heads, page), per_seq), hbm, hbm, hbm]
    grid_spec = pltpu.PrefetchScalarGridSpec(
        num_scalar_prefetch=1, grid=(db, n_steps), in_specs=in_specs,
        out_specs=pl.BlockSpec((1, n_heads, dh), per_seq),
        scratch_shapes=[pltpu.VMEM((n_heads, page), F32), pltpu.VMEM((n_heads, page), F32),
                        pltpu.VMEM((d_att, page), F32), pltpu.VMEM((n_heads, page), F32),
                        pltpu.VMEM((2, g_n, n_heads, dh, page), F32), pltpu.VMEM((2, g_n, n_heads, dh, page), F32),
                        pltpu.VMEM((2, g_n, n_heads, page), F32), pltpu.SemaphoreType.DMA((2, 3))])
    return pl.pallas_call(
        _attn_sample_body,
        grid_spec=grid_spec,
        out_shape=jax.ShapeDtypeStruct((db, n_heads, dh), F32),
        compiler_params=_params("arbitrary", "arbitrary"),
        name="attn_sample",
    )(page_table, lane_rep(q), q.reshape(db, n_heads, dh), k_new.reshape(db, n_heads, dh), lane_rep(v_new),
      lane_rep(lf_new), cache_kt, cache_vt, cache_lft)


def _merge_and_route(x, pooled, att, gates, wp_ref, ps_ref, wup_ref, wua_ref, wo_ref, nf_ref, wr_ref, br_ref,
                     *, precise, n_groups, n_per_group):
    tm, d = x.shape
    gw = pooled[0].shape[1]
    mixed = jnp.concatenate([_dot(pooled[g], wp_ref[g], precise) for g in range(len(pooled))], axis=-1)
    pool_out = mixed * ps_ref[...]
    y = gates[:, :d].astype(F32) * _dot(pool_out, wup_ref[...], precise) \
        + gates[:, d:].astype(F32) * _dot(att, wua_ref[...], precise)
    x2 = x + _dot(y, wo_ref[...], precise)
    h2 = _rmsnorm(x2, nf_ref[...])
    logits = _dot(h2, wr_ref[...], precise) + br_ref[...]
    lane = lax.broadcasted_iota(I32, logits.shape, 1)
    lanef = lane.astype(F32)
    neg = -jnp.inf
    is_g = lane < n_groups
    gmax = jnp.max(jnp.where(is_g, logits, neg), axis=-1, keepdims=True)
    gidx = jnp.min(jnp.where(is_g & (logits == gmax), lanef, float(LANES)), axis=-1, keepdims=True)
    gsum = jnp.sum(jnp.where(is_g, jnp.exp(logits - gmax), 0.0), axis=-1, keepdims=True)
    g_w = 1.0 / gsum
    n_exp = n_groups * n_per_group
    exp_id = lanef - float(n_groups)
    in_sel = (lane >= n_groups) & (lane < n_groups + n_exp) & (jnp.floor(exp_id / n_per_group) == gidx)
    v1 = jnp.max(jnp.where(in_sel, logits, neg), axis=-1, keepdims=True)
    i1 = jnp.min(jnp.where(in_sel & (logits == v1), lanef, float(LANES)), axis=-1, keepdims=True)
    in_sel2 = in_sel & (lanef != i1)
    v2 = jnp.max(jnp.where(in_sel2, logits, neg), axis=-1, keepdims=True)
    i2 = jnp.min(jnp.where(in_sel2 & (logits == v2), lanef, float(LANES)), axis=-1, keepdims=True)
    t = jnp.exp(v2 - v1)
    w1 = g_w * (1.0 / (1.0 + t))
    w2 = g_w * (t / (1.0 + t))
    e1 = i1 - float(n_groups)
    e2 = i2 - float(n_groups)
    hit1 = lanef == e1
    hit2 = lanef == e2
    onehot = (hit1 | hit2).astype(BF16)
    rr = lax.broadcasted_iota(I32, (tm, tm), 0)
    cc = lax.broadcasted_iota(I32, (tm, tm), 1)
    incl = jnp.dot((cc <= rr).astype(BF16), onehot, preferred_element_type=F32)
    counts = incl[tm - 1:tm, :]
    groups = jnp.floor((counts + (ROW_GROUP - 1.0)) * (1.0 / ROW_GROUP))
    ur = lax.broadcasted_iota(I32, (LANES, LANES), 0)
    uc = lax.broadcasted_iota(I32, (LANES, LANES), 1)
    before = jnp.dot(jnp.broadcast_to(groups, (8, LANES)).astype(BF16), (ur < uc).astype(BF16),
                     preferred_element_type=F32)[0:1]
    seg_start = before * float(ROW_GROUP)
    pick = lambda hit, tbl: jnp.sum(jnp.where(hit, tbl, 0.0), axis=-1, keepdims=True)
    r1 = pick(hit1, incl) - 1.0
    r2 = pick(hit2, incl) - 1.0
    row1 = pick(hit1, seg_start) + r1
    row2 = pick(hit2, seg_start) + r2
    slab = jnp.zeros((tm, LANES), F32)
    for i, val in enumerate((e1, e2, r1, r2, w1, w2, row1, row2)):
        slab = jnp.where(lane == i, val, slab)
    return x2, h2, slab, counts


def _sorted_copy(h2, row1, row2, n_rows):
    tm = h2.shape[0]
    r = lax.broadcasted_iota(I32, (n_rows, tm), 0).astype(F32)
    place = ((r == row1) | (r == row2)).astype(BF16)
    return jnp.dot(place, h2.astype(BF16), preferred_element_type=F32)


def _merge_prompt_body(x_ref, u_ref, halo_ref, att_ref, gate_ref, wp_ref, ps_ref, wup_ref, wua_ref, wo_ref,
                       nf_ref, wr_ref, br_ref, xs_last_hbm, x2_ref, xs_ref, slab_ref, counts_ref,
                       ext_ref, sem, *, seq_len, n_groups, n_per_group):
    i = pl.program_id(0)
    n_tiles = pl.num_programs(0) - 1
    tm = x_ref.shape[0]
    gw = u_ref.shape[1] // len(POOL_WINDOWS)

    @pl.when(i < n_tiles)
    def _():
        pos0 = (i * tm) % seq_len
        u = u_ref[...]
        ext_ref[0:POOL_HALO, :] = jnp.where(pos0 == 0, 0.0, halo_ref[...])
        ext_ref[POOL_HALO:, :] = u
        pos = pos0 + lax.broadcasted_iota(I32, (tm, 1), 0)
        pooled = []
        for g, w in enumerate(POOL_WINDOWS):
            lo = g * gw
            wsum = ext_ref[pl.ds(POOL_HALO, tm), lo:lo + gw]
            for j in range(1, w):
                wsum = wsum + ext_ref[pl.ds(POOL_HALO - j, tm), lo:lo + gw]
            count = jnp.minimum(pos + 1, w).astype(F32)
            pooled.append(wsum / count - u[:, lo:lo + gw])
        x2, h2, slab, counts = _merge_and_route(
            x_ref[...], pooled, att_ref[...], gate_ref[...], wp_ref, ps_ref, wup_ref, wua_ref, wo_ref, nf_ref,
            wr_ref, br_ref, precise=False, n_groups=n_groups, n_per_group=n_per_group)
        fields = slab.T[0:8, :]
        x2_ref[...] = x2
        xs_ref[...] = _sorted_copy(h2, fields[6:7, :], fields[7:8, :], xs_ref.shape[0])
        slab_ref[...] = slab
        counts_ref[0] = counts

    @pl.when(i == n_tiles)
    def _():
        copy = pltpu.make_async_copy(xs_last_hbm, xs_ref, sem)
        copy.start()
        copy.wait()


def _merge_prompt(x, u, att, gates, wp, ps, wup, wua, wo, nf, wr, br, xs_last, *, tm, seq_len, n_groups, n_per_group):
    n, d = x.shape
    d_pool, d_att = u.shape[1], att.shape[1]
    tile_rows = xs_last.shape[0]
    nt = n // tm
    clamp = lambda i: jnp.minimum(i, nt - 1)
    row = lambda i: (clamp(i), 0)
    const = lambda i: (0, 0)
    const3 = lambda i: (0, 0, 0)
    halo = lambda i: (jnp.maximum(clamp(i) * (tm // POOL_HALO) - 1, 0), 0)
    return pl.pallas_call(
        functools.partial(_merge_prompt_body, seq_len=seq_len, n_groups=n_groups, n_per_group=n_per_group),
        grid=(nt + 1,),
        in_specs=[pl.BlockSpec((tm, d), row), pl.BlockSpec((tm, d_pool), row), pl.BlockSpec((POOL_HALO, d_pool), halo),
                  pl.BlockSpec((tm, d_att), row), pl.BlockSpec((tm, 2 * d), row),
                  pl.BlockSpec(wp.shape, const3), pl.BlockSpec((1, d_pool), const),
                  pl.BlockSpec(wup.shape, const), pl.BlockSpec(wua.shape, const), pl.BlockSpec(wo.shape, const),
                  pl.BlockSpec((1, d), const), pl.BlockSpec(wr.shape, const), pl.BlockSpec((1, LANES), const),
                  pl.BlockSpec(memory_space=pl.ANY)],
        out_specs=[pl.BlockSpec((tm, d), row), pl.BlockSpec((tile_rows, d), lambda i: (i, 0)),
                   pl.BlockSpec((tm, LANES), row), pl.BlockSpec((1, 1, LANES), lambda i: (clamp(i), 0, 0))],
        out_shape=[jax.ShapeDtypeStruct((n, d), F32), jax.ShapeDtypeStruct(((nt + 1) * tile_rows, d), F32),
                   jax.ShapeDtypeStruct((n, LANES), F32), jax.ShapeDtypeStruct((nt, 1, LANES), F32)],
        scratch_shapes=[pltpu.VMEM((tm + POOL_HALO, d_pool), F32), pltpu.SemaphoreType.DMA],
        compiler_params=_params("arbitrary"),
        name="merge_prompt",
    )(x, u, u, att, gates, wp, ps, wup, wua, wo, nf, wr, br, xs_last)


def _merge_sample_body(x_ref, u_ref, st_ref, att_ref, gate_ref, wp_ref, ps_ref, wup_ref, wua_ref, wo_ref,
                       nf_ref, wr_ref, br_ref, x2_ref, xs_ref, slab_ref, counts_ref,
                       *, start_pos, n_groups, n_per_group):
    u = u_ref[...]
    gw = u.shape[1] // len(POOL_WINDOWS)
    n_state = st_ref.shape[0]
    pooled = []
    for g, w in enumerate(POOL_WINDOWS):
        lo = g * gw
        wsum = u[:, lo:lo + gw]
        for j in range(1, w):
            wsum = wsum + st_ref[n_state - j][:, lo:lo + gw]
        pooled.append(wsum / float(min(start_pos + 1, w)) - u[:, lo:lo + gw])
    x2, h2, slab, counts = _merge_and_route(
        x_ref[...], pooled, att_ref[...], gate_ref[...], wp_ref, ps_ref, wup_ref, wua_ref, wo_ref, nf_ref,
        wr_ref, br_ref, precise=True, n_groups=n_groups, n_per_group=n_per_group)
    n = slab.shape[0]
    fields = jnp.concatenate([slab, jnp.zeros((LANES - n, LANES), F32)], axis=0).T
    x2_ref[...] = x2
    xs_ref[...] = _sorted_copy(h2, fields[6:7, 0:n], fields[7:8, 0:n], xs_ref.shape[0])
    slab_ref[...] = slab
    counts_ref[0] = counts


def _merge_sample(x, u, state_t, att, gates, wp, ps, wup, wua, wo, nf, wr, br, *, start_pos, n_groups,
                  n_per_group, tile_rows):
    n, d = x.shape
    assert n <= LANES
    return pl.pallas_call(
        functools.partial(_merge_sample_body, start_pos=start_pos, n_groups=n_groups, n_per_group=n_per_group),
        out_shape=[jax.ShapeDtypeStruct((n, d), F32), jax.ShapeDtypeStruct((tile_rows, d), F32),
                   jax.ShapeDtypeStruct((n, LANES), F32), jax.ShapeDtypeStruct((1, 1, LANES), F32)],
        compiler_params=_params(),
        name="merge_sample",
    )(x, u, state_t, att, gates, wp, ps, wup, wua, wo, nf, wr, br)


def _moe_plan_body(cnt_ref, gsrc_ref, lsrc_ref, te_ref, tw_ref, loc_ref, *, n_ttiles, n_exp, tile_rows, gpt,
                   seg_groups_max, local_stride, zero_group):
    n_mm = te_ref.shape[0]
    gsrc_ref[...] = jnp.full(gsrc_ref.shape, zero_group, I32)
    lsrc_ref[...] = jnp.zeros(lsrc_ref.shape, I32)

    def fill_tiles(t, c):
        te_ref[t] = -1
        tw_ref[t] = n_exp - 1
        return c

    def fill_loc(i, c):
        loc_ref[i] = 0
        return c

    lax.fori_loop(0, n_mm, fill_tiles, 0)
    lax.fori_loop(0, n_ttiles, fill_loc, 0)
    step = lax.broadcasted_iota(I32, (seg_groups_max, LANES), 0)

    def per_expert(e, pos):
        def per_tile(i, p):
            g = (cnt_ref[i * n_exp + e] + (ROW_GROUP - 1)) // ROW_GROUP
            loc = loc_ref[i]
            gsrc_ref[pl.ds(p, seg_groups_max), :] = i * tile_rows + (loc + step) * ROW_GROUP
            lsrc_ref[pl.ds(i * local_stride + loc, seg_groups_max), :] = p + step
            loc_ref[i] = loc + g
            return p + g

        end = lax.fori_loop(0, n_ttiles, per_tile, pos)
        end_pad = ((end + (gpt - 1)) // gpt) * gpt
        gsrc_ref[pl.ds(end, seg_groups_max), :] = jnp.full((seg_groups_max, LANES), zero_group, I32)

        def mark(t, c):
            te_ref[t] = e
            tw_ref[t] = e
            return c

        lax.fori_loop(pos // gpt, end_pad // gpt, mark, 0)
        return end_pad

    lax.fori_loop(0, n_exp, per_expert, 0)

    def clear_tail(i, c):
        lsrc_ref[pl.ds(i * local_stride + loc_ref[i], seg_groups_max), :] = jnp.zeros((seg_groups_max, LANES), I32)
        return c

    lax.fori_loop(0, n_ttiles, clear_tail, 0)


def _moe_plan(cnt, *, n_ttiles, n_exp, tile_rows, gpt, n_mm_tiles, seg_groups_max):
    smem = pl.BlockSpec(memory_space=pltpu.SMEM)
    zero_group = n_ttiles * tile_rows - ROW_GROUP
    local_groups = tile_rows // ROW_GROUP
    local_stride = local_groups + seg_groups_max
    n_groups = n_mm_tiles * gpt
    gsrc, lsrc, te, tw = pl.pallas_call(
        functools.partial(_moe_plan_body, n_ttiles=n_ttiles, n_exp=n_exp, tile_rows=tile_rows, gpt=gpt,
                          seg_groups_max=seg_groups_max, local_stride=local_stride, zero_group=zero_group),
        in_specs=[smem], out_specs=[pl.BlockSpec(memory_space=pltpu.VMEM), pl.BlockSpec(memory_space=pltpu.VMEM),
                                    smem, smem],
        out_shape=[jax.ShapeDtypeStruct((n_groups + 2 * seg_groups_max, LANES), I32),
                   jax.ShapeDtypeStruct((n_ttiles * local_stride, LANES), I32),
                   jax.ShapeDtypeStruct((n_mm_tiles,), I32), jax.ShapeDtypeStruct((n_mm_tiles,), I32)],
        scratch_shapes=[pltpu.SMEM((n_ttiles,), I32)],
        name="moe_plan",
    )(cnt)
    return gsrc[:n_groups, 0], lsrc[:, 0], te, tw, local_stride


def _moe_mm_body(te_ref, tw_ref, src_ref, xs_hbm, wg_ref, wu_ref, wd_ref, ys_ref, xbuf, wgb_ref, wub_ref, wdb_ref,
                 sems):
    i = pl.program_id(0)
    tm = ys_ref.shape[0]
    groups = tm // ROW_GROUP
    expert = te_ref[i]
    prev = te_ref[jnp.maximum(i - 1, 0)]

    def fetch(slot, tile):
        for k in range(groups):
            src = pl.multiple_of(src_ref[tile * groups + k], ROW_GROUP)
            pltpu.make_async_copy(xs_hbm.at[pl.ds(src, ROW_GROUP)],
                                  xbuf.at[slot, pl.ds(k * ROW_GROUP, ROW_GROUP)], sems.at[slot]).start()

    slot = i % 2

    @pl.when(i == 0)
    def _():
        fetch(0, 0)

    @pl.when(i + 1 < pl.num_programs(0))
    def _():
        fetch(1 - slot, i + 1)

    pltpu.make_async_copy(xs_hbm.at[pl.ds(0, tm)], xbuf.at[slot], sems.at[slot]).wait()

    @pl.when((expert >= 0) & ((i == 0) | (expert != prev)))
    def _():
        wgb_ref[...] = wg_ref[0].astype(BF16)
        wub_ref[...] = wu_ref[0].astype(BF16)
        wdb_ref[...] = wd_ref[0].astype(BF16)

    @pl.when(expert >= 0)
    def _():
        x = xbuf[slot].astype(BF16)
        a = jnp.dot(x, wgb_ref[...], preferred_element_type=F32)
        b = jnp.dot(x, wub_ref[...], preferred_element_type=F32)
        hdn = (a * _sigmoid(a)) * b
        ys_ref[...] = jnp.dot(hdn.astype(BF16), wdb_ref[...], preferred_element_type=F32)

    @pl.when(expert < 0)
    def _():
        ys_ref[...] = jnp.zeros_like(ys_ref)


def _moe_mm(tile_expert, tile_weight, group_src, xs, w_gate, w_up, w_down, *, tm):
    n_tiles = tile_expert.shape[0]
    d = xs.shape[1]
    n_exp, _, de = w_gate.shape
    wmap = lambda i, te, tw, src: (tw[i], 0, 0)
    grid_spec = pltpu.PrefetchScalarGridSpec(
        num_scalar_prefetch=3, grid=(n_tiles,),
        in_specs=[pl.BlockSpec(memory_space=pl.ANY),
                  pl.BlockSpec((1, d, de), wmap), pl.BlockSpec((1, d, de), wmap), pl.BlockSpec((1, de, d), wmap)],
        out_specs=pl.BlockSpec((tm, d), lambda i, te, tw, src: (i, 0)),
        scratch_shapes=[pltpu.VMEM((2, tm, d), F32), pltpu.VMEM((d, de), BF16), pltpu.VMEM((d, de), BF16),
                        pltpu.VMEM((de, d), BF16), pltpu.SemaphoreType.DMA((2,))])
    return pl.pallas_call(
        _moe_mm_body, grid_spec=grid_spec,
        out_shape=jax.ShapeDtypeStruct((n_tiles * tm, d), F32),
        compiler_params=_params("arbitrary"),
        name="moe_mm",
    )(tile_expert, tile_weight, group_src, xs, w_gate, w_up, w_down)


def _moe_combine_body(lsrc_ref, x_ref, slab_ref, g_ref, ys_hbm, o_ref, ybuf, sems,
                      *, final_norm, first_tile, local_stride):
    i = pl.program_id(0)
    tm = x_ref.shape[0]
    tile_rows = ybuf.shape[1]
    unroll = 8

    def fetch(slot, tile):
        def body(c, carry):
            for j in range(unroll):
                lg = c * unroll + j
                src = pl.multiple_of(lsrc_ref[tile * local_stride + lg] * ROW_GROUP, ROW_GROUP)
                dst = pl.multiple_of(lg * ROW_GROUP, ROW_GROUP)
                pltpu.make_async_copy(ys_hbm.at[pl.ds(src, ROW_GROUP)], ybuf.at[slot, pl.ds(dst, ROW_GROUP)],
                                      sems.at[slot]).start()
            return carry

        lax.fori_loop(0, tile_rows // ROW_GROUP // unroll, body, 0)

    slot = i % 2

    @pl.when(i == 0)
    def _():
        fetch(0, first_tile)

    @pl.when(i + 1 < pl.num_programs(0))
    def _():
        fetch(1 - slot, first_tile + i + 1)

    pltpu.make_async_copy(ys_hbm.at[pl.ds(0, tile_rows)], ybuf.at[slot], sems.at[slot]).wait()
    y = ybuf[slot].astype(BF16)
    slab = slab_ref[...]
    r = lax.broadcasted_iota(I32, (tm, tile_rows), 1).astype(F32)
    ya = jnp.dot((r == slab[:, 6:7]).astype(BF16), y, preferred_element_type=F32)
    yb = jnp.dot((r == slab[:, 7:8]).astype(BF16), y, preferred_element_type=F32)
    out = x_ref[...] + (slab[:, 4:5] * ya + slab[:, 5:6] * yb)
    if final_norm:
        out = _rmsnorm(out, g_ref[...])
    o_ref[...] = out


def _moe_combine(local_src, x, slab, g, ys, *, ts, tile_rows, final_norm, first_tile, local_stride):
    n, d = x.shape
    return pl.pallas_call(
        functools.partial(_moe_combine_body, final_norm=final_norm, first_tile=first_tile,
                          local_stride=local_stride),
        grid=(n // ts,),
        in_specs=[pl.BlockSpec(memory_space=pltpu.SMEM),
                  pl.BlockSpec((ts, d), lambda i: (i, 0)), pl.BlockSpec((ts, LANES), lambda i: (i, 0)),
                  pl.BlockSpec((1, d), lambda i: (0, 0)), pl.BlockSpec(memory_space=pl.ANY)],
        out_specs=pl.BlockSpec((ts, d), lambda i: (i, 0)),
        out_shape=jax.ShapeDtypeStruct((n, d), F32),
        scratch_shapes=[pltpu.VMEM((2, tile_rows, d), F32), pltpu.SemaphoreType.DMA((2,))],
        compiler_params=_params("arbitrary"),
        name="moe_combine",
    )(local_src, x, slab, g, ys)


def kernel(x_prompt, x_sample, cache_k, cache_v, cache_logf, state_pool, page_table, norm_mix, w_in, b_forget,
           w_pool, pool_scale, w_up_pool, w_up_att, w_out, norm_ffn, w_router_group, b_router_group,
           w_router_expert, b_router_expert, w_gate, w_up, w_down, norm_final):
    depth = norm_mix.shape[0]
    assert depth == 1, "single trunk layer"
    b, t, d = x_prompt.shape
    db, dt, _ = x_sample.shape
    assert dt == 1, "one sample token per sequence"
    _, n_phys, page, n_heads, dh = cache_k.shape
    n_pages = page_table.shape[1]
    past = n_pages * page
    n_state, d_pool = state_pool.shape[2], state_pool.shape[3]
    d_att = n_heads * dh
    n_pool_groups = w_pool.shape[1]
    assert n_pool_groups == len(POOL_WINDOWS) and d_pool // n_pool_groups == LANES
    assert n_state == max(POOL_WINDOWS) - 1 and n_state < POOL_HALO
    n_groups, n_per_group = w_router_expert.shape[1], w_router_expert.shape[3]
    n_exp = n_groups * n_per_group
    assert n_groups + n_exp <= LANES and 2 * dh == LANES and n_heads % 2 == 0
    n = b * t
    q_scale = float(dh) ** -0.5
    tm = min(TOKEN_TILE, t)
    assert t % tm == 0 and t % ATTN_TILE == 0

    o_main = d_pool + 3 * d_att
    wi = w_in[0]
    wm_f, wf_f, wg_f = wi[:, :o_main], wi[:, o_main:o_main + n_heads], wi[:, o_main + n_heads:]
    wf_pad = jnp.pad(wf_f, ((0, 0), (0, LANES - n_heads)))
    wit = jnp.transpose(wi)
    wmt_f, wgt_f = wit[:o_main], wit[o_main + n_heads:]
    wft_pad = jnp.pad(wit[o_main:o_main + n_heads], ((0, LANES - n_heads), (0, 0)))
    bf_pad = jnp.pad(b_forget[0], (0, LANES - n_heads)).reshape(1, LANES)
    g_mix = norm_mix[0].reshape(1, d)
    g_ffn = norm_ffn[0].reshape(1, d)
    g_fin = norm_final.reshape(1, d)
    ps = pool_scale[0].reshape(1, d_pool)
    wr_f = jnp.concatenate([w_router_group[0], jnp.transpose(w_router_expert[0], (1, 0, 2)).reshape(d, n_exp)], axis=1)
    wr_pad = jnp.pad(wr_f, ((0, 0), (0, LANES - n_groups - n_exp)))
    br_pad = jnp.pad(jnp.concatenate([b_router_group[0], b_router_expert[0].reshape(n_exp)]),
                     (0, LANES - n_groups - n_exp)).reshape(1, LANES)
    bf = lambda a: a.astype(BF16)

    xp = x_prompt.reshape(n, d)
    u_p, q_p, kt_p, vt_p, kb_p, vb_p, lft_p, gate_p = _proj_prompt(
        xp, g_mix, bf(wm_f), bf(wf_pad), bf(wg_f), bf_pad, tm=tm, seq_len=t, d_pool=d_pool, d_att=d_att,
        n_heads=n_heads, q_scale=q_scale)
    c = _cumsum_lanes(lft_p.reshape(b * n_heads, t))
    nt = t // ATTN_TILE
    c_blk = jnp.transpose(c.reshape(b, n_heads // 2, 2, nt, ATTN_TILE), (0, 1, 3, 2, 4))
    att_p = _attn_prompt(q_p.reshape(b, t, d_att), kb_p.reshape(b, t, d_att), vb_p.reshape(b, t, d_att), c_blk,
                         tile=ATTN_TILE, dh=dh)
    n_ptiles = n // tm
    n_ttiles = n_ptiles + 1
    tile_rows = -(-(2 * tm + n_exp * (ROW_GROUP - 1)) // MOE_ROW_TILE) * MOE_ROW_TILE
    assert 2 * db + n_exp * (ROW_GROUP - 1) <= tile_rows - ROW_GROUP, "the sample tile must end in an unused row group"

    xs = x_sample.reshape(db, d)
    z_s, lf_s, gate_s = _proj_sample(xs, g_mix, wmt_f, wft_pad, wgt_f, bf_pad, tn=512)
    u_s = z_s[:, :d_pool]
    q_s = z_s[:, d_pool:d_pool + d_att] * q_scale
    k_s = z_s[:, d_pool + d_att:d_pool + 2 * d_att]
    v_s = z_s[:, d_pool + 2 * d_att:]
    att_s = _attn_sample(page_table, q_s, k_s, v_s, lf_s[:, :n_heads],
                         jnp.transpose(cache_k[0], (0, 2, 3, 1)), jnp.transpose(cache_v[0], (0, 2, 3, 1)),
                         jnp.transpose(cache_logf[0], (0, 2, 1)))
    state_t = jnp.transpose(state_pool[0], (1, 0, 2))
    x2_s, xs_rows_s, slab_s, counts_s = _merge_sample(
        xs, u_s, state_t, att_s.reshape(db, d_att), gate_s, w_pool[0], ps, w_up_pool[0], w_up_att[0], w_out[0],
        g_ffn, wr_pad, br_pad, start_pos=past, n_groups=n_groups, n_per_group=n_per_group, tile_rows=tile_rows)

    x2_p, xs_rows, slab_p, counts_p = _merge_prompt(
        xp, u_p, att_p.reshape(n, d_att), gate_p, bf(w_pool[0]), ps, bf(w_up_pool[0]), bf(w_up_att[0]),
        bf(w_out[0]), g_ffn, bf(wr_pad), br_pad, xs_rows_s, tm=tm, seq_len=t, n_groups=n_groups,
        n_per_group=n_per_group)

    tmm = MOE_ROW_TILE
    gpt = tmm // ROW_GROUP
    n_groups_max = -(-(2 * (n + db)) // ROW_GROUP) + n_ttiles * n_exp + n_exp * (gpt - 1)
    cnt = jnp.concatenate([counts_p, counts_s], axis=0)[:, 0, :n_exp].astype(I32).reshape(-1)
    group_src, local_src, tile_expert, tile_weight, local_stride = _moe_plan(
        cnt, n_ttiles=n_ttiles, n_exp=n_exp, tile_rows=tile_rows, gpt=gpt, n_mm_tiles=-(-n_groups_max // gpt),
        seg_groups_max=tm // ROW_GROUP)
    ys_rows = _moe_mm(tile_expert, tile_weight, group_src, xs_rows, w_gate[0], w_up[0], w_down[0], tm=tmm)
    y_prompt = _moe_combine(local_src, x2_p, slab_p, g_fin, ys_rows, ts=tm, tile_rows=tile_rows,
                            final_norm=True, first_tile=0, local_stride=local_stride)
    y_sample = _moe_combine(local_src, x2_s, slab_s, g_fin, ys_rows, ts=db, tile_rows=tile_rows,
                            final_norm=True, first_tile=n_ptiles, local_stride=local_stride)

    new_pool_p = u_p.reshape(b, t, d_pool)[:, t - n_state:, :]
    new_pool_s = jnp.concatenate([state_pool[0][:, 1:, :], u_s[:, None, :]], axis=1)
    to_heads = lambda a: jnp.transpose(a.reshape(b, n_heads, dh, t), (0, 3, 1, 2))[None]
    return (y_prompt.reshape(b, t, d), y_sample.reshape(db, 1, d),
            to_heads(kt_p), to_heads(vt_p), jnp.transpose(lft_p, (0, 2, 1))[None],
            new_pool_p[None],
            k_s.reshape(1, db, 1, n_heads, dh), v_s.reshape(1, db, 1, n_heads, dh),
            lf_s[:, :n_heads].reshape(1, db, 1, n_heads), new_pool_s[None])
```

```python
import functools

import jax
import jax.numpy as jnp
from jax import lax
from jax.experimental import pallas as pl
from jax.experimental.pallas import tpu as pltpu

F32 = jnp.float32
BF16 = jnp.bfloat16
I32 = jnp.int32
HIGHEST = lax.Precision.HIGHEST

RMS_EPS = 1e-6
POOL_WINDOWS = (2, 4, 8, 16)
POOL_HALO = 16
LANES = 128
VMEM_LIMIT_BYTES = 56 * 1024 * 1024

TOKEN_TILE = 512
ATTN_TILE = 512
MOE_ROW_TILE = 256
PAGES_PER_STEP = 16
ROW_GROUP = 8


def _params(*sem):
    return pltpu.CompilerParams(dimension_semantics=sem, vmem_limit_bytes=VMEM_LIMIT_BYTES)


def _rmsnorm(x, g):
    return x * lax.rsqrt(jnp.mean(x * x, axis=-1, keepdims=True) + RMS_EPS) * g


def _log_sigmoid(x):
    return jnp.minimum(x, 0.0) - jnp.log1p(jnp.exp(-jnp.abs(x)))


def _sigmoid(x):
    return 1.0 / (1.0 + jnp.exp(-x))


def _dot(a, b, precise):
    if precise:
        return jnp.dot(a.astype(F32), b.astype(F32), precision=HIGHEST, preferred_element_type=F32)
    return jnp.dot(a.astype(BF16), b.astype(BF16), preferred_element_type=F32)


def _split3(x):
    hi = x.astype(BF16)
    r = x - hi.astype(F32)
    mid = r.astype(BF16)
    lo = (r - mid.astype(F32)).astype(BF16)
    return hi, mid, lo


def _dot_exact_rhs(x, w_bf16):
    hi, mid, lo = _split3(x)
    d = lambda a: jnp.dot(a, w_bf16, preferred_element_type=F32)
    return d(hi) + d(mid) + d(lo)


def _proj_body(x_ref, g_ref, wm_ref, wf_ref, wg_ref, bf_ref,
               u_ref, q_ref, kt_ref, vt_ref, kb_ref, vb_ref, lft_ref, gate_ref, *, d_pool, d_att, n_heads, q_scale):
    h = _rmsnorm(x_ref[...], g_ref[...]).astype(BF16)
    z = jnp.dot(h, wm_ref[...], preferred_element_type=F32)
    o1, o2, o3 = d_pool, d_pool + d_att, d_pool + 2 * d_att
    u_ref[...] = z[:, :o1]
    q_ref[...] = (z[:, o1:o2] * q_scale).astype(BF16)
    k = z[:, o2:o3]
    v = z[:, o3:]
    kt_ref[0] = k.T
    vt_ref[0] = v.T
    kb_ref[...] = k.astype(BF16)
    vb_ref[...] = v.astype(BF16)
    lf = _log_sigmoid(jnp.dot(h, wf_ref[...], preferred_element_type=F32) + bf_ref[...])
    lft_ref[0] = lf.T[0:n_heads, :]
    gate_ref[...] = _sigmoid(jnp.dot(h, wg_ref[...], preferred_element_type=F32)).astype(BF16)


def _proj_prompt(x, g, wm, wf, wg, bfp, *, tm, seq_len, d_pool, d_att, n_heads, q_scale):
    n, d = x.shape
    b = n // seq_len
    tps = seq_len // tm
    row = lambda i: (i, 0)
    const = lambda i: (0, 0)
    tmin = lambda i: (i // tps, 0, i % tps)
    dg = wg.shape[1]
    out_shape = [
        jax.ShapeDtypeStruct((n, d_pool), F32), jax.ShapeDtypeStruct((n, d_att), BF16),
        jax.ShapeDtypeStruct((b, d_att, seq_len), F32), jax.ShapeDtypeStruct((b, d_att, seq_len), F32),
        jax.ShapeDtypeStruct((n, d_att), BF16), jax.ShapeDtypeStruct((n, d_att), BF16),
        jax.ShapeDtypeStruct((b, n_heads, seq_len), F32), jax.ShapeDtypeStruct((n, dg), BF16),
    ]
    return pl.pallas_call(
        functools.partial(_proj_body, d_pool=d_pool, d_att=d_att, n_heads=n_heads, q_scale=q_scale),
        grid=(n // tm,),
        in_specs=[pl.BlockSpec((tm, d), row), pl.BlockSpec((1, d), const),
                  pl.BlockSpec(wm.shape, const), pl.BlockSpec(wf.shape, const),
                  pl.BlockSpec(wg.shape, const), pl.BlockSpec((1, LANES), const)],
        out_specs=[pl.BlockSpec((tm, d_pool), row), pl.BlockSpec((tm, d_att), row),
                   pl.BlockSpec((1, d_att, tm), tmin), pl.BlockSpec((1, d_att, tm), tmin),
                   pl.BlockSpec((tm, d_att), row), pl.BlockSpec((tm, d_att), row),
                   pl.BlockSpec((1, n_heads, tm), tmin), pl.BlockSpec((tm, dg), row)],
        out_shape=out_shape,
        compiler_params=_params("arbitrary"),
        name="proj_prompt",
    )(x, g, wm, wf, wg, bfp)


def _dot_nt(a, bt, precise):
    dims = (((1,), (1,)), ((), ()))
    if precise:
        return lax.dot_general(a.astype(F32), bt.astype(F32), dims, precision=HIGHEST, preferred_element_type=F32)
    return lax.dot_general(a.astype(BF16), bt.astype(BF16), dims, preferred_element_type=F32)


def _proj_sample_body(x_ref, g_ref, wmt_ref, wft_ref, wgt_ref, bf_ref, z_ref, lf_ref, gate_ref):
    h = _rmsnorm(x_ref[...], g_ref[...])
    z_ref[...] = _dot_nt(h, wmt_ref[...], True)
    lf_ref[...] = _log_sigmoid(_dot_nt(h, wft_ref[...], True) + bf_ref[...])
    gate_ref[...] = _sigmoid(_dot_nt(h, wgt_ref[...], True))


def _proj_sample(x, g, wmt, wft, wgt, bfp, *, tn):
    n, d = x.shape
    dm, dg = wmt.shape[0], wgt.shape[0]
    assert dm == dg
    const = lambda j: (0, 0)
    chunk = lambda j: (j, 0)
    col = lambda j: (0, j)
    return pl.pallas_call(
        _proj_sample_body,
        grid=(dm // tn,),
        in_specs=[pl.BlockSpec((n, d), const), pl.BlockSpec((1, d), const),
                  pl.BlockSpec((tn, d), chunk), pl.BlockSpec(wft.shape, const),
                  pl.BlockSpec((tn, d), chunk), pl.BlockSpec((1, LANES), const)],
        out_specs=[pl.BlockSpec((n, tn), col), pl.BlockSpec((n, LANES), const), pl.BlockSpec((n, tn), col)],
        out_shape=[jax.ShapeDtypeStruct((n, dm), F32), jax.ShapeDtypeStruct((n, LANES), F32),
                   jax.ShapeDtypeStruct((n, dg), F32)],
        compiler_params=_params("arbitrary"),
        name="proj_sample",
    )(x, g, wmt, wft, wgt, bfp)


def _cumsum_body(x_ref, o_ref):
    c = x_ref[...]
    lane = lax.broadcasted_iota(I32, c.shape, 1)
    s = 1
    while s < c.shape[1]:
        c = c + jnp.where(lane >= s, pltpu.roll(c, s, 1), 0.0)
        s *= 2
    o_ref[...] = c


def _cumsum_lanes(x):
    return pl.pallas_call(_cumsum_body, out_shape=jax.ShapeDtypeStruct(x.shape, F32),
                          compiler_params=_params(), name="cumsum_logf")(x)


def _prompt_q_tile(qi, q_ref, kts, vhs, c_ref, o_ref, *, tile, dh, first, causal):
    q = q_ref[0, qi * tile:(qi + 1) * tile, :]
    zero = jnp.zeros_like(q)
    q_heads = (jnp.where(first, q, zero), jnp.where(first, zero, q))
    res = []
    for h in range(2):
        m = jnp.full((tile, 1), -1e30, F32)
        acc = jnp.zeros((tile, 2 * dh), F32)
        for kj in range(qi + 1):
            s = lax.dot_general(q_heads[h], kts[kj], (((1,), (1,)), ((), ())), preferred_element_type=F32)
            s = s - c_ref[0, 0, kj][h:h + 1, :]
            if kj == qi:
                s = jnp.where(causal, s, -jnp.inf)
            m_new = jnp.maximum(m, jnp.max(s, axis=-1, keepdims=True))
            alpha = jnp.exp(m - m_new)
            p = jnp.exp(s - m_new)
            acc = alpha * acc + jnp.dot(p.astype(BF16), vhs[kj][h], preferred_element_type=F32)
            m = m_new
        res.append(acc)
    a0, a1 = res
    out = jnp.where(first, a0 / a0[:, dh:dh + 1], a1 / a1[:, 0:1])
    o_ref[0, qi * tile:(qi + 1) * tile, :] = out.astype(o_ref.dtype)


def _sample_chunk(k_refs, v_refs, lf_refs, qrep, carry):
    m_prev, l, acc, s_run = carry
    n_heads, dh, page = acc.shape
    d_att = n_heads * dh
    g_n = len(k_refs)
    r = lax.broadcasted_iota(I32, (page, page), 0)
    c = lax.broadcasted_iota(I32, (page, page), 1)
    later = (r > c).astype(BF16)
    ones = jnp.ones((page, page), BF16)
    lf_all = jnp.concatenate([lf_refs[g][...] for g in range(g_n)], axis=0)
    suffix = _dot_exact_rhs(lf_all, later)
    total = _dot_exact_rhs(lf_all, ones)
    m_new = m_prev
    scores = []
    for g in range(g_n):
        kq = k_refs[g][...].reshape(d_att, page) * qrep
        s = jnp.sum(kq.reshape(n_heads, dh, page), axis=1)
        sb = s + s_run + suffix[g * n_heads:(g + 1) * n_heads]
        s_run = s_run + total[g * n_heads:(g + 1) * n_heads]
        scores.append(sb)
        m_new = jnp.maximum(m_new, jnp.max(sb, axis=-1, keepdims=True))
    alpha = jnp.exp(m_prev - m_new)
    l = alpha * l
    acc = acc * alpha[:, None, :]
    for g in range(g_n):
        p = jnp.exp(scores[g] - m_new)
        l = l + jnp.sum(p, axis=-1, keepdims=True)
        acc = acc + v_refs[g][...] * p[:, None, :]
    return m_new, l, acc, s_run


def _attn_body(pt_ref, q_ref, k_ref, v_ref, c_ref, qrep_ref, qs_ref, kn_ref, vrep_ref, lfn_ref,
               ck_hbm, cv_hbm, clf_hbm, o_ref, os_ref, kbuf, vbuf, lfbuf, sems, *, tile, dh):
    step = pl.program_id(0) * pl.num_programs(1) + pl.program_id(1)
    n_steps = pl.num_programs(0) * pl.num_programs(1)
    nt = q_ref.shape[1] // tile
    n_pages = pt_ref.shape[1]
    _, g_n, n_heads, _, page = kbuf.shape
    d_att = n_heads * dh
    seqs = qrep_ref.shape[0]
    n_chunks = n_pages // g_n
    n_items = seqs * n_chunks

    def fetch(slot, seq, chunk):
        for g in range(g_n):
            pid = pt_ref[seq, n_pages - 1 - (chunk * g_n + g)]
            pltpu.make_async_copy(ck_hbm.at[pid], kbuf.at[slot, g], sems.at[slot, 0]).start()
            pltpu.make_async_copy(cv_hbm.at[pid], vbuf.at[slot, g], sems.at[slot, 1]).start()
            pltpu.make_async_copy(clf_hbm.at[pid], lfbuf.at[slot, g], sems.at[slot, 2]).start()

    @pl.when(step == 0)
    def _():
        fetch(0, 0, 0)

    lane = lax.broadcasted_iota(I32, (tile, 2 * dh), 1)
    first = lane < dh
    row = lax.broadcasted_iota(I32, (tile, tile), 0)
    col = lax.broadcasted_iota(I32, (tile, tile), 1)
    causal = col <= row
    one = jnp.ones((tile, 2 * dh), BF16)
    kts, vhs = [], []
    for kj in range(nt):
        vt = v_ref[0, kj * tile:(kj + 1) * tile, :]
        kts.append(k_ref[0, kj * tile:(kj + 1) * tile, :])
        vhs.append((jnp.where(first, vt, one), jnp.where(first, one, vt)))

    q_done = 0
    carry = None
    for item in range(n_items):
        j, chunk = divmod(item, n_chunks)
        slot = item % 2
        if item + 1 < n_items:
            fetch(1 - slot, step * seqs + (item + 1) // n_chunks, (item + 1) % n_chunks)
        else:
            @pl.when(step + 1 < n_steps)
            def _():
                fetch(1 - slot, (step + 1) * seqs, 0)
        pltpu.make_async_copy(ck_hbm.at[pl.ds(0, g_n)], kbuf.at[slot], sems.at[slot, 0]).wait()
        pltpu.make_async_copy(cv_hbm.at[pl.ds(0, g_n)], vbuf.at[slot], sems.at[slot, 1]).wait()
        pltpu.make_async_copy(clf_hbm.at[pl.ds(0, g_n)], lfbuf.at[slot], sems.at[slot, 2]).wait()
        if chunk == 0:
            s_new = jnp.sum(qs_ref[j] * kn_ref[j], axis=-1, keepdims=True)
            lane_p = lax.broadcasted_iota(I32, (d_att, page), 1)
            carry = (jnp.broadcast_to(s_new, (n_heads, page)), jnp.ones((n_heads, page), F32),
                     jnp.where(lane_p == 0, vrep_ref[j], 0.0).reshape(n_heads, dh, page), lfn_ref[j])
        carry = _sample_chunk([kbuf.at[slot, g] for g in range(g_n)], [vbuf.at[slot, g] for g in range(g_n)],
                              [lfbuf.at[slot, g] for g in range(g_n)], qrep_ref[j], carry)
        if chunk == n_chunks - 1:
            _, l, acc, _ = carry
            os_ref[j] = jnp.sum(acc / l[:, None, :], axis=-1)
        q_until = ((item + 1) * nt) // n_items
        for qi in range(q_done, q_until):
            _prompt_q_tile(qi, q_ref, kts, vhs, c_ref, o_ref, tile=tile, dh=dh, first=first, causal=causal)
        q_done = q_until


def _attention(q, k, v, c, page_table, q_s, k_new, v_new, lf_new, cache_kt, cache_vt, cache_lft, *, tile, dh):
    b, t, da = q.shape
    hp = da // (2 * dh)
    nt = t // tile
    db, n_pages = page_table.shape
    n_phys, n_heads, _, page = cache_kt.shape
    d_att = n_heads * dh
    n_steps = b * hp
    assert db % n_steps == 0, "sample sequences are split evenly over the prompt grid steps"
    seqs = db // n_steps
    g_n = PAGES_PER_STEP
    while n_pages % g_n:
        g_n //= 2
    assert (seqs * (n_pages // g_n)) % 2 == 0, "buffer slots alternate per page chunk"
    lane_rep = lambda a: jnp.broadcast_to(a.reshape(db, -1, 1), (db, a.size // db, page))
    pair = pl.BlockSpec((1, t, 2 * dh), lambda bi, hi, pt: (bi, 0, hi))
    per_step = lambda bi, hi, pt: (bi * hp + hi, 0, 0)
    hbm = pl.BlockSpec(memory_space=pl.ANY)
    grid_spec = pltpu.PrefetchScalarGridSpec(
        num_scalar_prefetch=1, grid=(b, hp),
        in_specs=[pair, pair, pair, pl.BlockSpec((1, 1, nt, 2, tile), lambda bi, hi, pt: (bi, hi, 0, 0, 0)),
                  pl.BlockSpec((seqs, d_att, page), per_step), pl.BlockSpec((seqs, n_heads, dh), per_step),
                  pl.BlockSpec((seqs, n_heads, dh), per_step), pl.BlockSpec((seqs, d_att, page), per_step),
                  pl.BlockSpec((seqs, n_heads, page), per_step), hbm, hbm, hbm],
        out_specs=[pair, pl.BlockSpec((seqs, n_heads, dh), per_step)],
        scratch_shapes=[pltpu.VMEM((2, g_n, n_heads, dh, page), F32), pltpu.VMEM((2, g_n, n_heads, dh, page), F32),
                        pltpu.VMEM((2, g_n, n_heads, page), F32), pltpu.SemaphoreType.DMA((2, 3))])
    return pl.pallas_call(
        functools.partial(_attn_body, tile=tile, dh=dh),
        grid_spec=grid_spec,
        out_shape=[jax.ShapeDtypeStruct((b, t, da), BF16), jax.ShapeDtypeStruct((db, n_heads, dh), F32)],
        compiler_params=_params("arbitrary", "arbitrary"),
        name="attention",
    )(page_table, q, k, v, c, lane_rep(q_s), q_s.reshape(db, n_heads, dh), k_new.reshape(db, n_heads, dh),
      lane_rep(v_new), lane_rep(lf_new), cache_kt, cache_vt, cache_lft)


def _merge_and_route(x, pooled, att, gates, wp_ref, ps_ref, wup_ref, wua_ref, wo_ref, nf_ref, wr_ref, br_ref,
                     *, precise, n_groups, n_per_group):
    tm, d = x.shape
    gw = pooled[0].shape[1]
    mixed = jnp.concatenate([_dot(pooled[g], wp_ref[g], precise) for g in range(len(pooled))], axis=-1)
    pool_out = mixed * ps_ref[...]
    y = gates[:, :d].astype(F32) * _dot(pool_out, wup_ref[...], precise) \
        + gates[:, d:].astype(F32) * _dot(att, wua_ref[...], precise)
    x2 = x + _dot(y, wo_ref[...], precise)
    h2 = _rmsnorm(x2, nf_ref[...])
    logits = _dot(h2, wr_ref[...], precise) + br_ref[...]
    lane = lax.broadcasted_iota(I32, logits.shape, 1)
    lanef = lane.astype(F32)
    neg = -jnp.inf
    is_g = lane < n_groups
    gmax = jnp.max(jnp.where(is_g, logits, neg), axis=-1, keepdims=True)
    gidx = jnp.min(jnp.where(is_g & (logits == gmax), lanef, float(LANES)), axis=-1, keepdims=True)
    gsum = jnp.sum(jnp.where(is_g, jnp.exp(logits - gmax), 0.0), axis=-1, keepdims=True)
    g_w = 1.0 / gsum
    n_exp = n_groups * n_per_group
    exp_id = lanef - float(n_groups)
    in_sel = (lane >= n_groups) & (lane < n_groups + n_exp) & (jnp.floor(exp_id / n_per_group) == gidx)
    v1 = jnp.max(jnp.where(in_sel, logits, neg), axis=-1, keepdims=True)
    i1 = jnp.min(jnp.where(in_sel & (logits == v1), lanef, float(LANES)), axis=-1, keepdims=True)
    in_sel2 = in_sel & (lanef != i1)
    v2 = jnp.max(jnp.where(in_sel2, logits, neg), axis=-1, keepdims=True)
    i2 = jnp.min(jnp.where(in_sel2 & (logits == v2), lanef, float(LANES)), axis=-1, keepdims=True)
    t = jnp.exp(v2 - v1)
    w1 = g_w * (1.0 / (1.0 + t))
    w2 = g_w * (t / (1.0 + t))
    e1 = i1 - float(n_groups)
    e2 = i2 - float(n_groups)
    hit1 = lanef == e1
    hit2 = lanef == e2
    onehot = (hit1 | hit2).astype(BF16)
    rr = lax.broadcasted_iota(I32, (tm, tm), 0)
    cc = lax.broadcasted_iota(I32, (tm, tm), 1)
    incl = jnp.dot((cc <= rr).astype(BF16), onehot, preferred_element_type=F32)
    counts = incl[tm - 1:tm, :]
    groups = jnp.floor((counts + (ROW_GROUP - 1.0)) * (1.0 / ROW_GROUP))
    ur = lax.broadcasted_iota(I32, (LANES, LANES), 0)
    uc = lax.broadcasted_iota(I32, (LANES, LANES), 1)
    before = jnp.dot(jnp.broadcast_to(groups, (8, LANES)).astype(BF16), (ur < uc).astype(BF16),
                     preferred_element_type=F32)[0:1]
    seg_start = before * float(ROW_GROUP)
    pick = lambda hit, tbl: jnp.sum(jnp.where(hit, tbl, 0.0), axis=-1, keepdims=True)
    r1 = pick(hit1, incl) - 1.0
    r2 = pick(hit2, incl) - 1.0
    row1 = pick(hit1, seg_start) + r1
    row2 = pick(hit2, seg_start) + r2
    slab = jnp.zeros((tm, LANES), F32)
    for i, val in enumerate((e1, e2, r1, r2, w1, w2, row1, row2)):
        slab = jnp.where(lane == i, val, slab)
    return x2, h2, slab, counts


def _sorted_copy(h2, row1, row2, n_rows):
    tm = h2.shape[0]
    r = lax.broadcasted_iota(I32, (n_rows, tm), 0).astype(F32)
    place = ((r == row1) | (r == row2)).astype(BF16)
    return jnp.dot(place, h2.astype(BF16), preferred_element_type=F32)


def _merge_prompt_body(x_ref, u_ref, halo_ref, att_ref, gate_ref, wp_ref, ps_ref, wup_ref, wua_ref, wo_ref,
                       nf_ref, wr_ref, br_ref, xs_last_hbm, x2_ref, xs_ref, slab_ref, counts_ref,
                       ext_ref, sem, *, seq_len, n_groups, n_per_group):
    i = pl.program_id(0)
    n_tiles = pl.num_programs(0) - 1
    tm = x_ref.shape[0]
    gw = u_ref.shape[1] // len(POOL_WINDOWS)

    @pl.when(i < n_tiles)
    def _():
        pos0 = (i * tm) % seq_len
        u = u_ref[...]
        ext_ref[0:POOL_HALO, :] = jnp.where(pos0 == 0, 0.0, halo_ref[...])
        ext_ref[POOL_HALO:, :] = u
        pos = pos0 + lax.broadcasted_iota(I32, (tm, 1), 0)
        pooled = []
        for g, w in enumerate(POOL_WINDOWS):
            lo = g * gw
            wsum = ext_ref[pl.ds(POOL_HALO, tm), lo:lo + gw]
            for j in range(1, w):
                wsum = wsum + ext_ref[pl.ds(POOL_HALO - j, tm), lo:lo + gw]
            count = jnp.minimum(pos + 1, w).astype(F32)
            pooled.append(wsum / count - u[:, lo:lo + gw])
        x2, h2, slab, counts = _merge_and_route(
            x_ref[...], pooled, att_ref[...], gate_ref[...], wp_ref, ps_ref, wup_ref, wua_ref, wo_ref, nf_ref,
            wr_ref, br_ref, precise=False, n_groups=n_groups, n_per_group=n_per_group)
        fields = slab.T[0:8, :]
        x2_ref[...] = x2
        xs_ref[...] = _sorted_copy(h2, fields[6:7, :], fields[7:8, :], xs_ref.shape[0])
        slab_ref[...] = slab
        counts_ref[0] = counts

    @pl.when(i == n_tiles)
    def _():
        copy = pltpu.make_async_copy(xs_last_hbm, xs_ref, sem)
        copy.start()
        copy.wait()


def _merge_prompt(x, u, att, gates, wp, ps, wup, wua, wo, nf, wr, br, xs_last, *, tm, seq_len, n_groups, n_per_group):
    n, d = x.shape
    d_pool, d_att = u.shape[1], att.shape[1]
    tile_rows = xs_last.shape[0]
    nt = n // tm
    clamp = lambda i: jnp.minimum(i, nt - 1)
    row = lambda i: (clamp(i), 0)
    const = lambda i: (0, 0)
    const3 = lambda i: (0, 0, 0)
    halo = lambda i: (jnp.maximum(clamp(i) * (tm // POOL_HALO) - 1, 0), 0)
    return pl.pallas_call(
        functools.partial(_merge_prompt_body, seq_len=seq_len, n_groups=n_groups, n_per_group=n_per_group),
        grid=(nt + 1,),
        in_specs=[pl.BlockSpec((tm, d), row), pl.BlockSpec((tm, d_pool), row), pl.BlockSpec((POOL_HALO, d_pool), halo),
                  pl.BlockSpec((tm, d_att), row), pl.BlockSpec((tm, 2 * d), row),
                  pl.BlockSpec(wp.shape, const3), pl.BlockSpec((1, d_pool), const),
                  pl.BlockSpec(wup.shape, const), pl.BlockSpec(wua.shape, const), pl.BlockSpec(wo.shape, const),
                  pl.BlockSpec((1, d), const), pl.BlockSpec(wr.shape, const), pl.BlockSpec((1, LANES), const),
                  pl.BlockSpec(memory_space=pl.ANY)],
        out_specs=[pl.BlockSpec((tm, d), row), pl.BlockSpec((tile_rows, d), lambda i: (i, 0)),
                   pl.BlockSpec((tm, LANES), row), pl.BlockSpec((1, 1, LANES), lambda i: (clamp(i), 0, 0))],
        out_shape=[jax.ShapeDtypeStruct((n, d), F32), jax.ShapeDtypeStruct(((nt + 1) * tile_rows, d), F32),
                   jax.ShapeDtypeStruct((n, LANES), F32), jax.ShapeDtypeStruct((nt, 1, LANES), F32)],
        scratch_shapes=[pltpu.VMEM((tm + POOL_HALO, d_pool), F32), pltpu.SemaphoreType.DMA],
        compiler_params=_params("arbitrary"),
        name="merge_prompt",
    )(x, u, u, att, gates, wp, ps, wup, wua, wo, nf, wr, br, xs_last)


def _merge_sample_body(x_ref, u_ref, st_ref, att_ref, gate_ref, wp_ref, ps_ref, wup_ref, wua_ref, wo_ref,
                       nf_ref, wr_ref, br_ref, x2_ref, xs_ref, slab_ref, counts_ref,
                       *, start_pos, n_groups, n_per_group):
    u = u_ref[...]
    gw = u.shape[1] // len(POOL_WINDOWS)
    n_state = st_ref.shape[0]
    pooled = []
    for g, w in enumerate(POOL_WINDOWS):
        lo = g * gw
        wsum = u[:, lo:lo + gw]
        for j in range(1, w):
            wsum = wsum + st_ref[n_state - j][:, lo:lo + gw]
        pooled.append(wsum / float(min(start_pos + 1, w)) - u[:, lo:lo + gw])
    x2, h2, slab, counts = _merge_and_route(
        x_ref[...], pooled, att_ref[...], gate_ref[...], wp_ref, ps_ref, wup_ref, wua_ref, wo_ref, nf_ref,
        wr_ref, br_ref, precise=True, n_groups=n_groups, n_per_group=n_per_group)
    n = slab.shape[0]
    fields = jnp.concatenate([slab, jnp.zeros((LANES - n, LANES), F32)], axis=0).T
    x2_ref[...] = x2
    xs_ref[...] = _sorted_copy(h2, fields[6:7, 0:n], fields[7:8, 0:n], xs_ref.shape[0])
    slab_ref[...] = slab
    counts_ref[0] = counts


def _merge_sample(x, u, state_t, att, gates, wp, ps, wup, wua, wo, nf, wr, br, *, start_pos, n_groups,
                  n_per_group, tile_rows):
    n, d = x.shape
    assert n <= LANES
    return pl.pallas_call(
        functools.partial(_merge_sample_body, start_pos=start_pos, n_groups=n_groups, n_per_group=n_per_group),
        out_shape=[jax.ShapeDtypeStruct((n, d), F32), jax.ShapeDtypeStruct((tile_rows, d), F32),
                   jax.ShapeDtypeStruct((n, LANES), F32), jax.ShapeDtypeStruct((1, 1, LANES), F32)],
        compiler_params=_params(),
        name="merge_sample",
    )(x, u, state_t, att, gates, wp, ps, wup, wua, wo, nf, wr, br)


def _moe_plan_body(cnt_ref, gsrc_ref, lsrc_ref, te_ref, tw_ref, loc_ref, *, n_ttiles, n_exp, tile_rows, gpt,
                   seg_groups_max, local_stride, zero_group):
    n_mm = te_ref.shape[0]
    gsrc_ref[...] = jnp.full(gsrc_ref.shape, zero_group, I32)
    lsrc_ref[...] = jnp.zeros(lsrc_ref.shape, I32)

    def fill_tiles(t, c):
        te_ref[t] = -1
        tw_ref[t] = n_exp - 1
        return c

    def fill_loc(i, c):
        loc_ref[i] = 0
        return c

    lax.fori_loop(0, n_mm, fill_tiles, 0)
    lax.fori_loop(0, n_ttiles, fill_loc, 0)
    step = lax.broadcasted_iota(I32, (seg_groups_max, LANES), 0)

    def per_expert(e, pos):
        def per_tile(i, p):
            g = (cnt_ref[i * n_exp + e] + (ROW_GROUP - 1)) // ROW_GROUP
            loc = loc_ref[i]
            gsrc_ref[pl.ds(p, seg_groups_max), :] = i * tile_rows + (loc + step) * ROW_GROUP
            lsrc_ref[pl.ds(i * local_stride + loc, seg_groups_max), :] = p + step
            loc_ref[i] = loc + g
            return p + g

        end = lax.fori_loop(0, n_ttiles, per_tile, pos)
        end_pad = ((end + (gpt - 1)) // gpt) * gpt
        gsrc_ref[pl.ds(end, seg_groups_max), :] = jnp.full((seg_groups_max, LANES), zero_group, I32)

        def mark(t, c):
            te_ref[t] = e
            tw_ref[t] = e
            return c

        lax.fori_loop(pos // gpt, end_pad // gpt, mark, 0)
        return end_pad

    lax.fori_loop(0, n_exp, per_expert, 0)

    def clear_tail(i, c):
        lsrc_ref[pl.ds(i * local_stride + loc_ref[i], seg_groups_max), :] = jnp.zeros((seg_groups_max, LANES), I32)
        return c

    lax.fori_loop(0, n_ttiles, clear_tail, 0)


def _moe_plan(cnt, *, n_ttiles, n_exp, tile_rows, gpt, n_mm_tiles, seg_groups_max):
    smem = pl.BlockSpec(memory_space=pltpu.SMEM)
    zero_group = n_ttiles * tile_rows - ROW_GROUP
    local_groups = tile_rows // ROW_GROUP
    local_stride = local_groups + seg_groups_max
    n_groups = n_mm_tiles * gpt
    gsrc, lsrc, te, tw = pl.pallas_call(
        functools.partial(_moe_plan_body, n_ttiles=n_ttiles, n_exp=n_exp, tile_rows=tile_rows, gpt=gpt,
                          seg_groups_max=seg_groups_max, local_stride=local_stride, zero_group=zero_group),
        in_specs=[smem], out_specs=[pl.BlockSpec(memory_space=pltpu.VMEM), pl.BlockSpec(memory_space=pltpu.VMEM),
                                    smem, smem],
        out_shape=[jax.ShapeDtypeStruct((n_groups + 2 * seg_groups_max, LANES), I32),
                   jax.ShapeDtypeStruct((n_ttiles * local_stride, LANES), I32),
                   jax.ShapeDtypeStruct((n_mm_tiles,), I32), jax.ShapeDtypeStruct((n_mm_tiles,), I32)],
        scratch_shapes=[pltpu.SMEM((n_ttiles,), I32)],
        name="moe_plan",
    )(cnt)
    return gsrc[:n_groups, 0], lsrc[:, 0], te, tw, local_stride


def _moe_mm_body(te_ref, tw_ref, src_ref, xs_hbm, wg_ref, wu_ref, wd_ref, ys_ref, xbuf, wgb_ref, wub_ref, wdb_ref,
                 sems):
    i = pl.program_id(0)
    tm = ys_ref.shape[0]
    groups = tm // ROW_GROUP
    expert = te_ref[i]
    prev = te_ref[jnp.maximum(i - 1, 0)]

    def fetch(slot, tile):
        for k in range(groups):
            src = pl.multiple_of(src_ref[tile * groups + k], ROW_GROUP)
            pltpu.make_async_copy(xs_hbm.at[pl.ds(src, ROW_GROUP)],
                                  xbuf.at[slot, pl.ds(k * ROW_GROUP, ROW_GROUP)], sems.at[slot]).start()

    slot = i % 2

    @pl.when(i == 0)
    def _():
        fetch(0, 0)

    @pl.when(i + 1 < pl.num_programs(0))
    def _():
        fetch(1 - slot, i + 1)

    pltpu.make_async_copy(xs_hbm.at[pl.ds(0, tm)], xbuf.at[slot], sems.at[slot]).wait()

    @pl.when((expert >= 0) & ((i == 0) | (expert != prev)))
    def _():
        wgb_ref[...] = wg_ref[0].astype(BF16)
        wub_ref[...] = wu_ref[0].astype(BF16)
        wdb_ref[...] = wd_ref[0].astype(BF16)

    @pl.when(expert >= 0)
    def _():
        x = xbuf[slot].astype(BF16)
        a = jnp.dot(x, wgb_ref[...], preferred_element_type=F32)
        b = jnp.dot(x, wub_ref[...], preferred_element_type=F32)
        hdn = (a * _sigmoid(a)) * b
        ys_ref[...] = jnp.dot(hdn.astype(BF16), wdb_ref[...], preferred_element_type=F32)

    @pl.when(expert < 0)
    def _():
        ys_ref[...] = jnp.zeros_like(ys_ref)


def _moe_mm(tile_expert, tile_weight, group_src, xs, w_gate, w_up, w_down, *, tm):
    n_tiles = tile_expert.shape[0]
    d = xs.shape[1]
    n_exp, _, de = w_gate.shape
    wmap = lambda i, te, tw, src: (tw[i], 0, 0)
    grid_spec = pltpu.PrefetchScalarGridSpec(
        num_scalar_prefetch=3, grid=(n_tiles,),
        in_specs=[pl.BlockSpec(memory_space=pl.ANY),
                  pl.BlockSpec((1, d, de), wmap), pl.BlockSpec((1, d, de), wmap), pl.BlockSpec((1, de, d), wmap)],
        out_specs=pl.BlockSpec((tm, d), lambda i, te, tw, src: (i, 0)),
        scratch_shapes=[pltpu.VMEM((2, tm, d), F32), pltpu.VMEM((d, de), BF16), pltpu.VMEM((d, de), BF16),
                        pltpu.VMEM((de, d), BF16), pltpu.SemaphoreType.DMA((2,))])
    return pl.pallas_call(
        _moe_mm_body, grid_spec=grid_spec,
        out_shape=jax.ShapeDtypeStruct((n_tiles * tm, d), F32),
        compiler_params=_params("arbitrary"),
        name="moe_mm",
    )(tile_expert, tile_weight, group_src, xs, w_gate, w_up, w_down)


def _moe_combine_body(lsrc_ref, x_ref, slab_ref, g_ref, ys_hbm, o_ref, ybuf, sems,
                      *, final_norm, first_tile, local_stride):
    i = pl.program_id(0)
    tm = x_ref.shape[0]
    tile_rows = ybuf.shape[1]
    unroll = 8

    def fetch(slot, tile):
        def body(c, carry):
            for j in range(unroll):
                lg = c * unroll + j
                src = pl.multiple_of(lsrc_ref[tile * local_stride + lg] * ROW_GROUP, ROW_GROUP)
                dst = pl.multiple_of(lg * ROW_GROUP, ROW_GROUP)
                pltpu.make_async_copy(ys_hbm.at[pl.ds(src, ROW_GROUP)], ybuf.at[slot, pl.ds(dst, ROW_GROUP)],
                                      sems.at[slot]).start()
            return carry

        lax.fori_loop(0, tile_rows // ROW_GROUP // unroll, body, 0)

    slot = i % 2

    @pl.when(i == 0)
    def _():
        fetch(0, first_tile)

    @pl.when(i + 1 < pl.num_programs(0))
    def _():
        fetch(1 - slot, first_tile + i + 1)

    pltpu.make_async_copy(ys_hbm.at[pl.ds(0, tile_rows)], ybuf.at[slot], sems.at[slot]).wait()
    y = ybuf[slot].astype(BF16)
    slab = slab_ref[...]
    r = lax.broadcasted_iota(I32, (tm, tile_rows), 1).astype(F32)
    ya = jnp.dot((r == slab[:, 6:7]).astype(BF16), y, preferred_element_type=F32)
    yb = jnp.dot((r == slab[:, 7:8]).astype(BF16), y, preferred_element_type=F32)
    out = x_ref[...] + (slab[:, 4:5] * ya + slab[:, 5:6] * yb)
    if final_norm:
        out = _rmsnorm(out, g_ref[...])
    o_ref[...] = out


def _moe_combine(local_src, x, slab, g, ys, *, ts, tile_rows, final_norm, first_tile, local_stride):
    n, d = x.shape
    return pl.pallas_call(
        functools.partial(_moe_combine_body, final_norm=final_norm, first_tile=first_tile,
                          local_stride=local_stride),
        grid=(n // ts,),
        in_specs=[pl.BlockSpec(memory_space=pltpu.SMEM),
                  pl.BlockSpec((ts, d), lambda i: (i, 0)), pl.BlockSpec((ts, LANES), lambda i: (i, 0)),
                  pl.BlockSpec((1, d), lambda i: (0, 0)), pl.BlockSpec(memory_space=pl.ANY)],
        out_specs=pl.BlockSpec((ts, d), lambda i: (i, 0)),
        out_shape=jax.ShapeDtypeStruct((n, d), F32),
        scratch_shapes=[pltpu.VMEM((2, tile_rows, d), F32), pltpu.SemaphoreType.DMA((2,))],
        compiler_params=_params("arbitrary"),
        name="moe_combine",
    )(local_src, x, slab, g, ys)


def kernel(x_prompt, x_sample, cache_k, cache_v, cache_logf, state_pool, page_table, norm_mix, w_in, b_forget,
           w_pool, pool_scale, w_up_pool, w_up_att, w_out, norm_ffn, w_router_group, b_router_group,
           w_router_expert, b_router_expert, w_gate, w_up, w_down, norm_final):
    depth = norm_mix.shape[0]
    assert depth == 1, "single trunk layer"
    b, t, d = x_prompt.shape
    db, dt, _ = x_sample.shape
    assert dt == 1, "one sample token per sequence"
    _, n_phys, page, n_heads, dh = cache_k.shape
    n_pages = page_table.shape[1]
    past = n_pages * page
    n_state, d_pool = state_pool.shape[2], state_pool.shape[3]
    d_att = n_heads * dh
    n_pool_groups = w_pool.shape[1]
    assert n_pool_groups == len(POOL_WINDOWS) and d_pool // n_pool_groups == LANES
    assert n_state == max(POOL_WINDOWS) - 1 and n_state < POOL_HALO
    n_groups, n_per_group = w_router_expert.shape[1], w_router_expert.shape[3]
    n_exp = n_groups * n_per_group
    assert n_groups + n_exp <= LANES and 2 * dh == LANES and n_heads % 2 == 0
    n = b * t
    q_scale = float(dh) ** -0.5
    tm = min(TOKEN_TILE, t)
    assert t % tm == 0 and t % ATTN_TILE == 0

    o_main = d_pool + 3 * d_att
    wi = w_in[0]
    wm_f, wf_f, wg_f = wi[:, :o_main], wi[:, o_main:o_main + n_heads], wi[:, o_main + n_heads:]
    wf_pad = jnp.pad(wf_f, ((0, 0), (0, LANES - n_heads)))
    wit = jnp.transpose(wi)
    wmt_f, wgt_f = wit[:o_main], wit[o_main + n_heads:]
    wft_pad = jnp.pad(wit[o_main:o_main + n_heads], ((0, LANES - n_heads), (0, 0)))
    bf_pad = jnp.pad(b_forget[0], (0, LANES - n_heads)).reshape(1, LANES)
    g_mix = norm_mix[0].reshape(1, d)
    g_ffn = norm_ffn[0].reshape(1, d)
    g_fin = norm_final.reshape(1, d)
    ps = pool_scale[0].reshape(1, d_pool)
    wr_f = jnp.concatenate([w_router_group[0], jnp.transpose(w_router_expert[0], (1, 0, 2)).reshape(d, n_exp)], axis=1)
    wr_pad = jnp.pad(wr_f, ((0, 0), (0, LANES - n_groups - n_exp)))
    br_pad = jnp.pad(jnp.concatenate([b_router_group[0], b_router_expert[0].reshape(n_exp)]),
                     (0, LANES - n_groups - n_exp)).reshape(1, LANES)
    bf = lambda a: a.astype(BF16)

    xp = x_prompt.reshape(n, d)
    u_p, q_p, kt_p, vt_p, kb_p, vb_p, lft_p, gate_p = _proj_prompt(
        xp, g_mix, bf(wm_f), bf(wf_pad), bf(wg_f), bf_pad, tm=tm, seq_len=t, d_pool=d_pool, d_att=d_att,
        n_heads=n_heads, q_scale=q_scale)
    c = _cumsum_lanes(lft_p.reshape(b * n_heads, t))
    nt = t // ATTN_TILE
    c_blk = jnp.transpose(c.reshape(b, n_heads // 2, 2, nt, ATTN_TILE), (0, 1, 3, 2, 4))
    n_ptiles = n // tm
    n_ttiles = n_ptiles + 1
    tile_rows = -(-(2 * tm + n_exp * (ROW_GROUP - 1)) // MOE_ROW_TILE) * MOE_ROW_TILE
    assert 2 * db + n_exp * (ROW_GROUP - 1) <= tile_rows - ROW_GROUP, "the sample tile must end in an unused row group"

    xs = x_sample.reshape(db, d)
    z_s, lf_s, gate_s = _proj_sample(xs, g_mix, wmt_f, wft_pad, wgt_f, bf_pad, tn=512)
    u_s = z_s[:, :d_pool]
    q_s = z_s[:, d_pool:d_pool + d_att] * q_scale
    k_s = z_s[:, d_pool + d_att:d_pool + 2 * d_att]
    v_s = z_s[:, d_pool + 2 * d_att:]
    att_p, att_s = _attention(q_p.reshape(b, t, d_att), kb_p.reshape(b, t, d_att), vb_p.reshape(b, t, d_att), c_blk,
                              page_table, q_s, k_s, v_s, lf_s[:, :n_heads],
                              jnp.transpose(cache_k[0], (0, 2, 3, 1)), jnp.transpose(cache_v[0], (0, 2, 3, 1)),
                              jnp.transpose(cache_logf[0], (0, 2, 1)), tile=ATTN_TILE, dh=dh)
    state_t = jnp.transpose(state_pool[0], (1, 0, 2))
    x2_s, xs_rows_s, slab_s, counts_s = _merge_sample(
        xs, u_s, state_t, att_s.reshape(db, d_att), gate_s, w_pool[0], ps, w_up_pool[0], w_up_att[0], w_out[0],
        g_ffn, wr_pad, br_pad, start_pos=past, n_groups=n_groups, n_per_group=n_per_group, tile_rows=tile_rows)

    x2_p, xs_rows, slab_p, counts_p = _merge_prompt(
        xp, u_p, att_p.reshape(n, d_att), gate_p, bf(w_pool[0]), ps, bf(w_up_pool[0]), bf(w_up_att[0]),
        bf(w_out[0]), g_ffn, bf(wr_pad), br_pad, xs_rows_s, tm=tm, seq_len=t, n_groups=n_groups,
        n_per_group=n_per_group)

    tmm = MOE_ROW_TILE
    gpt = tmm // ROW_GROUP
    n_groups_max = -(-(2 * (n + db)) // ROW_GROUP) + n_ttiles * n_exp + n_exp * (gpt - 1)
    cnt = jnp.concatenate([counts_p, counts_s], axis=0)[:, 0, :n_exp].astype(I32).reshape(-1)
    group_src, local_src, tile_expert, tile_weight, local_stride = _moe_plan(
        cnt, n_ttiles=n_ttiles, n_exp=n_exp, tile_rows=tile_rows, gpt=gpt, n_mm_tiles=-(-n_groups_max // gpt),
        seg_groups_max=tm // ROW_GROUP)
    ys_rows = _moe_mm(tile_expert, tile_weight, group_src, xs_rows, w_gate[0], w_up[0], w_down[0], tm=tmm)
    y_prompt = _moe_combine(local_src, x2_p, slab_p, g_fin, ys_rows, ts=tm, tile_rows=tile_rows,
                            final_norm=True, first_tile=0, local_stride=local_stride)
    y_sample = _moe_combine(local_src, x2_s, slab_s, g_fin, ys_rows, ts=db, tile_rows=tile_rows,
                            final_norm=True, first_tile=n_ptiles, local_stride=local_stride)

    new_pool_p = u_p.reshape(b, t, d_pool)[:, t - n_state:, :]
    new_pool_s = jnp.concatenate([state_pool[0][:, 1:, :], u_s[:, None, :]], axis=1)
    to_heads = lambda a: jnp.transpose(a.reshape(b, n_heads, dh, t), (0, 3, 1, 2))[None]
    return (y_prompt.reshape(b, t, d), y_sample.reshape(db, 1, d),
            to_heads(kt_p), to_heads(vt_p), jnp.transpose(lft_p, (0, 2, 1))[None],
            new_pool_p[None],
            k_s.reshape(1, db, 1, n_heads, dh), v_s.reshape(1, db, 1, n_heads, dh),
            lf_s[:, :n_heads].reshape(1, db, 1, n_heads), new_pool_s[None])
```

```python
import functools

import jax
import jax.numpy as jnp
from jax import lax
from jax.experimental import pallas as pl
from jax.experimental.pallas import tpu as pltpu

F32 = jnp.float32
BF16 = jnp.bfloat16
I32 = jnp.int32
HIGHEST = lax.Precision.HIGHEST

RMS_EPS = 1e-6
POOL_WINDOWS = (2, 4, 8, 16)
POOL_HALO = 16
LANES = 128
VMEM_LIMIT_BYTES = 56 * 1024 * 1024

TOKEN_TILE = 512
ATTN_TILE = 512
MOE_ROW_TILE = 256
PAGES_PER_STEP = 16
ROW_GROUP = 16


def _params(*sem):
    return pltpu.CompilerParams(dimension_semantics=sem, vmem_limit_bytes=VMEM_LIMIT_BYTES)


def _rmsnorm(x, g):
    return x * lax.rsqrt(jnp.mean(x * x, axis=-1, keepdims=True) + RMS_EPS) * g


def _log_sigmoid(x):
    return jnp.minimum(x, 0.0) - jnp.log1p(jnp.exp(-jnp.abs(x)))


def _sigmoid(x):
    return 1.0 / (1.0 + jnp.exp(-x))


def _dot(a, b, precise):
    if precise:
        return jnp.dot(a.astype(F32), b.astype(F32), precision=HIGHEST, preferred_element_type=F32)
    return jnp.dot(a.astype(BF16), b.astype(BF16), preferred_element_type=F32)


def _split3(x):
    hi = x.astype(BF16)
    r = x - hi.astype(F32)
    mid = r.astype(BF16)
    lo = (r - mid.astype(F32)).astype(BF16)
    return hi, mid, lo


def _dot_exact_rhs(x, w_bf16):
    hi, mid, lo = _split3(x)
    d = lambda a: jnp.dot(a, w_bf16, preferred_element_type=F32)
    return d(hi) + d(mid) + d(lo)


def _proj_body(x_ref, g_ref, wm_ref, wf_ref, wg_ref, bf_ref,
               u_ref, q_ref, kt_ref, vt_ref, kb_ref, vb_ref, lft_ref, gate_ref, *, d_pool, d_att, n_heads, q_scale):
    h = _rmsnorm(x_ref[...], g_ref[...]).astype(BF16)
    z = jnp.dot(h, wm_ref[...], preferred_element_type=F32)
    o1, o2, o3 = d_pool, d_pool + d_att, d_pool + 2 * d_att
    u_ref[...] = z[:, :o1]
    q_ref[...] = (z[:, o1:o2] * q_scale).astype(BF16)
    k = z[:, o2:o3]
    v = z[:, o3:]
    kt_ref[0] = k.T
    vt_ref[0] = v.T
    kb_ref[...] = k.astype(BF16)
    vb_ref[...] = v.astype(BF16)
    lf = _log_sigmoid(jnp.dot(h, wf_ref[...], preferred_element_type=F32) + bf_ref[...])
    lft_ref[0] = lf.T[0:n_heads, :]
    gate_ref[...] = _sigmoid(jnp.dot(h, wg_ref[...], preferred_element_type=F32)).astype(BF16)


def _proj_prompt(x, g, wm, wf, wg, bfp, *, tm, seq_len, d_pool, d_att, n_heads, q_scale):
    n, d = x.shape
    b = n // seq_len
    tps = seq_len // tm
    row = lambda i: (i, 0)
    const = lambda i: (0, 0)
    tmin = lambda i: (i // tps, 0, i % tps)
    dg = wg.shape[1]
    out_shape = [
        jax.ShapeDtypeStruct((n, d_pool), F32), jax.ShapeDtypeStruct((n, d_att), BF16),
        jax.ShapeDtypeStruct((b, d_att, seq_len), F32), jax.ShapeDtypeStruct((b, d_att, seq_len), F32),
        jax.ShapeDtypeStruct((n, d_att), BF16), jax.ShapeDtypeStruct((n, d_att), BF16),
        jax.ShapeDtypeStruct((b, n_heads, seq_len), F32), jax.ShapeDtypeStruct((n, dg), BF16),
    ]
    return pl.pallas_call(
        functools.partial(_proj_body, d_pool=d_pool, d_att=d_att, n_heads=n_heads, q_scale=q_scale),
        grid=(n // tm,),
        in_specs=[pl.BlockSpec((tm, d), row), pl.BlockSpec((1, d), const),
                  pl.BlockSpec(wm.shape, const), pl.BlockSpec(wf.shape, const),
                  pl.BlockSpec(wg.shape, const), pl.BlockSpec((1, LANES), const)],
        out_specs=[pl.BlockSpec((tm, d_pool), row), pl.BlockSpec((tm, d_att), row),
                   pl.BlockSpec((1, d_att, tm), tmin), pl.BlockSpec((1, d_att, tm), tmin),
                   pl.BlockSpec((tm, d_att), row), pl.BlockSpec((tm, d_att), row),
                   pl.BlockSpec((1, n_heads, tm), tmin), pl.BlockSpec((tm, dg), row)],
        out_shape=out_shape,
        compiler_params=_params("arbitrary"),
        name="proj_prompt",
    )(x, g, wm, wf, wg, bfp)


def _dot_nt(a, bt, precise):
    dims = (((1,), (1,)), ((), ()))
    if precise:
        return lax.dot_general(a.astype(F32), bt.astype(F32), dims, precision=HIGHEST, preferred_element_type=F32)
    return lax.dot_general(a.astype(BF16), bt.astype(BF16), dims, preferred_element_type=F32)


def _proj_sample_body(x_ref, g_ref, wmt_ref, wft_ref, wgt_ref, bf_ref, z_ref, lf_ref, gate_ref):
    h = _rmsnorm(x_ref[...], g_ref[...])
    z_ref[...] = _dot_nt(h, wmt_ref[...], True)
    lf_ref[...] = _log_sigmoid(_dot_nt(h, wft_ref[...], True) + bf_ref[...])
    gate_ref[...] = _sigmoid(_dot_nt(h, wgt_ref[...], True))


def _proj_sample(x, g, wmt, wft, wgt, bfp, *, tn):
    n, d = x.shape
    dm, dg = wmt.shape[0], wgt.shape[0]
    assert dm == dg
    const = lambda j: (0, 0)
    chunk = lambda j: (j, 0)
    col = lambda j: (0, j)
    return pl.pallas_call(
        _proj_sample_body,
        grid=(dm // tn,),
        in_specs=[pl.BlockSpec((n, d), const), pl.BlockSpec((1, d), const),
                  pl.BlockSpec((tn, d), chunk), pl.BlockSpec(wft.shape, const),
                  pl.BlockSpec((tn, d), chunk), pl.BlockSpec((1, LANES), const)],
        out_specs=[pl.BlockSpec((n, tn), col), pl.BlockSpec((n, LANES), const), pl.BlockSpec((n, tn), col)],
        out_shape=[jax.ShapeDtypeStruct((n, dm), F32), jax.ShapeDtypeStruct((n, LANES), F32),
                   jax.ShapeDtypeStruct((n, dg), F32)],
        compiler_params=_params("arbitrary"),
        name="proj_sample",
    )(x, g, wmt, wft, wgt, bfp)


def _cumsum_body(x_ref, o_ref):
    c = x_ref[...]
    lane = lax.broadcasted_iota(I32, c.shape, 1)
    s = 1
    while s < c.shape[1]:
        c = c + jnp.where(lane >= s, pltpu.roll(c, s, 1), 0.0)
        s *= 2
    o_ref[...] = c


def _cumsum_lanes(x):
    return pl.pallas_call(_cumsum_body, out_shape=jax.ShapeDtypeStruct(x.shape, F32),
                          compiler_params=_params(), name="cumsum_logf")(x)


def _prompt_q_tile(qi, q_ref, kts, vhs, c_ref, o_ref, *, tile, dh, first, causal):
    q = q_ref[0, qi * tile:(qi + 1) * tile, :]
    zero = jnp.zeros_like(q)
    q_heads = (jnp.where(first, q, zero), jnp.where(first, zero, q))
    res = []
    for h in range(2):
        m = jnp.full((tile, 1), -1e30, F32)
        acc = jnp.zeros((tile, 2 * dh), F32)
        for kj in range(qi + 1):
            s = lax.dot_general(q_heads[h], kts[kj], (((1,), (1,)), ((), ())), preferred_element_type=F32)
            s = s - c_ref[0, 0, kj][h:h + 1, :]
            if kj == qi:
                s = jnp.where(causal, s, -jnp.inf)
            m_new = jnp.maximum(m, jnp.max(s, axis=-1, keepdims=True))
            alpha = jnp.exp(m - m_new)
            p = jnp.exp(s - m_new)
            acc = alpha * acc + jnp.dot(p.astype(BF16), vhs[kj][h], preferred_element_type=F32)
            m = m_new
        res.append(acc)
    a0, a1 = res
    out = jnp.where(first, a0 / a0[:, dh:dh + 1], a1 / a1[:, 0:1])
    o_ref[0, qi * tile:(qi + 1) * tile, :] = out.astype(o_ref.dtype)


def _sample_chunk(k_refs, v_refs, lf_refs, qrep, carry):
    m_prev, l, acc, s_run = carry
    n_heads, dh, page = acc.shape
    d_att = n_heads * dh
    g_n = len(k_refs)
    r = lax.broadcasted_iota(I32, (page, page), 0)
    c = lax.broadcasted_iota(I32, (page, page), 1)
    later = (r > c).astype(BF16)
    ones = jnp.ones((page, page), BF16)
    lf_all = jnp.concatenate([lf_refs[g][...] for g in range(g_n)], axis=0)
    suffix = _dot_exact_rhs(lf_all, later)
    total = _dot_exact_rhs(lf_all, ones)
    m_new = m_prev
    scores = []
    for g in range(g_n):
        kq = k_refs[g][...].reshape(d_att, page) * qrep
        s = jnp.sum(kq.reshape(n_heads, dh, page), axis=1)
        sb = s + s_run + suffix[g * n_heads:(g + 1) * n_heads]
        s_run = s_run + total[g * n_heads:(g + 1) * n_heads]
        scores.append(sb)
        m_new = jnp.maximum(m_new, jnp.max(sb, axis=-1, keepdims=True))
    alpha = jnp.exp(m_prev - m_new)
    l = alpha * l
    acc = acc * alpha[:, None, :]
    for g in range(g_n):
        p = jnp.exp(scores[g] - m_new)
        l = l + jnp.sum(p, axis=-1, keepdims=True)
        acc = acc + v_refs[g][...] * p[:, None, :]
    return m_new, l, acc, s_run


def _attn_body(pt_ref, q_ref, k_ref, v_ref, c_ref, qrep_ref, qs_ref, kn_ref, vrep_ref, lfn_ref,
               ck_hbm, cv_hbm, clf_hbm, o_ref, os_ref, kbuf, vbuf, lfbuf, sems, *, tile, dh):
    step = pl.program_id(0) * pl.num_programs(1) + pl.program_id(1)
    n_steps = pl.num_programs(0) * pl.num_programs(1)
    nt = q_ref.shape[1] // tile
    n_pages = pt_ref.shape[1]
    _, g_n, n_heads, _, page = kbuf.shape
    d_att = n_heads * dh
    seqs = qrep_ref.shape[0]
    n_chunks = n_pages // g_n
    n_items = seqs * n_chunks

    def fetch(slot, seq, chunk):
        for g in range(g_n):
            pid = pt_ref[seq, n_pages - 1 - (chunk * g_n + g)]
            pltpu.make_async_copy(ck_hbm.at[pid], kbuf.at[slot, g], sems.at[slot, 0]).start()
            pltpu.make_async_copy(cv_hbm.at[pid], vbuf.at[slot, g], sems.at[slot, 1]).start()
            pltpu.make_async_copy(clf_hbm.at[pid], lfbuf.at[slot, g], sems.at[slot, 2]).start()

    @pl.when(step == 0)
    def _():
        fetch(0, 0, 0)

    lane = lax.broadcasted_iota(I32, (tile, 2 * dh), 1)
    first = lane < dh
    row = lax.broadcasted_iota(I32, (tile, tile), 0)
    col = lax.broadcasted_iota(I32, (tile, tile), 1)
    causal = col <= row
    one = jnp.ones((tile, 2 * dh), BF16)
    kts, vhs = [], []
    for kj in range(nt):
        vt = v_ref[0, kj * tile:(kj + 1) * tile, :]
        kts.append(k_ref[0, kj * tile:(kj + 1) * tile, :])
        vhs.append((jnp.where(first, vt, one), jnp.where(first, one, vt)))

    q_done = 0
    carry = None
    for item in range(n_items):
        j, chunk = divmod(item, n_chunks)
        slot = item % 2
        if item + 1 < n_items:
            fetch(1 - slot, step * seqs + (item + 1) // n_chunks, (item + 1) % n_chunks)
        else:
            @pl.when(step + 1 < n_steps)
            def _():
                fetch(1 - slot, (step + 1) * seqs, 0)
        pltpu.make_async_copy(ck_hbm.at[pl.ds(0, g_n)], kbuf.at[slot], sems.at[slot, 0]).wait()
        pltpu.make_async_copy(cv_hbm.at[pl.ds(0, g_n)], vbuf.at[slot], sems.at[slot, 1]).wait()
        pltpu.make_async_copy(clf_hbm.at[pl.ds(0, g_n)], lfbuf.at[slot], sems.at[slot, 2]).wait()
        if chunk == 0:
            s_new = jnp.sum(qs_ref[j] * kn_ref[j], axis=-1, keepdims=True)
            lane_p = lax.broadcasted_iota(I32, (d_att, page), 1)
            carry = (jnp.broadcast_to(s_new, (n_heads, page)), jnp.ones((n_heads, page), F32),
                     jnp.where(lane_p == 0, vrep_ref[j], 0.0).reshape(n_heads, dh, page), lfn_ref[j])
        carry = _sample_chunk([kbuf.at[slot, g] for g in range(g_n)], [vbuf.at[slot, g] for g in range(g_n)],
                              [lfbuf.at[slot, g] for g in range(g_n)], qrep_ref[j], carry)
        if chunk == n_chunks - 1:
            _, l, acc, _ = carry
            os_ref[j] = jnp.sum(acc / l[:, None, :], axis=-1)
        q_until = ((item + 1) * nt) // n_items
        for qi in range(q_done, q_until):
            _prompt_q_tile(qi, q_ref, kts, vhs, c_ref, o_ref, tile=tile, dh=dh, first=first, causal=causal)
        q_done = q_until


def _attention(q, k, v, c, page_table, q_s, k_new, v_new, lf_new, cache_kt, cache_vt, cache_lft, *, tile, dh):
    b, t, da = q.shape
    hp = da // (2 * dh)
    nt = t // tile
    db, n_pages = page_table.shape
    n_phys, n_heads, _, page = cache_kt.shape
    d_att = n_heads * dh
    n_steps = b * hp
    assert db % n_steps == 0, "sample sequences are split evenly over the prompt grid steps"
    seqs = db // n_steps
    g_n = PAGES_PER_STEP
    while n_pages % g_n:
        g_n //= 2
    assert (seqs * (n_pages // g_n)) % 2 == 0, "buffer slots alternate per page chunk"
    lane_rep = lambda a: jnp.broadcast_to(a.reshape(db, -1, 1), (db, a.size // db, page))
    pair = pl.BlockSpec((1, t, 2 * dh), lambda bi, hi, pt: (bi, 0, hi))
    per_step = lambda bi, hi, pt: (bi * hp + hi, 0, 0)
    hbm = pl.BlockSpec(memory_space=pl.ANY)
    grid_spec = pltpu.PrefetchScalarGridSpec(
        num_scalar_prefetch=1, grid=(b, hp),
        in_specs=[pair, pair, pair, pl.BlockSpec((1, 1, nt, 2, tile), lambda bi, hi, pt: (bi, hi, 0, 0, 0)),
                  pl.BlockSpec((seqs, d_att, page), per_step), pl.BlockSpec((seqs, n_heads, dh), per_step),
                  pl.BlockSpec((seqs, n_heads, dh), per_step), pl.BlockSpec((seqs, d_att, page), per_step),
                  pl.BlockSpec((seqs, n_heads, page), per_step), hbm, hbm, hbm],
        out_specs=[pair, pl.BlockSpec((seqs, n_heads, dh), per_step)],
        scratch_shapes=[pltpu.VMEM((2, g_n, n_heads, dh, page), F32), pltpu.VMEM((2, g_n, n_heads, dh, page), F32),
                        pltpu.VMEM((2, g_n, n_heads, page), F32), pltpu.SemaphoreType.DMA((2, 3))])
    return pl.pallas_call(
        functools.partial(_attn_body, tile=tile, dh=dh),
        grid_spec=grid_spec,
        out_shape=[jax.ShapeDtypeStruct((b, t, da), BF16), jax.ShapeDtypeStruct((db, n_heads, dh), F32)],
        compiler_params=_params("arbitrary", "arbitrary"),
        name="attention",
    )(page_table, q, k, v, c, lane_rep(q_s), q_s.reshape(db, n_heads, dh), k_new.reshape(db, n_heads, dh),
      lane_rep(v_new), lane_rep(lf_new), cache_kt, cache_vt, cache_lft)


def _merge_and_route(x, pooled, att, gates, wp_ref, ps_ref, wup_ref, wua_ref, wo_ref, nf_ref, wr_ref, br_ref,
                     *, precise, n_groups, n_per_group):
    tm, d = x.shape
    gw = pooled[0].shape[1]
    mixed = jnp.concatenate([_dot(pooled[g], wp_ref[g], precise) for g in range(len(pooled))], axis=-1)
    pool_out = mixed * ps_ref[...]
    y = gates[:, :d].astype(F32) * _dot(pool_out, wup_ref[...], precise) \
        + gates[:, d:].astype(F32) * _dot(att, wua_ref[...], precise)
    x2 = x + _dot(y, wo_ref[...], precise)
    h2 = _rmsnorm(x2, nf_ref[...])
    logits = _dot(h2, wr_ref[...], precise) + br_ref[...]
    lane = lax.broadcasted_iota(I32, logits.shape, 1)
    lanef = lane.astype(F32)
    neg = -jnp.inf
    is_g = lane < n_groups
    gmax = jnp.max(jnp.where(is_g, logits, neg), axis=-1, keepdims=True)
    gidx = jnp.min(jnp.where(is_g & (logits == gmax), lanef, float(LANES)), axis=-1, keepdims=True)
    gsum = jnp.sum(jnp.where(is_g, jnp.exp(logits - gmax), 0.0), axis=-1, keepdims=True)
    g_w = 1.0 / gsum
    n_exp = n_groups * n_per_group
    exp_id = lanef - float(n_groups)
    in_sel = (lane >= n_groups) & (lane < n_groups + n_exp) & (jnp.floor(exp_id / n_per_group) == gidx)
    v1 = jnp.max(jnp.where(in_sel, logits, neg), axis=-1, keepdims=True)
    i1 = jnp.min(jnp.where(in_sel & (logits == v1), lanef, float(LANES)), axis=-1, keepdims=True)
    in_sel2 = in_sel & (lanef != i1)
    v2 = jnp.max(jnp.where(in_sel2, logits, neg), axis=-1, keepdims=True)
    i2 = jnp.min(jnp.where(in_sel2 & (logits == v2), lanef, float(LANES)), axis=-1, keepdims=True)
    t = jnp.exp(v2 - v1)
    w1 = g_w * (1.0 / (1.0 + t))
    w2 = g_w * (t / (1.0 + t))
    e1 = i1 - float(n_groups)
    e2 = i2 - float(n_groups)
    hit1 = lanef == e1
    hit2 = lanef == e2
    onehot = (hit1 | hit2).astype(BF16)
    rr = lax.broadcasted_iota(I32, (tm, tm), 0)
    cc = lax.broadcasted_iota(I32, (tm, tm), 1)
    incl = jnp.dot((cc <= rr).astype(BF16), onehot, preferred_element_type=F32)
    counts = incl[tm - 1:tm, :]
    groups = jnp.floor((counts + (ROW_GROUP - 1.0)) * (1.0 / ROW_GROUP))
    ur = lax.broadcasted_iota(I32, (LANES, LANES), 0)
    uc = lax.broadcasted_iota(I32, (LANES, LANES), 1)
    before = jnp.dot(jnp.broadcast_to(groups, (8, LANES)).astype(BF16), (ur < uc).astype(BF16),
                     preferred_element_type=F32)[0:1]
    seg_start = before * float(ROW_GROUP)
    pick = lambda hit, tbl: jnp.sum(jnp.where(hit, tbl, 0.0), axis=-1, keepdims=True)
    r1 = pick(hit1, incl) - 1.0
    r2 = pick(hit2, incl) - 1.0
    row1 = pick(hit1, seg_start) + r1
    row2 = pick(hit2, seg_start) + r2
    slab = jnp.zeros((tm, LANES), F32)
    for i, val in enumerate((e1, e2, r1, r2, w1, w2, row1, row2)):
        slab = jnp.where(lane == i, val, slab)
    return x2, h2, slab, counts


def _sorted_copy(h2, row1, row2, n_rows):
    tm = h2.shape[0]
    r = lax.broadcasted_iota(I32, (n_rows, tm), 0).astype(F32)
    place = ((r == row1) | (r == row2)).astype(BF16)
    return jnp.dot(place, h2.astype(BF16), preferred_element_type=F32).astype(BF16)


def _merge_prompt_body(x_ref, u_ref, halo_ref, att_ref, gate_ref, wp_ref, ps_ref, wup_ref, wua_ref, wo_ref,
                       nf_ref, wr_ref, br_ref, xs_last_hbm, x2_ref, xs_ref, slab_ref, counts_ref,
                       ext_ref, sem, *, seq_len, n_groups, n_per_group):
    i = pl.program_id(0)
    n_tiles = pl.num_programs(0) - 1
    tm = x_ref.shape[0]
    gw = u_ref.shape[1] // len(POOL_WINDOWS)

    @pl.when(i < n_tiles)
    def _():
        pos0 = (i * tm) % seq_len
        u = u_ref[...]
        ext_ref[0:POOL_HALO, :] = jnp.where(pos0 == 0, 0.0, halo_ref[...])
        ext_ref[POOL_HALO:, :] = u
        pos = pos0 + lax.broadcasted_iota(I32, (tm, 1), 0)
        pooled = []
        for g, w in enumerate(POOL_WINDOWS):
            lo = g * gw
            wsum = ext_ref[pl.ds(POOL_HALO, tm), lo:lo + gw]
            for j in range(1, w):
                wsum = wsum + ext_ref[pl.ds(POOL_HALO - j, tm), lo:lo + gw]
            count = jnp.minimum(pos + 1, w).astype(F32)
            pooled.append(wsum / count - u[:, lo:lo + gw])
        x2, h2, slab, counts = _merge_and_route(
            x_ref[...], pooled, att_ref[...], gate_ref[...], wp_ref, ps_ref, wup_ref, wua_ref, wo_ref, nf_ref,
            wr_ref, br_ref, precise=False, n_groups=n_groups, n_per_group=n_per_group)
        fields = slab.T[0:8, :]
        x2_ref[...] = x2
        xs_ref[...] = _sorted_copy(h2, fields[6:7, :], fields[7:8, :], xs_ref.shape[0])
        slab_ref[...] = slab
        counts_ref[0] = counts

    @pl.when(i == n_tiles)
    def _():
        copy = pltpu.make_async_copy(xs_last_hbm, xs_ref, sem)
        copy.start()
        copy.wait()


def _merge_prompt(x, u, att, gates, wp, ps, wup, wua, wo, nf, wr, br, xs_last, *, tm, seq_len, n_groups, n_per_group):
    n, d = x.shape
    d_pool, d_att = u.shape[1], att.shape[1]
    tile_rows = xs_last.shape[0]
    nt = n // tm
    clamp = lambda i: jnp.minimum(i, nt - 1)
    row = lambda i: (clamp(i), 0)
    const = lambda i: (0, 0)
    const3 = lambda i: (0, 0, 0)
    halo = lambda i: (jnp.maximum(clamp(i) * (tm // POOL_HALO) - 1, 0), 0)
    return pl.pallas_call(
        functools.partial(_merge_prompt_body, seq_len=seq_len, n_groups=n_groups, n_per_group=n_per_group),
        grid=(nt + 1,),
        in_specs=[pl.BlockSpec((tm, d), row), pl.BlockSpec((tm, d_pool), row), pl.BlockSpec((POOL_HALO, d_pool), halo),
                  pl.BlockSpec((tm, d_att), row), pl.BlockSpec((tm, 2 * d), row),
                  pl.BlockSpec(wp.shape, const3), pl.BlockSpec((1, d_pool), const),
                  pl.BlockSpec(wup.shape, const), pl.BlockSpec(wua.shape, const), pl.BlockSpec(wo.shape, const),
                  pl.BlockSpec((1, d), const), pl.BlockSpec(wr.shape, const), pl.BlockSpec((1, LANES), const),
                  pl.BlockSpec(memory_space=pl.ANY)],
        out_specs=[pl.BlockSpec((tm, d), row), pl.BlockSpec((tile_rows, d), lambda i: (i, 0)),
                   pl.BlockSpec((tm, LANES), row), pl.BlockSpec((1, 1, LANES), lambda i: (clamp(i), 0, 0))],
        out_shape=[jax.ShapeDtypeStruct((n, d), F32), jax.ShapeDtypeStruct(((nt + 1) * tile_rows, d), BF16),
                   jax.ShapeDtypeStruct((n, LANES), F32), jax.ShapeDtypeStruct((nt, 1, LANES), F32)],
        scratch_shapes=[pltpu.VMEM((tm + POOL_HALO, d_pool), F32), pltpu.SemaphoreType.DMA],
        compiler_params=_params("arbitrary"),
        name="merge_prompt",
    )(x, u, u, att, gates, wp, ps, wup, wua, wo, nf, wr, br, xs_last)


def _merge_sample_body(x_ref, u_ref, st_ref, att_ref, gate_ref, wp_ref, ps_ref, wup_ref, wua_ref, wo_ref,
                       nf_ref, wr_ref, br_ref, x2_ref, xs_ref, slab_ref, counts_ref,
                       *, start_pos, n_groups, n_per_group):
    u = u_ref[...]
    gw = u.shape[1] // len(POOL_WINDOWS)
    n_state = st_ref.shape[0]
    pooled = []
    for g, w in enumerate(POOL_WINDOWS):
        lo = g * gw
        wsum = u[:, lo:lo + gw]
        for j in range(1, w):
            wsum = wsum + st_ref[n_state - j][:, lo:lo + gw]
        pooled.append(wsum / float(min(start_pos + 1, w)) - u[:, lo:lo + gw])
    x2, h2, slab, counts = _merge_and_route(
        x_ref[...], pooled, att_ref[...], gate_ref[...], wp_ref, ps_ref, wup_ref, wua_ref, wo_ref, nf_ref,
        wr_ref, br_ref, precise=True, n_groups=n_groups, n_per_group=n_per_group)
    n = slab.shape[0]
    fields = jnp.concatenate([slab, jnp.zeros((LANES - n, LANES), F32)], axis=0).T
    x2_ref[...] = x2
    xs_ref[...] = _sorted_copy(h2, fields[6:7, 0:n], fields[7:8, 0:n], xs_ref.shape[0])
    slab_ref[...] = slab
    counts_ref[0] = counts


def _merge_sample(x, u, state_t, att, gates, wp, ps, wup, wua, wo, nf, wr, br, *, start_pos, n_groups,
                  n_per_group, tile_rows):
    n, d = x.shape
    assert n <= LANES
    return pl.pallas_call(
        functools.partial(_merge_sample_body, start_pos=start_pos, n_groups=n_groups, n_per_group=n_per_group),
        out_shape=[jax.ShapeDtypeStruct((n, d), F32), jax.ShapeDtypeStruct((tile_rows, d), BF16),
                   jax.ShapeDtypeStruct((n, LANES), F32), jax.ShapeDtypeStruct((1, 1, LANES), F32)],
        compiler_params=_params(),
        name="merge_sample",
    )(x, u, state_t, att, gates, wp, ps, wup, wua, wo, nf, wr, br)


def _moe_plan_body(cnt_ref, gsrc_ref, lsrc_ref, te_ref, tw_ref, loc_ref, *, n_ttiles, n_exp, tile_rows, gpt,
                   seg_groups_max, local_stride, zero_group):
    n_mm = te_ref.shape[0]
    gsrc_ref[...] = jnp.full(gsrc_ref.shape, zero_group, I32)
    lsrc_ref[...] = jnp.zeros(lsrc_ref.shape, I32)

    def fill_tiles(t, c):
        te_ref[t] = -1
        tw_ref[t] = n_exp - 1
        return c

    def fill_loc(i, c):
        loc_ref[i] = 0
        return c

    lax.fori_loop(0, n_mm, fill_tiles, 0)
    lax.fori_loop(0, n_ttiles, fill_loc, 0)
    step = lax.broadcasted_iota(I32, (seg_groups_max, LANES), 0)

    def per_expert(e, pos):
        def per_tile(i, p):
            g = (cnt_ref[i * n_exp + e] + (ROW_GROUP - 1)) // ROW_GROUP
            loc = loc_ref[i]
            gsrc_ref[pl.ds(p, seg_groups_max), :] = i * tile_rows + (loc + step) * ROW_GROUP
            lsrc_ref[pl.ds(i * local_stride + loc, seg_groups_max), :] = p + step
            loc_ref[i] = loc + g
            return p + g

        end = lax.fori_loop(0, n_ttiles, per_tile, pos)
        end_pad = ((end + (gpt - 1)) // gpt) * gpt
        gsrc_ref[pl.ds(end, seg_groups_max), :] = jnp.full((seg_groups_max, LANES), zero_group, I32)

        def mark(t, c):
            te_ref[t] = e
            tw_ref[t] = e
            return c

        lax.fori_loop(pos // gpt, end_pad // gpt, mark, 0)
        return end_pad

    lax.fori_loop(0, n_exp, per_expert, 0)

    def clear_tail(i, c):
        lsrc_ref[pl.ds(i * local_stride + loc_ref[i], seg_groups_max), :] = jnp.zeros((seg_groups_max, LANES), I32)
        return c

    lax.fori_loop(0, n_ttiles, clear_tail, 0)


def _moe_plan(cnt, *, n_ttiles, n_exp, tile_rows, gpt, n_mm_tiles, seg_groups_max):
    smem = pl.BlockSpec(memory_space=pltpu.SMEM)
    zero_group = n_ttiles * tile_rows - ROW_GROUP
    local_groups = tile_rows // ROW_GROUP
    local_stride = local_groups + seg_groups_max
    n_groups = n_mm_tiles * gpt
    gsrc, lsrc, te, tw = pl.pallas_call(
        functools.partial(_moe_plan_body, n_ttiles=n_ttiles, n_exp=n_exp, tile_rows=tile_rows, gpt=gpt,
                          seg_groups_max=seg_groups_max, local_stride=local_stride, zero_group=zero_group),
        in_specs=[smem], out_specs=[pl.BlockSpec(memory_space=pltpu.VMEM), pl.BlockSpec(memory_space=pltpu.VMEM),
                                    smem, smem],
        out_shape=[jax.ShapeDtypeStruct((n_groups + 2 * seg_groups_max, LANES), I32),
                   jax.ShapeDtypeStruct((n_ttiles * local_stride, LANES), I32),
                   jax.ShapeDtypeStruct((n_mm_tiles,), I32), jax.ShapeDtypeStruct((n_mm_tiles,), I32)],
        scratch_shapes=[pltpu.SMEM((n_ttiles,), I32)],
        name="moe_plan",
    )(cnt)
    return gsrc[:n_groups, 0], lsrc[:, 0], te, tw, local_stride


def _moe_mm_body(te_ref, tw_ref, src_ref, xs_hbm, wg_ref, wu_ref, wd_ref, ys_ref, xbuf, wgb_ref, wub_ref, wdb_ref,
                 sems):
    i = pl.program_id(0)
    tm = ys_ref.shape[0]
    groups = tm // ROW_GROUP
    expert = te_ref[i]
    prev = te_ref[jnp.maximum(i - 1, 0)]

    def fetch(slot, tile):
        for k in range(groups):
            src = pl.multiple_of(src_ref[tile * groups + k], ROW_GROUP)
            pltpu.make_async_copy(xs_hbm.at[pl.ds(src, ROW_GROUP)],
                                  xbuf.at[slot, pl.ds(k * ROW_GROUP, ROW_GROUP)], sems.at[slot]).start()

    slot = i % 2

    @pl.when(i == 0)
    def _():
        fetch(0, 0)

    @pl.when(i + 1 < pl.num_programs(0))
    def _():
        fetch(1 - slot, i + 1)

    pltpu.make_async_copy(xs_hbm.at[pl.ds(0, tm)], xbuf.at[slot], sems.at[slot]).wait()

    @pl.when((expert >= 0) & ((i == 0) | (expert != prev)))
    def _():
        wgb_ref[...] = wg_ref[0].astype(BF16)
        wub_ref[...] = wu_ref[0].astype(BF16)
        wdb_ref[...] = wd_ref[0].astype(BF16)

    @pl.when(expert >= 0)
    def _():
        x = xbuf[slot]
        a = jnp.dot(x, wgb_ref[...], preferred_element_type=F32)
        b = jnp.dot(x, wub_ref[...], preferred_element_type=F32)
        hdn = (a * _sigmoid(a)) * b
        ys_ref[...] = jnp.dot(hdn.astype(BF16), wdb_ref[...], preferred_element_type=F32).astype(ys_ref.dtype)

    @pl.when(expert < 0)
    def _():
        ys_ref[...] = jnp.zeros_like(ys_ref)


def _moe_mm(tile_expert, tile_weight, group_src, xs, w_gate, w_up, w_down, *, tm):
    n_tiles = tile_expert.shape[0]
    d = xs.shape[1]
    n_exp, _, de = w_gate.shape
    wmap = lambda i, te, tw, src: (tw[i], 0, 0)
    grid_spec = pltpu.PrefetchScalarGridSpec(
        num_scalar_prefetch=3, grid=(n_tiles,),
        in_specs=[pl.BlockSpec(memory_space=pl.ANY),
                  pl.BlockSpec((1, d, de), wmap), pl.BlockSpec((1, d, de), wmap), pl.BlockSpec((1, de, d), wmap)],
        out_specs=pl.BlockSpec((tm, d), lambda i, te, tw, src: (i, 0)),
        scratch_shapes=[pltpu.VMEM((2, tm, d), BF16), pltpu.VMEM((d, de), BF16), pltpu.VMEM((d, de), BF16),
                        pltpu.VMEM((de, d), BF16), pltpu.SemaphoreType.DMA((2,))])
    return pl.pallas_call(
        _moe_mm_body, grid_spec=grid_spec,
        out_shape=jax.ShapeDtypeStruct((n_tiles * tm, d), BF16),
        compiler_params=_params("arbitrary"),
        name="moe_mm",
    )(tile_expert, tile_weight, group_src, xs, w_gate, w_up, w_down)


def _moe_combine_body(lsrc_ref, x_ref, slab_ref, g_ref, ys_hbm, o_ref, ybuf, sems,
                      *, final_norm, first_tile, local_stride):
    i = pl.program_id(0)
    tm = x_ref.shape[0]
    tile_rows = ybuf.shape[1]
    unroll = 8

    def fetch(slot, tile):
        def body(c, carry):
            for j in range(unroll):
                lg = c * unroll + j
                src = pl.multiple_of(lsrc_ref[tile * local_stride + lg] * ROW_GROUP, ROW_GROUP)
                dst = pl.multiple_of(lg * ROW_GROUP, ROW_GROUP)
                pltpu.make_async_copy(ys_hbm.at[pl.ds(src, ROW_GROUP)], ybuf.at[slot, pl.ds(dst, ROW_GROUP)],
                                      sems.at[slot]).start()
            return carry

        lax.fori_loop(0, tile_rows // ROW_GROUP // unroll, body, 0)

    slot = i % 2

    @pl.when(i == 0)
    def _():
        fetch(0, first_tile)

    @pl.when(i + 1 < pl.num_programs(0))
    def _():
        fetch(1 - slot, first_tile + i + 1)

    pltpu.make_async_copy(ys_hbm.at[pl.ds(0, tile_rows)], ybuf.at[slot], sems.at[slot]).wait()
    y = ybuf[slot]
    slab = slab_ref[...]
    r = lax.broadcasted_iota(I32, (tm, tile_rows), 1).astype(F32)
    ya = jnp.dot((r == slab[:, 6:7]).astype(BF16), y, preferred_element_type=F32)
    yb = jnp.dot((r == slab[:, 7:8]).astype(BF16), y, preferred_element_type=F32)
    out = x_ref[...] + (slab[:, 4:5] * ya + slab[:, 5:6] * yb)
    if final_norm:
        out = _rmsnorm(out, g_ref[...])
    o_ref[...] = out


def _moe_combine(local_src, x, slab, g, ys, *, ts, tile_rows, final_norm, first_tile, local_stride):
    n, d = x.shape
    return pl.pallas_call(
        functools.partial(_moe_combine_body, final_norm=final_norm, first_tile=first_tile,
                          local_stride=local_stride),
        grid=(n // ts,),
        in_specs=[pl.BlockSpec(memory_space=pltpu.SMEM),
                  pl.BlockSpec((ts, d), lambda i: (i, 0)), pl.BlockSpec((ts, LANES), lambda i: (i, 0)),
                  pl.BlockSpec((1, d), lambda i: (0, 0)), pl.BlockSpec(memory_space=pl.ANY)],
        out_specs=pl.BlockSpec((ts, d), lambda i: (i, 0)),
        out_shape=jax.ShapeDtypeStruct((n, d), F32),
        scratch_shapes=[pltpu.VMEM((2, tile_rows, d), BF16), pltpu.SemaphoreType.DMA((2,))],
        compiler_params=_params("arbitrary"),
        name="moe_combine",
    )(local_src, x, slab, g, ys)


def kernel(x_prompt, x_sample, cache_k, cache_v, cache_logf, state_pool, page_table, norm_mix, w_in, b_forget,
           w_pool, pool_scale, w_up_pool, w_up_att, w_out, norm_ffn, w_router_group, b_router_group,
           w_router_expert, b_router_expert, w_gate, w_up, w_down, norm_final):
    depth = norm_mix.shape[0]
    assert depth == 1, "single trunk layer"
    b, t, d = x_prompt.shape
    db, dt, _ = x_sample.shape
    assert dt == 1, "one sample token per sequence"
    _, n_phys, page, n_heads, dh = cache_k.shape
    n_pages = page_table.shape[1]
    past = n_pages * page
    n_state, d_pool = state_pool.shape[2], state_pool.shape[3]
    d_att = n_heads * dh
    n_pool_groups = w_pool.shape[1]
    assert n_pool_groups == len(POOL_WINDOWS) and d_pool // n_pool_groups == LANES
    assert n_state == max(POOL_WINDOWS) - 1 and n_state < POOL_HALO
    n_groups, n_per_group = w_router_expert.shape[1], w_router_expert.shape[3]
    n_exp = n_groups * n_per_group
    assert n_groups + n_exp <= LANES and 2 * dh == LANES and n_heads % 2 == 0
    n = b * t
    q_scale = float(dh) ** -0.5
    tm = min(TOKEN_TILE, t)
    assert t % tm == 0 and t % ATTN_TILE == 0

    o_main = d_pool + 3 * d_att
    wi = w_in[0]
    wm_f, wf_f, wg_f = wi[:, :o_main], wi[:, o_main:o_main + n_heads], wi[:, o_main + n_heads:]
    wf_pad = jnp.pad(wf_f, ((0, 0), (0, LANES - n_heads)))
    wit = jnp.transpose(wi)
    wmt_f, wgt_f = wit[:o_main], wit[o_main + n_heads:]
    wft_pad = jnp.pad(wit[o_main:o_main + n_heads], ((0, LANES - n_heads), (0, 0)))
    bf_pad = jnp.pad(b_forget[0], (0, LANES - n_heads)).reshape(1, LANES)
    g_mix = norm_mix[0].reshape(1, d)
    g_ffn = norm_ffn[0].reshape(1, d)
    g_fin = norm_final.reshape(1, d)
    ps = pool_scale[0].reshape(1, d_pool)
    wr_f = jnp.concatenate([w_router_group[0], jnp.transpose(w_router_expert[0], (1, 0, 2)).reshape(d, n_exp)], axis=1)
    wr_pad = jnp.pad(wr_f, ((0, 0), (0, LANES - n_groups - n_exp)))
    br_pad = jnp.pad(jnp.concatenate([b_router_group[0], b_router_expert[0].reshape(n_exp)]),
                     (0, LANES - n_groups - n_exp)).reshape(1, LANES)
    bf = lambda a: a.astype(BF16)

    xp = x_prompt.reshape(n, d)
    u_p, q_p, kt_p, vt_p, kb_p, vb_p, lft_p, gate_p = _proj_prompt(
        xp, g_mix, bf(wm_f), bf(wf_pad), bf(wg_f), bf_pad, tm=tm, seq_len=t, d_pool=d_pool, d_att=d_att,
        n_heads=n_heads, q_scale=q_scale)
    c = _cumsum_lanes(lft_p.reshape(b * n_heads, t))
    nt = t // ATTN_TILE
    c_blk = jnp.transpose(c.reshape(b, n_heads // 2, 2, nt, ATTN_TILE), (0, 1, 3, 2, 4))
    n_ptiles = n // tm
    n_ttiles = n_ptiles + 1
    tile_rows = -(-(2 * tm + n_exp * (ROW_GROUP - 1)) // MOE_ROW_TILE) * MOE_ROW_TILE
    assert 2 * db + n_exp * (ROW_GROUP - 1) <= tile_rows - ROW_GROUP, "the sample tile must end in an unused row group"

    xs = x_sample.reshape(db, d)
    z_s, lf_s, gate_s = _proj_sample(xs, g_mix, wmt_f, wft_pad, wgt_f, bf_pad, tn=512)
    u_s = z_s[:, :d_pool]
    q_s = z_s[:, d_pool:d_pool + d_att] * q_scale
    k_s = z_s[:, d_pool + d_att:d_pool + 2 * d_att]
    v_s = z_s[:, d_pool + 2 * d_att:]
    att_p, att_s = _attention(q_p.reshape(b, t, d_att), kb_p.reshape(b, t, d_att), vb_p.reshape(b, t, d_att), c_blk,
                              page_table, q_s, k_s, v_s, lf_s[:, :n_heads],
                              jnp.transpose(cache_k[0], (0, 2, 3, 1)), jnp.transpose(cache_v[0], (0, 2, 3, 1)),
                              jnp.transpose(cache_logf[0], (0, 2, 1)), tile=ATTN_TILE, dh=dh)
    state_t = jnp.transpose(state_pool[0], (1, 0, 2))
    x2_s, xs_rows_s, slab_s, counts_s = _merge_sample(
        xs, u_s, state_t, att_s.reshape(db, d_att), gate_s, w_pool[0], ps, w_up_pool[0], w_up_att[0], w_out[0],
        g_ffn, wr_pad, br_pad, start_pos=past, n_groups=n_groups, n_per_group=n_per_group, tile_rows=tile_rows)

    x2_p, xs_rows, slab_p, counts_p = _merge_prompt(
        xp, u_p, att_p.reshape(n, d_att), gate_p, bf(w_pool[0]), ps, bf(w_up_pool[0]), bf(w_up_att[0]),
        bf(w_out[0]), g_ffn, bf(wr_pad), br_pad, xs_rows_s, tm=tm, seq_len=t, n_groups=n_groups,
        n_per_group=n_per_group)

    tmm = MOE_ROW_TILE
    gpt = tmm // ROW_GROUP
    n_groups_max = -(-(2 * (n + db)) // ROW_GROUP) + n_ttiles * n_exp + n_exp * (gpt - 1)
    cnt = jnp.concatenate([counts_p, counts_s], axis=0)[:, 0, :n_exp].astype(I32).reshape(-1)
    group_src, local_src, tile_expert, tile_weight, local_stride = _moe_plan(
        cnt, n_ttiles=n_ttiles, n_exp=n_exp, tile_rows=tile_rows, gpt=gpt, n_mm_tiles=-(-n_groups_max // gpt),
        seg_groups_max=tm // ROW_GROUP)
    ys_rows = _moe_mm(tile_expert, tile_weight, group_src, xs_rows, w_gate[0], w_up[0], w_down[0], tm=tmm)
    y_prompt = _moe_combine(local_src, x2_p, slab_p, g_fin, ys_rows, ts=tm, tile_rows=tile_rows,
                            final_norm=True, first_tile=0, local_stride=local_stride)
    y_sample = _moe_combine(local_src, x2_s, slab_s, g_fin, ys_rows, ts=db, tile_rows=tile_rows,
                            final_norm=True, first_tile=n_ptiles, local_stride=local_stride)

    new_pool_p = u_p.reshape(b, t, d_pool)[:, t - n_state:, :]
    new_pool_s = jnp.concatenate([state_pool[0][:, 1:, :], u_s[:, None, :]], axis=1)
    to_heads = lambda a: jnp.transpose(a.reshape(b, n_heads, dh, t), (0, 3, 1, 2))[None]
    return (y_prompt.reshape(b, t, d), y_sample.reshape(db, 1, d),
            to_heads(kt_p), to_heads(vt_p), jnp.transpose(lft_p, (0, 2, 1))[None],
            new_pool_p[None],
            k_s.reshape(1, db, 1, n_heads, dh), v_s.reshape(1, db, 1, n_heads, dh),
            lf_s[:, :n_heads].reshape(1, db, 1, n_heads), new_pool_s[None])
```

```python
import functools

import jax
import jax.numpy as jnp
from jax import lax
from jax.experimental import pallas as pl
from jax.experimental.pallas import tpu as pltpu

F32 = jnp.float32
BF16 = jnp.bfloat16
I32 = jnp.int32
HIGHEST = lax.Precision.HIGHEST

RMS_EPS = 1e-6
POOL_WINDOWS = (2, 4, 8, 16)
POOL_HALO = 16
LANES = 128
VMEM_LIMIT_BYTES = 56 * 1024 * 1024

TOKEN_TILE = 512
ATTN_TILE = 512
MOE_ROW_TILE = 512
PAGES_PER_STEP = 16
ROW_GROUP = 16


def _params(*sem):
    return pltpu.CompilerParams(dimension_semantics=sem, vmem_limit_bytes=VMEM_LIMIT_BYTES)


def _rmsnorm(x, g):
    return x * lax.rsqrt(jnp.mean(x * x, axis=-1, keepdims=True) + RMS_EPS) * g


def _log_sigmoid(x):
    return jnp.minimum(x, 0.0) - jnp.log1p(jnp.exp(-jnp.abs(x)))


def _sigmoid(x):
    return 1.0 / (1.0 + jnp.exp(-x))


def _dot(a, b, precise):
    if precise:
        return jnp.dot(a.astype(F32), b.astype(F32), precision=HIGHEST, preferred_element_type=F32)
    return jnp.dot(a.astype(BF16), b.astype(BF16), preferred_element_type=F32)


def _split3(x):
    hi = x.astype(BF16)
    r = x - hi.astype(F32)
    mid = r.astype(BF16)
    lo = (r - mid.astype(F32)).astype(BF16)
    return hi, mid, lo


def _dot_exact_rhs(x, w_bf16):
    hi, mid, lo = _split3(x)
    d = lambda a: jnp.dot(a, w_bf16, preferred_element_type=F32)
    return d(hi) + d(mid) + d(lo)


def _proj_body(x_ref, g_ref, wm_ref, wf_ref, wg_ref, bf_ref,
               u_ref, q_ref, kt_ref, vt_ref, kb_ref, vb_ref, lft_ref, gate_ref, *, d_pool, d_att, n_heads, q_scale):
    h = _rmsnorm(x_ref[...], g_ref[...]).astype(BF16)
    z = jnp.dot(h, wm_ref[...], preferred_element_type=F32)
    o1, o2, o3 = d_pool, d_pool + d_att, d_pool + 2 * d_att
    u_ref[...] = z[:, :o1]
    q_ref[...] = (z[:, o1:o2] * q_scale).astype(BF16)
    k = z[:, o2:o3]
    v = z[:, o3:]
    kt_ref[0] = k.T
    vt_ref[0] = v.T
    kb_ref[...] = k.astype(BF16)
    vb_ref[...] = v.astype(BF16)
    lf = _log_sigmoid(jnp.dot(h, wf_ref[...], preferred_element_type=F32) + bf_ref[...])
    lft_ref[0] = lf.T[0:n_heads, :]
    gate_ref[...] = _sigmoid(jnp.dot(h, wg_ref[...], preferred_element_type=F32)).astype(BF16)


def _proj_prompt(x, g, wm, wf, wg, bfp, *, tm, seq_len, d_pool, d_att, n_heads, q_scale):
    n, d = x.shape
    b = n // seq_len
    tps = seq_len // tm
    row = lambda i: (i, 0)
    const = lambda i: (0, 0)
    tmin = lambda i: (i // tps, 0, i % tps)
    dg = wg.shape[1]
    out_shape = [
        jax.ShapeDtypeStruct((n, d_pool), F32), jax.ShapeDtypeStruct((n, d_att), BF16),
        jax.ShapeDtypeStruct((b, d_att, seq_len), F32), jax.ShapeDtypeStruct((b, d_att, seq_len), F32),
        jax.ShapeDtypeStruct((n, d_att), BF16), jax.ShapeDtypeStruct((n, d_att), BF16),
        jax.ShapeDtypeStruct((b, n_heads, seq_len), F32), jax.ShapeDtypeStruct((n, dg), BF16),
    ]
    return pl.pallas_call(
        functools.partial(_proj_body, d_pool=d_pool, d_att=d_att, n_heads=n_heads, q_scale=q_scale),
        grid=(n // tm,),
        in_specs=[pl.BlockSpec((tm, d), row), pl.BlockSpec((1, d), const),
                  pl.BlockSpec(wm.shape, const), pl.BlockSpec(wf.shape, const),
                  pl.BlockSpec(wg.shape, const), pl.BlockSpec((1, LANES), const)],
        out_specs=[pl.BlockSpec((tm, d_pool), row), pl.BlockSpec((tm, d_att), row),
                   pl.BlockSpec((1, d_att, tm), tmin), pl.BlockSpec((1, d_att, tm), tmin),
                   pl.BlockSpec((tm, d_att), row), pl.BlockSpec((tm, d_att), row),
                   pl.BlockSpec((1, n_heads, tm), tmin), pl.BlockSpec((tm, dg), row)],
        out_shape=out_shape,
        compiler_params=_params("arbitrary"),
        name="proj_prompt",
    )(x, g, wm, wf, wg, bfp)


def _dot_nt(a, bt, precise):
    dims = (((1,), (1,)), ((), ()))
    if precise:
        return lax.dot_general(a.astype(F32), bt.astype(F32), dims, precision=HIGHEST, preferred_element_type=F32)
    return lax.dot_general(a.astype(BF16), bt.astype(BF16), dims, preferred_element_type=F32)


def _proj_sample_body(x_ref, g_ref, wmt_ref, wft_ref, wgt_ref, bf_ref, z_ref, lf_ref, gate_ref):
    h = _rmsnorm(x_ref[...], g_ref[...])
    z_ref[...] = _dot_nt(h, wmt_ref[...], True)
    lf_ref[...] = _log_sigmoid(_dot_nt(h, wft_ref[...], True) + bf_ref[...])
    gate_ref[...] = _sigmoid(_dot_nt(h, wgt_ref[...], True))


def _proj_sample(x, g, wmt, wft, wgt, bfp, *, tn):
    n, d = x.shape
    dm, dg = wmt.shape[0], wgt.shape[0]
    assert dm == dg
    const = lambda j: (0, 0)
    chunk = lambda j: (j, 0)
    col = lambda j: (0, j)
    return pl.pallas_call(
        _proj_sample_body,
        grid=(dm // tn,),
        in_specs=[pl.BlockSpec((n, d), const), pl.BlockSpec((1, d), const),
                  pl.BlockSpec((tn, d), chunk), pl.BlockSpec(wft.shape, const),
                  pl.BlockSpec((tn, d), chunk), pl.BlockSpec((1, LANES), const)],
        out_specs=[pl.BlockSpec((n, tn), col), pl.BlockSpec((n, LANES), const), pl.BlockSpec((n, tn), col)],
        out_shape=[jax.ShapeDtypeStruct((n, dm), F32), jax.ShapeDtypeStruct((n, LANES), F32),
                   jax.ShapeDtypeStruct((n, dg), F32)],
        compiler_params=_params("arbitrary"),
        name="proj_sample",
    )(x, g, wmt, wft, wgt, bfp)


def _cumsum_body(x_ref, o_ref):
    c = x_ref[...]
    lane = lax.broadcasted_iota(I32, c.shape, 1)
    s = 1
    while s < c.shape[1]:
        c = c + jnp.where(lane >= s, pltpu.roll(c, s, 1), 0.0)
        s *= 2
    o_ref[...] = c


def _cumsum_lanes(x):
    return pl.pallas_call(_cumsum_body, out_shape=jax.ShapeDtypeStruct(x.shape, F32),
                          compiler_params=_params(), name="cumsum_logf")(x)


def _prompt_q_tile(qi, q_ref, kts, vhs, c_ref, o_ref, *, tile, dh, first, causal):
    q = q_ref[0, qi * tile:(qi + 1) * tile, :]
    zero = jnp.zeros_like(q)
    q_heads = (jnp.where(first, q, zero), jnp.where(first, zero, q))
    res = []
    for h in range(2):
        m = jnp.full((tile, 1), -1e30, F32)
        acc = jnp.zeros((tile, 2 * dh), F32)
        for kj in range(qi + 1):
            s = lax.dot_general(q_heads[h], kts[kj], (((1,), (1,)), ((), ())), preferred_element_type=F32)
            s = s - c_ref[0, 0, kj][h:h + 1, :]
            if kj == qi:
                s = jnp.where(causal, s, -jnp.inf)
            m_new = jnp.maximum(m, jnp.max(s, axis=-1, keepdims=True))
            alpha = jnp.exp(m - m_new)
            p = jnp.exp(s - m_new)
            acc = alpha * acc + jnp.dot(p.astype(BF16), vhs[kj][h], preferred_element_type=F32)
            m = m_new
        res.append(acc)
    a0, a1 = res
    out = jnp.where(first, a0 / a0[:, dh:dh + 1], a1 / a1[:, 0:1])
    o_ref[0, qi * tile:(qi + 1) * tile, :] = out.astype(o_ref.dtype)


def _sample_chunk(k_refs, v_refs, lf_refs, qrep, carry):
    m_prev, l, acc, s_run = carry
    n_heads, dh, page = acc.shape
    d_att = n_heads * dh
    g_n = len(k_refs)
    r = lax.broadcasted_iota(I32, (page, page), 0)
    c = lax.broadcasted_iota(I32, (page, page), 1)
    later = (r > c).astype(BF16)
    ones = jnp.ones((page, page), BF16)
    lf_all = jnp.concatenate([lf_refs[g][...] for g in range(g_n)], axis=0)
    suffix = _dot_exact_rhs(lf_all, later)
    total = _dot_exact_rhs(lf_all, ones)
    m_new = m_prev
    scores = []
    for g in range(g_n):
        kq = k_refs[g][...].reshape(d_att, page) * qrep
        s = jnp.sum(kq.reshape(n_heads, dh, page), axis=1)
        sb = s + s_run + suffix[g * n_heads:(g + 1) * n_heads]
        s_run = s_run + total[g * n_heads:(g + 1) * n_heads]
        scores.append(sb)
        m_new = jnp.maximum(m_new, jnp.max(sb, axis=-1, keepdims=True))
    alpha = jnp.exp(m_prev - m_new)
    l = alpha * l
    acc = acc * alpha[:, None, :]
    for g in range(g_n):
        p = jnp.exp(scores[g] - m_new)
        l = l + jnp.sum(p, axis=-1, keepdims=True)
        acc = acc + v_refs[g][...] * p[:, None, :]
    return m_new, l, acc, s_run


def _attn_body(pt_ref, q_ref, k_ref, v_ref, c_ref, qrep_ref, qs_ref, kn_ref, vrep_ref, lfn_ref,
               ck_hbm, cv_hbm, clf_hbm, o_ref, os_ref, kbuf, vbuf, lfbuf, sems, *, tile, dh):
    step = pl.program_id(0) * pl.num_programs(1) + pl.program_id(1)
    n_steps = pl.num_programs(0) * pl.num_programs(1)
    nt = q_ref.shape[1] // tile
    n_pages = pt_ref.shape[1]
    _, g_n, n_heads, _, page = kbuf.shape
    d_att = n_heads * dh
    seqs = qrep_ref.shape[0]
    n_chunks = n_pages // g_n
    n_items = seqs * n_chunks

    def fetch(slot, seq, chunk):
        for g in range(g_n):
            pid = pt_ref[seq, n_pages - 1 - (chunk * g_n + g)]
            pltpu.make_async_copy(ck_hbm.at[pid], kbuf.at[slot, g], sems.at[slot, 0]).start()
            pltpu.make_async_copy(cv_hbm.at[pid], vbuf.at[slot, g], sems.at[slot, 1]).start()
            pltpu.make_async_copy(clf_hbm.at[pid], lfbuf.at[slot, g], sems.at[slot, 2]).start()

    @pl.when(step == 0)
    def _():
        fetch(0, 0, 0)

    lane = lax.broadcasted_iota(I32, (tile, 2 * dh), 1)
    first = lane < dh
    row = lax.broadcasted_iota(I32, (tile, tile), 0)
    col = lax.broadcasted_iota(I32, (tile, tile), 1)
    causal = col <= row
    one = jnp.ones((tile, 2 * dh), BF16)
    kts, vhs = [], []
    for kj in range(nt):
        vt = v_ref[0, kj * tile:(kj + 1) * tile, :]
        kts.append(k_ref[0, kj * tile:(kj + 1) * tile, :])
        vhs.append((jnp.where(first, vt, one), jnp.where(first, one, vt)))

    q_done = 0
    carry = None
    for item in range(n_items):
        j, chunk = divmod(item, n_chunks)
        slot = item % 2
        if item + 1 < n_items:
            fetch(1 - slot, step * seqs + (item + 1) // n_chunks, (item + 1) % n_chunks)
        else:
            @pl.when(step + 1 < n_steps)
            def _():
                fetch(1 - slot, (step + 1) * seqs, 0)
        pltpu.make_async_copy(ck_hbm.at[pl.ds(0, g_n)], kbuf.at[slot], sems.at[slot, 0]).wait()
        pltpu.make_async_copy(cv_hbm.at[pl.ds(0, g_n)], vbuf.at[slot], sems.at[slot, 1]).wait()
        pltpu.make_async_copy(clf_hbm.at[pl.ds(0, g_n)], lfbuf.at[slot], sems.at[slot, 2]).wait()
        if chunk == 0:
            s_new = jnp.sum(qs_ref[j] * kn_ref[j], axis=-1, keepdims=True)
            lane_p = lax.broadcasted_iota(I32, (d_att, page), 1)
            carry = (jnp.broadcast_to(s_new, (n_heads, page)), jnp.ones((n_heads, page), F32),
                     jnp.where(lane_p == 0, vrep_ref[j], 0.0).reshape(n_heads, dh, page), lfn_ref[j])
        carry = _sample_chunk([kbuf.at[slot, g] for g in range(g_n)], [vbuf.at[slot, g] for g in range(g_n)],
                              [lfbuf.at[slot, g] for g in range(g_n)], qrep_ref[j], carry)
        if chunk == n_chunks - 1:
            _, l, acc, _ = carry
            os_ref[j] = jnp.sum(acc / l[:, None, :], axis=-1)
        q_until = ((item + 1) * nt) // n_items
        for qi in range(q_done, q_until):
            _prompt_q_tile(qi, q_ref, kts, vhs, c_ref, o_ref, tile=tile, dh=dh, first=first, causal=causal)
        q_done = q_until


def _attention(q, k, v, c, page_table, q_s, k_new, v_new, lf_new, cache_kt, cache_vt, cache_lft, *, tile, dh):
    b, t, da = q.shape
    hp = da // (2 * dh)
    nt = t // tile
    db, n_pages = page_table.shape
    n_phys, n_heads, _, page = cache_kt.shape
    d_att = n_heads * dh
    n_steps = b * hp
    assert db % n_steps == 0, "sample sequences are split evenly over the prompt grid steps"
    seqs = db // n_steps
    g_n = PAGES_PER_STEP
    while n_pages % g_n:
        g_n //= 2
    assert (seqs * (n_pages // g_n)) % 2 == 0, "buffer slots alternate per page chunk"
    lane_rep = lambda a: jnp.broadcast_to(a.reshape(db, -1, 1), (db, a.size // db, page))
    pair = pl.BlockSpec((1, t, 2 * dh), lambda bi, hi, pt: (bi, 0, hi))
    per_step = lambda bi, hi, pt: (bi * hp + hi, 0, 0)
    hbm = pl.BlockSpec(memory_space=pl.ANY)
    grid_spec = pltpu.PrefetchScalarGridSpec(
        num_scalar_prefetch=1, grid=(b, hp),
        in_specs=[pair, pair, pair, pl.BlockSpec((1, 1, nt, 2, tile), lambda bi, hi, pt: (bi, hi, 0, 0, 0)),
                  pl.BlockSpec((seqs, d_att, page), per_step), pl.BlockSpec((seqs, n_heads, dh), per_step),
                  pl.BlockSpec((seqs, n_heads, dh), per_step), pl.BlockSpec((seqs, d_att, page), per_step),
                  pl.BlockSpec((seqs, n_heads, page), per_step), hbm, hbm, hbm],
        out_specs=[pair, pl.BlockSpec((seqs, n_heads, dh), per_step)],
        scratch_shapes=[pltpu.VMEM((2, g_n, n_heads, dh, page), F32), pltpu.VMEM((2, g_n, n_heads, dh, page), F32),
                        pltpu.VMEM((2, g_n, n_heads, page), F32), pltpu.SemaphoreType.DMA((2, 3))])
    return pl.pallas_call(
        functools.partial(_attn_body, tile=tile, dh=dh),
        grid_spec=grid_spec,
        out_shape=[jax.ShapeDtypeStruct((b, t, da), BF16), jax.ShapeDtypeStruct((db, n_heads, dh), F32)],
        compiler_params=_params("arbitrary", "arbitrary"),
        name="attention",
    )(page_table, q, k, v, c, lane_rep(q_s), q_s.reshape(db, n_heads, dh), k_new.reshape(db, n_heads, dh),
      lane_rep(v_new), lane_rep(lf_new), cache_kt, cache_vt, cache_lft)


def _merge_and_route(x, pooled, att, gates, wp_ref, ps_ref, wup_ref, wua_ref, wo_ref, nf_ref, wr_ref, br_ref,
                     *, precise, n_groups, n_per_group):
    tm, d = x.shape
    gw = pooled[0].shape[1]
    mixed = jnp.concatenate([_dot(pooled[g], wp_ref[g], precise) for g in range(len(pooled))], axis=-1)
    pool_out = mixed * ps_ref[...]
    y = gates[:, :d].astype(F32) * _dot(pool_out, wup_ref[...], precise) \
        + gates[:, d:].astype(F32) * _dot(att, wua_ref[...], precise)
    x2 = x + _dot(y, wo_ref[...], precise)
    h2 = _rmsnorm(x2, nf_ref[...])
    logits = _dot(h2, wr_ref[...], precise) + br_ref[...]
    lane = lax.broadcasted_iota(I32, logits.shape, 1)
    lanef = lane.astype(F32)
    neg = -jnp.inf
    is_g = lane < n_groups
    gmax = jnp.max(jnp.where(is_g, logits, neg), axis=-1, keepdims=True)
    gidx = jnp.min(jnp.where(is_g & (logits == gmax), lanef, float(LANES)), axis=-1, keepdims=True)
    gsum = jnp.sum(jnp.where(is_g, jnp.exp(logits - gmax), 0.0), axis=-1, keepdims=True)
    g_w = 1.0 / gsum
    n_exp = n_groups * n_per_group
    exp_id = lanef - float(n_groups)
    in_sel = (lane >= n_groups) & (lane < n_groups + n_exp) & (jnp.floor(exp_id / n_per_group) == gidx)
    v1 = jnp.max(jnp.where(in_sel, logits, neg), axis=-1, keepdims=True)
    i1 = jnp.min(jnp.where(in_sel & (logits == v1), lanef, float(LANES)), axis=-1, keepdims=True)
    in_sel2 = in_sel & (lanef != i1)
    v2 = jnp.max(jnp.where(in_sel2, logits, neg), axis=-1, keepdims=True)
    i2 = jnp.min(jnp.where(in_sel2 & (logits == v2), lanef, float(LANES)), axis=-1, keepdims=True)
    t = jnp.exp(v2 - v1)
    w1 = g_w * (1.0 / (1.0 + t))
    w2 = g_w * (t / (1.0 + t))
    e1 = i1 - float(n_groups)
    e2 = i2 - float(n_groups)
    hit1 = lanef == e1
    hit2 = lanef == e2
    onehot = (hit1 | hit2).astype(BF16)
    rr = lax.broadcasted_iota(I32, (tm, tm), 0)
    cc = lax.broadcasted_iota(I32, (tm, tm), 1)
    incl = jnp.dot((cc <= rr).astype(BF16), onehot, preferred_element_type=F32)
    counts = incl[tm - 1:tm, :]
    groups = jnp.floor((counts + (ROW_GROUP - 1.0)) * (1.0 / ROW_GROUP))
    ur = lax.broadcasted_iota(I32, (LANES, LANES), 0)
    uc = lax.broadcasted_iota(I32, (LANES, LANES), 1)
    before = jnp.dot(jnp.broadcast_to(groups, (8, LANES)).astype(BF16), (ur < uc).astype(BF16),
                     preferred_element_type=F32)[0:1]
    seg_start = before * float(ROW_GROUP)
    pick = lambda hit, tbl: jnp.sum(jnp.where(hit, tbl, 0.0), axis=-1, keepdims=True)
    r1 = pick(hit1, incl) - 1.0
    r2 = pick(hit2, incl) - 1.0
    row1 = pick(hit1, seg_start) + r1
    row2 = pick(hit2, seg_start) + r2
    slab = jnp.zeros((tm, LANES), F32)
    for i, val in enumerate((e1, e2, r1, r2, w1, w2, row1, row2)):
        slab = jnp.where(lane == i, val, slab)
    return x2, h2, slab, counts


def _sorted_copy(h2, row1, row2, n_rows):
    tm = h2.shape[0]
    r = lax.broadcasted_iota(I32, (n_rows, tm), 0).astype(F32)
    place = ((r == row1) | (r == row2)).astype(BF16)
    return jnp.dot(place, h2.astype(BF16), preferred_element_type=F32).astype(BF16)


def _merge_prompt_body(x_ref, u_ref, halo_ref, att_ref, gate_ref, wp_ref, ps_ref, wup_ref, wua_ref, wo_ref,
                       nf_ref, wr_ref, br_ref, xs_last_hbm, x2_ref, xs_ref, slab_ref, counts_ref,
                       ext_ref, sem, *, seq_len, n_groups, n_per_group):
    i = pl.program_id(0)
    n_tiles = pl.num_programs(0) - 1
    tm = x_ref.shape[0]
    gw = u_ref.shape[1] // len(POOL_WINDOWS)

    @pl.when(i < n_tiles)
    def _():
        pos0 = (i * tm) % seq_len
        u = u_ref[...]
        ext_ref[0:POOL_HALO, :] = jnp.where(pos0 == 0, 0.0, halo_ref[...])
        ext_ref[POOL_HALO:, :] = u
        pos = pos0 + lax.broadcasted_iota(I32, (tm, 1), 0)
        pooled = []
        for g, w in enumerate(POOL_WINDOWS):
            lo = g * gw
            wsum = ext_ref[pl.ds(POOL_HALO, tm), lo:lo + gw]
            for j in range(1, w):
                wsum = wsum + ext_ref[pl.ds(POOL_HALO - j, tm), lo:lo + gw]
            count = jnp.minimum(pos + 1, w).astype(F32)
            pooled.append(wsum / count - u[:, lo:lo + gw])
        x2, h2, slab, counts = _merge_and_route(
            x_ref[...], pooled, att_ref[...], gate_ref[...], wp_ref, ps_ref, wup_ref, wua_ref, wo_ref, nf_ref,
            wr_ref, br_ref, precise=False, n_groups=n_groups, n_per_group=n_per_group)
        fields = slab.T[0:8, :]
        x2_ref[...] = x2
        xs_ref[...] = _sorted_copy(h2, fields[6:7, :], fields[7:8, :], xs_ref.shape[0])
        slab_ref[...] = slab
        counts_ref[0] = counts

    @pl.when(i == n_tiles)
    def _():
        copy = pltpu.make_async_copy(xs_last_hbm, xs_ref, sem)
        copy.start()
        copy.wait()


def _merge_prompt(x, u, att, gates, wp, ps, wup, wua, wo, nf, wr, br, xs_last, *, tm, seq_len, n_groups, n_per_group):
    n, d = x.shape
    d_pool, d_att = u.shape[1], att.shape[1]
    tile_rows = xs_last.shape[0]
    nt = n // tm
    clamp = lambda i: jnp.minimum(i, nt - 1)
    row = lambda i: (clamp(i), 0)
    const = lambda i: (0, 0)
    const3 = lambda i: (0, 0, 0)
    halo = lambda i: (jnp.maximum(clamp(i) * (tm // POOL_HALO) - 1, 0), 0)
    return pl.pallas_call(
        functools.partial(_merge_prompt_body, seq_len=seq_len, n_groups=n_groups, n_per_group=n_per_group),
        grid=(nt + 1,),
        in_specs=[pl.BlockSpec((tm, d), row), pl.BlockSpec((tm, d_pool), row), pl.BlockSpec((POOL_HALO, d_pool), halo),
                  pl.BlockSpec((tm, d_att), row), pl.BlockSpec((tm, 2 * d), row),
                  pl.BlockSpec(wp.shape, const3), pl.BlockSpec((1, d_pool), const),
                  pl.BlockSpec(wup.shape, const), pl.BlockSpec(wua.shape, const), pl.BlockSpec(wo.shape, const),
                  pl.BlockSpec((1, d), const), pl.BlockSpec(wr.shape, const), pl.BlockSpec((1, LANES), const),
                  pl.BlockSpec(memory_space=pl.ANY)],
        out_specs=[pl.BlockSpec((tm, d), row), pl.BlockSpec((tile_rows, d), lambda i: (i, 0)),
                   pl.BlockSpec((tm, LANES), row), pl.BlockSpec((1, 1, LANES), lambda i: (clamp(i), 0, 0))],
        out_shape=[jax.ShapeDtypeStruct((n, d), F32), jax.ShapeDtypeStruct(((nt + 1) * tile_rows, d), BF16),
                   jax.ShapeDtypeStruct((n, LANES), F32), jax.ShapeDtypeStruct((nt, 1, LANES), F32)],
        scratch_shapes=[pltpu.VMEM((tm + POOL_HALO, d_pool), F32), pltpu.SemaphoreType.DMA],
        compiler_params=_params("arbitrary"),
        name="merge_prompt",
    )(x, u, u, att, gates, wp, ps, wup, wua, wo, nf, wr, br, xs_last)


def _merge_sample_body(x_ref, u_ref, st_ref, att_ref, gate_ref, wp_ref, ps_ref, wup_ref, wua_ref, wo_ref,
                       nf_ref, wr_ref, br_ref, x2_ref, xs_ref, slab_ref, counts_ref,
                       *, start_pos, n_groups, n_per_group):
    u = u_ref[...]
    gw = u.shape[1] // len(POOL_WINDOWS)
    n_state = st_ref.shape[0]
    pooled = []
    for g, w in enumerate(POOL_WINDOWS):
        lo = g * gw
        wsum = u[:, lo:lo + gw]
        for j in range(1, w):
            wsum = wsum + st_ref[n_state - j][:, lo:lo + gw]
        pooled.append(wsum / float(min(start_pos + 1, w)) - u[:, lo:lo + gw])
    x2, h2, slab, counts = _merge_and_route(
        x_ref[...], pooled, att_ref[...], gate_ref[...], wp_ref, ps_ref, wup_ref, wua_ref, wo_ref, nf_ref,
        wr_ref, br_ref, precise=True, n_groups=n_groups, n_per_group=n_per_group)
    n = slab.shape[0]
    fields = jnp.concatenate([slab, jnp.zeros((LANES - n, LANES), F32)], axis=0).T
    x2_ref[...] = x2
    xs_ref[...] = _sorted_copy(h2, fields[6:7, 0:n], fields[7:8, 0:n], xs_ref.shape[0])
    slab_ref[...] = slab
    counts_ref[0] = counts


def _merge_sample(x, u, state_t, att, gates, wp, ps, wup, wua, wo, nf, wr, br, *, start_pos, n_groups,
                  n_per_group, tile_rows):
    n, d = x.shape
    assert n <= LANES
    return pl.pallas_call(
        functools.partial(_merge_sample_body, start_pos=start_pos, n_groups=n_groups, n_per_group=n_per_group),
        out_shape=[jax.ShapeDtypeStruct((n, d), F32), jax.ShapeDtypeStruct((tile_rows, d), BF16),
                   jax.ShapeDtypeStruct((n, LANES), F32), jax.ShapeDtypeStruct((1, 1, LANES), F32)],
        compiler_params=_params(),
        name="merge_sample",
    )(x, u, state_t, att, gates, wp, ps, wup, wua, wo, nf, wr, br)


def _moe_plan_body(cnt_ref, gsrc_ref, lsrc_ref, te_ref, tw_ref, loc_ref, *, n_ttiles, n_exp, tile_rows, gpt,
                   seg_groups_max, local_stride, zero_group):
    n_mm = te_ref.shape[0]
    gsrc_ref[...] = jnp.full(gsrc_ref.shape, zero_group, I32)
    lsrc_ref[...] = jnp.zeros(lsrc_ref.shape, I32)

    def fill_tiles(t, c):
        te_ref[t] = -1
        tw_ref[t] = n_exp - 1
        return c

    def fill_loc(i, c):
        loc_ref[i] = 0
        return c

    lax.fori_loop(0, n_mm, fill_tiles, 0)
    lax.fori_loop(0, n_ttiles, fill_loc, 0)
    step = lax.broadcasted_iota(I32, (seg_groups_max, LANES), 0)

    def per_expert(e, pos):
        def per_tile(i, p):
            g = (cnt_ref[i * n_exp + e] + (ROW_GROUP - 1)) // ROW_GROUP
            loc = loc_ref[i]
            gsrc_ref[pl.ds(p, seg_groups_max), :] = i * tile_rows + (loc + step) * ROW_GROUP
            lsrc_ref[pl.ds(i * local_stride + loc, seg_groups_max), :] = p + step
            loc_ref[i] = loc + g
            return p + g

        end = lax.fori_loop(0, n_ttiles, per_tile, pos)
        end_pad = ((end + (gpt - 1)) // gpt) * gpt
        gsrc_ref[pl.ds(end, seg_groups_max), :] = jnp.full((seg_groups_max, LANES), zero_group, I32)

        def mark(t, c):
            te_ref[t] = e
            tw_ref[t] = e
            return c

        lax.fori_loop(pos // gpt, end_pad // gpt, mark, 0)
        return end_pad

    lax.fori_loop(0, n_exp, per_expert, 0)

    def clear_tail(i, c):
        lsrc_ref[pl.ds(i * local_stride + loc_ref[i], seg_groups_max), :] = jnp.zeros((seg_groups_max, LANES), I32)
        return c

    lax.fori_loop(0, n_ttiles, clear_tail, 0)


def _moe_plan(cnt, *, n_ttiles, n_exp, tile_rows, gpt, n_mm_tiles, seg_groups_max):
    smem = pl.BlockSpec(memory_space=pltpu.SMEM)
    zero_group = n_ttiles * tile_rows - ROW_GROUP
    local_groups = tile_rows // ROW_GROUP
    local_stride = local_groups + seg_groups_max
    n_groups = n_mm_tiles * gpt
    gsrc, lsrc, te, tw = pl.pallas_call(
        functools.partial(_moe_plan_body, n_ttiles=n_ttiles, n_exp=n_exp, tile_rows=tile_rows, gpt=gpt,
                          seg_groups_max=seg_groups_max, local_stride=local_stride, zero_group=zero_group),
        in_specs=[smem], out_specs=[pl.BlockSpec(memory_space=pltpu.VMEM), pl.BlockSpec(memory_space=pltpu.VMEM),
                                    smem, smem],
        out_shape=[jax.ShapeDtypeStruct((n_groups + 2 * seg_groups_max, LANES), I32),
                   jax.ShapeDtypeStruct((n_ttiles * local_stride, LANES), I32),
                   jax.ShapeDtypeStruct((n_mm_tiles,), I32), jax.ShapeDtypeStruct((n_mm_tiles,), I32)],
        scratch_shapes=[pltpu.SMEM((n_ttiles,), I32)],
        name="moe_plan",
    )(cnt)
    return gsrc[:n_groups, 0], lsrc[:, 0], te, tw, local_stride


def _moe_mm_body(te_ref, tw_ref, src_ref, xs_hbm, wg_ref, wu_ref, wd_ref, ys_ref, xbuf, wgb_ref, wub_ref, wdb_ref,
                 sems):
    i = pl.program_id(0)
    tm = ys_ref.shape[0]
    groups = tm // ROW_GROUP
    expert = te_ref[i]
    prev = te_ref[jnp.maximum(i - 1, 0)]

    def fetch(slot, tile):
        for k in range(groups):
            src = pl.multiple_of(src_ref[tile * groups + k], ROW_GROUP)
            pltpu.make_async_copy(xs_hbm.at[pl.ds(src, ROW_GROUP)],
                                  xbuf.at[slot, pl.ds(k * ROW_GROUP, ROW_GROUP)], sems.at[slot]).start()

    slot = i % 2

    @pl.when(i == 0)
    def _():
        fetch(0, 0)

    @pl.when(i + 1 < pl.num_programs(0))
    def _():
        fetch(1 - slot, i + 1)

    pltpu.make_async_copy(xs_hbm.at[pl.ds(0, tm)], xbuf.at[slot], sems.at[slot]).wait()

    @pl.when((expert >= 0) & ((i == 0) | (expert != prev)))
    def _():
        wgb_ref[...] = wg_ref[0].astype(BF16)
        wub_ref[...] = wu_ref[0].astype(BF16)
        wdb_ref[...] = wd_ref[0].astype(BF16)

    @pl.when(expert >= 0)
    def _():
        x = xbuf[slot]
        a = jnp.dot(x, wgb_ref[...], preferred_element_type=F32)
        b = jnp.dot(x, wub_ref[...], preferred_element_type=F32)
        hdn = (a * _sigmoid(a)) * b
        ys_ref[...] = jnp.dot(hdn.astype(BF16), wdb_ref[...], preferred_element_type=F32).astype(ys_ref.dtype)

    @pl.when(expert < 0)
    def _():
        ys_ref[...] = jnp.zeros_like(ys_ref)


def _moe_mm(tile_expert, tile_weight, group_src, xs, w_gate, w_up, w_down, *, tm):
    n_tiles = tile_expert.shape[0]
    d = xs.shape[1]
    n_exp, _, de = w_gate.shape
    wmap = lambda i, te, tw, src: (tw[i], 0, 0)
    grid_spec = pltpu.PrefetchScalarGridSpec(
        num_scalar_prefetch=3, grid=(n_tiles,),
        in_specs=[pl.BlockSpec(memory_space=pl.ANY),
                  pl.BlockSpec((1, d, de), wmap), pl.BlockSpec((1, d, de), wmap), pl.BlockSpec((1, de, d), wmap)],
        out_specs=pl.BlockSpec((tm, d), lambda i, te, tw, src: (i, 0)),
        scratch_shapes=[pltpu.VMEM((2, tm, d), BF16), pltpu.VMEM((d, de), BF16), pltpu.VMEM((d, de), BF16),
                        pltpu.VMEM((de, d), BF16), pltpu.SemaphoreType.DMA((2,))])
    return pl.pallas_call(
        _moe_mm_body, grid_spec=grid_spec,
        out_shape=jax.ShapeDtypeStruct((n_tiles * tm, d), BF16),
        compiler_params=_params("arbitrary"),
        name="moe_mm",
    )(tile_expert, tile_weight, group_src, xs, w_gate, w_up, w_down)


def _moe_combine_body(lsrc_ref, x_ref, slab_ref, g_ref, ys_hbm, o_ref, ybuf, sems,
                      *, final_norm, first_tile, local_stride):
    i = pl.program_id(0)
    tm = x_ref.shape[0]
    tile_rows = ybuf.shape[1]
    unroll = 8

    def fetch(slot, tile):
        def body(c, carry):
            for j in range(unroll):
                lg = c * unroll + j
                src = pl.multiple_of(lsrc_ref[tile * local_stride + lg] * ROW_GROUP, ROW_GROUP)
                dst = pl.multiple_of(lg * ROW_GROUP, ROW_GROUP)
                pltpu.make_async_copy(ys_hbm.at[pl.ds(src, ROW_GROUP)], ybuf.at[slot, pl.ds(dst, ROW_GROUP)],
                                      sems.at[slot]).start()
            return carry

        lax.fori_loop(0, tile_rows // ROW_GROUP // unroll, body, 0)

    slot = i % 2

    @pl.when(i == 0)
    def _():
        fetch(0, first_tile)

    @pl.when(i + 1 < pl.num_programs(0))
    def _():
        fetch(1 - slot, first_tile + i + 1)

    pltpu.make_async_copy(ys_hbm.at[pl.ds(0, tile_rows)], ybuf.at[slot], sems.at[slot]).wait()
    y = ybuf[slot]
    slab = slab_ref[...]
    r = lax.broadcasted_iota(I32, (tm, tile_rows), 1).astype(F32)
    ya = jnp.dot((r == slab[:, 6:7]).astype(BF16), y, preferred_element_type=F32)
    yb = jnp.dot((r == slab[:, 7:8]).astype(BF16), y, preferred_element_type=F32)
    out = x_ref[...] + (slab[:, 4:5] * ya + slab[:, 5:6] * yb)
    if final_norm:
        out = _rmsnorm(out, g_ref[...])
    o_ref[...] = out


def _moe_combine(local_src, x, slab, g, ys, *, ts, tile_rows, final_norm, first_tile, local_stride):
    n, d = x.shape
    return pl.pallas_call(
        functools.partial(_moe_combine_body, final_norm=final_norm, first_tile=first_tile,
                          local_stride=local_stride),
        grid=(n // ts,),
        in_specs=[pl.BlockSpec(memory_space=pltpu.SMEM),
                  pl.BlockSpec((ts, d), lambda i: (i, 0)), pl.BlockSpec((ts, LANES), lambda i: (i, 0)),
                  pl.BlockSpec((1, d), lambda i: (0, 0)), pl.BlockSpec(memory_space=pl.ANY)],
        out_specs=pl.BlockSpec((ts, d), lambda i: (i, 0)),
        out_shape=jax.ShapeDtypeStruct((n, d), F32),
        scratch_shapes=[pltpu.VMEM((2, tile_rows, d), BF16), pltpu.SemaphoreType.DMA((2,))],
        compiler_params=_params("arbitrary"),
        name="moe_combine",
    )(local_src, x, slab, g, ys)


def kernel(x_prompt, x_sample, cache_k, cache_v, cache_logf, state_pool, page_table, norm_mix, w_in, b_forget,
           w_pool, pool_scale, w_up_pool, w_up_att, w_out, norm_ffn, w_router_group, b_router_group,
           w_router_expert, b_router_expert, w_gate, w_up, w_down, norm_final):
    depth = norm_mix.shape[0]
    assert depth == 1, "single trunk layer"
    b, t, d = x_prompt.shape
    db, dt, _ = x_sample.shape
    assert dt == 1, "one sample token per sequence"
    _, n_phys, page, n_heads, dh = cache_k.shape
    n_pages = page_table.shape[1]
    past = n_pages * page
    n_state, d_pool = state_pool.shape[2], state_pool.shape[3]
    d_att = n_heads * dh
    n_pool_groups = w_pool.shape[1]
    assert n_pool_groups == len(POOL_WINDOWS) and d_pool // n_pool_groups == LANES
    assert n_state == max(POOL_WINDOWS) - 1 and n_state < POOL_HALO
    n_groups, n_per_group = w_router_expert.shape[1], w_router_expert.shape[3]
    n_exp = n_groups * n_per_group
    assert n_groups + n_exp <= LANES and 2 * dh == LANES and n_heads % 2 == 0
    n = b * t
    q_scale = float(dh) ** -0.5
    tm = min(TOKEN_TILE, t)
    assert t % tm == 0 and t % ATTN_TILE == 0

    o_main = d_pool + 3 * d_att
    wi = w_in[0]
    wm_f, wf_f, wg_f = wi[:, :o_main], wi[:, o_main:o_main + n_heads], wi[:, o_main + n_heads:]
    wf_pad = jnp.pad(wf_f, ((0, 0), (0, LANES - n_heads)))
    wit = jnp.transpose(wi)
    wmt_f, wgt_f = wit[:o_main], wit[o_main + n_heads:]
    wft_pad = jnp.pad(wit[o_main:o_main + n_heads], ((0, LANES - n_heads), (0, 0)))
    bf_pad = jnp.pad(b_forget[0], (0, LANES - n_heads)).reshape(1, LANES)
    g_mix = norm_mix[0].reshape(1, d)
    g_ffn = norm_ffn[0].reshape(1, d)
    g_fin = norm_final.reshape(1, d)
    ps = pool_scale[0].reshape(1, d_pool)
    wr_f = jnp.concatenate([w_router_group[0], jnp.transpose(w_router_expert[0], (1, 0, 2)).reshape(d, n_exp)], axis=1)
    wr_pad = jnp.pad(wr_f, ((0, 0), (0, LANES - n_groups - n_exp)))
    br_pad = jnp.pad(jnp.concatenate([b_router_group[0], b_router_expert[0].reshape(n_exp)]),
                     (0, LANES - n_groups - n_exp)).reshape(1, LANES)
    bf = lambda a: a.astype(BF16)

    xp = x_prompt.reshape(n, d)
    u_p, q_p, kt_p, vt_p, kb_p, vb_p, lft_p, gate_p = _proj_prompt(
        xp, g_mix, bf(wm_f), bf(wf_pad), bf(wg_f), bf_pad, tm=tm, seq_len=t, d_pool=d_pool, d_att=d_att,
        n_heads=n_heads, q_scale=q_scale)
    c = _cumsum_lanes(lft_p.reshape(b * n_heads, t))
    nt = t // ATTN_TILE
    c_blk = jnp.transpose(c.reshape(b, n_heads // 2, 2, nt, ATTN_TILE), (0, 1, 3, 2, 4))
    n_ptiles = n // tm
    n_ttiles = n_ptiles + 1
    tile_rows = -(-(2 * tm + n_exp * (ROW_GROUP - 1)) // MOE_ROW_TILE) * MOE_ROW_TILE
    assert 2 * db + n_exp * (ROW_GROUP - 1) <= tile_rows - ROW_GROUP, "the sample tile must end in an unused row group"

    xs = x_sample.reshape(db, d)
    z_s, lf_s, gate_s = _proj_sample(xs, g_mix, wmt_f, wft_pad, wgt_f, bf_pad, tn=512)
    u_s = z_s[:, :d_pool]
    q_s = z_s[:, d_pool:d_pool + d_att] * q_scale
    k_s = z_s[:, d_pool + d_att:d_pool + 2 * d_att]
    v_s = z_s[:, d_pool + 2 * d_att:]
    att_p, att_s = _attention(q_p.reshape(b, t, d_att), kb_p.reshape(b, t, d_att), vb_p.reshape(b, t, d_att), c_blk,
                              page_table, q_s, k_s, v_s, lf_s[:, :n_heads],
                              jnp.transpose(cache_k[0], (0, 2, 3, 1)), jnp.transpose(cache_v[0], (0, 2, 3, 1)),
                              jnp.transpose(cache_logf[0], (0, 2, 1)), tile=ATTN_TILE, dh=dh)
    state_t = jnp.transpose(state_pool[0], (1, 0, 2))
    x2_s, xs_rows_s, slab_s, counts_s = _merge_sample(
        xs, u_s, state_t, att_s.reshape(db, d_att), gate_s, w_pool[0], ps, w_up_pool[0], w_up_att[0], w_out[0],
        g_ffn, wr_pad, br_pad, start_pos=past, n_groups=n_groups, n_per_group=n_per_group, tile_rows=tile_rows)

    x2_p, xs_rows, slab_p, counts_p = _merge_prompt(
        xp, u_p, att_p.reshape(n, d_att), gate_p, bf(w_pool[0]), ps, bf(w_up_pool[0]), bf(w_up_att[0]),
        bf(w_out[0]), g_ffn, bf(wr_pad), br_pad, xs_rows_s, tm=tm, seq_len=t, n_groups=n_groups,
        n_per_group=n_per_group)

    tmm = MOE_ROW_TILE
    gpt = tmm // ROW_GROUP
    n_groups_max = -(-(2 * (n + db)) // ROW_GROUP) + n_ttiles * n_exp + n_exp * (gpt - 1)
    cnt = jnp.concatenate([counts_p, counts_s], axis=0)[:, 0, :n_exp].astype(I32).reshape(-1)
    group_src, local_src, tile_expert, tile_weight, local_stride = _moe_plan(
        cnt, n_ttiles=n_ttiles, n_exp=n_exp, tile_rows=tile_rows, gpt=gpt, n_mm_tiles=-(-n_groups_max // gpt),
        seg_groups_max=tm // ROW_GROUP)
    ys_rows = _moe_mm(tile_expert, tile_weight, group_src, xs_rows, w_gate[0], w_up[0], w_down[0], tm=tmm)
    y_prompt = _moe_combine(local_src, x2_p, slab_p, g_fin, ys_rows, ts=tm, tile_rows=tile_rows,
                            final_norm=True, first_tile=0, local_stride=local_stride)
    y_sample = _moe_combine(local_src, x2_s, slab_s, g_fin, ys_rows, ts=db, tile_rows=tile_rows,
                            final_norm=True, first_tile=n_ptiles, local_stride=local_stride)

    new_pool_p = u_p.reshape(b, t, d_pool)[:, t - n_state:, :]
    new_pool_s = jnp.concatenate([state_pool[0][:, 1:, :], u_s[:, None, :]], axis=1)
    to_heads = lambda a: jnp.transpose(a.reshape(b, n_heads, dh, t), (0, 3, 1, 2))[None]
    return (y_prompt.reshape(b, t, d), y_sample.reshape(db, 1, d),
            to_heads(kt_p), to_heads(vt_p), jnp.transpose(lft_p, (0, 2, 1))[None],
            new_pool_p[None],
            k_s.reshape(1, db, 1, n_heads, dh), v_s.reshape(1, db, 1, n_heads, dh),
            lf_s[:, :n_heads].reshape(1, db, 1, n_heads), new_pool_s[None])
```

```python
import functools

import jax
import jax.numpy as jnp
from jax import lax
from jax.experimental import pallas as pl
from jax.experimental.pallas import tpu as pltpu

F32 = jnp.float32
BF16 = jnp.bfloat16
I32 = jnp.int32
HIGHEST = lax.Precision.HIGHEST

RMS_EPS = 1e-6
POOL_WINDOWS = (2, 4, 8, 16)
POOL_HALO = 16
LANES = 128
VMEM_LIMIT_BYTES = 56 * 1024 * 1024

TOKEN_TILE = 512
ATTN_TILE = 512
MOE_ROW_TILE = 512
PAGES_PER_STEP = 16
ROW_GROUP = 16


def _params(*sem):
    return pltpu.CompilerParams(dimension_semantics=sem, vmem_limit_bytes=VMEM_LIMIT_BYTES)


def _rmsnorm(x, g):
    return x * lax.rsqrt(jnp.mean(x * x, axis=-1, keepdims=True) + RMS_EPS) * g


def _log_sigmoid(x):
    return jnp.minimum(x, 0.0) - jnp.log1p(jnp.exp(-jnp.abs(x)))


def _sigmoid(x):
    return 1.0 / (1.0 + jnp.exp(-x))


def _dot(a, b, precise):
    if precise:
        return jnp.dot(a.astype(F32), b.astype(F32), precision=HIGHEST, preferred_element_type=F32)
    return jnp.dot(a.astype(BF16), b.astype(BF16), preferred_element_type=F32)


def _split3(x):
    hi = x.astype(BF16)
    r = x - hi.astype(F32)
    mid = r.astype(BF16)
    lo = (r - mid.astype(F32)).astype(BF16)
    return hi, mid, lo


def _dot_exact_rhs(x, w_bf16):
    hi, mid, lo = _split3(x)
    d = lambda a: jnp.dot(a, w_bf16, preferred_element_type=F32)
    return d(hi) + d(mid) + d(lo)


def _proj_body(x_ref, g_ref, wm_ref, wf_ref, wg_ref, bf_ref,
               u_ref, q_ref, kt_ref, vt_ref, kb_ref, vb_ref, lft_ref, gate_ref, *, d_pool, d_att, n_heads, q_scale):
    h = _rmsnorm(x_ref[...], g_ref[...]).astype(BF16)
    z = jnp.dot(h, wm_ref[...], preferred_element_type=F32)
    o1, o2, o3 = d_pool, d_pool + d_att, d_pool + 2 * d_att
    u_ref[...] = z[:, :o1]
    q_ref[...] = (z[:, o1:o2] * q_scale).astype(BF16)
    k = z[:, o2:o3]
    v = z[:, o3:]
    kt_ref[0] = k.T
    vt_ref[0] = v.T
    kb_ref[...] = k.astype(BF16)
    vb_ref[...] = v.astype(BF16)
    lf = _log_sigmoid(jnp.dot(h, wf_ref[...], preferred_element_type=F32) + bf_ref[...])
    lft_ref[0] = lf.T[0:n_heads, :]
    gate_ref[...] = _sigmoid(jnp.dot(h, wg_ref[...], preferred_element_type=F32)).astype(BF16)


def _proj_prompt(x, g, wm, wf, wg, bfp, *, tm, seq_len, d_pool, d_att, n_heads, q_scale):
    n, d = x.shape
    b = n // seq_len
    tps = seq_len // tm
    row = lambda i: (i, 0)
    const = lambda i: (0, 0)
    tmin = lambda i: (i // tps, 0, i % tps)
    dg = wg.shape[1]
    out_shape = [
        jax.ShapeDtypeStruct((n, d_pool), F32), jax.ShapeDtypeStruct((n, d_att), BF16),
        jax.ShapeDtypeStruct((b, d_att, seq_len), F32), jax.ShapeDtypeStruct((b, d_att, seq_len), F32),
        jax.ShapeDtypeStruct((n, d_att), BF16), jax.ShapeDtypeStruct((n, d_att), BF16),
        jax.ShapeDtypeStruct((b, n_heads, seq_len), F32), jax.ShapeDtypeStruct((n, dg), BF16),
    ]
    return pl.pallas_call(
        functools.partial(_proj_body, d_pool=d_pool, d_att=d_att, n_heads=n_heads, q_scale=q_scale),
        grid=(n // tm,),
        in_specs=[pl.BlockSpec((tm, d), row), pl.BlockSpec((1, d), const),
                  pl.BlockSpec(wm.shape, const), pl.BlockSpec(wf.shape, const),
                  pl.BlockSpec(wg.shape, const), pl.BlockSpec((1, LANES), const)],
        out_specs=[pl.BlockSpec((tm, d_pool), row), pl.BlockSpec((tm, d_att), row),
                   pl.BlockSpec((1, d_att, tm), tmin), pl.BlockSpec((1, d_att, tm), tmin),
                   pl.BlockSpec((tm, d_att), row), pl.BlockSpec((tm, d_att), row),
                   pl.BlockSpec((1, n_heads, tm), tmin), pl.BlockSpec((tm, dg), row)],
        out_shape=out_shape,
        compiler_params=_params("arbitrary"),
        name="proj_prompt",
    )(x, g, wm, wf, wg, bfp)


def _dot_nt(a, bt, precise):
    dims = (((1,), (1,)), ((), ()))
    if precise:
        return lax.dot_general(a.astype(F32), bt.astype(F32), dims, precision=HIGHEST, preferred_element_type=F32)
    return lax.dot_general(a.astype(BF16), bt.astype(BF16), dims, preferred_element_type=F32)


def _proj_sample_body(x_ref, g_ref, wmt_ref, wft_ref, wgt_ref, bf_ref, z_ref, lf_ref, gate_ref):
    h = _rmsnorm(x_ref[...], g_ref[...])
    z_ref[...] = _dot_nt(h, wmt_ref[...], True)
    lf_ref[...] = _log_sigmoid(_dot_nt(h, wft_ref[...], True) + bf_ref[...])
    gate_ref[...] = _sigmoid(_dot_nt(h, wgt_ref[...], True))


def _proj_sample(x, g, wmt, wft, wgt, bfp, *, tn):
    n, d = x.shape
    dm, dg = wmt.shape[0], wgt.shape[0]
    assert dm == dg
    const = lambda j: (0, 0)
    chunk = lambda j: (j, 0)
    col = lambda j: (0, j)
    return pl.pallas_call(
        _proj_sample_body,
        grid=(dm // tn,),
        in_specs=[pl.BlockSpec((n, d), const), pl.BlockSpec((1, d), const),
                  pl.BlockSpec((tn, d), chunk), pl.BlockSpec(wft.shape, const),
                  pl.BlockSpec((tn, d), chunk), pl.BlockSpec((1, LANES), const)],
        out_specs=[pl.BlockSpec((n, tn), col), pl.BlockSpec((n, LANES), const), pl.BlockSpec((n, tn), col)],
        out_shape=[jax.ShapeDtypeStruct((n, dm), F32), jax.ShapeDtypeStruct((n, LANES), F32),
                   jax.ShapeDtypeStruct((n, dg), F32)],
        compiler_params=_params("arbitrary"),
        name="proj_sample",
    )(x, g, wmt, wft, wgt, bfp)


def _cumsum_body(x_ref, o_ref):
    c = x_ref[...]
    lane = lax.broadcasted_iota(I32, c.shape, 1)
    s = 1
    while s < c.shape[1]:
        c = c + jnp.where(lane >= s, pltpu.roll(c, s, 1), 0.0)
        s *= 2
    o_ref[...] = c


def _cumsum_lanes(x):
    return pl.pallas_call(_cumsum_body, out_shape=jax.ShapeDtypeStruct(x.shape, F32),
                          compiler_params=_params(), name="cumsum_logf")(x)


def _prompt_q_tile(qi, q_ref, kts, vhs, c_ref, o_ref, *, tile, dh, first, causal):
    q = q_ref[0, qi * tile:(qi + 1) * tile, :]
    zero = jnp.zeros_like(q)
    q_heads = (jnp.where(first, q, zero), jnp.where(first, zero, q))
    res = []
    for h in range(2):
        m = jnp.full((tile, 1), -1e30, F32)
        acc = jnp.zeros((tile, 2 * dh), F32)
        for kj in range(qi + 1):
            s = lax.dot_general(q_heads[h], kts[kj], (((1,), (1,)), ((), ())), preferred_element_type=F32)
            s = s - c_ref[0, 0, kj][h:h + 1, :]
            if kj == qi:
                s = jnp.where(causal, s, -jnp.inf)
            m_new = jnp.maximum(m, jnp.max(s, axis=-1, keepdims=True))
            alpha = jnp.exp(m - m_new)
            p = jnp.exp(s - m_new)
            acc = alpha * acc + jnp.dot(p.astype(BF16), vhs[kj][h], preferred_element_type=F32)
            m = m_new
        res.append(acc)
    a0, a1 = res
    out = jnp.where(first, a0 / a0[:, dh:dh + 1], a1 / a1[:, 0:1])
    o_ref[0, qi * tile:(qi + 1) * tile, :] = out.astype(o_ref.dtype)


def _sample_chunk(k_refs, v_refs, lf_refs, qrep, carry):
    m_prev, l, acc, s_run = carry
    n_heads, dh, page = acc.shape
    d_att = n_heads * dh
    g_n = len(k_refs)
    r = lax.broadcasted_iota(I32, (page, page), 0)
    c = lax.broadcasted_iota(I32, (page, page), 1)
    later = (r > c).astype(BF16)
    ones = jnp.ones((page, page), BF16)
    lf_all = jnp.concatenate([lf_refs[g][...] for g in range(g_n)], axis=0)
    suffix = _dot_exact_rhs(lf_all, later)
    total = _dot_exact_rhs(lf_all, ones)
    m_new = m_prev
    scores = []
    for g in range(g_n):
        kq = k_refs[g][...].reshape(d_att, page) * qrep
        s = jnp.sum(kq.reshape(n_heads, dh, page), axis=1)
        sb = s + s_run + suffix[g * n_heads:(g + 1) * n_heads]
        s_run = s_run + total[g * n_heads:(g + 1) * n_heads]
        scores.append(sb)
        m_new = jnp.maximum(m_new, jnp.max(sb, axis=-1, keepdims=True))
    alpha = jnp.exp(m_prev - m_new)
    l = alpha * l
    acc = acc * alpha[:, None, :]
    for g in range(g_n):
        p = jnp.exp(scores[g] - m_new)
        l = l + jnp.sum(p, axis=-1, keepdims=True)
        acc = acc + v_refs[g][...] * p[:, None, :]
    return m_new, l, acc, s_run


def _attn_body(pt_ref, q_ref, k_ref, v_ref, c_ref, qrep_ref, qs_ref, kn_ref, vrep_ref, lfn_ref,
               ck_hbm, cv_hbm, clf_hbm, wg_ref, wu_ref, wd_ref, o_ref, os_ref, wgb_ref, wub_ref, wdb_ref,
               kbuf, vbuf, lfbuf, sems, *, tile, dh):
    wgb_ref[...] = wg_ref[...].astype(BF16)
    wub_ref[...] = wu_ref[...].astype(BF16)
    wdb_ref[...] = wd_ref[...].astype(BF16)
    step = pl.program_id(0) * pl.num_programs(1) + pl.program_id(1)
    n_steps = pl.num_programs(0) * pl.num_programs(1)
    nt = q_ref.shape[1] // tile
    n_pages = pt_ref.shape[1]
    _, g_n, n_heads, _, page = kbuf.shape
    d_att = n_heads * dh
    seqs = qrep_ref.shape[0]
    n_chunks = n_pages // g_n
    n_items = seqs * n_chunks

    def fetch(slot, seq, chunk):
        for g in range(g_n):
            pid = pt_ref[seq, n_pages - 1 - (chunk * g_n + g)]
            pltpu.make_async_copy(ck_hbm.at[pid], kbuf.at[slot, g], sems.at[slot, 0]).start()
            pltpu.make_async_copy(cv_hbm.at[pid], vbuf.at[slot, g], sems.at[slot, 1]).start()
            pltpu.make_async_copy(clf_hbm.at[pid], lfbuf.at[slot, g], sems.at[slot, 2]).start()

    @pl.when(step == 0)
    def _():
        fetch(0, 0, 0)

    lane = lax.broadcasted_iota(I32, (tile, 2 * dh), 1)
    first = lane < dh
    row = lax.broadcasted_iota(I32, (tile, tile), 0)
    col = lax.broadcasted_iota(I32, (tile, tile), 1)
    causal = col <= row
    one = jnp.ones((tile, 2 * dh), BF16)
    kts, vhs = [], []
    for kj in range(nt):
        vt = v_ref[0, kj * tile:(kj + 1) * tile, :]
        kts.append(k_ref[0, kj * tile:(kj + 1) * tile, :])
        vhs.append((jnp.where(first, vt, one), jnp.where(first, one, vt)))

    q_done = 0
    carry = None
    for item in range(n_items):
        j, chunk = divmod(item, n_chunks)
        slot = item % 2
        if item + 1 < n_items:
            fetch(1 - slot, step * seqs + (item + 1) // n_chunks, (item + 1) % n_chunks)
        else:
            @pl.when(step + 1 < n_steps)
            def _():
                fetch(1 - slot, (step + 1) * seqs, 0)
        pltpu.make_async_copy(ck_hbm.at[pl.ds(0, g_n)], kbuf.at[slot], sems.at[slot, 0]).wait()
        pltpu.make_async_copy(cv_hbm.at[pl.ds(0, g_n)], vbuf.at[slot], sems.at[slot, 1]).wait()
        pltpu.make_async_copy(clf_hbm.at[pl.ds(0, g_n)], lfbuf.at[slot], sems.at[slot, 2]).wait()
        if chunk == 0:
            s_new = jnp.sum(qs_ref[j] * kn_ref[j], axis=-1, keepdims=True)
            lane_p = lax.broadcasted_iota(I32, (d_att, page), 1)
            carry = (jnp.broadcast_to(s_new, (n_heads, page)), jnp.ones((n_heads, page), F32),
                     jnp.where(lane_p == 0, vrep_ref[j], 0.0).reshape(n_heads, dh, page), lfn_ref[j])
        carry = _sample_chunk([kbuf.at[slot, g] for g in range(g_n)], [vbuf.at[slot, g] for g in range(g_n)],
                              [lfbuf.at[slot, g] for g in range(g_n)], qrep_ref[j], carry)
        if chunk == n_chunks - 1:
            _, l, acc, _ = carry
            os_ref[j] = jnp.sum(acc / l[:, None, :], axis=-1)
        q_until = ((item + 1) * nt) // n_items
        for qi in range(q_done, q_until):
            _prompt_q_tile(qi, q_ref, kts, vhs, c_ref, o_ref, tile=tile, dh=dh, first=first, causal=causal)
        q_done = q_until


def _attention(q, k, v, c, page_table, q_s, k_new, v_new, lf_new, cache_kt, cache_vt, cache_lft,
               w_gate, w_up, w_down, *, tile, dh):
    b, t, da = q.shape
    hp = da // (2 * dh)
    nt = t // tile
    db, n_pages = page_table.shape
    n_phys, n_heads, _, page = cache_kt.shape
    d_att = n_heads * dh
    n_steps = b * hp
    assert db % n_steps == 0, "sample sequences are split evenly over the prompt grid steps"
    seqs = db // n_steps
    g_n = PAGES_PER_STEP
    while n_pages % g_n:
        g_n //= 2
    assert (seqs * (n_pages // g_n)) % 2 == 0, "buffer slots alternate per page chunk"
    n_exp, d_model, de = w_gate.shape
    assert n_exp % n_steps == 0, "experts are split evenly over the grid steps"
    eps = n_exp // n_steps
    lane_rep = lambda a: jnp.broadcast_to(a.reshape(db, -1, 1), (db, a.size // db, page))
    pair = pl.BlockSpec((1, t, 2 * dh), lambda bi, hi, pt: (bi, 0, hi))
    per_step = lambda bi, hi, pt: (bi * hp + hi, 0, 0)
    hbm = pl.BlockSpec(memory_space=pl.ANY)
    grid_spec = pltpu.PrefetchScalarGridSpec(
        num_scalar_prefetch=1, grid=(b, hp),
        in_specs=[pair, pair, pair, pl.BlockSpec((1, 1, nt, 2, tile), lambda bi, hi, pt: (bi, hi, 0, 0, 0)),
                  pl.BlockSpec((seqs, d_att, page), per_step), pl.BlockSpec((seqs, n_heads, dh), per_step),
                  pl.BlockSpec((seqs, n_heads, dh), per_step), pl.BlockSpec((seqs, d_att, page), per_step),
                  pl.BlockSpec((seqs, n_heads, page), per_step), hbm, hbm, hbm,
                  pl.BlockSpec((eps, d_model, de), per_step), pl.BlockSpec((eps, d_model, de), per_step),
                  pl.BlockSpec((eps, de, d_model), per_step)],
        out_specs=[pair, pl.BlockSpec((seqs, n_heads, dh), per_step),
                   pl.BlockSpec((eps, d_model, de), per_step), pl.BlockSpec((eps, d_model, de), per_step),
                   pl.BlockSpec((eps, de, d_model), per_step)],
        scratch_shapes=[pltpu.VMEM((2, g_n, n_heads, dh, page), F32), pltpu.VMEM((2, g_n, n_heads, dh, page), F32),
                        pltpu.VMEM((2, g_n, n_heads, page), F32), pltpu.SemaphoreType.DMA((2, 3))])
    return pl.pallas_call(
        functools.partial(_attn_body, tile=tile, dh=dh),
        grid_spec=grid_spec,
        out_shape=[jax.ShapeDtypeStruct((b, t, da), BF16), jax.ShapeDtypeStruct((db, n_heads, dh), F32),
                   jax.ShapeDtypeStruct(w_gate.shape, BF16), jax.ShapeDtypeStruct(w_up.shape, BF16),
                   jax.ShapeDtypeStruct(w_down.shape, BF16)],
        compiler_params=_params("arbitrary", "arbitrary"),
        name="attention",
    )(page_table, q, k, v, c, lane_rep(q_s), q_s.reshape(db, n_heads, dh), k_new.reshape(db, n_heads, dh),
      lane_rep(v_new), lane_rep(lf_new), cache_kt, cache_vt, cache_lft, w_gate, w_up, w_down)


def _merge_and_route(x, pooled, att, gates, wp_ref, ps_ref, wup_ref, wua_ref, wo_ref, nf_ref, wr_ref, br_ref,
                     *, precise, n_groups, n_per_group):
    tm, d = x.shape
    gw = pooled[0].shape[1]
    mixed = jnp.concatenate([_dot(pooled[g], wp_ref[g], precise) for g in range(len(pooled))], axis=-1)
    pool_out = mixed * ps_ref[...]
    y = gates[:, :d].astype(F32) * _dot(pool_out, wup_ref[...], precise) \
        + gates[:, d:].astype(F32) * _dot(att, wua_ref[...], precise)
    x2 = x + _dot(y, wo_ref[...], precise)
    h2 = _rmsnorm(x2, nf_ref[...])
    logits = _dot(h2, wr_ref[...], precise) + br_ref[...]
    lane = lax.broadcasted_iota(I32, logits.shape, 1)
    lanef = lane.astype(F32)
    neg = -jnp.inf
    is_g = lane < n_groups
    gmax = jnp.max(jnp.where(is_g, logits, neg), axis=-1, keepdims=True)
    gidx = jnp.min(jnp.where(is_g & (logits == gmax), lanef, float(LANES)), axis=-1, keepdims=True)
    gsum = jnp.sum(jnp.where(is_g, jnp.exp(logits - gmax), 0.0), axis=-1, keepdims=True)
    g_w = 1.0 / gsum
    n_exp = n_groups * n_per_group
    exp_id = lanef - float(n_groups)
    in_sel = (lane >= n_groups) & (lane < n_groups + n_exp) & (jnp.floor(exp_id / n_per_group) == gidx)
    v1 = jnp.max(jnp.where(in_sel, logits, neg), axis=-1, keepdims=True)
    i1 = jnp.min(jnp.where(in_sel & (logits == v1), lanef, float(LANES)), axis=-1, keepdims=True)
    in_sel2 = in_sel & (lanef != i1)
    v2 = jnp.max(jnp.where(in_sel2, logits, neg), axis=-1, keepdims=True)
    i2 = jnp.min(jnp.where(in_sel2 & (logits == v2), lanef, float(LANES)), axis=-1, keepdims=True)
    t = jnp.exp(v2 - v1)
    w1 = g_w * (1.0 / (1.0 + t))
    w2 = g_w * (t / (1.0 + t))
    e1 = i1 - float(n_groups)
    e2 = i2 - float(n_groups)
    hit1 = lanef == e1
    hit2 = lanef == e2
    onehot = (hit1 | hit2).astype(BF16)
    rr = lax.broadcasted_iota(I32, (tm, tm), 0)
    cc = lax.broadcasted_iota(I32, (tm, tm), 1)
    incl = jnp.dot((cc <= rr).astype(BF16), onehot, preferred_element_type=F32)
    counts = incl[tm - 1:tm, :]
    groups = jnp.floor((counts + (ROW_GROUP - 1.0)) * (1.0 / ROW_GROUP))
    ur = lax.broadcasted_iota(I32, (LANES, LANES), 0)
    uc = lax.broadcasted_iota(I32, (LANES, LANES), 1)
    before = jnp.dot(jnp.broadcast_to(groups, (8, LANES)).astype(BF16), (ur < uc).astype(BF16),
                     preferred_element_type=F32)[0:1]
    seg_start = before * float(ROW_GROUP)
    pick = lambda hit, tbl: jnp.sum(jnp.where(hit, tbl, 0.0), axis=-1, keepdims=True)
    r1 = pick(hit1, incl) - 1.0
    r2 = pick(hit2, incl) - 1.0
    row1 = pick(hit1, seg_start) + r1
    row2 = pick(hit2, seg_start) + r2
    slab = jnp.zeros((tm, LANES), F32)
    for i, val in enumerate((e1, e2, r1, r2, w1, w2, row1, row2)):
        slab = jnp.where(lane == i, val, slab)
    return x2, h2, slab, counts


def _sorted_copy(h2, row1, row2, n_rows):
    tm = h2.shape[0]
    r = lax.broadcasted_iota(I32, (n_rows, tm), 0).astype(F32)
    place = ((r == row1) | (r == row2)).astype(BF16)
    return jnp.dot(place, h2.astype(BF16), preferred_element_type=F32).astype(BF16)


def _merge_prompt_body(x_ref, u_ref, halo_ref, att_ref, gate_ref, wp_ref, ps_ref, wup_ref, wua_ref, wo_ref,
                       nf_ref, wr_ref, br_ref, xs_last_hbm, x2_ref, xs_ref, slab_ref, counts_ref,
                       ext_ref, sem, *, seq_len, n_groups, n_per_group):
    i = pl.program_id(0)
    n_tiles = pl.num_programs(0) - 1
    tm = x_ref.shape[0]
    gw = u_ref.shape[1] // len(POOL_WINDOWS)

    @pl.when(i < n_tiles)
    def _():
        pos0 = (i * tm) % seq_len
        u = u_ref[...]
        ext_ref[0:POOL_HALO, :] = jnp.where(pos0 == 0, 0.0, halo_ref[...])
        ext_ref[POOL_HALO:, :] = u
        pos = pos0 + lax.broadcasted_iota(I32, (tm, 1), 0)
        pooled = []
        for g, w in enumerate(POOL_WINDOWS):
            lo = g * gw
            wsum = ext_ref[pl.ds(POOL_HALO, tm), lo:lo + gw]
            for j in range(1, w):
                wsum = wsum + ext_ref[pl.ds(POOL_HALO - j, tm), lo:lo + gw]
            count = jnp.minimum(pos + 1, w).astype(F32)
            pooled.append(wsum / count - u[:, lo:lo + gw])
        x2, h2, slab, counts = _merge_and_route(
            x_ref[...], pooled, att_ref[...], gate_ref[...], wp_ref, ps_ref, wup_ref, wua_ref, wo_ref, nf_ref,
            wr_ref, br_ref, precise=False, n_groups=n_groups, n_per_group=n_per_group)
        fields = slab.T[0:8, :]
        x2_ref[...] = x2
        xs_ref[...] = _sorted_copy(h2, fields[6:7, :], fields[7:8, :], xs_ref.shape[0])
        slab_ref[...] = slab
        counts_ref[0] = counts

    @pl.when(i == n_tiles)
    def _():
        copy = pltpu.make_async_copy(xs_last_hbm, xs_ref, sem)
        copy.start()
        copy.wait()


def _merge_prompt(x, u, att, gates, wp, ps, wup, wua, wo, nf, wr, br, xs_last, *, tm, seq_len, n_groups, n_per_group):
    n, d = x.shape
    d_pool, d_att = u.shape[1], att.shape[1]
    tile_rows = xs_last.shape[0]
    nt = n // tm
    clamp = lambda i: jnp.minimum(i, nt - 1)
    row = lambda i: (clamp(i), 0)
    const = lambda i: (0, 0)
    const3 = lambda i: (0, 0, 0)
    halo = lambda i: (jnp.maximum(clamp(i) * (tm // POOL_HALO) - 1, 0), 0)
    return pl.pallas_call(
        functools.partial(_merge_prompt_body, seq_len=seq_len, n_groups=n_groups, n_per_group=n_per_group),
        grid=(nt + 1,),
        in_specs=[pl.BlockSpec((tm, d), row), pl.BlockSpec((tm, d_pool), row), pl.BlockSpec((POOL_HALO, d_pool), halo),
                  pl.BlockSpec((tm, d_att), row), pl.BlockSpec((tm, 2 * d), row),
                  pl.BlockSpec(wp.shape, const3), pl.BlockSpec((1, d_pool), const),
                  pl.BlockSpec(wup.shape, const), pl.BlockSpec(wua.shape, const), pl.BlockSpec(wo.shape, const),
                  pl.BlockSpec((1, d), const), pl.BlockSpec(wr.shape, const), pl.BlockSpec((1, LANES), const),
                  pl.BlockSpec(memory_space=pl.ANY)],
        out_specs=[pl.BlockSpec((tm, d), row), pl.BlockSpec((tile_rows, d), lambda i: (i, 0)),
                   pl.BlockSpec((tm, LANES), row), pl.BlockSpec((1, 1, LANES), lambda i: (clamp(i), 0, 0))],
        out_shape=[jax.ShapeDtypeStruct((n, d), F32), jax.ShapeDtypeStruct(((nt + 1) * tile_rows, d), BF16),
                   jax.ShapeDtypeStruct((n, LANES), F32), jax.ShapeDtypeStruct((nt, 1, LANES), F32)],
        scratch_shapes=[pltpu.VMEM((tm + POOL_HALO, d_pool), F32), pltpu.SemaphoreType.DMA],
        compiler_params=_params("arbitrary"),
        name="merge_prompt",
    )(x, u, u, att, gates, wp, ps, wup, wua, wo, nf, wr, br, xs_last)


def _merge_sample_body(x_ref, u_ref, st_ref, att_ref, gate_ref, wp_ref, ps_ref, wup_ref, wua_ref, wo_ref,
                       nf_ref, wr_ref, br_ref, x2_ref, xs_ref, slab_ref, counts_ref,
                       *, start_pos, n_groups, n_per_group):
    u = u_ref[...]
    gw = u.shape[1] // len(POOL_WINDOWS)
    n_state = st_ref.shape[0]
    pooled = []
    for g, w in enumerate(POOL_WINDOWS):
        lo = g * gw
        wsum = u[:, lo:lo + gw]
        for j in range(1, w):
            wsum = wsum + st_ref[n_state - j][:, lo:lo + gw]
        pooled.append(wsum / float(min(start_pos + 1, w)) - u[:, lo:lo + gw])
    x2, h2, slab, counts = _merge_and_route(
        x_ref[...], pooled, att_ref[...], gate_ref[...], wp_ref, ps_ref, wup_ref, wua_ref, wo_ref, nf_ref,
        wr_ref, br_ref, precise=True, n_groups=n_groups, n_per_group=n_per_group)
    n = slab.shape[0]
    fields = jnp.concatenate([slab, jnp.zeros((LANES - n, LANES), F32)], axis=0).T
    x2_ref[...] = x2
    xs_ref[...] = _sorted_copy(h2, fields[6:7, 0:n], fields[7:8, 0:n], xs_ref.shape[0])
    slab_ref[...] = slab
    counts_ref[0] = counts


def _merge_sample(x, u, state_t, att, gates, wp, ps, wup, wua, wo, nf, wr, br, *, start_pos, n_groups,
                  n_per_group, tile_rows):
    n, d = x.shape
    assert n <= LANES
    return pl.pallas_call(
        functools.partial(_merge_sample_body, start_pos=start_pos, n_groups=n_groups, n_per_group=n_per_group),
        out_shape=[jax.ShapeDtypeStruct((n, d), F32), jax.ShapeDtypeStruct((tile_rows, d), BF16),
                   jax.ShapeDtypeStruct((n, LANES), F32), jax.ShapeDtypeStruct((1, 1, LANES), F32)],
        compiler_params=_params(),
        name="merge_sample",
    )(x, u, state_t, att, gates, wp, ps, wup, wua, wo, nf, wr, br)


def _moe_plan_body(cnt_ref, gsrc_ref, lsrc_ref, te_ref, tw_ref, loc_ref, *, n_ttiles, n_exp, tile_rows, gpt,
                   seg_groups_max, local_stride, zero_group):
    n_mm = te_ref.shape[0]
    gsrc_ref[...] = jnp.full(gsrc_ref.shape, zero_group, I32)
    lsrc_ref[...] = jnp.zeros(lsrc_ref.shape, I32)

    def fill_tiles(t, c):
        te_ref[t] = -1
        tw_ref[t] = n_exp - 1
        return c

    def fill_loc(i, c):
        loc_ref[i] = 0
        return c

    lax.fori_loop(0, n_mm, fill_tiles, 0)
    lax.fori_loop(0, n_ttiles, fill_loc, 0)
    step = lax.broadcasted_iota(I32, (seg_groups_max, LANES), 0)

    def per_expert(e, pos):
        def per_tile(i, p):
            g = (cnt_ref[i * n_exp + e] + (ROW_GROUP - 1)) // ROW_GROUP
            loc = loc_ref[i]
            gsrc_ref[pl.ds(p, seg_groups_max), :] = i * tile_rows + (loc + step) * ROW_GROUP
            lsrc_ref[pl.ds(i * local_stride + loc, seg_groups_max), :] = p + step
            loc_ref[i] = loc + g
            return p + g

        end = lax.fori_loop(0, n_ttiles, per_tile, pos)
        end_pad = ((end + (gpt - 1)) // gpt) * gpt
        gsrc_ref[pl.ds(end, seg_groups_max), :] = jnp.full((seg_groups_max, LANES), zero_group, I32)

        def mark(t, c):
            te_ref[t] = e
            tw_ref[t] = e
            return c

        lax.fori_loop(pos // gpt, end_pad // gpt, mark, 0)
        return end_pad

    lax.fori_loop(0, n_exp, per_expert, 0)

    def clear_tail(i, c):
        lsrc_ref[pl.ds(i * local_stride + loc_ref[i], seg_groups_max), :] = jnp.zeros((seg_groups_max, LANES), I32)
        return c

    lax.fori_loop(0, n_ttiles, clear_tail, 0)


def _moe_plan(cnt, *, n_ttiles, n_exp, tile_rows, gpt, n_mm_tiles, seg_groups_max):
    smem = pl.BlockSpec(memory_space=pltpu.SMEM)
    zero_group = n_ttiles * tile_rows - ROW_GROUP
    local_groups = tile_rows // ROW_GROUP
    local_stride = local_groups + seg_groups_max
    n_groups = n_mm_tiles * gpt
    gsrc, lsrc, te, tw = pl.pallas_call(
        functools.partial(_moe_plan_body, n_ttiles=n_ttiles, n_exp=n_exp, tile_rows=tile_rows, gpt=gpt,
                          seg_groups_max=seg_groups_max, local_stride=local_stride, zero_group=zero_group),
        in_specs=[smem], out_specs=[pl.BlockSpec(memory_space=pltpu.VMEM), pl.BlockSpec(memory_space=pltpu.VMEM),
                                    smem, smem],
        out_shape=[jax.ShapeDtypeStruct((n_groups + 2 * seg_groups_max, LANES), I32),
                   jax.ShapeDtypeStruct((n_ttiles * local_stride, LANES), I32),
                   jax.ShapeDtypeStruct((n_mm_tiles,), I32), jax.ShapeDtypeStruct((n_mm_tiles,), I32)],
        scratch_shapes=[pltpu.SMEM((n_ttiles,), I32)],
        name="moe_plan",
    )(cnt)
    return gsrc[:n_groups, 0], lsrc[:, 0], te, tw, local_stride


def _moe_mm_body(te_ref, tw_ref, src_ref, xs_hbm, wg_ref, wu_ref, wd_ref, ys_ref, xbuf, sems):
    i = pl.program_id(0)
    tm = ys_ref.shape[0]
    groups = tm // ROW_GROUP
    expert = te_ref[i]

    def fetch(slot, tile):
        for k in range(groups):
            src = pl.multiple_of(src_ref[tile * groups + k], ROW_GROUP)
            pltpu.make_async_copy(xs_hbm.at[pl.ds(src, ROW_GROUP)],
                                  xbuf.at[slot, pl.ds(k * ROW_GROUP, ROW_GROUP)], sems.at[slot]).start()

    slot = i % 2

    @pl.when(i == 0)
    def _():
        fetch(0, 0)

    @pl.when(i + 1 < pl.num_programs(0))
    def _():
        fetch(1 - slot, i + 1)

    pltpu.make_async_copy(xs_hbm.at[pl.ds(0, tm)], xbuf.at[slot], sems.at[slot]).wait()

    @pl.when(expert >= 0)
    def _():
        x = xbuf[slot]
        a = jnp.dot(x, wg_ref[0], preferred_element_type=F32)
        b = jnp.dot(x, wu_ref[0], preferred_element_type=F32)
        hdn = (a * _sigmoid(a)) * b
        ys_ref[...] = jnp.dot(hdn.astype(BF16), wd_ref[0], preferred_element_type=F32).astype(ys_ref.dtype)

    @pl.when(expert < 0)
    def _():
        ys_ref[...] = jnp.zeros_like(ys_ref)


def _moe_mm(tile_expert, tile_weight, group_src, xs, w_gate, w_up, w_down, *, tm):
    n_tiles = tile_expert.shape[0]
    d = xs.shape[1]
    n_exp, _, de = w_gate.shape
    wmap = lambda i, te, tw, src: (tw[i], 0, 0)
    grid_spec = pltpu.PrefetchScalarGridSpec(
        num_scalar_prefetch=3, grid=(n_tiles,),
        in_specs=[pl.BlockSpec(memory_space=pl.ANY),
                  pl.BlockSpec((1, d, de), wmap), pl.BlockSpec((1, d, de), wmap), pl.BlockSpec((1, de, d), wmap)],
        out_specs=pl.BlockSpec((tm, d), lambda i, te, tw, src: (i, 0)),
        scratch_shapes=[pltpu.VMEM((2, tm, d), BF16), pltpu.SemaphoreType.DMA((2,))])
    return pl.pallas_call(
        _moe_mm_body, grid_spec=grid_spec,
        out_shape=jax.ShapeDtypeStruct((n_tiles * tm, d), BF16),
        compiler_params=_params("arbitrary"),
        name="moe_mm",
    )(tile_expert, tile_weight, group_src, xs, w_gate, w_up, w_down)


def _moe_combine_body(lsrc_ref, x_ref, slab_ref, g_ref, ys_hbm, o_ref, ybuf, sems,
                      *, final_norm, first_tile, local_stride):
    i = pl.program_id(0)
    tm = x_ref.shape[0]
    tile_rows = ybuf.shape[1]
    unroll = 8

    def fetch(slot, tile):
        def body(c, carry):
            for j in range(unroll):
                lg = c * unroll + j
                src = pl.multiple_of(lsrc_ref[tile * local_stride + lg] * ROW_GROUP, ROW_GROUP)
                dst = pl.multiple_of(lg * ROW_GROUP, ROW_GROUP)
                pltpu.make_async_copy(ys_hbm.at[pl.ds(src, ROW_GROUP)], ybuf.at[slot, pl.ds(dst, ROW_GROUP)],
                                      sems.at[slot]).start()
            return carry

        lax.fori_loop(0, tile_rows // ROW_GROUP // unroll, body, 0)

    slot = i % 2

    @pl.when(i == 0)
    def _():
        fetch(0, first_tile)

    @pl.when(i + 1 < pl.num_programs(0))
    def _():
        fetch(1 - slot, first_tile + i + 1)

    pltpu.make_async_copy(ys_hbm.at[pl.ds(0, tile_rows)], ybuf.at[slot], sems.at[slot]).wait()
    y = ybuf[slot]
    slab = slab_ref[...]
    r = lax.broadcasted_iota(I32, (tm, tile_rows), 1).astype(F32)
    ya = jnp.dot((r == slab[:, 6:7]).astype(BF16), y, preferred_element_type=F32)
    yb = jnp.dot((r == slab[:, 7:8]).astype(BF16), y, preferred_element_type=F32)
    out = x_ref[...] + (slab[:, 4:5] * ya + slab[:, 5:6] * yb)
    if final_norm:
        out = _rmsnorm(out, g_ref[...])
    o_ref[...] = out


def _moe_combine(local_src, x, slab, g, ys, *, ts, tile_rows, final_norm, first_tile, local_stride):
    n, d = x.shape
    return pl.pallas_call(
        functools.partial(_moe_combine_body, final_norm=final_norm, first_tile=first_tile,
                          local_stride=local_stride),
        grid=(n // ts,),
        in_specs=[pl.BlockSpec(memory_space=pltpu.SMEM),
                  pl.BlockSpec((ts, d), lambda i: (i, 0)), pl.BlockSpec((ts, LANES), lambda i: (i, 0)),
                  pl.BlockSpec((1, d), lambda i: (0, 0)), pl.BlockSpec(memory_space=pl.ANY)],
        out_specs=pl.BlockSpec((ts, d), lambda i: (i, 0)),
        out_shape=jax.ShapeDtypeStruct((n, d), F32),
        scratch_shapes=[pltpu.VMEM((2, tile_rows, d), BF16), pltpu.SemaphoreType.DMA((2,))],
        compiler_params=_params("arbitrary"),
        name="moe_combine",
    )(local_src, x, slab, g, ys)


def kernel(x_prompt, x_sample, cache_k, cache_v, cache_logf, state_pool, page_table, norm_mix, w_in, b_forget,
           w_pool, pool_scale, w_up_pool, w_up_att, w_out, norm_ffn, w_router_group, b_router_group,
           w_router_expert, b_router_expert, w_gate, w_up, w_down, norm_final):
    depth = norm_mix.shape[0]
    assert depth == 1, "single trunk layer"
    b, t, d = x_prompt.shape
    db, dt, _ = x_sample.shape
    assert dt == 1, "one sample token per sequence"
    _, n_phys, page, n_heads, dh = cache_k.shape
    n_pages = page_table.shape[1]
    past = n_pages * page
    n_state, d_pool = state_pool.shape[2], state_pool.shape[3]
    d_att = n_heads * dh
    n_pool_groups = w_pool.shape[1]
    assert n_pool_groups == len(POOL_WINDOWS) and d_pool // n_pool_groups == LANES
    assert n_state == max(POOL_WINDOWS) - 1 and n_state < POOL_HALO
    n_groups, n_per_group = w_router_expert.shape[1], w_router_expert.shape[3]
    n_exp = n_groups * n_per_group
    assert n_groups + n_exp <= LANES and 2 * dh == LANES and n_heads % 2 == 0
    n = b * t
    q_scale = float(dh) ** -0.5
    tm = min(TOKEN_TILE, t)
    assert t % tm == 0 and t % ATTN_TILE == 0

    o_main = d_pool + 3 * d_att
    wi = w_in[0]
    wm_f, wf_f, wg_f = wi[:, :o_main], wi[:, o_main:o_main + n_heads], wi[:, o_main + n_heads:]
    wf_pad = jnp.pad(wf_f, ((0, 0), (0, LANES - n_heads)))
    wit = jnp.transpose(wi)
    wmt_f, wgt_f = wit[:o_main], wit[o_main + n_heads:]
    wft_pad = jnp.pad(wit[o_main:o_main + n_heads], ((0, LANES - n_heads), (0, 0)))
    bf_pad = jnp.pad(b_forget[0], (0, LANES - n_heads)).reshape(1, LANES)
    g_mix = norm_mix[0].reshape(1, d)
    g_ffn = norm_ffn[0].reshape(1, d)
    g_fin = norm_final.reshape(1, d)
    ps = pool_scale[0].reshape(1, d_pool)
    wr_f = jnp.concatenate([w_router_group[0], jnp.transpose(w_router_expert[0], (1, 0, 2)).reshape(d, n_exp)], axis=1)
    wr_pad = jnp.pad(wr_f, ((0, 0), (0, LANES - n_groups - n_exp)))
    br_pad = jnp.pad(jnp.concatenate([b_router_group[0], b_router_expert[0].reshape(n_exp)]),
                     (0, LANES - n_groups - n_exp)).reshape(1, LANES)
    bf = lambda a: a.astype(BF16)

    xp = x_prompt.reshape(n, d)
    u_p, q_p, kt_p, vt_p, kb_p, vb_p, lft_p, gate_p = _proj_prompt(
        xp, g_mix, bf(wm_f), bf(wf_pad), bf(wg_f), bf_pad, tm=tm, seq_len=t, d_pool=d_pool, d_att=d_att,
        n_heads=n_heads, q_scale=q_scale)
    c = _cumsum_lanes(lft_p.reshape(b * n_heads, t))
    nt = t // ATTN_TILE
    c_blk = jnp.transpose(c.reshape(b, n_heads // 2, 2, nt, ATTN_TILE), (0, 1, 3, 2, 4))
    n_ptiles = n // tm
    n_ttiles = n_ptiles + 1
    tile_rows = -(-(2 * tm + n_exp * (ROW_GROUP - 1)) // MOE_ROW_TILE) * MOE_ROW_TILE
    assert 2 * db + n_exp * (ROW_GROUP - 1) <= tile_rows - ROW_GROUP, "the sample tile must end in an unused row group"

    xs = x_sample.reshape(db, d)
    z_s, lf_s, gate_s = _proj_sample(xs, g_mix, wmt_f, wft_pad, wgt_f, bf_pad, tn=512)
    u_s = z_s[:, :d_pool]
    q_s = z_s[:, d_pool:d_pool + d_att] * q_scale
    k_s = z_s[:, d_pool + d_att:d_pool + 2 * d_att]
    v_s = z_s[:, d_pool + 2 * d_att:]
    att_p, att_s, wg_b, wu_b, wd_b = _attention(
        q_p.reshape(b, t, d_att), kb_p.reshape(b, t, d_att), vb_p.reshape(b, t, d_att), c_blk,
        page_table, q_s, k_s, v_s, lf_s[:, :n_heads],
        jnp.transpose(cache_k[0], (0, 2, 3, 1)), jnp.transpose(cache_v[0], (0, 2, 3, 1)),
        jnp.transpose(cache_logf[0], (0, 2, 1)), w_gate[0], w_up[0], w_down[0], tile=ATTN_TILE, dh=dh)
    state_t = jnp.transpose(state_pool[0], (1, 0, 2))
    x2_s, xs_rows_s, slab_s, counts_s = _merge_sample(
        xs, u_s, state_t, att_s.reshape(db, d_att), gate_s, w_pool[0], ps, w_up_pool[0], w_up_att[0], w_out[0],
        g_ffn, wr_pad, br_pad, start_pos=past, n_groups=n_groups, n_per_group=n_per_group, tile_rows=tile_rows)

    x2_p, xs_rows, slab_p, counts_p = _merge_prompt(
        xp, u_p, att_p.reshape(n, d_att), gate_p, bf(w_pool[0]), ps, bf(w_up_pool[0]), bf(w_up_att[0]),
        bf(w_out[0]), g_ffn, bf(wr_pad), br_pad, xs_rows_s, tm=tm, seq_len=t, n_groups=n_groups,
        n_per_group=n_per_group)

    tmm = MOE_ROW_TILE
    gpt = tmm // ROW_GROUP
    n_groups_max = -(-(2 * (n + db)) // ROW_GROUP) + n_ttiles * n_exp + n_exp * (gpt - 1)
    cnt = jnp.concatenate([counts_p, counts_s], axis=0)[:, 0, :n_exp].astype(I32).reshape(-1)
    group_src, local_src, tile_expert, tile_weight, local_stride = _moe_plan(
        cnt, n_ttiles=n_ttiles, n_exp=n_exp, tile_rows=tile_rows, gpt=gpt, n_mm_tiles=-(-n_groups_max // gpt),
        seg_groups_max=tm // ROW_GROUP)
    ys_rows = _moe_mm(tile_expert, tile_weight, group_src, xs_rows, wg_b, wu_b, wd_b, tm=tmm)
    y_prompt = _moe_combine(local_src, x2_p, slab_p, g_fin, ys_rows, ts=tm, tile_rows=tile_rows,
                            final_norm=True, first_tile=0, local_stride=local_stride)
    y_sample = _moe_combine(local_src, x2_s, slab_s, g_fin, ys_rows, ts=db, tile_rows=tile_rows,
                            final_norm=True, first_tile=n_ptiles, local_stride=local_stride)

    new_pool_p = u_p.reshape(b, t, d_pool)[:, t - n_state:, :]
    new_pool_s = jnp.concatenate([state_pool[0][:, 1:, :], u_s[:, None, :]], axis=1)
    to_heads = lambda a: jnp.transpose(a.reshape(b, n_heads, dh, t), (0, 3, 1, 2))[None]
    return (y_prompt.reshape(b, t, d), y_sample.reshape(db, 1, d),
            to_heads(kt_p), to_heads(vt_p), jnp.transpose(lft_p, (0, 2, 1))[None],
            new_pool_p[None],
            k_s.reshape(1, db, 1, n_heads, dh), v_s.reshape(1, db, 1, n_heads, dh),
            lf_s[:, :n_heads].reshape(1, db, 1, n_heads), new_pool_s[None])
```

```python
import functools

import jax
import jax.numpy as jnp
from jax import lax
from jax.experimental import pallas as pl
from jax.experimental.pallas import tpu as pltpu

F32 = jnp.float32
BF16 = jnp.bfloat16
I32 = jnp.int32
HIGHEST = lax.Precision.HIGHEST

RMS_EPS = 1e-6
POOL_WINDOWS = (2, 4, 8, 16)
POOL_HALO = 16
LANES = 128
VMEM_LIMIT_BYTES = 56 * 1024 * 1024

TOKEN_TILE = 512
ATTN_TILE = 512
MOE_ROW_TILE = 512
PAGES_PER_STEP = 16
ROW_GROUP = 16


def _params(*sem):
    return pltpu.CompilerParams(dimension_semantics=sem, vmem_limit_bytes=VMEM_LIMIT_BYTES)


def _rmsnorm(x, g):
    return x * lax.rsqrt(jnp.mean(x * x, axis=-1, keepdims=True) + RMS_EPS) * g


def _log_sigmoid(x):
    return jnp.minimum(x, 0.0) - jnp.log1p(jnp.exp(-jnp.abs(x)))


def _sigmoid(x):
    return 1.0 / (1.0 + jnp.exp(-x))


def _dot(a, b, precise):
    if precise:
        return jnp.dot(a.astype(F32), b.astype(F32), precision=HIGHEST, preferred_element_type=F32)
    return jnp.dot(a.astype(BF16), b.astype(BF16), preferred_element_type=F32)


def _split3(x):
    hi = x.astype(BF16)
    r = x - hi.astype(F32)
    mid = r.astype(BF16)
    lo = (r - mid.astype(F32)).astype(BF16)
    return hi, mid, lo


def _dot_exact_rhs(x, w_bf16):
    hi, mid, lo = _split3(x)
    d = lambda a: jnp.dot(a, w_bf16, preferred_element_type=F32)
    return d(hi) + d(mid) + d(lo)


def _proj_body(x_ref, g_ref, wm_ref, wf_ref, wg_ref, bf_ref,
               u_ref, q_ref, kt_ref, vt_ref, kb_ref, vb_ref, lft_ref, gate_ref, *, d_pool, d_att, n_heads, q_scale):
    h = _rmsnorm(x_ref[...], g_ref[...]).astype(BF16)
    z = jnp.dot(h, wm_ref[...], preferred_element_type=F32)
    o1, o2, o3 = d_pool, d_pool + d_att, d_pool + 2 * d_att
    u_ref[...] = z[:, :o1]
    q_ref[...] = (z[:, o1:o2] * q_scale).astype(BF16)
    k = z[:, o2:o3]
    v = z[:, o3:]
    kt_ref[0] = k.T
    vt_ref[0] = v.T
    kb_ref[...] = k.astype(BF16)
    vb_ref[...] = v.astype(BF16)
    lf = _log_sigmoid(jnp.dot(h, wf_ref[...], preferred_element_type=F32) + bf_ref[...])
    lft_ref[0] = lf.T[0:n_heads, :]
    gate_ref[...] = _sigmoid(jnp.dot(h, wg_ref[...], preferred_element_type=F32)).astype(BF16)


def _proj_prompt(x, g, wm, wf, wg, bfp, *, tm, seq_len, d_pool, d_att, n_heads, q_scale):
    n, d = x.shape
    b = n // seq_len
    tps = seq_len // tm
    row = lambda i: (i, 0)
    const = lambda i: (0, 0)
    tmin = lambda i: (i // tps, 0, i % tps)
    dg = wg.shape[1]
    out_shape = [
        jax.ShapeDtypeStruct((n, d_pool), F32), jax.ShapeDtypeStruct((n, d_att), BF16),
        jax.ShapeDtypeStruct((b, d_att, seq_len), F32), jax.ShapeDtypeStruct((b, d_att, seq_len), F32),
        jax.ShapeDtypeStruct((n, d_att), BF16), jax.ShapeDtypeStruct((n, d_att), BF16),
        jax.ShapeDtypeStruct((b, n_heads, seq_len), F32), jax.ShapeDtypeStruct((n, dg), BF16),
    ]
    return pl.pallas_call(
        functools.partial(_proj_body, d_pool=d_pool, d_att=d_att, n_heads=n_heads, q_scale=q_scale),
        grid=(n // tm,),
        in_specs=[pl.BlockSpec((tm, d), row), pl.BlockSpec((1, d), const),
                  pl.BlockSpec(wm.shape, const), pl.BlockSpec(wf.shape, const),
                  pl.BlockSpec(wg.shape, const), pl.BlockSpec((1, LANES), const)],
        out_specs=[pl.BlockSpec((tm, d_pool), row), pl.BlockSpec((tm, d_att), row),
                   pl.BlockSpec((1, d_att, tm), tmin), pl.BlockSpec((1, d_att, tm), tmin),
                   pl.BlockSpec((tm, d_att), row), pl.BlockSpec((tm, d_att), row),
                   pl.BlockSpec((1, n_heads, tm), tmin), pl.BlockSpec((tm, dg), row)],
        out_shape=out_shape,
        compiler_params=_params("arbitrary"),
        name="proj_prompt",
    )(x, g, wm, wf, wg, bfp)


def _dot_nt(a, bt, precise):
    dims = (((1,), (1,)), ((), ()))
    if precise:
        return lax.dot_general(a.astype(F32), bt.astype(F32), dims, precision=HIGHEST, preferred_element_type=F32)
    return lax.dot_general(a.astype(BF16), bt.astype(BF16), dims, preferred_element_type=F32)


def _proj_sample_body(x_ref, g_ref, wmt_ref, wft_ref, wgt_ref, bf_ref, z_ref, lf_ref, gate_ref):
    h = _rmsnorm(x_ref[...], g_ref[...])
    z_ref[...] = _dot_nt(h, wmt_ref[...], True)
    lf_ref[...] = _log_sigmoid(_dot_nt(h, wft_ref[...], True) + bf_ref[...])
    gate_ref[...] = _sigmoid(_dot_nt(h, wgt_ref[...], True))


def _proj_sample(x, g, wmt, wft, wgt, bfp, *, tn):
    n, d = x.shape
    dm, dg = wmt.shape[0], wgt.shape[0]
    assert dm == dg
    const = lambda j: (0, 0)
    chunk = lambda j: (j, 0)
    col = lambda j: (0, j)
    return pl.pallas_call(
        _proj_sample_body,
        grid=(dm // tn,),
        in_specs=[pl.BlockSpec((n, d), const), pl.BlockSpec((1, d), const),
                  pl.BlockSpec((tn, d), chunk), pl.BlockSpec(wft.shape, const),
                  pl.BlockSpec((tn, d), chunk), pl.BlockSpec((1, LANES), const)],
        out_specs=[pl.BlockSpec((n, tn), col), pl.BlockSpec((n, LANES), const), pl.BlockSpec((n, tn), col)],
        out_shape=[jax.ShapeDtypeStruct((n, dm), F32), jax.ShapeDtypeStruct((n, LANES), F32),
                   jax.ShapeDtypeStruct((n, dg), F32)],
        compiler_params=_params("arbitrary"),
        name="proj_sample",
    )(x, g, wmt, wft, wgt, bfp)


def _cumsum_body(x_ref, o_ref):
    c = x_ref[...]
    lane = lax.broadcasted_iota(I32, c.shape, 1)
    s = 1
    while s < c.shape[1]:
        c = c + jnp.where(lane >= s, pltpu.roll(c, s, 1), 0.0)
        s *= 2
    o_ref[...] = c


def _cumsum_lanes(x):
    return pl.pallas_call(_cumsum_body, out_shape=jax.ShapeDtypeStruct(x.shape, F32),
                          compiler_params=_params(), name="cumsum_logf")(x)


def _prompt_q_tile(qi, q_ref, kts, vhs, c_ref, o_ref, *, tile, dh, first, causal):
    q = q_ref[0, qi * tile:(qi + 1) * tile, :]
    zero = jnp.zeros_like(q)
    q_heads = (jnp.where(first, q, zero), jnp.where(first, zero, q))
    res = []
    for h in range(2):
        m = jnp.full((tile, 1), -1e30, F32)
        acc = jnp.zeros((tile, 2 * dh), F32)
        for kj in range(qi + 1):
            s = lax.dot_general(q_heads[h], kts[kj], (((1,), (1,)), ((), ())), preferred_element_type=F32)
            s = s - c_ref[0, 0, kj][h:h + 1, :]
            if kj == qi:
                s = jnp.where(causal, s, -jnp.inf)
            m_new = jnp.maximum(m, jnp.max(s, axis=-1, keepdims=True))
            alpha = jnp.exp(m - m_new)
            p = jnp.exp(s - m_new)
            acc = alpha * acc + jnp.dot(p.astype(BF16), vhs[kj][h], preferred_element_type=F32)
            m = m_new
        res.append(acc)
    a0, a1 = res
    out = jnp.where(first, a0 / a0[:, dh:dh + 1], a1 / a1[:, 0:1])
    o_ref[0, qi * tile:(qi + 1) * tile, :] = out.astype(o_ref.dtype)


def _sample_chunk(k_refs, v_refs, lf_refs, qrep, carry):
    m_prev, l, acc, s_run = carry
    n_heads, dh, page = acc.shape
    d_att = n_heads * dh
    g_n = len(k_refs)
    r = lax.broadcasted_iota(I32, (page, page), 0)
    c = lax.broadcasted_iota(I32, (page, page), 1)
    later = (r > c).astype(BF16)
    ones = jnp.ones((page, page), BF16)
    lf_all = jnp.concatenate([lf_refs[g][...] for g in range(g_n)], axis=0)
    suffix = _dot_exact_rhs(lf_all, later)
    total = _dot_exact_rhs(lf_all, ones)
    m_new = m_prev
    scores = []
    for g in range(g_n):
        kq = k_refs[g][...].reshape(d_att, page) * qrep
        s = jnp.sum(kq.reshape(n_heads, dh, page), axis=1)
        sb = s + s_run + suffix[g * n_heads:(g + 1) * n_heads]
        s_run = s_run + total[g * n_heads:(g + 1) * n_heads]
        scores.append(sb)
        m_new = jnp.maximum(m_new, jnp.max(sb, axis=-1, keepdims=True))
    alpha = jnp.exp(m_prev - m_new)
    l = alpha * l
    acc = acc * alpha[:, None, :]
    for g in range(g_n):
        p = jnp.exp(scores[g] - m_new)
        l = l + jnp.sum(p, axis=-1, keepdims=True)
        acc = acc + v_refs[g][...] * p[:, None, :]
    return m_new, l, acc, s_run


def _attn_body(pt_ref, q_ref, k_ref, v_ref, c_ref, qrep_ref, qs_ref, kn_ref, vrep_ref, lfn_ref,
               ck_hbm, cv_hbm, clf_hbm, o_ref, os_ref, kbuf, vbuf, lfbuf, sems, *, tile, dh):
    step = pl.program_id(0) * pl.num_programs(1) + pl.program_id(1)
    n_steps = pl.num_programs(0) * pl.num_programs(1)
    nt = q_ref.shape[1] // tile
    n_pages = pt_ref.shape[1]
    _, g_n, n_heads, _, page = kbuf.shape
    d_att = n_heads * dh
    seqs = qrep_ref.shape[0]
    n_chunks = n_pages // g_n
    n_items = seqs * n_chunks

    def fetch(slot, seq, chunk):
        for g in range(g_n):
            pid = pt_ref[seq, n_pages - 1 - (chunk * g_n + g)]
            pltpu.make_async_copy(ck_hbm.at[pid], kbuf.at[slot, g], sems.at[slot, 0]).start()
            pltpu.make_async_copy(cv_hbm.at[pid], vbuf.at[slot, g], sems.at[slot, 1]).start()
            pltpu.make_async_copy(clf_hbm.at[pid], lfbuf.at[slot, g], sems.at[slot, 2]).start()

    @pl.when(step == 0)
    def _():
        fetch(0, 0, 0)

    lane = lax.broadcasted_iota(I32, (tile, 2 * dh), 1)
    first = lane < dh
    row = lax.broadcasted_iota(I32, (tile, tile), 0)
    col = lax.broadcasted_iota(I32, (tile, tile), 1)
    causal = col <= row
    one = jnp.ones((tile, 2 * dh), BF16)
    kts, vhs = [], []
    for kj in range(nt):
        vt = v_ref[0, kj * tile:(kj + 1) * tile, :]
        kts.append(k_ref[0, kj * tile:(kj + 1) * tile, :])
        vhs.append((jnp.where(first, vt, one), jnp.where(first, one, vt)))

    q_done = 0
    carry = None
    for item in range(n_items):
        j, chunk = divmod(item, n_chunks)
        slot = item % 2
        if item + 1 < n_items:
            fetch(1 - slot, step * seqs + (item + 1) // n_chunks, (item + 1) % n_chunks)
        else:
            @pl.when(step + 1 < n_steps)
            def _():
                fetch(1 - slot, (step + 1) * seqs, 0)
        pltpu.make_async_copy(ck_hbm.at[pl.ds(0, g_n)], kbuf.at[slot], sems.at[slot, 0]).wait()
        pltpu.make_async_copy(cv_hbm.at[pl.ds(0, g_n)], vbuf.at[slot], sems.at[slot, 1]).wait()
        pltpu.make_async_copy(clf_hbm.at[pl.ds(0, g_n)], lfbuf.at[slot], sems.at[slot, 2]).wait()
        if chunk == 0:
            s_new = jnp.sum(qs_ref[j] * kn_ref[j], axis=-1, keepdims=True)
            lane_p = lax.broadcasted_iota(I32, (d_att, page), 1)
            carry = (jnp.broadcast_to(s_new, (n_heads, page)), jnp.ones((n_heads, page), F32),
                     jnp.where(lane_p == 0, vrep_ref[j], 0.0).reshape(n_heads, dh, page), lfn_ref[j])
        carry = _sample_chunk([kbuf.at[slot, g] for g in range(g_n)], [vbuf.at[slot, g] for g in range(g_n)],
                              [lfbuf.at[slot, g] for g in range(g_n)], qrep_ref[j], carry)
        if chunk == n_chunks - 1:
            _, l, acc, _ = carry
            os_ref[j] = jnp.sum(acc / l[:, None, :], axis=-1)
        q_until = ((item + 1) * nt) // n_items
        for qi in range(q_done, q_until):
            _prompt_q_tile(qi, q_ref, kts, vhs, c_ref, o_ref, tile=tile, dh=dh, first=first, causal=causal)
        q_done = q_until


def _attention(q, k, v, c, page_table, q_s, k_new, v_new, lf_new, cache_kt, cache_vt, cache_lft, *, tile, dh):
    b, t, da = q.shape
    hp = da // (2 * dh)
    nt = t // tile
    db, n_pages = page_table.shape
    _, n_heads, _, page = cache_kt.shape
    d_att = n_heads * dh
    n_steps = b * hp
    assert db % n_steps == 0, "sample sequences are split evenly over the prompt grid steps"
    seqs = db // n_steps
    g_n = PAGES_PER_STEP
    while n_pages % g_n:
        g_n //= 2
    assert (seqs * (n_pages // g_n)) % 2 == 0, "buffer slots alternate per page chunk"
    lane_rep = lambda a: jnp.broadcast_to(a.reshape(db, -1, 1), (db, a.size // db, page))
    pair = pl.BlockSpec((1, t, 2 * dh), lambda bi, hi, pt: (bi, 0, hi))
    per_step = lambda bi, hi, pt: (bi * hp + hi, 0, 0)
    hbm = pl.BlockSpec(memory_space=pl.ANY)
    grid_spec = pltpu.PrefetchScalarGridSpec(
        num_scalar_prefetch=1, grid=(b, hp),
        in_specs=[pair, pair, pair, pl.BlockSpec((1, 1, nt, 2, tile), lambda bi, hi, pt: (bi, hi, 0, 0, 0)),
                  pl.BlockSpec((seqs, d_att, page), per_step), pl.BlockSpec((seqs, n_heads, dh), per_step),
                  pl.BlockSpec((seqs, n_heads, dh), per_step), pl.BlockSpec((seqs, d_att, page), per_step),
                  pl.BlockSpec((seqs, n_heads, page), per_step), hbm, hbm, hbm],
        out_specs=[pair, pl.BlockSpec((seqs, n_heads, dh), per_step)],
        scratch_shapes=[pltpu.VMEM((2, g_n, n_heads, dh, page), F32), pltpu.VMEM((2, g_n, n_heads, dh, page), F32),
                        pltpu.VMEM((2, g_n, n_heads, page), F32), pltpu.SemaphoreType.DMA((2, 3))])
    return pl.pallas_call(
        functools.partial(_attn_body, tile=tile, dh=dh),
        grid_spec=grid_spec,
        out_shape=[jax.ShapeDtypeStruct((b, t, da), BF16), jax.ShapeDtypeStruct((db, n_heads, dh), F32)],
        compiler_params=_params("arbitrary", "arbitrary"),
        name="attention",
    )(page_table, q, k, v, c, lane_rep(q_s), q_s.reshape(db, n_heads, dh), k_new.reshape(db, n_heads, dh),
      lane_rep(v_new), lane_rep(lf_new), cache_kt, cache_vt, cache_lft)


def _merge_and_route(x, pooled, att, gates, wp_ref, ps_ref, wup_ref, wua_ref, wo_ref, nf_ref, wr_ref, br_ref,
                     *, precise, n_groups, n_per_group):
    tm, d = x.shape
    mixed = jnp.concatenate([_dot(pooled[g], wp_ref[g], precise) for g in range(len(pooled))], axis=-1)
    pool_out = mixed * ps_ref[...]
    y = gates[:, :d].astype(F32) * _dot(pool_out, wup_ref[...], precise) \
        + gates[:, d:].astype(F32) * _dot(att, wua_ref[...], precise)
    x2 = x + _dot(y, wo_ref[...], precise)
    h2 = _rmsnorm(x2, nf_ref[...])
    logits = _dot(h2, wr_ref[...], precise) + br_ref[...]
    lane = lax.broadcasted_iota(I32, logits.shape, 1)
    lanef = lane.astype(F32)
    neg = -jnp.inf
    is_g = lane < n_groups
    gmax = jnp.max(jnp.where(is_g, logits, neg), axis=-1, keepdims=True)
    gidx = jnp.min(jnp.where(is_g & (logits == gmax), lanef, float(LANES)), axis=-1, keepdims=True)
    gsum = jnp.sum(jnp.where(is_g, jnp.exp(logits - gmax), 0.0), axis=-1, keepdims=True)
    g_w = 1.0 / gsum
    n_exp = n_groups * n_per_group
    exp_id = lanef - float(n_groups)
    in_sel = (lane >= n_groups) & (lane < n_groups + n_exp) & (jnp.floor(exp_id / n_per_group) == gidx)
    v1 = jnp.max(jnp.where(in_sel, logits, neg), axis=-1, keepdims=True)
    i1 = jnp.min(jnp.where(in_sel & (logits == v1), lanef, float(LANES)), axis=-1, keepdims=True)
    in_sel2 = in_sel & (lanef != i1)
    v2 = jnp.max(jnp.where(in_sel2, logits, neg), axis=-1, keepdims=True)
    i2 = jnp.min(jnp.where(in_sel2 & (logits == v2), lanef, float(LANES)), axis=-1, keepdims=True)
    t = jnp.exp(v2 - v1)
    w1 = g_w * (1.0 / (1.0 + t))
    w2 = g_w * (t / (1.0 + t))
    e1 = i1 - float(n_groups)
    e2 = i2 - float(n_groups)
    hit1 = lanef == e1
    hit2 = lanef == e2
    onehot = (hit1 | hit2).astype(BF16)
    rr = lax.broadcasted_iota(I32, (tm, tm), 0)
    cc = lax.broadcasted_iota(I32, (tm, tm), 1)
    incl = jnp.dot((cc <= rr).astype(BF16), onehot, preferred_element_type=F32)
    counts = incl[tm - 1:tm, :]
    groups = jnp.floor((counts + (ROW_GROUP - 1.0)) * (1.0 / ROW_GROUP))
    ur = lax.broadcasted_iota(I32, (LANES, LANES), 0)
    uc = lax.broadcasted_iota(I32, (LANES, LANES), 1)
    before = jnp.dot(jnp.broadcast_to(groups, (8, LANES)).astype(BF16), (ur < uc).astype(BF16),
                     preferred_element_type=F32)[0:1]
    seg_start = before * float(ROW_GROUP)
    pick = lambda hit, tbl: jnp.sum(jnp.where(hit, tbl, 0.0), axis=-1, keepdims=True)
    r1 = pick(hit1, incl) - 1.0
    r2 = pick(hit2, incl) - 1.0
    row1 = pick(hit1, seg_start) + r1
    row2 = pick(hit2, seg_start) + r2
    slab = jnp.zeros((tm, LANES), F32)
    for i, val in enumerate((e1, e2, r1, r2, w1, w2, row1, row2)):
        slab = jnp.where(lane == i, val, slab)
    return x2, h2, slab, counts


def _sorted_copy(h2, row1, row2, n_rows):
    tm = h2.shape[0]
    r = lax.broadcasted_iota(I32, (n_rows, tm), 0).astype(F32)
    place = ((r == row1) | (r == row2)).astype(BF16)
    return jnp.dot(place, h2.astype(BF16), preferred_element_type=F32).astype(BF16)


def _merge_prompt_body(x_ref, u_ref, halo_ref, att_ref, gate_ref, wp_ref, ps_ref, wup_ref, wua_ref, wo_ref,
                       nf_ref, wr_ref, br_ref, xs_last_hbm, x2_ref, xs_ref, slab_ref, counts_ref,
                       ext_ref, sem, *, seq_len, n_groups, n_per_group):
    i = pl.program_id(0)
    n_tiles = pl.num_programs(0) - 1
    tm = x_ref.shape[0]
    gw = u_ref.shape[1] // len(POOL_WINDOWS)

    @pl.when(i < n_tiles)
    def _():
        pos0 = (i * tm) % seq_len
        u = u_ref[...]
        ext_ref[0:POOL_HALO, :] = jnp.where(pos0 == 0, 0.0, halo_ref[...])
        ext_ref[POOL_HALO:, :] = u
        pos = pos0 + lax.broadcasted_iota(I32, (tm, 1), 0)
        pooled = []
        for g, w in enumerate(POOL_WINDOWS):
            lo = g * gw
            wsum = ext_ref[pl.ds(POOL_HALO, tm), lo:lo + gw]
            for j in range(1, w):
                wsum = wsum + ext_ref[pl.ds(POOL_HALO - j, tm), lo:lo + gw]
            count = jnp.minimum(pos + 1, w).astype(F32)
            pooled.append(wsum / count - u[:, lo:lo + gw])
        x2, h2, slab, counts = _merge_and_route(
            x_ref[...], pooled, att_ref[...], gate_ref[...], wp_ref, ps_ref, wup_ref, wua_ref, wo_ref, nf_ref,
            wr_ref, br_ref, precise=False, n_groups=n_groups, n_per_group=n_per_group)
        fields = slab.T[0:8, :]
        x2_ref[...] = x2
        xs_ref[...] = _sorted_copy(h2, fields[6:7, :], fields[7:8, :], xs_ref.shape[0])
        slab_ref[...] = slab
        counts_ref[0] = counts

    @pl.when(i == n_tiles)
    def _():
        copy = pltpu.make_async_copy(xs_last_hbm, xs_ref, sem)
        copy.start()
        copy.wait()


def _merge_prompt(x, u, att, gates, wp, ps, wup, wua, wo, nf, wr, br, xs_last, *, tm, seq_len, n_groups, n_per_group):
    n, d = x.shape
    d_pool, d_att = u.shape[1], att.shape[1]
    tile_rows = xs_last.shape[0]
    nt = n // tm
    clamp = lambda i: jnp.minimum(i, nt - 1)
    row = lambda i: (clamp(i), 0)
    const = lambda i: (0, 0)
    const3 = lambda i: (0, 0, 0)
    halo = lambda i: (jnp.maximum(clamp(i) * (tm // POOL_HALO) - 1, 0), 0)
    return pl.pallas_call(
        functools.partial(_merge_prompt_body, seq_len=seq_len, n_groups=n_groups, n_per_group=n_per_group),
        grid=(nt + 1,),
        in_specs=[pl.BlockSpec((tm, d), row), pl.BlockSpec((tm, d_pool), row), pl.BlockSpec((POOL_HALO, d_pool), halo),
                  pl.BlockSpec((tm, d_att), row), pl.BlockSpec((tm, 2 * d), row),
                  pl.BlockSpec(wp.shape, const3), pl.BlockSpec((1, d_pool), const),
                  pl.BlockSpec(wup.shape, const), pl.BlockSpec(wua.shape, const), pl.BlockSpec(wo.shape, const),
                  pl.BlockSpec((1, d), const), pl.BlockSpec(wr.shape, const), pl.BlockSpec((1, LANES), const),
                  pl.BlockSpec(memory_space=pl.ANY)],
        out_specs=[pl.BlockSpec((tm, d), row), pl.BlockSpec((tile_rows, d), lambda i: (i, 0)),
                   pl.BlockSpec((tm, LANES), row), pl.BlockSpec((1, 1, LANES), lambda i: (clamp(i), 0, 0))],
        out_shape=[jax.ShapeDtypeStruct((n, d), F32), jax.ShapeDtypeStruct(((nt + 1) * tile_rows, d), BF16),
                   jax.ShapeDtypeStruct((n, LANES), F32), jax.ShapeDtypeStruct((nt, 1, LANES), F32)],
        scratch_shapes=[pltpu.VMEM((tm + POOL_HALO, d_pool), F32), pltpu.SemaphoreType.DMA],
        compiler_params=_params("arbitrary"),
        name="merge_prompt",
    )(x, u, u, att, gates, wp, ps, wup, wua, wo, nf, wr, br, xs_last)


def _merge_sample_body(x_ref, u_ref, st_ref, att_ref, gate_ref, wp_ref, ps_ref, wup_ref, wua_ref, wo_ref,
                       nf_ref, wr_ref, br_ref, x2_ref, xs_ref, slab_ref, counts_ref,
                       *, start_pos, n_groups, n_per_group):
    u = u_ref[...]
    gw = u.shape[1] // len(POOL_WINDOWS)
    n_state = st_ref.shape[0]
    pooled = []
    for g, w in enumerate(POOL_WINDOWS):
        lo = g * gw
        wsum = u[:, lo:lo + gw]
        for j in range(1, w):
            wsum = wsum + st_ref[n_state - j][:, lo:lo + gw]
        pooled.append(wsum / float(min(start_pos + 1, w)) - u[:, lo:lo + gw])
    x2, h2, slab, counts = _merge_and_route(
        x_ref[...], pooled, att_ref[...], gate_ref[...], wp_ref, ps_ref, wup_ref, wua_ref, wo_ref, nf_ref,
        wr_ref, br_ref, precise=True, n_groups=n_groups, n_per_group=n_per_group)
    n = slab.shape[0]
    fields = jnp.concatenate([slab, jnp.zeros((LANES - n, LANES), F32)], axis=0).T
    x2_ref[...] = x2
    xs_ref[...] = _sorted_copy(h2, fields[6:7, 0:n], fields[7:8, 0:n], xs_ref.shape[0])
    slab_ref[...] = slab
    counts_ref[0] = counts


def _merge_sample(x, u, state_t, att, gates, wp, ps, wup, wua, wo, nf, wr, br, *, start_pos, n_groups,
                  n_per_group, tile_rows):
    n, d = x.shape
    assert n <= LANES
    return pl.pallas_call(
        functools.partial(_merge_sample_body, start_pos=start_pos, n_groups=n_groups, n_per_group=n_per_group),
        out_shape=[jax.ShapeDtypeStruct((n, d), F32), jax.ShapeDtypeStruct((tile_rows, d), BF16),
                   jax.ShapeDtypeStruct((n, LANES), F32), jax.ShapeDtypeStruct((1, 1, LANES), F32)],
        compiler_params=_params(),
        name="merge_sample",
    )(x, u, state_t, att, gates, wp, ps, wup, wua, wo, nf, wr, br)


def _moe_plan_body(cnt_ref, gsrc_ref, lsrc_ref, te_ref, tw_ref, loc_ref, *, n_ttiles, n_exp, tile_rows, gpt,
                   seg_groups_max, local_stride, zero_group):
    n_mm = te_ref.shape[0]
    gsrc_ref[...] = jnp.full(gsrc_ref.shape, zero_group, I32)
    lsrc_ref[...] = jnp.zeros(lsrc_ref.shape, I32)

    def fill_tiles(t, c):
        te_ref[t] = -1
        tw_ref[t] = n_exp - 1
        return c

    def fill_loc(i, c):
        loc_ref[i] = 0
        return c

    lax.fori_loop(0, n_mm, fill_tiles, 0)
    lax.fori_loop(0, n_ttiles, fill_loc, 0)
    step = lax.broadcasted_iota(I32, (seg_groups_max, LANES), 0)

    def per_expert(e, pos):
        def per_tile(i, p):
            g = (cnt_ref[i * n_exp + e] + (ROW_GROUP - 1)) // ROW_GROUP
            loc = loc_ref[i]
            gsrc_ref[pl.ds(p, seg_groups_max), :] = i * tile_rows + (loc + step) * ROW_GROUP
            lsrc_ref[pl.ds(i * local_stride + loc, seg_groups_max), :] = p + step
            loc_ref[i] = loc + g
            return p + g

        end = lax.fori_loop(0, n_ttiles, per_tile, pos)
        end_pad = ((end + (gpt - 1)) // gpt) * gpt
        gsrc_ref[pl.ds(end, seg_groups_max), :] = jnp.full((seg_groups_max, LANES), zero_group, I32)

        def mark(t, c):
            te_ref[t] = e
            tw_ref[t] = e
            return c

        lax.fori_loop(pos // gpt, end_pad // gpt, mark, 0)
        return end_pad

    lax.fori_loop(0, n_exp, per_expert, 0)

    def clear_tail(i, c):
        lsrc_ref[pl.ds(i * local_stride + loc_ref[i], seg_groups_max), :] = jnp.zeros((seg_groups_max, LANES), I32)
        return c

    lax.fori_loop(0, n_ttiles, clear_tail, 0)


def _moe_plan(cnt, *, n_ttiles, n_exp, tile_rows, gpt, n_mm_tiles, seg_groups_max):
    smem = pl.BlockSpec(memory_space=pltpu.SMEM)
    zero_group = n_ttiles * tile_rows - ROW_GROUP
    local_groups = tile_rows // ROW_GROUP
    local_stride = local_groups + seg_groups_max
    n_groups = n_mm_tiles * gpt
    gsrc, lsrc, te, tw = pl.pallas_call(
        functools.partial(_moe_plan_body, n_ttiles=n_ttiles, n_exp=n_exp, tile_rows=tile_rows, gpt=gpt,
                          seg_groups_max=seg_groups_max, local_stride=local_stride, zero_group=zero_group),
        in_specs=[smem], out_specs=[pl.BlockSpec(memory_space=pltpu.VMEM), pl.BlockSpec(memory_space=pltpu.VMEM),
                                    smem, smem],
        out_shape=[jax.ShapeDtypeStruct((n_groups + 2 * seg_groups_max, LANES), I32),
                   jax.ShapeDtypeStruct((n_ttiles * local_stride, LANES), I32),
                   jax.ShapeDtypeStruct((n_mm_tiles,), I32), jax.ShapeDtypeStruct((n_mm_tiles,), I32)],
        scratch_shapes=[pltpu.SMEM((n_ttiles,), I32)],
        name="moe_plan",
    )(cnt)
    return gsrc[:n_groups, 0], lsrc[:, 0], te, tw, local_stride


def _moe_mm_body(te_ref, tw_ref, src_ref, xs_hbm, wg_ref, wu_ref, wd_ref, ys_ref, xbuf, wgb_ref, wub_ref, wdb_ref,
                 sems):
    i = pl.program_id(0)
    tm = ys_ref.shape[0]
    groups = tm // ROW_GROUP
    expert = te_ref[i]
    prev = te_ref[jnp.maximum(i - 1, 0)]

    def fetch(slot, tile):
        for k in range(groups):
            src = pl.multiple_of(src_ref[tile * groups + k], ROW_GROUP)
            pltpu.make_async_copy(xs_hbm.at[pl.ds(src, ROW_GROUP)],
                                  xbuf.at[slot, pl.ds(k * ROW_GROUP, ROW_GROUP)], sems.at[slot]).start()

    slot = i % 2

    @pl.when(i == 0)
    def _():
        fetch(0, 0)

    @pl.when(i + 1 < pl.num_programs(0))
    def _():
        fetch(1 - slot, i + 1)

    pltpu.make_async_copy(xs_hbm.at[pl.ds(0, tm)], xbuf.at[slot], sems.at[slot]).wait()

    @pl.when((expert >= 0) & ((i == 0) | (expert != prev)))
    def _():
        wgb_ref[...] = wg_ref[0].astype(BF16)
        wub_ref[...] = wu_ref[0].astype(BF16)
        wdb_ref[...] = wd_ref[0].astype(BF16)

    @pl.when(expert >= 0)
    def _():
        x = xbuf[slot]
        a = jnp.dot(x, wgb_ref[...], preferred_element_type=F32)
        b = jnp.dot(x, wub_ref[...], preferred_element_type=F32)
        hdn = (a * _sigmoid(a)) * b
        ys_ref[...] = jnp.dot(hdn.astype(BF16), wdb_ref[...], preferred_element_type=F32).astype(ys_ref.dtype)

    @pl.when(expert < 0)
    def _():
        ys_ref[...] = jnp.zeros_like(ys_ref)


def _moe_mm(tile_expert, tile_weight, group_src, xs, w_gate, w_up, w_down, *, tm):
    n_tiles = tile_expert.shape[0]
    d = xs.shape[1]
    n_exp, _, de = w_gate.shape
    wmap = lambda i, te, tw, src: (tw[i], 0, 0)
    grid_spec = pltpu.PrefetchScalarGridSpec(
        num_scalar_prefetch=3, grid=(n_tiles,),
        in_specs=[pl.BlockSpec(memory_space=pl.ANY),
                  pl.BlockSpec((1, d, de), wmap), pl.BlockSpec((1, d, de), wmap), pl.BlockSpec((1, de, d), wmap)],
        out_specs=pl.BlockSpec((tm, d), lambda i, te, tw, src: (i, 0)),
        scratch_shapes=[pltpu.VMEM((2, tm, d), BF16), pltpu.VMEM((d, de), BF16), pltpu.VMEM((d, de), BF16),
                        pltpu.VMEM((de, d), BF16), pltpu.SemaphoreType.DMA((2,))])
    return pl.pallas_call(
        _moe_mm_body, grid_spec=grid_spec,
        out_shape=jax.ShapeDtypeStruct((n_tiles * tm, d), BF16),
        compiler_params=_params("arbitrary"),
        name="moe_mm",
    )(tile_expert, tile_weight, group_src, xs, w_gate, w_up, w_down)


def _moe_combine_body(lsrc_ref, x_ref, slab_ref, g_ref, ys_hbm, o_ref, ybuf, sems,
                      *, final_norm, first_tile, local_stride):
    i = pl.program_id(0)
    tm = x_ref.shape[0]
    tile_rows = ybuf.shape[1]
    unroll = 8

    def fetch(slot, tile):
        def body(c, carry):
            for j in range(unroll):
                lg = c * unroll + j
                src = pl.multiple_of(lsrc_ref[tile * local_stride + lg] * ROW_GROUP, ROW_GROUP)
                dst = pl.multiple_of(lg * ROW_GROUP, ROW_GROUP)
                pltpu.make_async_copy(ys_hbm.at[pl.ds(src, ROW_GROUP)], ybuf.at[slot, pl.ds(dst, ROW_GROUP)],
                                      sems.at[slot]).start()
            return carry

        lax.fori_loop(0, tile_rows // ROW_GROUP // unroll, body, 0)

    slot = i % 2

    @pl.when(i == 0)
    def _():
        fetch(0, first_tile)

    @pl.when(i + 1 < pl.num_programs(0))
    def _():
        fetch(1 - slot, first_tile + i + 1)

    pltpu.make_async_copy(ys_hbm.at[pl.ds(0, tile_rows)], ybuf.at[slot], sems.at[slot]).wait()
    y = ybuf[slot]
    slab = slab_ref[...]
    r = lax.broadcasted_iota(I32, (tm, tile_rows), 1).astype(F32)
    ya = jnp.dot((r == slab[:, 6:7]).astype(BF16), y, preferred_element_type=F32)
    yb = jnp.dot((r == slab[:, 7:8]).astype(BF16), y, preferred_element_type=F32)
    out = x_ref[...] + (slab[:, 4:5] * ya + slab[:, 5:6] * yb)
    if final_norm:
        out = _rmsnorm(out, g_ref[...])
    o_ref[...] = out


def _moe_combine(local_src, x, slab, g, ys, *, ts, tile_rows, final_norm, first_tile, local_stride):
    n, d = x.shape
    return pl.pallas_call(
        functools.partial(_moe_combine_body, final_norm=final_norm, first_tile=first_tile,
                          local_stride=local_stride),
        grid=(n // ts,),
        in_specs=[pl.BlockSpec(memory_space=pltpu.SMEM),
                  pl.BlockSpec((ts, d), lambda i: (i, 0)), pl.BlockSpec((ts, LANES), lambda i: (i, 0)),
                  pl.BlockSpec((1, d), lambda i: (0, 0)), pl.BlockSpec(memory_space=pl.ANY)],
        out_specs=pl.BlockSpec((ts, d), lambda i: (i, 0)),
        out_shape=jax.ShapeDtypeStruct((n, d), F32),
        scratch_shapes=[pltpu.VMEM((2, tile_rows, d), BF16), pltpu.SemaphoreType.DMA((2,))],
        compiler_params=_params("arbitrary"),
        name="moe_combine",
    )(local_src, x, slab, g, ys)


def kernel(x_prompt, x_sample, cache_k, cache_v, cache_logf, state_pool, page_table, norm_mix, w_in, b_forget,
           w_pool, pool_scale, w_up_pool, w_up_att, w_out, norm_ffn, w_router_group, b_router_group,
           w_router_expert, b_router_expert, w_gate, w_up, w_down, norm_final):
    depth = norm_mix.shape[0]
    assert depth == 1, "single trunk layer"
    b, t, d = x_prompt.shape
    db, dt, _ = x_sample.shape
    assert dt == 1, "one sample token per sequence"
    _, _, page, n_heads, dh = cache_k.shape
    n_pages = page_table.shape[1]
    past = n_pages * page
    n_state, d_pool = state_pool.shape[2], state_pool.shape[3]
    d_att = n_heads * dh
    n_pool_groups = w_pool.shape[1]
    assert n_pool_groups == len(POOL_WINDOWS) and d_pool // n_pool_groups == LANES
    assert n_state == max(POOL_WINDOWS) - 1 and n_state < POOL_HALO
    n_groups, n_per_group = w_router_expert.shape[1], w_router_expert.shape[3]
    n_exp = n_groups * n_per_group
    assert n_groups + n_exp <= LANES and 2 * dh == LANES and n_heads % 2 == 0
    n = b * t
    q_scale = float(dh) ** -0.5
    tm = min(TOKEN_TILE, t)
    assert t % tm == 0 and t % ATTN_TILE == 0

    o_main = d_pool + 3 * d_att
    wi = w_in[0]
    wm_f, wf_f, wg_f = wi[:, :o_main], wi[:, o_main:o_main + n_heads], wi[:, o_main + n_heads:]
    wf_pad = jnp.pad(wf_f, ((0, 0), (0, LANES - n_heads)))
    wit = jnp.transpose(wi)
    wmt_f, wgt_f = wit[:o_main], wit[o_main + n_heads:]
    wft_pad = jnp.pad(wit[o_main:o_main + n_heads], ((0, LANES - n_heads), (0, 0)))
    bf_pad = jnp.pad(b_forget[0], (0, LANES - n_heads)).reshape(1, LANES)
    g_mix = norm_mix[0].reshape(1, d)
    g_ffn = norm_ffn[0].reshape(1, d)
    g_fin = norm_final.reshape(1, d)
    ps = pool_scale[0].reshape(1, d_pool)
    wr_f = jnp.concatenate([w_router_group[0], jnp.transpose(w_router_expert[0], (1, 0, 2)).reshape(d, n_exp)], axis=1)
    wr_pad = jnp.pad(wr_f, ((0, 0), (0, LANES - n_groups - n_exp)))
    br_pad = jnp.pad(jnp.concatenate([b_router_group[0], b_router_expert[0].reshape(n_exp)]),
                     (0, LANES - n_groups - n_exp)).reshape(1, LANES)
    bf = lambda a: a.astype(BF16)

    xp = x_prompt.reshape(n, d)
    u_p, q_p, kt_p, vt_p, kb_p, vb_p, lft_p, gate_p = _proj_prompt(
        xp, g_mix, bf(wm_f), bf(wf_pad), bf(wg_f), bf_pad, tm=tm, seq_len=t, d_pool=d_pool, d_att=d_att,
        n_heads=n_heads, q_scale=q_scale)
    c = _cumsum_lanes(lft_p.reshape(b * n_heads, t))
    nt = t // ATTN_TILE
    c_blk = jnp.transpose(c.reshape(b, n_heads // 2, 2, nt, ATTN_TILE), (0, 1, 3, 2, 4))
    n_ptiles = n // tm
    n_ttiles = n_ptiles + 1
    tile_rows = -(-(2 * tm + n_exp * (ROW_GROUP - 1)) // MOE_ROW_TILE) * MOE_ROW_TILE
    assert 2 * db + n_exp * (ROW_GROUP - 1) <= tile_rows - ROW_GROUP, "the sample tile must end in an unused row group"

    xs = x_sample.reshape(db, d)
    z_s, lf_s, gate_s = _proj_sample(xs, g_mix, wmt_f, wft_pad, wgt_f, bf_pad, tn=512)
    u_s = z_s[:, :d_pool]
    q_s = z_s[:, d_pool:d_pool + d_att] * q_scale
    k_s = z_s[:, d_pool + d_att:d_pool + 2 * d_att]
    v_s = z_s[:, d_pool + 2 * d_att:]
    att_p, att_s = _attention(q_p.reshape(b, t, d_att), kb_p.reshape(b, t, d_att), vb_p.reshape(b, t, d_att), c_blk,
                              page_table, q_s, k_s, v_s, lf_s[:, :n_heads],
                              jnp.transpose(cache_k[0], (0, 2, 3, 1)), jnp.transpose(cache_v[0], (0, 2, 3, 1)),
                              jnp.transpose(cache_logf[0], (0, 2, 1)), tile=ATTN_TILE, dh=dh)
    state_t = jnp.transpose(state_pool[0], (1, 0, 2))
    x2_s, xs_rows_s, slab_s, counts_s = _merge_sample(
        xs, u_s, state_t, att_s.reshape(db, d_att), gate_s, w_pool[0], ps, w_up_pool[0], w_up_att[0], w_out[0],
        g_ffn, wr_pad, br_pad, start_pos=past, n_groups=n_groups, n_per_group=n_per_group, tile_rows=tile_rows)

    x2_p, xs_rows, slab_p, counts_p = _merge_prompt(
        xp, u_p, att_p.reshape(n, d_att), gate_p, bf(w_pool[0]), ps, bf(w_up_pool[0]), bf(w_up_att[0]),
        bf(w_out[0]), g_ffn, bf(wr_pad), br_pad, xs_rows_s, tm=tm, seq_len=t, n_groups=n_groups,
        n_per_group=n_per_group)

    tmm = MOE_ROW_TILE
    gpt = tmm // ROW_GROUP
    n_groups_max = -(-(2 * (n + db)) // ROW_GROUP) + n_ttiles * n_exp + n_exp * (gpt - 1)
    cnt = jnp.concatenate([counts_p, counts_s], axis=0)[:, 0, :n_exp].astype(I32).reshape(-1)
    group_src, local_src, tile_expert, tile_weight, local_stride = _moe_plan(
        cnt, n_ttiles=n_ttiles, n_exp=n_exp, tile_rows=tile_rows, gpt=gpt, n_mm_tiles=-(-n_groups_max // gpt),
        seg_groups_max=tm // ROW_GROUP)
    ys_rows = _moe_mm(tile_expert, tile_weight, group_src, xs_rows, w_gate[0], w_up[0], w_down[0], tm=tmm)
    y_prompt = _moe_combine(local_src, x2_p, slab_p, g_fin, ys_rows, ts=tm, tile_rows=tile_rows,
                            final_norm=True, first_tile=0, local_stride=local_stride)
    y_sample = _moe_combine(local_src, x2_s, slab_s, g_fin, ys_rows, ts=db, tile_rows=tile_rows,
                            final_norm=True, first_tile=n_ptiles, local_stride=local_stride)

    new_pool_p = u_p.reshape(b, t, d_pool)[:, t - n_state:, :]
    new_pool_s = jnp.concatenate([state_pool[0][:, 1:, :], u_s[:, None, :]], axis=1)
    to_heads = lambda a: jnp.transpose(a.reshape(b, n_heads, dh, t), (0, 3, 1, 2))[None]
    return (y_prompt.reshape(b, t, d), y_sample.reshape(db, 1, d),
            to_heads(kt_p), to_heads(vt_p), jnp.transpose(lft_p, (0, 2, 1))[None],
            new_pool_p[None],
            k_s.reshape(1, db, 1, n_heads, dh), v_s.reshape(1, db, 1, n_heads, dh),
            lf_s[:, :n_heads].reshape(1, db, 1, n_heads), new_pool_s[None])
```

```python
import functools

import jax
import jax.numpy as jnp
from jax import lax
from jax.experimental import pallas as pl
from jax.experimental.pallas import tpu as pltpu

F32 = jnp.float32
BF16 = jnp.bfloat16
I32 = jnp.int32
HIGHEST = lax.Precision.HIGHEST

RMS_EPS = 1e-6
POOL_WINDOWS = (2, 4, 8, 16)
POOL_HALO = 16
LANES = 128
VMEM_LIMIT_BYTES = 56 * 1024 * 1024

TOKEN_TILE = 512
ATTN_TILE = 512
MOE_ROW_TILE = 512
PAGES_PER_STEP = 16
ROW_GROUP = 16


def _params(*sem):
    return pltpu.CompilerParams(dimension_semantics=sem, vmem_limit_bytes=VMEM_LIMIT_BYTES)


def _rmsnorm(x, g):
    return x * lax.rsqrt(jnp.mean(x * x, axis=-1, keepdims=True) + RMS_EPS) * g


def _log_sigmoid(x):
    return jnp.minimum(x, 0.0) - jnp.log1p(jnp.exp(-jnp.abs(x)))


def _sigmoid(x):
    return 1.0 / (1.0 + jnp.exp(-x))


def _dot(a, b, precise):
    if precise:
        return jnp.dot(a.astype(F32), b.astype(F32), precision=HIGHEST, preferred_element_type=F32)
    return jnp.dot(a.astype(BF16), b.astype(BF16), preferred_element_type=F32)


def _split3(x):
    hi = x.astype(BF16)
    r = x - hi.astype(F32)
    mid = r.astype(BF16)
    lo = (r - mid.astype(F32)).astype(BF16)
    return hi, mid, lo


def _dot_exact_rhs(x, w_bf16):
    hi, mid, lo = _split3(x)
    d = lambda a: jnp.dot(a, w_bf16, preferred_element_type=F32)
    return d(hi) + d(mid) + d(lo)


def _proj_body(x_ref, g_ref, wm_ref, wf_ref, wg_ref, bf_ref,
               u_ref, q_ref, kt_ref, vt_ref, kb_ref, vb_ref, lft_ref, gate_ref, *, d_pool, d_att, n_heads, q_scale):
    h = _rmsnorm(x_ref[...], g_ref[...]).astype(BF16)
    z = jnp.dot(h, wm_ref[...], preferred_element_type=F32)
    o1, o2, o3 = d_pool, d_pool + d_att, d_pool + 2 * d_att
    u_ref[...] = z[:, :o1]
    q_ref[...] = (z[:, o1:o2] * q_scale).astype(BF16)
    k = z[:, o2:o3]
    v = z[:, o3:]
    kt_ref[0] = k.T
    vt_ref[0] = v.T
    kb_ref[...] = k.astype(BF16)
    vb_ref[...] = v.astype(BF16)
    lf = _log_sigmoid(jnp.dot(h, wf_ref[...], preferred_element_type=F32) + bf_ref[...])
    lft_ref[0] = lf.T[0:n_heads, :]
    gate_ref[...] = _sigmoid(jnp.dot(h, wg_ref[...], preferred_element_type=F32)).astype(BF16)


def _proj_prompt(x, g, wm, wf, wg, bfp, *, tm, seq_len, d_pool, d_att, n_heads, q_scale):
    n, d = x.shape
    b = n // seq_len
    tps = seq_len // tm
    row = lambda i: (i, 0)
    const = lambda i: (0, 0)
    tmin = lambda i: (i // tps, 0, i % tps)
    dg = wg.shape[1]
    out_shape = [
        jax.ShapeDtypeStruct((n, d_pool), F32), jax.ShapeDtypeStruct((n, d_att), BF16),
        jax.ShapeDtypeStruct((b, d_att, seq_len), F32), jax.ShapeDtypeStruct((b, d_att, seq_len), F32),
        jax.ShapeDtypeStruct((n, d_att), BF16), jax.ShapeDtypeStruct((n, d_att), BF16),
        jax.ShapeDtypeStruct((b, n_heads, seq_len), F32), jax.ShapeDtypeStruct((n, dg), BF16),
    ]
    return pl.pallas_call(
        functools.partial(_proj_body, d_pool=d_pool, d_att=d_att, n_heads=n_heads, q_scale=q_scale),
        grid=(n // tm,),
        in_specs=[pl.BlockSpec((tm, d), row), pl.BlockSpec((1, d), const),
                  pl.BlockSpec(wm.shape, const), pl.BlockSpec(wf.shape, const),
                  pl.BlockSpec(wg.shape, const), pl.BlockSpec((1, LANES), const)],
        out_specs=[pl.BlockSpec((tm, d_pool), row), pl.BlockSpec((tm, d_att), row),
                   pl.BlockSpec((1, d_att, tm), tmin), pl.BlockSpec((1, d_att, tm), tmin),
                   pl.BlockSpec((tm, d_att), row), pl.BlockSpec((tm, d_att), row),
                   pl.BlockSpec((1, n_heads, tm), tmin), pl.BlockSpec((tm, dg), row)],
        out_shape=out_shape,
        compiler_params=_params("arbitrary"),
        name="proj_prompt",
    )(x, g, wm, wf, wg, bfp)


def _dot_nt(a, bt, precise):
    dims = (((1,), (1,)), ((), ()))
    if precise:
        return lax.dot_general(a.astype(F32), bt.astype(F32), dims, precision=HIGHEST, preferred_element_type=F32)
    return lax.dot_general(a.astype(BF16), bt.astype(BF16), dims, preferred_element_type=F32)


def _proj_sample_body(x_ref, g_ref, wmt_ref, wft_ref, wgt_ref, bf_ref, z_ref, lf_ref, gate_ref):
    h = _rmsnorm(x_ref[...], g_ref[...])
    z_ref[...] = _dot_nt(h, wmt_ref[...], True)
    lf_ref[...] = _log_sigmoid(_dot_nt(h, wft_ref[...], True) + bf_ref[...])
    gate_ref[...] = _sigmoid(_dot_nt(h, wgt_ref[...], True))


def _proj_sample(x, g, wmt, wft, wgt, bfp, *, tn):
    n, d = x.shape
    dm, dg = wmt.shape[0], wgt.shape[0]
    assert dm == dg
    const = lambda j: (0, 0)
    chunk = lambda j: (j, 0)
    col = lambda j: (0, j)
    return pl.pallas_call(
        _proj_sample_body,
        grid=(dm // tn,),
        in_specs=[pl.BlockSpec((n, d), const), pl.BlockSpec((1, d), const),
                  pl.BlockSpec((tn, d), chunk), pl.BlockSpec(wft.shape, const),
                  pl.BlockSpec((tn, d), chunk), pl.BlockSpec((1, LANES), const)],
        out_specs=[pl.BlockSpec((n, tn), col), pl.BlockSpec((n, LANES), const), pl.BlockSpec((n, tn), col)],
        out_shape=[jax.ShapeDtypeStruct((n, dm), F32), jax.ShapeDtypeStruct((n, LANES), F32),
                   jax.ShapeDtypeStruct((n, dg), F32)],
        compiler_params=_params("arbitrary"),
        name="proj_sample",
    )(x, g, wmt, wft, wgt, bfp)


def _cumsum_body(x_ref, o_ref):
    c = x_ref[...]
    lane = lax.broadcasted_iota(I32, c.shape, 1)
    s = 1
    while s < c.shape[1]:
        c = c + jnp.where(lane >= s, pltpu.roll(c, s, 1), 0.0)
        s *= 2
    o_ref[...] = c


def _cumsum_lanes(x):
    return pl.pallas_call(_cumsum_body, out_shape=jax.ShapeDtypeStruct(x.shape, F32),
                          compiler_params=_params(), name="cumsum_logf")(x)


def _prompt_q_tile(qi, q_ref, kts, vhs, c_ref, o_ref, *, tile, dh, first, causal):
    q = q_ref[0, qi * tile:(qi + 1) * tile, :]
    zero = jnp.zeros_like(q)
    q_heads = (jnp.where(first, q, zero), jnp.where(first, zero, q))
    res = []
    for h in range(2):
        m = jnp.full((tile, 1), -1e30, F32)
        acc = jnp.zeros((tile, 2 * dh), F32)
        for kj in range(qi + 1):
            s = lax.dot_general(q_heads[h], kts[kj], (((1,), (1,)), ((), ())), preferred_element_type=F32)
            s = s - c_ref[0, 0, kj][h:h + 1, :]
            if kj == qi:
                s = jnp.where(causal, s, -jnp.inf)
            m_new = jnp.maximum(m, jnp.max(s, axis=-1, keepdims=True))
            alpha = jnp.exp(m - m_new)
            p = jnp.exp(s - m_new)
            acc = alpha * acc + jnp.dot(p.astype(BF16), vhs[kj][h], preferred_element_type=F32)
            m = m_new
        res.append(acc)
    a0, a1 = res
    out = jnp.where(first, a0 / a0[:, dh:dh + 1], a1 / a1[:, 0:1])
    o_ref[0, qi * tile:(qi + 1) * tile, :] = out.astype(o_ref.dtype)


def _sample_chunk(k_refs, v_refs, lf_refs, qrep, carry):
    m_prev, l, acc, s_run = carry
    n_heads, dh, page = acc.shape
    d_att = n_heads * dh
    g_n = len(k_refs)
    r = lax.broadcasted_iota(I32, (page, page), 0)
    c = lax.broadcasted_iota(I32, (page, page), 1)
    later = (r > c).astype(BF16)
    ones = jnp.ones((page, page), BF16)
    lf_all = jnp.concatenate([lf_refs[g][...] for g in range(g_n)], axis=0)
    suffix = _dot_exact_rhs(lf_all, later)
    total = _dot_exact_rhs(lf_all, ones)
    m_new = m_prev
    scores = []
    for g in range(g_n):
        kq = k_refs[g][...].reshape(d_att, page) * qrep
        s = jnp.sum(kq.reshape(n_heads, dh, page), axis=1)
        sb = s + s_run + suffix[g * n_heads:(g + 1) * n_heads]
        s_run = s_run + total[g * n_heads:(g + 1) * n_heads]
        scores.append(sb)
        m_new = jnp.maximum(m_new, jnp.max(sb, axis=-1, keepdims=True))
    alpha = jnp.exp(m_prev - m_new)
    l = alpha * l
    acc = acc * alpha[:, None, :]
    for g in range(g_n):
        p = jnp.exp(scores[g] - m_new)
        l = l + jnp.sum(p, axis=-1, keepdims=True)
        acc = acc + v_refs[g][...] * p[:, None, :]
    return m_new, l, acc, s_run


def _attn_body(pt_ref, q_ref, k_ref, v_ref, c_ref, qrep_ref, qs_ref, kn_ref, vrep_ref, lfn_ref,
               ck_hbm, cv_hbm, clf_hbm, o_ref, os_ref, kbuf, vbuf, lfbuf, sems, *, tile, dh):
    step = pl.program_id(0) * pl.num_programs(1) + pl.program_id(1)
    n_steps = pl.num_programs(0) * pl.num_programs(1)
    nt = q_ref.shape[1] // tile
    n_pages = pt_ref.shape[1]
    _, g_n, n_heads, _, page = kbuf.shape
    d_att = n_heads * dh
    seqs = qrep_ref.shape[0]
    n_chunks = n_pages // g_n
    n_items = seqs * n_chunks

    def fetch(slot, seq, chunk):
        for g in range(g_n):
            pid = pt_ref[seq, n_pages - 1 - (chunk * g_n + g)]
            pltpu.make_async_copy(ck_hbm.at[pid], kbuf.at[slot, g], sems.at[slot, 0]).start()
            pltpu.make_async_copy(cv_hbm.at[pid], vbuf.at[slot, g], sems.at[slot, 1]).start()
            pltpu.make_async_copy(clf_hbm.at[pid], lfbuf.at[slot, g], sems.at[slot, 2]).start()

    @pl.when(step == 0)
    def _():
        fetch(0, 0, 0)

    lane = lax.broadcasted_iota(I32, (tile, 2 * dh), 1)
    first = lane < dh
    row = lax.broadcasted_iota(I32, (tile, tile), 0)
    col = lax.broadcasted_iota(I32, (tile, tile), 1)
    causal = col <= row
    one = jnp.ones((tile, 2 * dh), BF16)
    kts, vhs = [], []
    for kj in range(nt):
        vt = v_ref[0, kj * tile:(kj + 1) * tile, :]
        kts.append(k_ref[0, kj * tile:(kj + 1) * tile, :])
        vhs.append((jnp.where(first, vt, one), jnp.where(first, one, vt)))

    q_done = 0
    carry = None
    for item in range(n_items):
        j, chunk = divmod(item, n_chunks)
        slot = item % 2
        if item + 1 < n_items:
            fetch(1 - slot, step * seqs + (item + 1) // n_chunks, (item + 1) % n_chunks)
        else:
            @pl.when(step + 1 < n_steps)
            def _():
                fetch(1 - slot, (step + 1) * seqs, 0)
        pltpu.make_async_copy(ck_hbm.at[pl.ds(0, g_n)], kbuf.at[slot], sems.at[slot, 0]).wait()
        pltpu.make_async_copy(cv_hbm.at[pl.ds(0, g_n)], vbuf.at[slot], sems.at[slot, 1]).wait()
        pltpu.make_async_copy(clf_hbm.at[pl.ds(0, g_n)], lfbuf.at[slot], sems.at[slot, 2]).wait()
        if chunk == 0:
            s_new = jnp.sum(qs_ref[j] * kn_ref[j], axis=-1, keepdims=True)
            lane_p = lax.broadcasted_iota(I32, (d_att, page), 1)
            carry = (jnp.broadcast_to(s_new, (n_heads, page)), jnp.ones((n_heads, page), F32),
                     jnp.where(lane_p == 0, vrep_ref[j], 0.0).reshape(n_heads, dh, page), lfn_ref[j])
        carry = _sample_chunk([kbuf.at[slot, g] for g in range(g_n)], [vbuf.at[slot, g] for g in range(g_n)],
                              [lfbuf.at[slot, g] for g in range(g_n)], qrep_ref[j], carry)
        if chunk == n_chunks - 1:
            _, l, acc, _ = carry
            os_ref[j] = jnp.sum(acc / l[:, None, :], axis=-1)
        q_until = ((item + 1) * nt) // n_items
        for qi in range(q_done, q_until):
            _prompt_q_tile(qi, q_ref, kts, vhs, c_ref, o_ref, tile=tile, dh=dh, first=first, causal=causal)
        q_done = q_until


def _attention(q, k, v, c, page_table, q_s, k_new, v_new, lf_new, cache_kt, cache_vt, cache_lft, *, tile, dh):
    b, t, da = q.shape
    hp = da // (2 * dh)
    nt = t // tile
    db, n_pages = page_table.shape
    _, n_heads, _, page = cache_kt.shape
    d_att = n_heads * dh
    n_steps = b * hp
    assert db % n_steps == 0, "sample sequences are split evenly over the prompt grid steps"
    seqs = db // n_steps
    g_n = PAGES_PER_STEP
    while n_pages % g_n:
        g_n //= 2
    assert (seqs * (n_pages // g_n)) % 2 == 0, "buffer slots alternate per page chunk"
    lane_rep = lambda a: jnp.broadcast_to(a.reshape(db, -1, 1), (db, a.size // db, page))
    pair = pl.BlockSpec((1, t, 2 * dh), lambda bi, hi, pt: (bi, 0, hi))
    per_step = lambda bi, hi, pt: (bi * hp + hi, 0, 0)
    hbm = pl.BlockSpec(memory_space=pl.ANY)
    grid_spec = pltpu.PrefetchScalarGridSpec(
        num_scalar_prefetch=1, grid=(b, hp),
        in_specs=[pair, pair, pair, pl.BlockSpec((1, 1, nt, 2, tile), lambda bi, hi, pt: (bi, hi, 0, 0, 0)),
                  pl.BlockSpec((seqs, d_att, page), per_step), pl.BlockSpec((seqs, n_heads, dh), per_step),
                  pl.BlockSpec((seqs, n_heads, dh), per_step), pl.BlockSpec((seqs, d_att, page), per_step),
                  pl.BlockSpec((seqs, n_heads, page), per_step), hbm, hbm, hbm],
        out_specs=[pair, pl.BlockSpec((seqs, n_heads, dh), per_step)],
        scratch_shapes=[pltpu.VMEM((2, g_n, n_heads, dh, page), F32), pltpu.VMEM((2, g_n, n_heads, dh, page), F32),
                        pltpu.VMEM((2, g_n, n_heads, page), F32), pltpu.SemaphoreType.DMA((2, 3))])
    return pl.pallas_call(
        functools.partial(_attn_body, tile=tile, dh=dh),
        grid_spec=grid_spec,
        out_shape=[jax.ShapeDtypeStruct((b, t, da), BF16), jax.ShapeDtypeStruct((db, n_heads, dh), F32)],
        compiler_params=_params("arbitrary", "arbitrary"),
        name="attention",
    )(page_table, q, k, v, c, lane_rep(q_s), q_s.reshape(db, n_heads, dh), k_new.reshape(db, n_heads, dh),
      lane_rep(v_new), lane_rep(lf_new), cache_kt, cache_vt, cache_lft)


def _merge_and_route(x, pooled, att, gates, wp_ref, ps_ref, wup_ref, wua_ref, wo_ref, nf_ref, wr_ref, br_ref,
                     *, precise, n_groups, n_per_group):
    tm, d = x.shape
    mixed = jnp.concatenate([_dot(pooled[g], wp_ref[g], precise) for g in range(len(pooled))], axis=-1)
    pool_out = mixed * ps_ref[...]
    y = gates[:, :d].astype(F32) * _dot(pool_out, wup_ref[...], precise) \
        + gates[:, d:].astype(F32) * _dot(att, wua_ref[...], precise)
    x2 = x + _dot(y, wo_ref[...], precise)
    h2 = _rmsnorm(x2, nf_ref[...])
    logits = _dot(h2, wr_ref[...], precise) + br_ref[...]
    lane = lax.broadcasted_iota(I32, logits.shape, 1)
    lanef = lane.astype(F32)
    neg = -jnp.inf
    is_g = lane < n_groups
    gmax = jnp.max(jnp.where(is_g, logits, neg), axis=-1, keepdims=True)
    gidx = jnp.min(jnp.where(is_g & (logits == gmax), lanef, float(LANES)), axis=-1, keepdims=True)
    gsum = jnp.sum(jnp.where(is_g, jnp.exp(logits - gmax), 0.0), axis=-1, keepdims=True)
    g_w = 1.0 / gsum
    n_exp = n_groups * n_per_group
    exp_id = lanef - float(n_groups)
    in_sel = (lane >= n_groups) & (lane < n_groups + n_exp) & (jnp.floor(exp_id / n_per_group) == gidx)
    v1 = jnp.max(jnp.where(in_sel, logits, neg), axis=-1, keepdims=True)
    i1 = jnp.min(jnp.where(in_sel & (logits == v1), lanef, float(LANES)), axis=-1, keepdims=True)
    in_sel2 = in_sel & (lanef != i1)
    v2 = jnp.max(jnp.where(in_sel2, logits, neg), axis=-1, keepdims=True)
    i2 = jnp.min(jnp.where(in_sel2 & (logits == v2), lanef, float(LANES)), axis=-1, keepdims=True)
    t = jnp.exp(v2 - v1)
    w1 = g_w * (1.0 / (1.0 + t))
    w2 = g_w * (t / (1.0 + t))
    e1 = i1 - float(n_groups)
    e2 = i2 - float(n_groups)
    hit1 = lanef == e1
    hit2 = lanef == e2
    onehot = (hit1 | hit2).astype(BF16)
    rr = lax.broadcasted_iota(I32, (tm, tm), 0)
    cc = lax.broadcasted_iota(I32, (tm, tm), 1)
    incl = jnp.dot((cc <= rr).astype(BF16), onehot, preferred_element_type=F32)
    counts = incl[tm - 1:tm, :]
    groups = jnp.floor((counts + (ROW_GROUP - 1.0)) * (1.0 / ROW_GROUP))
    ur = lax.broadcasted_iota(I32, (LANES, LANES), 0)
    uc = lax.broadcasted_iota(I32, (LANES, LANES), 1)
    before = jnp.dot(jnp.broadcast_to(groups, (8, LANES)).astype(BF16), (ur < uc).astype(BF16),
                     preferred_element_type=F32)[0:1]
    seg_start = before * float(ROW_GROUP)
    pick = lambda hit, tbl: jnp.sum(jnp.where(hit, tbl, 0.0), axis=-1, keepdims=True)
    r1 = pick(hit1, incl) - 1.0
    r2 = pick(hit2, incl) - 1.0
    row1 = pick(hit1, seg_start) + r1
    row2 = pick(hit2, seg_start) + r2
    slab = jnp.zeros((tm, LANES), F32)
    for i, val in enumerate((e1, e2, r1, r2, w1, w2, row1, row2)):
        slab = jnp.where(lane == i, val, slab)
    return x2, h2, slab, counts


def _sorted_copy(h2, row1, row2, n_rows):
    tm = h2.shape[0]
    r = lax.broadcasted_iota(I32, (n_rows, tm), 0).astype(F32)
    place = ((r == row1) | (r == row2)).astype(BF16)
    return jnp.dot(place, h2.astype(BF16), preferred_element_type=F32).astype(BF16)


def _merge_prompt_body(x_ref, u_ref, halo_ref, att_ref, gate_ref, wp_ref, ps_ref, wup_ref, wua_ref, wo_ref,
                       nf_ref, wr_ref, br_ref, xs_last_hbm, x2_ref, xs_ref, slab_ref, counts_ref,
                       ext_ref, sem, *, seq_len, n_groups, n_per_group):
    i = pl.program_id(0)
    n_tiles = pl.num_programs(0) - 1
    tm = x_ref.shape[0]
    gw = u_ref.shape[1] // len(POOL_WINDOWS)

    @pl.when(i < n_tiles)
    def _():
        pos0 = (i * tm) % seq_len
        u = u_ref[...]
        ext_ref[0:POOL_HALO, :] = jnp.where(pos0 == 0, 0.0, halo_ref[...])
        ext_ref[POOL_HALO:, :] = u
        pos = pos0 + lax.broadcasted_iota(I32, (tm, 1), 0)
        pooled = []
        for g, w in enumerate(POOL_WINDOWS):
            lo = g * gw
            wsum = ext_ref[pl.ds(POOL_HALO, tm), lo:lo + gw]
            for j in range(1, w):
                wsum = wsum + ext_ref[pl.ds(POOL_HALO - j, tm), lo:lo + gw]
            count = jnp.minimum(pos + 1, w).astype(F32)
            pooled.append(wsum / count - u[:, lo:lo + gw])
        x2, h2, slab, counts = _merge_and_route(
            x_ref[...], pooled, att_ref[...], gate_ref[...], wp_ref, ps_ref, wup_ref, wua_ref, wo_ref, nf_ref,
            wr_ref, br_ref, precise=False, n_groups=n_groups, n_per_group=n_per_group)
        fields = slab.T[0:8, :]
        x2_ref[...] = x2
        xs_ref[...] = _sorted_copy(h2, fields[6:7, :], fields[7:8, :], xs_ref.shape[0])
        slab_ref[...] = slab
        counts_ref[0] = counts

    @pl.when(i == n_tiles)
    def _():
        copy = pltpu.make_async_copy(xs_last_hbm, xs_ref, sem)
        copy.start()
        copy.wait()


def _merge_prompt(x, u, att, gates, wp, ps, wup, wua, wo, nf, wr, br, xs_last, *, tm, seq_len, n_groups, n_per_group):
    n, d = x.shape
    d_pool, d_att = u.shape[1], att.shape[1]
    tile_rows = xs_last.shape[0]
    nt = n // tm
    clamp = lambda i: jnp.minimum(i, nt - 1)
    row = lambda i: (clamp(i), 0)
    const = lambda i: (0, 0)
    const3 = lambda i: (0, 0, 0)
    halo = lambda i: (jnp.maximum(clamp(i) * (tm // POOL_HALO) - 1, 0), 0)
    return pl.pallas_call(
        functools.partial(_merge_prompt_body, seq_len=seq_len, n_groups=n_groups, n_per_group=n_per_group),
        grid=(nt + 1,),
        in_specs=[pl.BlockSpec((tm, d), row), pl.BlockSpec((tm, d_pool), row), pl.BlockSpec((POOL_HALO, d_pool), halo),
                  pl.BlockSpec((tm, d_att), row), pl.BlockSpec((tm, 2 * d), row),
                  pl.BlockSpec(wp.shape, const3), pl.BlockSpec((1, d_pool), const),
                  pl.BlockSpec(wup.shape, const), pl.BlockSpec(wua.shape, const), pl.BlockSpec(wo.shape, const),
                  pl.BlockSpec((1, d), const), pl.BlockSpec(wr.shape, const), pl.BlockSpec((1, LANES), const),
                  pl.BlockSpec(memory_space=pl.ANY)],
        out_specs=[pl.BlockSpec((tm, d), row), pl.BlockSpec((tile_rows, d), lambda i: (i, 0)),
                   pl.BlockSpec((tm, LANES), row), pl.BlockSpec((1, 1, LANES), lambda i: (clamp(i), 0, 0))],
        out_shape=[jax.ShapeDtypeStruct((n, d), F32), jax.ShapeDtypeStruct(((nt + 1) * tile_rows, d), BF16),
                   jax.ShapeDtypeStruct((n, LANES), F32), jax.ShapeDtypeStruct((nt, 1, LANES), F32)],
        scratch_shapes=[pltpu.VMEM((tm + POOL_HALO, d_pool), F32), pltpu.SemaphoreType.DMA],
        compiler_params=_params("arbitrary"),
        name="merge_prompt",
    )(x, u, u, att, gates, wp, ps, wup, wua, wo, nf, wr, br, xs_last)


def _merge_sample_body(x_ref, u_ref, st_ref, att_ref, gate_ref, wp_ref, ps_ref, wup_ref, wua_ref, wo_ref,
                       nf_ref, wr_ref, br_ref, x2_ref, xs_ref, slab_ref, counts_ref,
                       *, start_pos, n_groups, n_per_group):
    u = u_ref[...]
    gw = u.shape[1] // len(POOL_WINDOWS)
    n_state = st_ref.shape[0]
    pooled = []
    for g, w in enumerate(POOL_WINDOWS):
        lo = g * gw
        wsum = u[:, lo:lo + gw]
        for j in range(1, w):
            wsum = wsum + st_ref[n_state - j][:, lo:lo + gw]
        pooled.append(wsum / float(min(start_pos + 1, w)) - u[:, lo:lo + gw])
    x2, h2, slab, counts = _merge_and_route(
        x_ref[...], pooled, att_ref[...], gate_ref[...], wp_ref, ps_ref, wup_ref, wua_ref, wo_ref, nf_ref,
        wr_ref, br_ref, precise=True, n_groups=n_groups, n_per_group=n_per_group)
    n = slab.shape[0]
    fields = jnp.concatenate([slab, jnp.zeros((LANES - n, LANES), F32)], axis=0).T
    x2_ref[...] = x2
    xs_ref[...] = _sorted_copy(h2, fields[6:7, 0:n], fields[7:8, 0:n], xs_ref.shape[0])
    slab_ref[...] = slab
    counts_ref[0] = counts


def _merge_sample(x, u, state_t, att, gates, wp, ps, wup, wua, wo, nf, wr, br, *, start_pos, n_groups,
                  n_per_group, tile_rows):
    n, d = x.shape
    assert n <= LANES
    return pl.pallas_call(
        functools.partial(_merge_sample_body, start_pos=start_pos, n_groups=n_groups, n_per_group=n_per_group),
        out_shape=[jax.ShapeDtypeStruct((n, d), F32), jax.ShapeDtypeStruct((tile_rows, d), BF16),
                   jax.ShapeDtypeStruct((n, LANES), F32), jax.ShapeDtypeStruct((1, 1, LANES), F32)],
        compiler_params=_params(),
        name="merge_sample",
    )(x, u, state_t, att, gates, wp, ps, wup, wua, wo, nf, wr, br)


def _moe_plan_body(cnt_ref, gsrc_ref, lsrc_ref, te_ref, tw_ref, loc_ref, *, n_ttiles, n_exp, tile_rows, gpt,
                   seg_groups_max, local_stride, zero_group):
    n_mm = te_ref.shape[0]
    gsrc_ref[...] = jnp.full(gsrc_ref.shape, zero_group, I32)
    lsrc_ref[...] = jnp.zeros(lsrc_ref.shape, I32)

    def fill_tiles(t, c):
        te_ref[t] = -1
        tw_ref[t] = n_exp - 1
        return c

    def fill_loc(i, c):
        loc_ref[i] = 0
        return c

    lax.fori_loop(0, n_mm, fill_tiles, 0)
    lax.fori_loop(0, n_ttiles, fill_loc, 0)
    step = lax.broadcasted_iota(I32, (seg_groups_max, LANES), 0)

    def per_expert(e, pos):
        def per_tile(i, p):
            g = (cnt_ref[i * n_exp + e] + (ROW_GROUP - 1)) // ROW_GROUP
            loc = loc_ref[i]
            gsrc_ref[pl.ds(p, seg_groups_max), :] = i * tile_rows + (loc + step) * ROW_GROUP
            lsrc_ref[pl.ds(i * local_stride + loc, seg_groups_max), :] = p + step
            loc_ref[i] = loc + g
            return p + g

        end = lax.fori_loop(0, n_ttiles, per_tile, pos)
        end_pad = ((end + (gpt - 1)) // gpt) * gpt
        gsrc_ref[pl.ds(end, seg_groups_max), :] = jnp.full((seg_groups_max, LANES), zero_group, I32)

        def mark(t, c):
            te_ref[t] = e
            tw_ref[t] = e
            return c

        lax.fori_loop(pos // gpt, end_pad // gpt, mark, 0)
        return end_pad

    lax.fori_loop(0, n_exp, per_expert, 0)

    def clear_tail(i, c):
        lsrc_ref[pl.ds(i * local_stride + loc_ref[i], seg_groups_max), :] = jnp.zeros((seg_groups_max, LANES), I32)
        return c

    lax.fori_loop(0, n_ttiles, clear_tail, 0)


def _moe_plan(cnt, *, n_ttiles, n_exp, tile_rows, gpt, n_mm_tiles, seg_groups_max):
    smem = pl.BlockSpec(memory_space=pltpu.SMEM)
    zero_group = n_ttiles * tile_rows - ROW_GROUP
    local_groups = tile_rows // ROW_GROUP
    local_stride = local_groups + seg_groups_max
    n_groups = n_mm_tiles * gpt
    gsrc, lsrc, te, tw = pl.pallas_call(
        functools.partial(_moe_plan_body, n_ttiles=n_ttiles, n_exp=n_exp, tile_rows=tile_rows, gpt=gpt,
                          seg_groups_max=seg_groups_max, local_stride=local_stride, zero_group=zero_group),
        in_specs=[smem], out_specs=[pl.BlockSpec(memory_space=pltpu.VMEM), pl.BlockSpec(memory_space=pltpu.VMEM),
                                    smem, smem],
        out_shape=[jax.ShapeDtypeStruct((n_groups + 2 * seg_groups_max, LANES), I32),
                   jax.ShapeDtypeStruct((n_ttiles * local_stride, LANES), I32),
                   jax.ShapeDtypeStruct((n_mm_tiles,), I32), jax.ShapeDtypeStruct((n_mm_tiles,), I32)],
        scratch_shapes=[pltpu.SMEM((n_ttiles,), I32)],
        name="moe_plan",
    )(cnt)
    return gsrc[:n_groups, 0], lsrc[:, 0], te, tw, local_stride


def _moe_mm_body(te_ref, tw_ref, src_ref, xs_hbm, wg_ref, wu_ref, wd_ref, ys_ref, xbuf, wgb_ref, wub_ref, wdb_ref,
                 sems):
    i = pl.program_id(0)
    tm = ys_ref.shape[0]
    groups = tm // ROW_GROUP
    expert = te_ref[i]
    prev = te_ref[jnp.maximum(i - 1, 0)]

    def fetch(slot, tile):
        for k in range(groups):
            src = pl.multiple_of(src_ref[tile * groups + k], ROW_GROUP)
            pltpu.make_async_copy(xs_hbm.at[pl.ds(src, ROW_GROUP)],
                                  xbuf.at[slot, pl.ds(k * ROW_GROUP, ROW_GROUP)], sems.at[slot]).start()

    slot = i % 2

    last = pl.num_programs(0) - 1

    @pl.when((i == 0) & (expert >= 0))
    def _():
        fetch(0, 0)

    @pl.when((i < last) & (te_ref[jnp.minimum(i + 1, last)] >= 0))
    def _():
        fetch(1 - slot, i + 1)

    @pl.when(expert >= 0)
    def _():
        pltpu.make_async_copy(xs_hbm.at[pl.ds(0, tm)], xbuf.at[slot], sems.at[slot]).wait()

    @pl.when((expert >= 0) & ((i == 0) | (expert != prev)))
    def _():
        wgb_ref[...] = wg_ref[0].astype(BF16)
        wub_ref[...] = wu_ref[0].astype(BF16)
        wdb_ref[...] = wd_ref[0].astype(BF16)

    @pl.when(expert >= 0)
    def _():
        x = xbuf[slot]
        a = jnp.dot(x, wgb_ref[...], preferred_element_type=F32)
        b = jnp.dot(x, wub_ref[...], preferred_element_type=F32)
        hdn = (a * _sigmoid(a)) * b
        ys_ref[...] = jnp.dot(hdn.astype(BF16), wdb_ref[...], preferred_element_type=F32).astype(ys_ref.dtype)

    @pl.when(expert < 0)
    def _():
        ys_ref[...] = jnp.zeros_like(ys_ref)


def _moe_mm(tile_expert, tile_weight, group_src, xs, w_gate, w_up, w_down, *, tm):
    n_tiles = tile_expert.shape[0]
    d = xs.shape[1]
    n_exp, _, de = w_gate.shape
    wmap = lambda i, te, tw, src: (tw[i], 0, 0)
    grid_spec = pltpu.PrefetchScalarGridSpec(
        num_scalar_prefetch=3, grid=(n_tiles,),
        in_specs=[pl.BlockSpec(memory_space=pl.ANY),
                  pl.BlockSpec((1, d, de), wmap), pl.BlockSpec((1, d, de), wmap), pl.BlockSpec((1, de, d), wmap)],
        out_specs=pl.BlockSpec((tm, d), lambda i, te, tw, src: (i, 0)),
        scratch_shapes=[pltpu.VMEM((2, tm, d), BF16), pltpu.VMEM((d, de), BF16), pltpu.VMEM((d, de), BF16),
                        pltpu.VMEM((de, d), BF16), pltpu.SemaphoreType.DMA((2,))])
    return pl.pallas_call(
        _moe_mm_body, grid_spec=grid_spec,
        out_shape=jax.ShapeDtypeStruct((n_tiles * tm, d), BF16),
        compiler_params=_params("arbitrary"),
        name="moe_mm",
    )(tile_expert, tile_weight, group_src, xs, w_gate, w_up, w_down)


def _moe_combine_body(lsrc_ref, x_ref, slab_ref, g_ref, ys_hbm, o_ref, ybuf, sems,
                      *, final_norm, first_tile, local_stride):
    i = pl.program_id(0)
    tm = x_ref.shape[0]
    tile_rows = ybuf.shape[1]
    unroll = 8

    def fetch(slot, tile):
        def body(c, carry):
            for j in range(unroll):
                lg = c * unroll + j
                src = pl.multiple_of(lsrc_ref[tile * local_stride + lg] * ROW_GROUP, ROW_GROUP)
                dst = pl.multiple_of(lg * ROW_GROUP, ROW_GROUP)
                pltpu.make_async_copy(ys_hbm.at[pl.ds(src, ROW_GROUP)], ybuf.at[slot, pl.ds(dst, ROW_GROUP)],
                                      sems.at[slot]).start()
            return carry

        lax.fori_loop(0, tile_rows // ROW_GROUP // unroll, body, 0)

    slot = i % 2

    @pl.when(i == 0)
    def _():
        fetch(0, first_tile)

    @pl.when(i + 1 < pl.num_programs(0))
    def _():
        fetch(1 - slot, first_tile + i + 1)

    pltpu.make_async_copy(ys_hbm.at[pl.ds(0, tile_rows)], ybuf.at[slot], sems.at[slot]).wait()
    y = ybuf[slot]
    slab = slab_ref[...]
    r = lax.broadcasted_iota(I32, (tm, tile_rows), 1).astype(F32)
    ya = jnp.dot((r == slab[:, 6:7]).astype(BF16), y, preferred_element_type=F32)
    yb = jnp.dot((r == slab[:, 7:8]).astype(BF16), y, preferred_element_type=F32)
    out = x_ref[...] + (slab[:, 4:5] * ya + slab[:, 5:6] * yb)
    if final_norm:
        out = _rmsnorm(out, g_ref[...])
    o_ref[...] = out


def _moe_combine(local_src, x, slab, g, ys, *, ts, tile_rows, final_norm, first_tile, local_stride):
    n, d = x.shape
    return pl.pallas_call(
        functools.partial(_moe_combine_body, final_norm=final_norm, first_tile=first_tile,
                          local_stride=local_stride),
        grid=(n // ts,),
        in_specs=[pl.BlockSpec(memory_space=pltpu.SMEM),
                  pl.BlockSpec((ts, d), lambda i: (i, 0)), pl.BlockSpec((ts, LANES), lambda i: (i, 0)),
                  pl.BlockSpec((1, d), lambda i: (0, 0)), pl.BlockSpec(memory_space=pl.ANY)],
        out_specs=pl.BlockSpec((ts, d), lambda i: (i, 0)),
        out_shape=jax.ShapeDtypeStruct((n, d), F32),
        scratch_shapes=[pltpu.VMEM((2, tile_rows, d), BF16), pltpu.SemaphoreType.DMA((2,))],
        compiler_params=_params("arbitrary"),
        name="moe_combine",
    )(local_src, x, slab, g, ys)


def kernel(x_prompt, x_sample, cache_k, cache_v, cache_logf, state_pool, page_table, norm_mix, w_in, b_forget,
           w_pool, pool_scale, w_up_pool, w_up_att, w_out, norm_ffn, w_router_group, b_router_group,
           w_router_expert, b_router_expert, w_gate, w_up, w_down, norm_final):
    depth = norm_mix.shape[0]
    assert depth == 1, "single trunk layer"
    b, t, d = x_prompt.shape
    db, dt, _ = x_sample.shape
    assert dt == 1, "one sample token per sequence"
    _, _, page, n_heads, dh = cache_k.shape
    n_pages = page_table.shape[1]
    past = n_pages * page
    n_state, d_pool = state_pool.shape[2], state_pool.shape[3]
    d_att = n_heads * dh
    n_pool_groups = w_pool.shape[1]
    assert n_pool_groups == len(POOL_WINDOWS) and d_pool // n_pool_groups == LANES
    assert n_state == max(POOL_WINDOWS) - 1 and n_state < POOL_HALO
    n_groups, n_per_group = w_router_expert.shape[1], w_router_expert.shape[3]
    n_exp = n_groups * n_per_group
    assert n_groups + n_exp <= LANES and 2 * dh == LANES and n_heads % 2 == 0
    n = b * t
    q_scale = float(dh) ** -0.5
    tm = min(TOKEN_TILE, t)
    assert t % tm == 0 and t % ATTN_TILE == 0

    o_main = d_pool + 3 * d_att
    wi = w_in[0]
    wm_f, wf_f, wg_f = wi[:, :o_main], wi[:, o_main:o_main + n_heads], wi[:, o_main + n_heads:]
    wf_pad = jnp.pad(wf_f, ((0, 0), (0, LANES - n_heads)))
    wit = jnp.transpose(wi)
    wmt_f, wgt_f = wit[:o_main], wit[o_main + n_heads:]
    wft_pad = jnp.pad(wit[o_main:o_main + n_heads], ((0, LANES - n_heads), (0, 0)))
    bf_pad = jnp.pad(b_forget[0], (0, LANES - n_heads)).reshape(1, LANES)
    g_mix = norm_mix[0].reshape(1, d)
    g_ffn = norm_ffn[0].reshape(1, d)
    g_fin = norm_final.reshape(1, d)
    ps = pool_scale[0].reshape(1, d_pool)
    wr_f = jnp.concatenate([w_router_group[0], jnp.transpose(w_router_expert[0], (1, 0, 2)).reshape(d, n_exp)], axis=1)
    wr_pad = jnp.pad(wr_f, ((0, 0), (0, LANES - n_groups - n_exp)))
    br_pad = jnp.pad(jnp.concatenate([b_router_group[0], b_router_expert[0].reshape(n_exp)]),
                     (0, LANES - n_groups - n_exp)).reshape(1, LANES)
    bf = lambda a: a.astype(BF16)

    xp = x_prompt.reshape(n, d)
    u_p, q_p, kt_p, vt_p, kb_p, vb_p, lft_p, gate_p = _proj_prompt(
        xp, g_mix, bf(wm_f), bf(wf_pad), bf(wg_f), bf_pad, tm=tm, seq_len=t, d_pool=d_pool, d_att=d_att,
        n_heads=n_heads, q_scale=q_scale)
    c = _cumsum_lanes(lft_p.reshape(b * n_heads, t))
    nt = t // ATTN_TILE
    c_blk = jnp.transpose(c.reshape(b, n_heads // 2, 2, nt, ATTN_TILE), (0, 1, 3, 2, 4))
    n_ptiles = n // tm
    n_ttiles = n_ptiles + 1
    tile_rows = -(-(2 * tm + n_exp * (ROW_GROUP - 1)) // MOE_ROW_TILE) * MOE_ROW_TILE
    assert 2 * db + n_exp * (ROW_GROUP - 1) <= tile_rows - ROW_GROUP, "the sample tile must end in an unused row group"

    xs = x_sample.reshape(db, d)
    z_s, lf_s, gate_s = _proj_sample(xs, g_mix, wmt_f, wft_pad, wgt_f, bf_pad, tn=512)
    u_s = z_s[:, :d_pool]
    q_s = z_s[:, d_pool:d_pool + d_att] * q_scale
    k_s = z_s[:, d_pool + d_att:d_pool + 2 * d_att]
    v_s = z_s[:, d_pool + 2 * d_att:]
    att_p, att_s = _attention(q_p.reshape(b, t, d_att), kb_p.reshape(b, t, d_att), vb_p.reshape(b, t, d_att), c_blk,
                              page_table, q_s, k_s, v_s, lf_s[:, :n_heads],
                              jnp.transpose(cache_k[0], (0, 2, 3, 1)), jnp.transpose(cache_v[0], (0, 2, 3, 1)),
                              jnp.transpose(cache_logf[0], (0, 2, 1)), tile=ATTN_TILE, dh=dh)
    state_t = jnp.transpose(state_pool[0], (1, 0, 2))
    x2_s, xs_rows_s, slab_s, counts_s = _merge_sample(
        xs, u_s, state_t, att_s.reshape(db, d_att), gate_s, w_pool[0], ps, w_up_pool[0], w_up_att[0], w_out[0],
        g_ffn, wr_pad, br_pad, start_pos=past, n_groups=n_groups, n_per_group=n_per_group, tile_rows=tile_rows)

    x2_p, xs_rows, slab_p, counts_p = _merge_prompt(
        xp, u_p, att_p.reshape(n, d_att), gate_p, bf(w_pool[0]), ps, bf(w_up_pool[0]), bf(w_up_att[0]),
        bf(w_out[0]), g_ffn, bf(wr_pad), br_pad, xs_rows_s, tm=tm, seq_len=t, n_groups=n_groups,
        n_per_group=n_per_group)

    tmm = MOE_ROW_TILE
    gpt = tmm // ROW_GROUP
    n_groups_max = -(-(2 * (n + db)) // ROW_GROUP) + n_ttiles * n_exp + n_exp * (gpt - 1)
    cnt = jnp.concatenate([counts_p, counts_s], axis=0)[:, 0, :n_exp].astype(I32).reshape(-1)
    group_src, local_src, tile_expert, tile_weight, local_stride = _moe_plan(
        cnt, n_ttiles=n_ttiles, n_exp=n_exp, tile_rows=tile_rows, gpt=gpt, n_mm_tiles=-(-n_groups_max // gpt),
        seg_groups_max=tm // ROW_GROUP)
    ys_rows = _moe_mm(tile_expert, tile_weight, group_src, xs_rows, w_gate[0], w_up[0], w_down[0], tm=tmm)
    y_prompt = _moe_combine(local_src, x2_p, slab_p, g_fin, ys_rows, ts=tm, tile_rows=tile_rows,
                            final_norm=True, first_tile=0, local_stride=local_stride)
    y_sample = _moe_combine(local_src, x2_s, slab_s, g_fin, ys_rows, ts=db, tile_rows=tile_rows,
                            final_norm=True, first_tile=n_ptiles, local_stride=local_stride)

    new_pool_p = u_p.reshape(b, t, d_pool)[:, t - n_state:, :]
    new_pool_s = jnp.concatenate([state_pool[0][:, 1:, :], u_s[:, None, :]], axis=1)
    to_heads = lambda a: jnp.transpose(a.reshape(b, n_heads, dh, t), (0, 3, 1, 2))[None]
    return (y_prompt.reshape(b, t, d), y_sample.reshape(db, 1, d),
            to_heads(kt_p), to_heads(vt_p), jnp.transpose(lft_p, (0, 2, 1))[None],
            new_pool_p[None],
            k_s.reshape(1, db, 1, n_heads, dh), v_s.reshape(1, db, 1, n_heads, dh),
            lf_s[:, :n_heads].reshape(1, db, 1, n_heads), new_pool_s[None])
```

```python
import functools

import jax
import jax.numpy as jnp
from jax import lax
from jax.experimental import pallas as pl
from jax.experimental.pallas import tpu as pltpu

F32 = jnp.float32
BF16 = jnp.bfloat16
I32 = jnp.int32
HIGHEST = lax.Precision.HIGHEST

RMS_EPS = 1e-6
POOL_WINDOWS = (2, 4, 8, 16)
POOL_HALO = 16
LANES = 128
VMEM_LIMIT_BYTES = 56 * 1024 * 1024

TOKEN_TILE = 512
ATTN_TILE = 512
MOE_ROW_TILE = 512
PAGES_PER_STEP = 16
ROW_GROUP = 16


def _params(*sem):
    return pltpu.CompilerParams(dimension_semantics=sem, vmem_limit_bytes=VMEM_LIMIT_BYTES)


def _rmsnorm(x, g):
    return x * lax.rsqrt(jnp.mean(x * x, axis=-1, keepdims=True) + RMS_EPS) * g


def _log_sigmoid(x):
    return jnp.minimum(x, 0.0) - jnp.log1p(jnp.exp(-jnp.abs(x)))


def _sigmoid(x):
    return 1.0 / (1.0 + jnp.exp(-x))


def _dot(a, b, precise):
    if precise:
        return jnp.dot(a.astype(F32), b.astype(F32), precision=HIGHEST, preferred_element_type=F32)
    return jnp.dot(a.astype(BF16), b.astype(BF16), preferred_element_type=F32)


def _split3(x):
    hi = x.astype(BF16)
    r = x - hi.astype(F32)
    mid = r.astype(BF16)
    lo = (r - mid.astype(F32)).astype(BF16)
    return hi, mid, lo


def _dot_exact_rhs(x, w_bf16):
    hi, mid, lo = _split3(x)
    d = lambda a: jnp.dot(a, w_bf16, preferred_element_type=F32)
    return d(hi) + d(mid) + d(lo)


def _proj_body(x_ref, g_ref, wm_ref, wf_ref, wg_ref, bf_ref,
               u_ref, q_ref, kt_ref, vt_ref, kb_ref, vb_ref, lft_ref, gate_ref, *, d_pool, d_att, n_heads, q_scale):
    h = _rmsnorm(x_ref[...], g_ref[...]).astype(BF16)
    z = jnp.dot(h, wm_ref[...], preferred_element_type=F32)
    o1, o2, o3 = d_pool, d_pool + d_att, d_pool + 2 * d_att
    u_ref[...] = z[:, :o1]
    q_ref[...] = (z[:, o1:o2] * q_scale).astype(BF16)
    k = z[:, o2:o3]
    v = z[:, o3:]
    kt_ref[0] = k.T
    vt_ref[0] = v.T
    kb_ref[...] = k.astype(BF16)
    vb_ref[...] = v.astype(BF16)
    lf = _log_sigmoid(jnp.dot(h, wf_ref[...], preferred_element_type=F32) + bf_ref[...])
    lft_ref[0] = lf.T[0:n_heads, :]
    gate_ref[...] = _sigmoid(jnp.dot(h, wg_ref[...], preferred_element_type=F32)).astype(BF16)


def _proj_prompt(x, g, wm, wf, wg, bfp, *, tm, seq_len, d_pool, d_att, n_heads, q_scale):
    n, d = x.shape
    b = n // seq_len
    tps = seq_len // tm
    row = lambda i: (i, 0)
    const = lambda i: (0, 0)
    tmin = lambda i: (i // tps, 0, i % tps)
    dg = wg.shape[1]
    out_shape = [
        jax.ShapeDtypeStruct((n, d_pool), F32), jax.ShapeDtypeStruct((n, d_att), BF16),
        jax.ShapeDtypeStruct((b, d_att, seq_len), F32), jax.ShapeDtypeStruct((b, d_att, seq_len), F32),
        jax.ShapeDtypeStruct((n, d_att), BF16), jax.ShapeDtypeStruct((n, d_att), BF16),
        jax.ShapeDtypeStruct((b, n_heads, seq_len), F32), jax.ShapeDtypeStruct((n, dg), BF16),
    ]
    return pl.pallas_call(
        functools.partial(_proj_body, d_pool=d_pool, d_att=d_att, n_heads=n_heads, q_scale=q_scale),
        grid=(n // tm,),
        in_specs=[pl.BlockSpec((tm, d), row), pl.BlockSpec((1, d), const),
                  pl.BlockSpec(wm.shape, const), pl.BlockSpec(wf.shape, const),
                  pl.BlockSpec(wg.shape, const), pl.BlockSpec((1, LANES), const)],
        out_specs=[pl.BlockSpec((tm, d_pool), row), pl.BlockSpec((tm, d_att), row),
                   pl.BlockSpec((1, d_att, tm), tmin), pl.BlockSpec((1, d_att, tm), tmin),
                   pl.BlockSpec((tm, d_att), row), pl.BlockSpec((tm, d_att), row),
                   pl.BlockSpec((1, n_heads, tm), tmin), pl.BlockSpec((tm, dg), row)],
        out_shape=out_shape,
        compiler_params=_params("arbitrary"),
        name="proj_prompt",
    )(x, g, wm, wf, wg, bfp)


def _dot_nt(a, bt, precise):
    dims = (((1,), (1,)), ((), ()))
    if precise:
        return lax.dot_general(a.astype(F32), bt.astype(F32), dims, precision=HIGHEST, preferred_element_type=F32)
    return lax.dot_general(a.astype(BF16), bt.astype(BF16), dims, preferred_element_type=F32)


def _proj_sample_body(x_ref, g_ref, wmt_ref, wft_ref, wgt_ref, bf_ref, z_ref, lf_ref, gate_ref):
    h = _rmsnorm(x_ref[...], g_ref[...])
    z_ref[...] = _dot_nt(h, wmt_ref[...], True)
    lf_ref[...] = _log_sigmoid(_dot_nt(h, wft_ref[...], True) + bf_ref[...])
    gate_ref[...] = _sigmoid(_dot_nt(h, wgt_ref[...], True))


def _proj_sample(x, g, wmt, wft, wgt, bfp, *, tn):
    n, d = x.shape
    dm, dg = wmt.shape[0], wgt.shape[0]
    assert dm == dg
    const = lambda j: (0, 0)
    chunk = lambda j: (j, 0)
    col = lambda j: (0, j)
    return pl.pallas_call(
        _proj_sample_body,
        grid=(dm // tn,),
        in_specs=[pl.BlockSpec((n, d), const), pl.BlockSpec((1, d), const),
                  pl.BlockSpec((tn, d), chunk), pl.BlockSpec(wft.shape, const),
                  pl.BlockSpec((tn, d), chunk), pl.BlockSpec((1, LANES), const)],
        out_specs=[pl.BlockSpec((n, tn), col), pl.BlockSpec((n, LANES), const), pl.BlockSpec((n, tn), col)],
        out_shape=[jax.ShapeDtypeStruct((n, dm), F32), jax.ShapeDtypeStruct((n, LANES), F32),
                   jax.ShapeDtypeStruct((n, dg), F32)],
        compiler_params=_params("arbitrary"),
        name="proj_sample",
    )(x, g, wmt, wft, wgt, bfp)


def _cumsum_body(x_ref, o_ref):
    c = x_ref[...]
    lane = lax.broadcasted_iota(I32, c.shape, 1)
    s = 1
    while s < c.shape[1]:
        c = c + jnp.where(lane >= s, pltpu.roll(c, s, 1), 0.0)
        s *= 2
    o_ref[...] = c


def _cumsum_lanes(x):
    return pl.pallas_call(_cumsum_body, out_shape=jax.ShapeDtypeStruct(x.shape, F32),
                          compiler_params=_params(), name="cumsum_logf")(x)


def _prompt_q_tile(qi, q_ref, kts, vhs, c_ref, o_ref, *, tile, dh, first, causal):
    q = q_ref[0, qi * tile:(qi + 1) * tile, :]
    zero = jnp.zeros_like(q)
    q_heads = (jnp.where(first, q, zero), jnp.where(first, zero, q))
    res = []
    for h in range(2):
        m = jnp.full((tile, 1), -1e30, F32)
        acc = jnp.zeros((tile, 2 * dh), F32)
        for kj in range(qi + 1):
            s = lax.dot_general(q_heads[h], kts[kj], (((1,), (1,)), ((), ())), preferred_element_type=F32)
            s = s - c_ref[0, 0, kj][h:h + 1, :]
            if kj == qi:
                s = jnp.where(causal, s, -jnp.inf)
            m_new = jnp.maximum(m, jnp.max(s, axis=-1, keepdims=True))
            alpha = jnp.exp(m - m_new)
            p = jnp.exp(s - m_new)
            acc = alpha * acc + jnp.dot(p.astype(BF16), vhs[kj][h], preferred_element_type=F32)
            m = m_new
        res.append(acc)
    a0, a1 = res
    out = jnp.where(first, a0 / a0[:, dh:dh + 1], a1 / a1[:, 0:1])
    o_ref[0, qi * tile:(qi + 1) * tile, :] = out.astype(o_ref.dtype)


def _sample_chunk(k_refs, v_refs, lf_refs, qrep, carry):
    m_prev, l, acc, s_run = carry
    n_heads, dh, page = acc.shape
    d_att = n_heads * dh
    g_n = len(k_refs)
    r = lax.broadcasted_iota(I32, (page, page), 0)
    c = lax.broadcasted_iota(I32, (page, page), 1)
    later = (r > c).astype(BF16)
    ones = jnp.ones((page, page), BF16)
    lf_all = jnp.concatenate([lf_refs[g][...] for g in range(g_n)], axis=0)
    suffix = _dot_exact_rhs(lf_all, later)
    total = _dot_exact_rhs(lf_all, ones)
    m_new = m_prev
    scores = []
    for g in range(g_n):
        kq = k_refs[g][...].reshape(d_att, page) * qrep
        s = jnp.sum(kq.reshape(n_heads, dh, page), axis=1)
        sb = s + s_run + suffix[g * n_heads:(g + 1) * n_heads]
        s_run = s_run + total[g * n_heads:(g + 1) * n_heads]
        scores.append(sb)
        m_new = jnp.maximum(m_new, jnp.max(sb, axis=-1, keepdims=True))
    alpha = jnp.exp(m_prev - m_new)
    l = alpha * l
    acc = acc * alpha[:, None, :]
    for g in range(g_n):
        p = jnp.exp(scores[g] - m_new)
        l = l + jnp.sum(p, axis=-1, keepdims=True)
        acc = acc + v_refs[g][...] * p[:, None, :]
    return m_new, l, acc, s_run


def _attn_body(pt_ref, q_ref, k_ref, v_ref, c_ref, qrep_ref, qs_ref, kn_ref, vrep_ref, lfn_ref,
               ck_hbm, cv_hbm, clf_hbm, o_ref, os_ref, kbuf, vbuf, lfbuf, sems, *, tile, dh):
    step = pl.program_id(0) * pl.num_programs(1) + pl.program_id(1)
    n_steps = pl.num_programs(0) * pl.num_programs(1)
    nt = q_ref.shape[1] // tile
    n_pages = pt_ref.shape[1]
    _, g_n, n_heads, _, page = kbuf.shape
    d_att = n_heads * dh
    seqs = qrep_ref.shape[0]
    n_chunks = n_pages // g_n
    n_items = seqs * n_chunks

    def fetch(slot, seq, chunk):
        for g in range(g_n):
            pid = pt_ref[seq, n_pages - 1 - (chunk * g_n + g)]
            pltpu.make_async_copy(ck_hbm.at[pid], kbuf.at[slot, g], sems.at[slot, 0]).start()
            pltpu.make_async_copy(cv_hbm.at[pid], vbuf.at[slot, g], sems.at[slot, 1]).start()
            pltpu.make_async_copy(clf_hbm.at[pid], lfbuf.at[slot, g], sems.at[slot, 2]).start()

    @pl.when(step == 0)
    def _():
        fetch(0, 0, 0)

    lane = lax.broadcasted_iota(I32, (tile, 2 * dh), 1)
    first = lane < dh
    row = lax.broadcasted_iota(I32, (tile, tile), 0)
    col = lax.broadcasted_iota(I32, (tile, tile), 1)
    causal = col <= row
    one = jnp.ones((tile, 2 * dh), BF16)
    kts, vhs = [], []
    for kj in range(nt):
        vt = v_ref[0, kj * tile:(kj + 1) * tile, :]
        kts.append(k_ref[0, kj * tile:(kj + 1) * tile, :])
        vhs.append((jnp.where(first, vt, one), jnp.where(first, one, vt)))

    q_done = 0
    carry = None
    for item in range(n_items):
        j, chunk = divmod(item, n_chunks)
        slot = item % 2
        if item + 1 < n_items:
            fetch(1 - slot, step * seqs + (item + 1) // n_chunks, (item + 1) % n_chunks)
        else:
            @pl.when(step + 1 < n_steps)
            def _():
                fetch(1 - slot, (step + 1) * seqs, 0)
        pltpu.make_async_copy(ck_hbm.at[pl.ds(0, g_n)], kbuf.at[slot], sems.at[slot, 0]).wait()
        pltpu.make_async_copy(cv_hbm.at[pl.ds(0, g_n)], vbuf.at[slot], sems.at[slot, 1]).wait()
        pltpu.make_async_copy(clf_hbm.at[pl.ds(0, g_n)], lfbuf.at[slot], sems.at[slot, 2]).wait()
        if chunk == 0:
            s_new = jnp.sum(qs_ref[j] * kn_ref[j], axis=-1, keepdims=True)
            lane_p = lax.broadcasted_iota(I32, (d_att, page), 1)
            carry = (jnp.broadcast_to(s_new, (n_heads, page)), jnp.ones((n_heads, page), F32),
                     jnp.where(lane_p == 0, vrep_ref[j], 0.0).reshape(n_heads, dh, page), lfn_ref[j])
        carry = _sample_chunk([kbuf.at[slot, g] for g in range(g_n)], [vbuf.at[slot, g] for g in range(g_n)],
                              [lfbuf.at[slot, g] for g in range(g_n)], qrep_ref[j], carry)
        if chunk == n_chunks - 1:
            _, l, acc, _ = carry
            os_ref[j] = jnp.sum(acc / l[:, None, :], axis=-1)
        q_until = ((item + 1) * nt) // n_items
        for qi in range(q_done, q_until):
            _prompt_q_tile(qi, q_ref, kts, vhs, c_ref, o_ref, tile=tile, dh=dh, first=first, causal=causal)
        q_done = q_until


def _attention(q, k, v, c, page_table, q_s, k_new, v_new, lf_new, cache_kt, cache_vt, cache_lft, *, tile, dh):
    b, t, da = q.shape
    hp = da // (2 * dh)
    nt = t // tile
    db, n_pages = page_table.shape
    _, n_heads, _, page = cache_kt.shape
    d_att = n_heads * dh
    n_steps = b * hp
    assert db % n_steps == 0, "sample sequences are split evenly over the prompt grid steps"
    seqs = db // n_steps
    g_n = PAGES_PER_STEP
    while n_pages % g_n:
        g_n //= 2
    assert (seqs * (n_pages // g_n)) % 2 == 0, "buffer slots alternate per page chunk"
    lane_rep = lambda a: jnp.broadcast_to(a.reshape(db, -1, 1), (db, a.size // db, page))
    pair = pl.BlockSpec((1, t, 2 * dh), lambda bi, hi, pt: (bi, 0, hi))
    per_step = lambda bi, hi, pt: (bi * hp + hi, 0, 0)
    hbm = pl.BlockSpec(memory_space=pl.ANY)
    grid_spec = pltpu.PrefetchScalarGridSpec(
        num_scalar_prefetch=1, grid=(b, hp),
        in_specs=[pair, pair, pair, pl.BlockSpec((1, 1, nt, 2, tile), lambda bi, hi, pt: (bi, hi, 0, 0, 0)),
                  pl.BlockSpec((seqs, d_att, page), per_step), pl.BlockSpec((seqs, n_heads, dh), per_step),
                  pl.BlockSpec((seqs, n_heads, dh), per_step), pl.BlockSpec((seqs, d_att, page), per_step),
                  pl.BlockSpec((seqs, n_heads, page), per_step), hbm, hbm, hbm],
        out_specs=[pair, pl.BlockSpec((seqs, n_heads, dh), per_step)],
        scratch_shapes=[pltpu.VMEM((2, g_n, n_heads, dh, page), F32), pltpu.VMEM((2, g_n, n_heads, dh, page), F32),
                        pltpu.VMEM((2, g_n, n_heads, page), F32), pltpu.SemaphoreType.DMA((2, 3))])
    return pl.pallas_call(
        functools.partial(_attn_body, tile=tile, dh=dh),
        grid_spec=grid_spec,
        out_shape=[jax.ShapeDtypeStruct((b, t, da), BF16), jax.ShapeDtypeStruct((db, n_heads, dh), F32)],
        compiler_params=_params("arbitrary", "arbitrary"),
        name="attention",
    )(page_table, q, k, v, c, lane_rep(q_s), q_s.reshape(db, n_heads, dh), k_new.reshape(db, n_heads, dh),
      lane_rep(v_new), lane_rep(lf_new), cache_kt, cache_vt, cache_lft)


def _merge_and_route(x, pooled, att, gates, wp_ref, ps_ref, wup_ref, wua_ref, wo_ref, nf_ref, wr_ref, br_ref,
                     *, precise, n_groups, n_per_group):
    tm, d = x.shape
    mixed = jnp.concatenate([_dot(pooled[g], wp_ref[g], precise) for g in range(len(pooled))], axis=-1)
    pool_out = mixed * ps_ref[...]
    y = gates[:, :d].astype(F32) * _dot(pool_out, wup_ref[...], precise) \
        + gates[:, d:].astype(F32) * _dot(att, wua_ref[...], precise)
    x2 = x + _dot(y, wo_ref[...], precise)
    h2 = _rmsnorm(x2, nf_ref[...])
    logits = _dot(h2, wr_ref[...], precise) + br_ref[...]
    lane = lax.broadcasted_iota(I32, logits.shape, 1)
    lanef = lane.astype(F32)
    neg = -jnp.inf
    is_g = lane < n_groups
    gmax = jnp.max(jnp.where(is_g, logits, neg), axis=-1, keepdims=True)
    gidx = jnp.min(jnp.where(is_g & (logits == gmax), lanef, float(LANES)), axis=-1, keepdims=True)
    gsum = jnp.sum(jnp.where(is_g, jnp.exp(logits - gmax), 0.0), axis=-1, keepdims=True)
    g_w = 1.0 / gsum
    n_exp = n_groups * n_per_group
    exp_id = lanef - float(n_groups)
    in_sel = (lane >= n_groups) & (lane < n_groups + n_exp) & (jnp.floor(exp_id / n_per_group) == gidx)
    v1 = jnp.max(jnp.where(in_sel, logits, neg), axis=-1, keepdims=True)
    i1 = jnp.min(jnp.where(in_sel & (logits == v1), lanef, float(LANES)), axis=-1, keepdims=True)
    in_sel2 = in_sel & (lanef != i1)
    v2 = jnp.max(jnp.where(in_sel2, logits, neg), axis=-1, keepdims=True)
    i2 = jnp.min(jnp.where(in_sel2 & (logits == v2), lanef, float(LANES)), axis=-1, keepdims=True)
    t = jnp.exp(v2 - v1)
    w1 = g_w * (1.0 / (1.0 + t))
    w2 = g_w * (t / (1.0 + t))
    e1 = i1 - float(n_groups)
    e2 = i2 - float(n_groups)
    hit1 = lanef == e1
    hit2 = lanef == e2
    onehot = (hit1 | hit2).astype(BF16)
    rr = lax.broadcasted_iota(I32, (tm, tm), 0)
    cc = lax.broadcasted_iota(I32, (tm, tm), 1)
    incl = jnp.dot((cc <= rr).astype(BF16), onehot, preferred_element_type=F32)
    counts = incl[tm - 1:tm, :]
    groups = jnp.floor((counts + (ROW_GROUP - 1.0)) * (1.0 / ROW_GROUP))
    ur = lax.broadcasted_iota(I32, (LANES, LANES), 0)
    uc = lax.broadcasted_iota(I32, (LANES, LANES), 1)
    before = jnp.dot(jnp.broadcast_to(groups, (8, LANES)).astype(BF16), (ur < uc).astype(BF16),
                     preferred_element_type=F32)[0:1]
    seg_start = before * float(ROW_GROUP)
    pick = lambda hit, tbl: jnp.sum(jnp.where(hit, tbl, 0.0), axis=-1, keepdims=True)
    r1 = pick(hit1, incl) - 1.0
    r2 = pick(hit2, incl) - 1.0
    row1 = pick(hit1, seg_start) + r1
    row2 = pick(hit2, seg_start) + r2
    slab = jnp.zeros((tm, LANES), F32)
    for i, val in enumerate((e1, e2, r1, r2, w1, w2, row1, row2)):
        slab = jnp.where(lane == i, val, slab)
    return x2, h2, slab, counts


def _sorted_copy(h2, row1, row2, n_rows):
    tm = h2.shape[0]
    r = lax.broadcasted_iota(I32, (n_rows, tm), 0).astype(F32)
    place = ((r == row1) | (r == row2)).astype(BF16)
    return jnp.dot(place, h2.astype(BF16), preferred_element_type=F32).astype(BF16)


def _merge_prompt_body(x_ref, u_ref, halo_ref, att_ref, gate_ref, wp_ref, ps_ref, wup_ref, wua_ref, wo_ref,
                       nf_ref, wr_ref, br_ref, xs_last_hbm, x2_ref, xs_ref, slab_ref, counts_ref,
                       ext_ref, sem, *, seq_len, n_groups, n_per_group):
    i = pl.program_id(0)
    n_tiles = pl.num_programs(0) - 1
    tm = x_ref.shape[0]
    gw = u_ref.shape[1] // len(POOL_WINDOWS)

    @pl.when(i < n_tiles)
    def _():
        pos0 = (i * tm) % seq_len
        u = u_ref[...]
        ext_ref[0:POOL_HALO, :] = jnp.where(pos0 == 0, 0.0, halo_ref[...])
        ext_ref[POOL_HALO:, :] = u
        pos = pos0 + lax.broadcasted_iota(I32, (tm, 1), 0)
        pooled = []
        for g, w in enumerate(POOL_WINDOWS):
            lo = g * gw
            wsum = ext_ref[pl.ds(POOL_HALO, tm), lo:lo + gw]
            for j in range(1, w):
                wsum = wsum + ext_ref[pl.ds(POOL_HALO - j, tm), lo:lo + gw]
            count = jnp.minimum(pos + 1, w).astype(F32)
            pooled.append(wsum / count - u[:, lo:lo + gw])
        x2, h2, slab, counts = _merge_and_route(
            x_ref[...], pooled, att_ref[...], gate_ref[...], wp_ref, ps_ref, wup_ref, wua_ref, wo_ref, nf_ref,
            wr_ref, br_ref, precise=False, n_groups=n_groups, n_per_group=n_per_group)
        fields = slab.T[0:8, :]
        x2_ref[...] = x2
        xs_ref[...] = _sorted_copy(h2, fields[6:7, :], fields[7:8, :], xs_ref.shape[0])
        slab_ref[...] = slab
        counts_ref[0] = counts

    @pl.when(i == n_tiles)
    def _():
        copy = pltpu.make_async_copy(xs_last_hbm, xs_ref, sem)
        copy.start()
        copy.wait()


def _merge_prompt(x, u, att, gates, wp, ps, wup, wua, wo, nf, wr, br, xs_last, *, tm, seq_len, n_groups, n_per_group):
    n, d = x.shape
    d_pool, d_att = u.shape[1], att.shape[1]
    tile_rows = xs_last.shape[0]
    nt = n // tm
    clamp = lambda i: jnp.minimum(i, nt - 1)
    row = lambda i: (clamp(i), 0)
    const = lambda i: (0, 0)
    const3 = lambda i: (0, 0, 0)
    halo = lambda i: (jnp.maximum(clamp(i) * (tm // POOL_HALO) - 1, 0), 0)
    return pl.pallas_call(
        functools.partial(_merge_prompt_body, seq_len=seq_len, n_groups=n_groups, n_per_group=n_per_group),
        grid=(nt + 1,),
        in_specs=[pl.BlockSpec((tm, d), row), pl.BlockSpec((tm, d_pool), row), pl.BlockSpec((POOL_HALO, d_pool), halo),
                  pl.BlockSpec((tm, d_att), row), pl.BlockSpec((tm, 2 * d), row),
                  pl.BlockSpec(wp.shape, const3), pl.BlockSpec((1, d_pool), const),
                  pl.BlockSpec(wup.shape, const), pl.BlockSpec(wua.shape, const), pl.BlockSpec(wo.shape, const),
                  pl.BlockSpec((1, d), const), pl.BlockSpec(wr.shape, const), pl.BlockSpec((1, LANES), const),
                  pl.BlockSpec(memory_space=pl.ANY)],
        out_specs=[pl.BlockSpec((tm, d), row), pl.BlockSpec((tile_rows, d), lambda i: (i, 0)),
                   pl.BlockSpec((tm, LANES), row), pl.BlockSpec((1, 1, LANES), lambda i: (clamp(i), 0, 0))],
        out_shape=[jax.ShapeDtypeStruct((n, d), F32), jax.ShapeDtypeStruct(((nt + 1) * tile_rows, d), BF16),
                   jax.ShapeDtypeStruct((n, LANES), F32), jax.ShapeDtypeStruct((nt, 1, LANES), F32)],
        scratch_shapes=[pltpu.VMEM((tm + POOL_HALO, d_pool), F32), pltpu.SemaphoreType.DMA],
        compiler_params=_params("arbitrary"),
        name="merge_prompt",
    )(x, u, u, att, gates, wp, ps, wup, wua, wo, nf, wr, br, xs_last)


def _merge_sample_body(x_ref, u_ref, st_ref, att_ref, gate_ref, wp_ref, ps_ref, wup_ref, wua_ref, wo_ref,
                       nf_ref, wr_ref, br_ref, x2_ref, xs_ref, slab_ref, counts_ref,
                       *, start_pos, n_groups, n_per_group):
    u = u_ref[...]
    gw = u.shape[1] // len(POOL_WINDOWS)
    n_state = st_ref.shape[0]
    pooled = []
    for g, w in enumerate(POOL_WINDOWS):
        lo = g * gw
        wsum = u[:, lo:lo + gw]
        for j in range(1, w):
            wsum = wsum + st_ref[n_state - j][:, lo:lo + gw]
        pooled.append(wsum / float(min(start_pos + 1, w)) - u[:, lo:lo + gw])
    x2, h2, slab, counts = _merge_and_route(
        x_ref[...], pooled, att_ref[...], gate_ref[...], wp_ref, ps_ref, wup_ref, wua_ref, wo_ref, nf_ref,
        wr_ref, br_ref, precise=True, n_groups=n_groups, n_per_group=n_per_group)
    n = slab.shape[0]
    fields = jnp.concatenate([slab, jnp.zeros((LANES - n, LANES), F32)], axis=0).T
    x2_ref[...] = x2
    xs_ref[...] = _sorted_copy(h2, fields[6:7, 0:n], fields[7:8, 0:n], xs_ref.shape[0])
    slab_ref[...] = slab
    counts_ref[0] = counts


def _merge_sample(x, u, state_t, att, gates, wp, ps, wup, wua, wo, nf, wr, br, *, start_pos, n_groups,
                  n_per_group, tile_rows):
    n, d = x.shape
    assert n <= LANES
    return pl.pallas_call(
        functools.partial(_merge_sample_body, start_pos=start_pos, n_groups=n_groups, n_per_group=n_per_group),
        out_shape=[jax.ShapeDtypeStruct((n, d), F32), jax.ShapeDtypeStruct((tile_rows, d), BF16),
                   jax.ShapeDtypeStruct((n, LANES), F32), jax.ShapeDtypeStruct((1, 1, LANES), F32)],
        compiler_params=_params(),
        name="merge_sample",
    )(x, u, state_t, att, gates, wp, ps, wup, wua, wo, nf, wr, br)


def _moe_plan_body(cnt_ref, gsrc_ref, lsrc_ref, te_ref, tw_ref, loc_ref, *, n_ttiles, n_exp, tile_rows, gpt,
                   seg_groups_max, local_stride):
    n_mm = te_ref.shape[0]
    last_src_group = n_ttiles * tile_rows // ROW_GROUP - 1
    filler = lambda first, shape: jnp.minimum(first + lax.broadcasted_iota(I32, shape, 0), last_src_group) * ROW_GROUP
    gsrc_ref[...] = filler(0, gsrc_ref.shape)
    lsrc_ref[...] = jnp.zeros(lsrc_ref.shape, I32)

    def fill_tiles(t, c):
        te_ref[t] = -1
        tw_ref[t] = n_exp - 1
        return c

    def fill_loc(i, c):
        loc_ref[i] = 0
        return c

    lax.fori_loop(0, n_mm, fill_tiles, 0)
    lax.fori_loop(0, n_ttiles, fill_loc, 0)
    step = lax.broadcasted_iota(I32, (seg_groups_max, LANES), 0)

    def per_expert(e, pos):
        def per_tile(i, p):
            g = (cnt_ref[i * n_exp + e] + (ROW_GROUP - 1)) // ROW_GROUP
            loc = loc_ref[i]
            gsrc_ref[pl.ds(p, seg_groups_max), :] = i * tile_rows + (loc + step) * ROW_GROUP
            lsrc_ref[pl.ds(i * local_stride + loc, seg_groups_max), :] = p + step
            loc_ref[i] = loc + g
            return p + g

        end = lax.fori_loop(0, n_ttiles, per_tile, pos)
        end_pad = ((end + (gpt - 1)) // gpt) * gpt
        gsrc_ref[pl.ds(end, seg_groups_max), :] = filler(end, (seg_groups_max, LANES))

        def mark(t, c):
            te_ref[t] = e
            tw_ref[t] = e
            return c

        lax.fori_loop(pos // gpt, end_pad // gpt, mark, 0)
        return end_pad

    lax.fori_loop(0, n_exp, per_expert, 0)

    def clear_tail(i, c):
        lsrc_ref[pl.ds(i * local_stride + loc_ref[i], seg_groups_max), :] = (
            i * seg_groups_max + lax.broadcasted_iota(I32, (seg_groups_max, LANES), 0))
        return c

    lax.fori_loop(0, n_ttiles, clear_tail, 0)


def _moe_plan(cnt, *, n_ttiles, n_exp, tile_rows, gpt, n_mm_tiles, seg_groups_max):
    smem = pl.BlockSpec(memory_space=pltpu.SMEM)
    local_groups = tile_rows // ROW_GROUP
    local_stride = local_groups + seg_groups_max
    n_groups = n_mm_tiles * gpt
    assert n_ttiles * seg_groups_max <= n_groups, "filler entries of the local lists must name existing row groups"
    gsrc, lsrc, te, tw = pl.pallas_call(
        functools.partial(_moe_plan_body, n_ttiles=n_ttiles, n_exp=n_exp, tile_rows=tile_rows, gpt=gpt,
                          seg_groups_max=seg_groups_max, local_stride=local_stride),
        in_specs=[smem], out_specs=[pl.BlockSpec(memory_space=pltpu.VMEM), pl.BlockSpec(memory_space=pltpu.VMEM),
                                    smem, smem],
        out_shape=[jax.ShapeDtypeStruct((n_groups + 2 * seg_groups_max, LANES), I32),
                   jax.ShapeDtypeStruct((n_ttiles * local_stride, LANES), I32),
                   jax.ShapeDtypeStruct((n_mm_tiles,), I32), jax.ShapeDtypeStruct((n_mm_tiles,), I32)],
        scratch_shapes=[pltpu.SMEM((n_ttiles,), I32)],
        name="moe_plan",
    )(cnt)
    return gsrc[:n_groups, 0], lsrc[:, 0], te, tw, local_stride


def _moe_mm_body(te_ref, tw_ref, src_ref, xs_hbm, wg_ref, wu_ref, wd_ref, ys_ref, xbuf, wgb_ref, wub_ref, wdb_ref,
                 sems):
    i = pl.program_id(0)
    tm = ys_ref.shape[0]
    groups = tm // ROW_GROUP
    expert = te_ref[i]
    prev = te_ref[jnp.maximum(i - 1, 0)]

    def fetch(slot, tile):
        for k in range(groups):
            src = pl.multiple_of(src_ref[tile * groups + k], ROW_GROUP)
            pltpu.make_async_copy(xs_hbm.at[pl.ds(src, ROW_GROUP)],
                                  xbuf.at[slot, pl.ds(k * ROW_GROUP, ROW_GROUP)], sems.at[slot]).start()

    slot = i % 2

    last = pl.num_programs(0) - 1

    @pl.when((i == 0) & (expert >= 0))
    def _():
        fetch(0, 0)

    @pl.when((i < last) & (te_ref[jnp.minimum(i + 1, last)] >= 0))
    def _():
        fetch(1 - slot, i + 1)

    @pl.when(expert >= 0)
    def _():
        pltpu.make_async_copy(xs_hbm.at[pl.ds(0, tm)], xbuf.at[slot], sems.at[slot]).wait()

    @pl.when((expert >= 0) & ((i == 0) | (expert != prev)))
    def _():
        wgb_ref[...] = wg_ref[0].astype(BF16)
        wub_ref[...] = wu_ref[0].astype(BF16)
        wdb_ref[...] = wd_ref[0].astype(BF16)

    @pl.when(expert >= 0)
    def _():
        x = xbuf[slot]
        a = jnp.dot(x, wgb_ref[...], preferred_element_type=F32)
        b = jnp.dot(x, wub_ref[...], preferred_element_type=F32)
        hdn = (a * _sigmoid(a)) * b
        ys_ref[...] = jnp.dot(hdn.astype(BF16), wdb_ref[...], preferred_element_type=F32).astype(ys_ref.dtype)

    @pl.when(expert < 0)
    def _():
        ys_ref[...] = jnp.zeros_like(ys_ref)


def _moe_mm(tile_expert, tile_weight, group_src, xs, w_gate, w_up, w_down, *, tm):
    n_tiles = tile_expert.shape[0]
    d = xs.shape[1]
    n_exp, _, de = w_gate.shape
    wmap = lambda i, te, tw, src: (tw[i], 0, 0)
    grid_spec = pltpu.PrefetchScalarGridSpec(
        num_scalar_prefetch=3, grid=(n_tiles,),
        in_specs=[pl.BlockSpec(memory_space=pl.ANY),
                  pl.BlockSpec((1, d, de), wmap), pl.BlockSpec((1, d, de), wmap), pl.BlockSpec((1, de, d), wmap)],
        out_specs=pl.BlockSpec((tm, d), lambda i, te, tw, src: (i, 0)),
        scratch_shapes=[pltpu.VMEM((2, tm, d), BF16), pltpu.VMEM((d, de), BF16), pltpu.VMEM((d, de), BF16),
                        pltpu.VMEM((de, d), BF16), pltpu.SemaphoreType.DMA((2,))])
    return pl.pallas_call(
        _moe_mm_body, grid_spec=grid_spec,
        out_shape=jax.ShapeDtypeStruct((n_tiles * tm, d), BF16),
        compiler_params=_params("arbitrary"),
        name="moe_mm",
    )(tile_expert, tile_weight, group_src, xs, w_gate, w_up, w_down)


def _moe_combine_body(lsrc_ref, x_ref, slab_ref, g_ref, ys_hbm, o_ref, ybuf, sems,
                      *, final_norm, first_tile, local_stride):
    i = pl.program_id(0)
    tm = x_ref.shape[0]
    tile_rows = ybuf.shape[1]
    unroll = 8

    def fetch(slot, tile):
        def body(c, carry):
            for j in range(unroll):
                lg = c * unroll + j
                src = pl.multiple_of(lsrc_ref[tile * local_stride + lg] * ROW_GROUP, ROW_GROUP)
                dst = pl.multiple_of(lg * ROW_GROUP, ROW_GROUP)
                pltpu.make_async_copy(ys_hbm.at[pl.ds(src, ROW_GROUP)], ybuf.at[slot, pl.ds(dst, ROW_GROUP)],
                                      sems.at[slot]).start()
            return carry

        lax.fori_loop(0, tile_rows // ROW_GROUP // unroll, body, 0)

    slot = i % 2

    @pl.when(i == 0)
    def _():
        fetch(0, first_tile)

    @pl.when(i + 1 < pl.num_programs(0))
    def _():
        fetch(1 - slot, first_tile + i + 1)

    pltpu.make_async_copy(ys_hbm.at[pl.ds(0, tile_rows)], ybuf.at[slot], sems.at[slot]).wait()
    y = ybuf[slot]
    slab = slab_ref[...]
    r = lax.broadcasted_iota(I32, (tm, tile_rows), 1).astype(F32)
    ya = jnp.dot((r == slab[:, 6:7]).astype(BF16), y, preferred_element_type=F32)
    yb = jnp.dot((r == slab[:, 7:8]).astype(BF16), y, preferred_element_type=F32)
    out = x_ref[...] + (slab[:, 4:5] * ya + slab[:, 5:6] * yb)
    if final_norm:
        out = _rmsnorm(out, g_ref[...])
    o_ref[...] = out


def _moe_combine(local_src, x, slab, g, ys, *, ts, tile_rows, final_norm, first_tile, local_stride):
    n, d = x.shape
    return pl.pallas_call(
        functools.partial(_moe_combine_body, final_norm=final_norm, first_tile=first_tile,
                          local_stride=local_stride),
        grid=(n // ts,),
        in_specs=[pl.BlockSpec(memory_space=pltpu.SMEM),
                  pl.BlockSpec((ts, d), lambda i: (i, 0)), pl.BlockSpec((ts, LANES), lambda i: (i, 0)),
                  pl.BlockSpec((1, d), lambda i: (0, 0)), pl.BlockSpec(memory_space=pl.ANY)],
        out_specs=pl.BlockSpec((ts, d), lambda i: (i, 0)),
        out_shape=jax.ShapeDtypeStruct((n, d), F32),
        scratch_shapes=[pltpu.VMEM((2, tile_rows, d), BF16), pltpu.SemaphoreType.DMA((2,))],
        compiler_params=_params("arbitrary"),
        name="moe_combine",
    )(local_src, x, slab, g, ys)


def kernel(x_prompt, x_sample, cache_k, cache_v, cache_logf, state_pool, page_table, norm_mix, w_in, b_forget,
           w_pool, pool_scale, w_up_pool, w_up_att, w_out, norm_ffn, w_router_group, b_router_group,
           w_router_expert, b_router_expert, w_gate, w_up, w_down, norm_final):
    depth = norm_mix.shape[0]
    assert depth == 1, "single trunk layer"
    b, t, d = x_prompt.shape
    db, dt, _ = x_sample.shape
    assert dt == 1, "one sample token per sequence"
    _, _, page, n_heads, dh = cache_k.shape
    n_pages = page_table.shape[1]
    past = n_pages * page
    n_state, d_pool = state_pool.shape[2], state_pool.shape[3]
    d_att = n_heads * dh
    n_pool_groups = w_pool.shape[1]
    assert n_pool_groups == len(POOL_WINDOWS) and d_pool // n_pool_groups == LANES
    assert n_state == max(POOL_WINDOWS) - 1 and n_state < POOL_HALO
    n_groups, n_per_group = w_router_expert.shape[1], w_router_expert.shape[3]
    n_exp = n_groups * n_per_group
    assert n_groups + n_exp <= LANES and 2 * dh == LANES and n_heads % 2 == 0
    n = b * t
    q_scale = float(dh) ** -0.5
    tm = min(TOKEN_TILE, t)
    assert t % tm == 0 and t % ATTN_TILE == 0

    o_main = d_pool + 3 * d_att
    wi = w_in[0]
    wm_f, wf_f, wg_f = wi[:, :o_main], wi[:, o_main:o_main + n_heads], wi[:, o_main + n_heads:]
    wf_pad = jnp.pad(wf_f, ((0, 0), (0, LANES - n_heads)))
    wit = jnp.transpose(wi)
    wmt_f, wgt_f = wit[:o_main], wit[o_main + n_heads:]
    wft_pad = jnp.pad(wit[o_main:o_main + n_heads], ((0, LANES - n_heads), (0, 0)))
    bf_pad = jnp.pad(b_forget[0], (0, LANES - n_heads)).reshape(1, LANES)
    g_mix = norm_mix[0].reshape(1, d)
    g_ffn = norm_ffn[0].reshape(1, d)
    g_fin = norm_final.reshape(1, d)
    ps = pool_scale[0].reshape(1, d_pool)
    wr_f = jnp.concatenate([w_router_group[0], jnp.transpose(w_router_expert[0], (1, 0, 2)).reshape(d, n_exp)], axis=1)
    wr_pad = jnp.pad(wr_f, ((0, 0), (0, LANES - n_groups - n_exp)))
    br_pad = jnp.pad(jnp.concatenate([b_router_group[0], b_router_expert[0].reshape(n_exp)]),
                     (0, LANES - n_groups - n_exp)).reshape(1, LANES)
    bf = lambda a: a.astype(BF16)

    xp = x_prompt.reshape(n, d)
    u_p, q_p, kt_p, vt_p, kb_p, vb_p, lft_p, gate_p = _proj_prompt(
        xp, g_mix, bf(wm_f), bf(wf_pad), bf(wg_f), bf_pad, tm=tm, seq_len=t, d_pool=d_pool, d_att=d_att,
        n_heads=n_heads, q_scale=q_scale)
    c = _cumsum_lanes(lft_p.reshape(b * n_heads, t))
    nt = t // ATTN_TILE
    c_blk = jnp.transpose(c.reshape(b, n_heads // 2, 2, nt, ATTN_TILE), (0, 1, 3, 2, 4))
    n_ptiles = n // tm
    n_ttiles = n_ptiles + 1
    tile_rows = -(-(2 * tm + n_exp * (ROW_GROUP - 1)) // MOE_ROW_TILE) * MOE_ROW_TILE
    assert 2 * db + n_exp * (ROW_GROUP - 1) <= tile_rows, "the sample tile's sorted copy fits a tile"

    xs = x_sample.reshape(db, d)
    z_s, lf_s, gate_s = _proj_sample(xs, g_mix, wmt_f, wft_pad, wgt_f, bf_pad, tn=512)
    u_s = z_s[:, :d_pool]
    q_s = z_s[:, d_pool:d_pool + d_att] * q_scale
    k_s = z_s[:, d_pool + d_att:d_pool + 2 * d_att]
    v_s = z_s[:, d_pool + 2 * d_att:]
    att_p, att_s = _attention(q_p.reshape(b, t, d_att), kb_p.reshape(b, t, d_att), vb_p.reshape(b, t, d_att), c_blk,
                              page_table, q_s, k_s, v_s, lf_s[:, :n_heads],
                              jnp.transpose(cache_k[0], (0, 2, 3, 1)), jnp.transpose(cache_v[0], (0, 2, 3, 1)),
                              jnp.transpose(cache_logf[0], (0, 2, 1)), tile=ATTN_TILE, dh=dh)
    state_t = jnp.transpose(state_pool[0], (1, 0, 2))
    x2_s, xs_rows_s, slab_s, counts_s = _merge_sample(
        xs, u_s, state_t, att_s.reshape(db, d_att), gate_s, w_pool[0], ps, w_up_pool[0], w_up_att[0], w_out[0],
        g_ffn, wr_pad, br_pad, start_pos=past, n_groups=n_groups, n_per_group=n_per_group, tile_rows=tile_rows)

    x2_p, xs_rows, slab_p, counts_p = _merge_prompt(
        xp, u_p, att_p.reshape(n, d_att), gate_p, bf(w_pool[0]), ps, bf(w_up_pool[0]), bf(w_up_att[0]),
        bf(w_out[0]), g_ffn, bf(wr_pad), br_pad, xs_rows_s, tm=tm, seq_len=t, n_groups=n_groups,
        n_per_group=n_per_group)

    tmm = MOE_ROW_TILE
    gpt = tmm // ROW_GROUP
    n_groups_max = -(-(2 * (n + db)) // ROW_GROUP) + n_ttiles * n_exp + n_exp * (gpt - 1)
    cnt = jnp.concatenate([counts_p, counts_s], axis=0)[:, 0, :n_exp].astype(I32).reshape(-1)
    group_src, local_src, tile_expert, tile_weight, local_stride = _moe_plan(
        cnt, n_ttiles=n_ttiles, n_exp=n_exp, tile_rows=tile_rows, gpt=gpt, n_mm_tiles=-(-n_groups_max // gpt),
        seg_groups_max=tm // ROW_GROUP)
    ys_rows = _moe_mm(tile_expert, tile_weight, group_src, xs_rows, w_gate[0], w_up[0], w_down[0], tm=tmm)
    y_prompt = _moe_combine(local_src, x2_p, slab_p, g_fin, ys_rows, ts=tm, tile_rows=tile_rows,
                            final_norm=True, first_tile=0, local_stride=local_stride)
    y_sample = _moe_combine(local_src, x2_s, slab_s, g_fin, ys_rows, ts=db, tile_rows=tile_rows,
                            final_norm=True, first_tile=n_ptiles, local_stride=local_stride)

    new_pool_p = u_p.reshape(b, t, d_pool)[:, t - n_state:, :]
    new_pool_s = jnp.concatenate([state_pool[0][:, 1:, :], u_s[:, None, :]], axis=1)
    to_heads = lambda a: jnp.transpose(a.reshape(b, n_heads, dh, t), (0, 3, 1, 2))[None]
    return (y_prompt.reshape(b, t, d), y_sample.reshape(db, 1, d),
            to_heads(kt_p), to_heads(vt_p), jnp.transpose(lft_p, (0, 2, 1))[None],
            new_pool_p[None],
            k_s.reshape(1, db, 1, n_heads, dh), v_s.reshape(1, db, 1, n_heads, dh),
            lf_s[:, :n_heads].reshape(1, db, 1, n_heads), new_pool_s[None])
```

```python
import functools

import jax
import jax.numpy as jnp
from jax import lax
from jax.experimental import pallas as pl
from jax.experimental.pallas import tpu as pltpu

F32 = jnp.float32
BF16 = jnp.bfloat16
I32 = jnp.int32
HIGHEST = lax.Precision.HIGHEST

RMS_EPS = 1e-6
POOL_WINDOWS = (2, 4, 8, 16)
POOL_HALO = 16
LANES = 128
VMEM_LIMIT_BYTES = 56 * 1024 * 1024

TOKEN_TILE = 512
ATTN_TILE = 512
MOE_ROW_TILE = 512
PAGES_PER_STEP = 32
ROW_GROUP = 16


def _params(*sem):
    return pltpu.CompilerParams(dimension_semantics=sem, vmem_limit_bytes=VMEM_LIMIT_BYTES)


def _rmsnorm(x, g):
    return x * lax.rsqrt(jnp.mean(x * x, axis=-1, keepdims=True) + RMS_EPS) * g


def _log_sigmoid(x):
    return jnp.minimum(x, 0.0) - jnp.log1p(jnp.exp(-jnp.abs(x)))


def _sigmoid(x):
    return 1.0 / (1.0 + jnp.exp(-x))


def _dot(a, b, precise):
    if precise:
        return jnp.dot(a.astype(F32), b.astype(F32), precision=HIGHEST, preferred_element_type=F32)
    return jnp.dot(a.astype(BF16), b.astype(BF16), preferred_element_type=F32)


def _split3(x):
    hi = x.astype(BF16)
    r = x - hi.astype(F32)
    mid = r.astype(BF16)
    lo = (r - mid.astype(F32)).astype(BF16)
    return hi, mid, lo


def _dot_exact_rhs(x, w_bf16):
    hi, mid, lo = _split3(x)
    d = lambda a: jnp.dot(a, w_bf16, preferred_element_type=F32)
    return d(hi) + d(mid) + d(lo)


def _proj_body(x_ref, g_ref, wm_ref, wf_ref, wg_ref, bf_ref,
               u_ref, q_ref, kt_ref, vt_ref, kb_ref, vb_ref, lft_ref, gate_ref, *, d_pool, d_att, n_heads, q_scale):
    h = _rmsnorm(x_ref[...], g_ref[...]).astype(BF16)
    z = jnp.dot(h, wm_ref[...], preferred_element_type=F32)
    o1, o2, o3 = d_pool, d_pool + d_att, d_pool + 2 * d_att
    u_ref[...] = z[:, :o1]
    q_ref[...] = (z[:, o1:o2] * q_scale).astype(BF16)
    k = z[:, o2:o3]
    v = z[:, o3:]
    kt_ref[0] = k.T
    vt_ref[0] = v.T
    kb_ref[...] = k.astype(BF16)
    vb_ref[...] = v.astype(BF16)
    lf = _log_sigmoid(jnp.dot(h, wf_ref[...], preferred_element_type=F32) + bf_ref[...])
    lft_ref[0] = lf.T[0:n_heads, :]
    gate_ref[...] = _sigmoid(jnp.dot(h, wg_ref[...], preferred_element_type=F32)).astype(BF16)


def _proj_prompt(x, g, wm, wf, wg, bfp, *, tm, seq_len, d_pool, d_att, n_heads, q_scale):
    n, d = x.shape
    b = n // seq_len
    tps = seq_len // tm
    row = lambda i: (i, 0)
    const = lambda i: (0, 0)
    tmin = lambda i: (i // tps, 0, i % tps)
    dg = wg.shape[1]
    out_shape = [
        jax.ShapeDtypeStruct((n, d_pool), F32), jax.ShapeDtypeStruct((n, d_att), BF16),
        jax.ShapeDtypeStruct((b, d_att, seq_len), F32), jax.ShapeDtypeStruct((b, d_att, seq_len), F32),
        jax.ShapeDtypeStruct((n, d_att), BF16), jax.ShapeDtypeStruct((n, d_att), BF16),
        jax.ShapeDtypeStruct((b, n_heads, seq_len), F32), jax.ShapeDtypeStruct((n, dg), BF16),
    ]
    return pl.pallas_call(
        functools.partial(_proj_body, d_pool=d_pool, d_att=d_att, n_heads=n_heads, q_scale=q_scale),
        grid=(n // tm,),
        in_specs=[pl.BlockSpec((tm, d), row), pl.BlockSpec((1, d), const),
                  pl.BlockSpec(wm.shape, const), pl.BlockSpec(wf.shape, const),
                  pl.BlockSpec(wg.shape, const), pl.BlockSpec((1, LANES), const)],
        out_specs=[pl.BlockSpec((tm, d_pool), row), pl.BlockSpec((tm, d_att), row),
                   pl.BlockSpec((1, d_att, tm), tmin), pl.BlockSpec((1, d_att, tm), tmin),
                   pl.BlockSpec((tm, d_att), row), pl.BlockSpec((tm, d_att), row),
                   pl.BlockSpec((1, n_heads, tm), tmin), pl.BlockSpec((tm, dg), row)],
        out_shape=out_shape,
        compiler_params=_params("arbitrary"),
        name="proj_prompt",
    )(x, g, wm, wf, wg, bfp)


def _dot_nt(a, bt, precise):
    dims = (((1,), (1,)), ((), ()))
    if precise:
        return lax.dot_general(a.astype(F32), bt.astype(F32), dims, precision=HIGHEST, preferred_element_type=F32)
    return lax.dot_general(a.astype(BF16), bt.astype(BF16), dims, preferred_element_type=F32)


def _proj_sample_body(x_ref, g_ref, wmt_ref, wft_ref, wgt_ref, bf_ref, z_ref, lf_ref, gate_ref):
    h = _rmsnorm(x_ref[...], g_ref[...])
    z_ref[...] = _dot_nt(h, wmt_ref[...], True)
    lf_ref[...] = _log_sigmoid(_dot_nt(h, wft_ref[...], True) + bf_ref[...])
    gate_ref[...] = _sigmoid(_dot_nt(h, wgt_ref[...], True))


def _proj_sample(x, g, wmt, wft, wgt, bfp, *, tn):
    n, d = x.shape
    dm, dg = wmt.shape[0], wgt.shape[0]
    assert dm == dg
    const = lambda j: (0, 0)
    chunk = lambda j: (j, 0)
    col = lambda j: (0, j)
    return pl.pallas_call(
        _proj_sample_body,
        grid=(dm // tn,),
        in_specs=[pl.BlockSpec((n, d), const), pl.BlockSpec((1, d), const),
                  pl.BlockSpec((tn, d), chunk), pl.BlockSpec(wft.shape, const),
                  pl.BlockSpec((tn, d), chunk), pl.BlockSpec((1, LANES), const)],
        out_specs=[pl.BlockSpec((n, tn), col), pl.BlockSpec((n, LANES), const), pl.BlockSpec((n, tn), col)],
        out_shape=[jax.ShapeDtypeStruct((n, dm), F32), jax.ShapeDtypeStruct((n, LANES), F32),
                   jax.ShapeDtypeStruct((n, dg), F32)],
        compiler_params=_params("arbitrary"),
        name="proj_sample",
    )(x, g, wmt, wft, wgt, bfp)


def _cumsum_body(x_ref, o_ref):
    c = x_ref[...]
    lane = lax.broadcasted_iota(I32, c.shape, 1)
    s = 1
    while s < c.shape[1]:
        c = c + jnp.where(lane >= s, pltpu.roll(c, s, 1), 0.0)
        s *= 2
    o_ref[...] = c


def _cumsum_lanes(x):
    return pl.pallas_call(_cumsum_body, out_shape=jax.ShapeDtypeStruct(x.shape, F32),
                          compiler_params=_params(), name="cumsum_logf")(x)


def _prompt_q_tile(qi, q_ref, kts, vhs, c_ref, o_ref, *, tile, dh, first, causal):
    q = q_ref[0, qi * tile:(qi + 1) * tile, :]
    zero = jnp.zeros_like(q)
    q_heads = (jnp.where(first, q, zero), jnp.where(first, zero, q))
    res = []
    for h in range(2):
        m = jnp.full((tile, 1), -1e30, F32)
        acc = jnp.zeros((tile, 2 * dh), F32)
        for kj in range(qi + 1):
            s = lax.dot_general(q_heads[h], kts[kj], (((1,), (1,)), ((), ())), preferred_element_type=F32)
            s = s - c_ref[0, 0, kj][h:h + 1, :]
            if kj == qi:
                s = jnp.where(causal, s, -jnp.inf)
            m_new = jnp.maximum(m, jnp.max(s, axis=-1, keepdims=True))
            alpha = jnp.exp(m - m_new)
            p = jnp.exp(s - m_new)
            acc = alpha * acc + jnp.dot(p.astype(BF16), vhs[kj][h], preferred_element_type=F32)
            m = m_new
        res.append(acc)
    a0, a1 = res
    out = jnp.where(first, a0 / a0[:, dh:dh + 1], a1 / a1[:, 0:1])
    o_ref[0, qi * tile:(qi + 1) * tile, :] = out.astype(o_ref.dtype)


def _sample_chunk(k_refs, v_refs, lf_refs, qrep, carry):
    m_prev, l, acc, s_run = carry
    n_heads, dh, page = acc.shape
    d_att = n_heads * dh
    g_n = len(k_refs)
    r = lax.broadcasted_iota(I32, (page, page), 0)
    c = lax.broadcasted_iota(I32, (page, page), 1)
    later = (r > c).astype(BF16)
    ones = jnp.ones((page, page), BF16)
    lf_all = jnp.concatenate([lf_refs[g][...] for g in range(g_n)], axis=0)
    suffix = _dot_exact_rhs(lf_all, later)
    total = _dot_exact_rhs(lf_all, ones)
    m_new = m_prev
    scores = []
    for g in range(g_n):
        kq = k_refs[g][...].reshape(d_att, page) * qrep
        s = jnp.sum(kq.reshape(n_heads, dh, page), axis=1)
        sb = s + s_run + suffix[g * n_heads:(g + 1) * n_heads]
        s_run = s_run + total[g * n_heads:(g + 1) * n_heads]
        scores.append(sb)
        m_new = jnp.maximum(m_new, jnp.max(sb, axis=-1, keepdims=True))
    alpha = jnp.exp(m_prev - m_new)
    l = alpha * l
    acc = acc * alpha[:, None, :]
    for g in range(g_n):
        p = jnp.exp(scores[g] - m_new)
        l = l + jnp.sum(p, axis=-1, keepdims=True)
        acc = acc + v_refs[g][...] * p[:, None, :]
    return m_new, l, acc, s_run


def _attn_body(pt_ref, q_ref, k_ref, v_ref, c_ref, qrep_ref, qs_ref, kn_ref, vrep_ref, lfn_ref,
               ck_hbm, cv_hbm, clf_hbm, o_ref, os_ref, kbuf, vbuf, lfbuf, sems, *, tile, dh):
    step = pl.program_id(0) * pl.num_programs(1) + pl.program_id(1)
    n_steps = pl.num_programs(0) * pl.num_programs(1)
    nt = q_ref.shape[1] // tile
    n_pages = pt_ref.shape[1]
    _, g_n, n_heads, _, page = kbuf.shape
    d_att = n_heads * dh
    seqs = qrep_ref.shape[0]
    n_chunks = n_pages // g_n
    n_items = seqs * n_chunks

    def fetch(slot, seq, chunk):
        for g in range(g_n):
            pid = pt_ref[seq, n_pages - 1 - (chunk * g_n + g)]
            pltpu.make_async_copy(ck_hbm.at[pid], kbuf.at[slot, g], sems.at[slot, 0]).start()
            pltpu.make_async_copy(cv_hbm.at[pid], vbuf.at[slot, g], sems.at[slot, 1]).start()
            pltpu.make_async_copy(clf_hbm.at[pid], lfbuf.at[slot, g], sems.at[slot, 2]).start()

    @pl.when(step == 0)
    def _():
        fetch(0, 0, 0)

    lane = lax.broadcasted_iota(I32, (tile, 2 * dh), 1)
    first = lane < dh
    row = lax.broadcasted_iota(I32, (tile, tile), 0)
    col = lax.broadcasted_iota(I32, (tile, tile), 1)
    causal = col <= row
    one = jnp.ones((tile, 2 * dh), BF16)
    kts, vhs = [], []
    for kj in range(nt):
        vt = v_ref[0, kj * tile:(kj + 1) * tile, :]
        kts.append(k_ref[0, kj * tile:(kj + 1) * tile, :])
        vhs.append((jnp.where(first, vt, one), jnp.where(first, one, vt)))

    q_done = 0
    carry = None
    for item in range(n_items):
        j, chunk = divmod(item, n_chunks)
        slot = item % 2
        if item + 1 < n_items:
            fetch(1 - slot, step * seqs + (item + 1) // n_chunks, (item + 1) % n_chunks)
        else:
            @pl.when(step + 1 < n_steps)
            def _():
                fetch(1 - slot, (step + 1) * seqs, 0)
        pltpu.make_async_copy(ck_hbm.at[pl.ds(0, g_n)], kbuf.at[slot], sems.at[slot, 0]).wait()
        pltpu.make_async_copy(cv_hbm.at[pl.ds(0, g_n)], vbuf.at[slot], sems.at[slot, 1]).wait()
        pltpu.make_async_copy(clf_hbm.at[pl.ds(0, g_n)], lfbuf.at[slot], sems.at[slot, 2]).wait()
        if chunk == 0:
            s_new = jnp.sum(qs_ref[j] * kn_ref[j], axis=-1, keepdims=True)
            lane_p = lax.broadcasted_iota(I32, (d_att, page), 1)
            carry = (jnp.broadcast_to(s_new, (n_heads, page)), jnp.ones((n_heads, page), F32),
                     jnp.where(lane_p == 0, vrep_ref[j], 0.0).reshape(n_heads, dh, page), lfn_ref[j])
        carry = _sample_chunk([kbuf.at[slot, g] for g in range(g_n)], [vbuf.at[slot, g] for g in range(g_n)],
                              [lfbuf.at[slot, g] for g in range(g_n)], qrep_ref[j], carry)
        if chunk == n_chunks - 1:
            _, l, acc, _ = carry
            os_ref[j] = jnp.sum(acc / l[:, None, :], axis=-1)
        q_until = ((item + 1) * nt) // n_items
        for qi in range(q_done, q_until):
            _prompt_q_tile(qi, q_ref, kts, vhs, c_ref, o_ref, tile=tile, dh=dh, first=first, causal=causal)
        q_done = q_until


def _attention(q, k, v, c, page_table, q_s, k_new, v_new, lf_new, cache_kt, cache_vt, cache_lft, *, tile, dh):
    b, t, da = q.shape
    hp = da // (2 * dh)
    nt = t // tile
    db, n_pages = page_table.shape
    _, n_heads, _, page = cache_kt.shape
    d_att = n_heads * dh
    n_steps = b * hp
    assert db % n_steps == 0, "sample sequences are split evenly over the prompt grid steps"
    seqs = db // n_steps
    g_n = PAGES_PER_STEP
    while n_pages % g_n:
        g_n //= 2
    assert (seqs * (n_pages // g_n)) % 2 == 0, "buffer slots alternate per page chunk"
    lane_rep = lambda a: jnp.broadcast_to(a.reshape(db, -1, 1), (db, a.size // db, page))
    pair = pl.BlockSpec((1, t, 2 * dh), lambda bi, hi, pt: (bi, 0, hi))
    per_step = lambda bi, hi, pt: (bi * hp + hi, 0, 0)
    hbm = pl.BlockSpec(memory_space=pl.ANY)
    grid_spec = pltpu.PrefetchScalarGridSpec(
        num_scalar_prefetch=1, grid=(b, hp),
        in_specs=[pair, pair, pair, pl.BlockSpec((1, 1, nt, 2, tile), lambda bi, hi, pt: (bi, hi, 0, 0, 0)),
                  pl.BlockSpec((seqs, d_att, page), per_step), pl.BlockSpec((seqs, n_heads, dh), per_step),
                  pl.BlockSpec((seqs, n_heads, dh), per_step), pl.BlockSpec((seqs, d_att, page), per_step),
                  pl.BlockSpec((seqs, n_heads, page), per_step), hbm, hbm, hbm],
        out_specs=[pair, pl.BlockSpec((seqs, n_heads, dh), per_step)],
        scratch_shapes=[pltpu.VMEM((2, g_n, n_heads, dh, page), F32), pltpu.VMEM((2, g_n, n_heads, dh, page), F32),
                        pltpu.VMEM((2, g_n, n_heads, page), F32), pltpu.SemaphoreType.DMA((2, 3))])
    return pl.pallas_call(
        functools.partial(_attn_body, tile=tile, dh=dh),
        grid_spec=grid_spec,
        out_shape=[jax.ShapeDtypeStruct((b, t, da), BF16), jax.ShapeDtypeStruct((db, n_heads, dh), F32)],
        compiler_params=_params("arbitrary", "arbitrary"),
        name="attention",
    )(page_table, q, k, v, c, lane_rep(q_s), q_s.reshape(db, n_heads, dh), k_new.reshape(db, n_heads, dh),
      lane_rep(v_new), lane_rep(lf_new), cache_kt, cache_vt, cache_lft)


def _merge_and_route(x, pooled, att, gates, wp_ref, ps_ref, wup_ref, wua_ref, wo_ref, nf_ref, wr_ref, br_ref,
                     *, precise, n_groups, n_per_group):
    tm, d = x.shape
    mixed = jnp.concatenate([_dot(pooled[g], wp_ref[g], precise) for g in range(len(pooled))], axis=-1)
    pool_out = mixed * ps_ref[...]
    y = gates[:, :d].astype(F32) * _dot(pool_out, wup_ref[...], precise) \
        + gates[:, d:].astype(F32) * _dot(att, wua_ref[...], precise)
    x2 = x + _dot(y, wo_ref[...], precise)
    h2 = _rmsnorm(x2, nf_ref[...])
    logits = _dot(h2, wr_ref[...], precise) + br_ref[...]
    lane = lax.broadcasted_iota(I32, logits.shape, 1)
    lanef = lane.astype(F32)
    neg = -jnp.inf
    is_g = lane < n_groups
    gmax = jnp.max(jnp.where(is_g, logits, neg), axis=-1, keepdims=True)
    gidx = jnp.min(jnp.where(is_g & (logits == gmax), lanef, float(LANES)), axis=-1, keepdims=True)
    gsum = jnp.sum(jnp.where(is_g, jnp.exp(logits - gmax), 0.0), axis=-1, keepdims=True)
    g_w = 1.0 / gsum
    n_exp = n_groups * n_per_group
    exp_id = lanef - float(n_groups)
    in_sel = (lane >= n_groups) & (lane < n_groups + n_exp) & (jnp.floor(exp_id / n_per_group) == gidx)
    v1 = jnp.max(jnp.where(in_sel, logits, neg), axis=-1, keepdims=True)
    i1 = jnp.min(jnp.where(in_sel & (logits == v1), lanef, float(LANES)), axis=-1, keepdims=True)
    in_sel2 = in_sel & (lanef != i1)
    v2 = jnp.max(jnp.where(in_sel2, logits, neg), axis=-1, keepdims=True)
    i2 = jnp.min(jnp.where(in_sel2 & (logits == v2), lanef, float(LANES)), axis=-1, keepdims=True)
    t = jnp.exp(v2 - v1)
    w1 = g_w * (1.0 / (1.0 + t))
    w2 = g_w * (t / (1.0 + t))
    e1 = i1 - float(n_groups)
    e2 = i2 - float(n_groups)
    hit1 = lanef == e1
    hit2 = lanef == e2
    onehot = (hit1 | hit2).astype(BF16)
    rr = lax.broadcasted_iota(I32, (tm, tm), 0)
    cc = lax.broadcasted_iota(I32, (tm, tm), 1)
    incl = jnp.dot((cc <= rr).astype(BF16), onehot, preferred_element_type=F32)
    counts = incl[tm - 1:tm, :]
    groups = jnp.floor((counts + (ROW_GROUP - 1.0)) * (1.0 / ROW_GROUP))
    ur = lax.broadcasted_iota(I32, (LANES, LANES), 0)
    uc = lax.broadcasted_iota(I32, (LANES, LANES), 1)
    before = jnp.dot(jnp.broadcast_to(groups, (8, LANES)).astype(BF16), (ur < uc).astype(BF16),
                     preferred_element_type=F32)[0:1]
    seg_start = before * float(ROW_GROUP)
    pick = lambda hit, tbl: jnp.sum(jnp.where(hit, tbl, 0.0), axis=-1, keepdims=True)
    r1 = pick(hit1, incl) - 1.0
    r2 = pick(hit2, incl) - 1.0
    row1 = pick(hit1, seg_start) + r1
    row2 = pick(hit2, seg_start) + r2
    slab = jnp.zeros((tm, LANES), F32)
    for i, val in enumerate((e1, e2, r1, r2, w1, w2, row1, row2)):
        slab = jnp.where(lane == i, val, slab)
    return x2, h2, slab, counts


def _sorted_copy(h2, row1, row2, n_rows):
    tm = h2.shape[0]
    r = lax.broadcasted_iota(I32, (n_rows, tm), 0).astype(F32)
    place = ((r == row1) | (r == row2)).astype(BF16)
    return jnp.dot(place, h2.astype(BF16), preferred_element_type=F32).astype(BF16)


def _merge_prompt_body(x_ref, u_ref, halo_ref, att_ref, gate_ref, wp_ref, ps_ref, wup_ref, wua_ref, wo_ref,
                       nf_ref, wr_ref, br_ref, xs_last_hbm, x2_ref, xs_ref, slab_ref, counts_ref,
                       ext_ref, sem, *, seq_len, n_groups, n_per_group):
    i = pl.program_id(0)
    n_tiles = pl.num_programs(0) - 1
    tm = x_ref.shape[0]
    gw = u_ref.shape[1] // len(POOL_WINDOWS)

    @pl.when(i < n_tiles)
    def _():
        pos0 = (i * tm) % seq_len
        u = u_ref[...]
        ext_ref[0:POOL_HALO, :] = jnp.where(pos0 == 0, 0.0, halo_ref[...])
        ext_ref[POOL_HALO:, :] = u
        pos = pos0 + lax.broadcasted_iota(I32, (tm, 1), 0)
        pooled = []
        for g, w in enumerate(POOL_WINDOWS):
            lo = g * gw
            wsum = ext_ref[pl.ds(POOL_HALO, tm), lo:lo + gw]
            for j in range(1, w):
                wsum = wsum + ext_ref[pl.ds(POOL_HALO - j, tm), lo:lo + gw]
            count = jnp.minimum(pos + 1, w).astype(F32)
            pooled.append(wsum / count - u[:, lo:lo + gw])
        x2, h2, slab, counts = _merge_and_route(
            x_ref[...], pooled, att_ref[...], gate_ref[...], wp_ref, ps_ref, wup_ref, wua_ref, wo_ref, nf_ref,
            wr_ref, br_ref, precise=False, n_groups=n_groups, n_per_group=n_per_group)
        fields = slab.T[0:8, :]
        x2_ref[...] = x2
        xs_ref[...] = _sorted_copy(h2, fields[6:7, :], fields[7:8, :], xs_ref.shape[0])
        slab_ref[...] = slab
        counts_ref[0] = counts

    @pl.when(i == n_tiles)
    def _():
        copy = pltpu.make_async_copy(xs_last_hbm, xs_ref, sem)
        copy.start()
        copy.wait()


def _merge_prompt(x, u, att, gates, wp, ps, wup, wua, wo, nf, wr, br, xs_last, *, tm, seq_len, n_groups, n_per_group):
    n, d = x.shape
    d_pool, d_att = u.shape[1], att.shape[1]
    tile_rows = xs_last.shape[0]
    nt = n // tm
    clamp = lambda i: jnp.minimum(i, nt - 1)
    row = lambda i: (clamp(i), 0)
    const = lambda i: (0, 0)
    const3 = lambda i: (0, 0, 0)
    halo = lambda i: (jnp.maximum(clamp(i) * (tm // POOL_HALO) - 1, 0), 0)
    return pl.pallas_call(
        functools.partial(_merge_prompt_body, seq_len=seq_len, n_groups=n_groups, n_per_group=n_per_group),
        grid=(nt + 1,),
        in_specs=[pl.BlockSpec((tm, d), row), pl.BlockSpec((tm, d_pool), row), pl.BlockSpec((POOL_HALO, d_pool), halo),
                  pl.BlockSpec((tm, d_att), row), pl.BlockSpec((tm, 2 * d), row),
                  pl.BlockSpec(wp.shape, const3), pl.BlockSpec((1, d_pool), const),
                  pl.BlockSpec(wup.shape, const), pl.BlockSpec(wua.shape, const), pl.BlockSpec(wo.shape, const),
                  pl.BlockSpec((1, d), const), pl.BlockSpec(wr.shape, const), pl.BlockSpec((1, LANES), const),
                  pl.BlockSpec(memory_space=pl.ANY)],
        out_specs=[pl.BlockSpec((tm, d), row), pl.BlockSpec((tile_rows, d), lambda i: (i, 0)),
                   pl.BlockSpec((tm, LANES), row), pl.BlockSpec((1, 1, LANES), lambda i: (clamp(i), 0, 0))],
        out_shape=[jax.ShapeDtypeStruct((n, d), F32), jax.ShapeDtypeStruct(((nt + 1) * tile_rows, d), BF16),
                   jax.ShapeDtypeStruct((n, LANES), F32), jax.ShapeDtypeStruct((nt, 1, LANES), F32)],
        scratch_shapes=[pltpu.VMEM((tm + POOL_HALO, d_pool), F32), pltpu.SemaphoreType.DMA],
        compiler_params=_params("arbitrary"),
        name="merge_prompt",
    )(x, u, u, att, gates, wp, ps, wup, wua, wo, nf, wr, br, xs_last)


def _merge_sample_body(x_ref, u_ref, st_ref, att_ref, gate_ref, wp_ref, ps_ref, wup_ref, wua_ref, wo_ref,
                       nf_ref, wr_ref, br_ref, x2_ref, xs_ref, slab_ref, counts_ref,
                       *, start_pos, n_groups, n_per_group):
    u = u_ref[...]
    gw = u.shape[1] // len(POOL_WINDOWS)
    n_state = st_ref.shape[0]
    pooled = []
    for g, w in enumerate(POOL_WINDOWS):
        lo = g * gw
        wsum = u[:, lo:lo + gw]
        for j in range(1, w):
            wsum = wsum + st_ref[n_state - j][:, lo:lo + gw]
        pooled.append(wsum / float(min(start_pos + 1, w)) - u[:, lo:lo + gw])
    x2, h2, slab, counts = _merge_and_route(
        x_ref[...], pooled, att_ref[...], gate_ref[...], wp_ref, ps_ref, wup_ref, wua_ref, wo_ref, nf_ref,
        wr_ref, br_ref, precise=True, n_groups=n_groups, n_per_group=n_per_group)
    n = slab.shape[0]
    fields = jnp.concatenate([slab, jnp.zeros((LANES - n, LANES), F32)], axis=0).T
    x2_ref[...] = x2
    xs_ref[...] = _sorted_copy(h2, fields[6:7, 0:n], fields[7:8, 0:n], xs_ref.shape[0])
    slab_ref[...] = slab
    counts_ref[0] = counts


def _merge_sample(x, u, state_t, att, gates, wp, ps, wup, wua, wo, nf, wr, br, *, start_pos, n_groups,
                  n_per_group, tile_rows):
    n, d = x.shape
    assert n <= LANES
    return pl.pallas_call(
        functools.partial(_merge_sample_body, start_pos=start_pos, n_groups=n_groups, n_per_group=n_per_group),
        out_shape=[jax.ShapeDtypeStruct((n, d), F32), jax.ShapeDtypeStruct((tile_rows, d), BF16),
                   jax.ShapeDtypeStruct((n, LANES), F32), jax.ShapeDtypeStruct((1, 1, LANES), F32)],
        compiler_params=_params(),
        name="merge_sample",
    )(x, u, state_t, att, gates, wp, ps, wup, wua, wo, nf, wr, br)


def _moe_plan_body(cnt_ref, gsrc_ref, lsrc_ref, te_ref, tw_ref, loc_ref, *, n_ttiles, n_exp, tile_rows, gpt,
                   seg_groups_max, local_stride, zero_group):
    n_mm = te_ref.shape[0]
    gsrc_ref[...] = jnp.full(gsrc_ref.shape, zero_group, I32)
    lsrc_ref[...] = jnp.zeros(lsrc_ref.shape, I32)

    def fill_tiles(t, c):
        te_ref[t] = -1
        tw_ref[t] = n_exp - 1
        return c

    def fill_loc(i, c):
        loc_ref[i] = 0
        return c

    lax.fori_loop(0, n_mm, fill_tiles, 0)
    lax.fori_loop(0, n_ttiles, fill_loc, 0)
    step = lax.broadcasted_iota(I32, (seg_groups_max, LANES), 0)

    def per_expert(e, pos):
        def per_tile(i, p):
            g = (cnt_ref[i * n_exp + e] + (ROW_GROUP - 1)) // ROW_GROUP
            loc = loc_ref[i]
            gsrc_ref[pl.ds(p, seg_groups_max), :] = i * tile_rows + (loc + step) * ROW_GROUP
            lsrc_ref[pl.ds(i * local_stride + loc, seg_groups_max), :] = p + step
            loc_ref[i] = loc + g
            return p + g

        end = lax.fori_loop(0, n_ttiles, per_tile, pos)
        end_pad = ((end + (gpt - 1)) // gpt) * gpt
        gsrc_ref[pl.ds(end, seg_groups_max), :] = jnp.full((seg_groups_max, LANES), zero_group, I32)

        def mark(t, c):
            te_ref[t] = e
            tw_ref[t] = e
            return c

        lax.fori_loop(pos // gpt, end_pad // gpt, mark, 0)
        return end_pad

    lax.fori_loop(0, n_exp, per_expert, 0)

    def clear_tail(i, c):
        lsrc_ref[pl.ds(i * local_stride + loc_ref[i], seg_groups_max), :] = jnp.zeros((seg_groups_max, LANES), I32)
        return c

    lax.fori_loop(0, n_ttiles, clear_tail, 0)


def _moe_plan(cnt, *, n_ttiles, n_exp, tile_rows, gpt, n_mm_tiles, seg_groups_max):
    smem = pl.BlockSpec(memory_space=pltpu.SMEM)
    zero_group = n_ttiles * tile_rows - ROW_GROUP
    local_groups = tile_rows // ROW_GROUP
    local_stride = local_groups + seg_groups_max
    n_groups = n_mm_tiles * gpt
    gsrc, lsrc, te, tw = pl.pallas_call(
        functools.partial(_moe_plan_body, n_ttiles=n_ttiles, n_exp=n_exp, tile_rows=tile_rows, gpt=gpt,
                          seg_groups_max=seg_groups_max, local_stride=local_stride, zero_group=zero_group),
        in_specs=[smem], out_specs=[pl.BlockSpec(memory_space=pltpu.VMEM), pl.BlockSpec(memory_space=pltpu.VMEM),
                                    smem, smem],
        out_shape=[jax.ShapeDtypeStruct((n_groups + 2 * seg_groups_max, LANES), I32),
                   jax.ShapeDtypeStruct((n_ttiles * local_stride, LANES), I32),
                   jax.ShapeDtypeStruct((n_mm_tiles,), I32), jax.ShapeDtypeStruct((n_mm_tiles,), I32)],
        scratch_shapes=[pltpu.SMEM((n_ttiles,), I32)],
        name="moe_plan",
    )(cnt)
    return gsrc[:n_groups, 0], lsrc[:, 0], te, tw, local_stride


def _moe_mm_body(te_ref, tw_ref, src_ref, xs_hbm, wg_ref, wu_ref, wd_ref, ys_ref, xbuf, wgb_ref, wub_ref, wdb_ref,
                 sems):
    i = pl.program_id(0)
    tm = ys_ref.shape[0]
    groups = tm // ROW_GROUP
    expert = te_ref[i]
    prev = te_ref[jnp.maximum(i - 1, 0)]

    def fetch(slot, tile):
        for k in range(groups):
            src = pl.multiple_of(src_ref[tile * groups + k], ROW_GROUP)
            pltpu.make_async_copy(xs_hbm.at[pl.ds(src, ROW_GROUP)],
                                  xbuf.at[slot, pl.ds(k * ROW_GROUP, ROW_GROUP)], sems.at[slot]).start()

    slot = i % 2

    last = pl.num_programs(0) - 1

    @pl.when((i == 0) & (expert >= 0))
    def _():
        fetch(0, 0)

    @pl.when((i < last) & (te_ref[jnp.minimum(i + 1, last)] >= 0))
    def _():
        fetch(1 - slot, i + 1)

    @pl.when(expert >= 0)
    def _():
        pltpu.make_async_copy(xs_hbm.at[pl.ds(0, tm)], xbuf.at[slot], sems.at[slot]).wait()

    @pl.when((expert >= 0) & ((i == 0) | (expert != prev)))
    def _():
        wgb_ref[...] = wg_ref[0].astype(BF16)
        wub_ref[...] = wu_ref[0].astype(BF16)
        wdb_ref[...] = wd_ref[0].astype(BF16)

    @pl.when(expert >= 0)
    def _():
        x = xbuf[slot]
        a = jnp.dot(x, wgb_ref[...], preferred_element_type=F32)
        b = jnp.dot(x, wub_ref[...], preferred_element_type=F32)
        hdn = (a * _sigmoid(a)) * b
        ys_ref[...] = jnp.dot(hdn.astype(BF16), wdb_ref[...], preferred_element_type=F32).astype(ys_ref.dtype)

    @pl.when(expert < 0)
    def _():
        ys_ref[...] = jnp.zeros_like(ys_ref)


def _moe_mm(tile_expert, tile_weight, group_src, xs, w_gate, w_up, w_down, *, tm):
    n_tiles = tile_expert.shape[0]
    d = xs.shape[1]
    n_exp, _, de = w_gate.shape
    wmap = lambda i, te, tw, src: (tw[i], 0, 0)
    grid_spec = pltpu.PrefetchScalarGridSpec(
        num_scalar_prefetch=3, grid=(n_tiles,),
        in_specs=[pl.BlockSpec(memory_space=pl.ANY),
                  pl.BlockSpec((1, d, de), wmap), pl.BlockSpec((1, d, de), wmap), pl.BlockSpec((1, de, d), wmap)],
        out_specs=pl.BlockSpec((tm, d), lambda i, te, tw, src: (i, 0)),
        scratch_shapes=[pltpu.VMEM((2, tm, d), BF16), pltpu.VMEM((d, de), BF16), pltpu.VMEM((d, de), BF16),
                        pltpu.VMEM((de, d), BF16), pltpu.SemaphoreType.DMA((2,))])
    return pl.pallas_call(
        _moe_mm_body, grid_spec=grid_spec,
        out_shape=jax.ShapeDtypeStruct((n_tiles * tm, d), BF16),
        compiler_params=_params("arbitrary"),
        name="moe_mm",
    )(tile_expert, tile_weight, group_src, xs, w_gate, w_up, w_down)


def _moe_combine_body(lsrc_ref, x_ref, slab_ref, g_ref, ys_hbm, o_ref, ybuf, sems,
                      *, final_norm, first_tile, local_stride):
    i = pl.program_id(0)
    tm = x_ref.shape[0]
    tile_rows = ybuf.shape[1]
    unroll = 8

    def fetch(slot, tile):
        def body(c, carry):
            for j in range(unroll):
                lg = c * unroll + j
                src = pl.multiple_of(lsrc_ref[tile * local_stride + lg] * ROW_GROUP, ROW_GROUP)
                dst = pl.multiple_of(lg * ROW_GROUP, ROW_GROUP)
                pltpu.make_async_copy(ys_hbm.at[pl.ds(src, ROW_GROUP)], ybuf.at[slot, pl.ds(dst, ROW_GROUP)],
                                      sems.at[slot]).start()
            return carry

        lax.fori_loop(0, tile_rows // ROW_GROUP // unroll, body, 0)

    slot = i % 2

    @pl.when(i == 0)
    def _():
        fetch(0, first_tile)

    @pl.when(i + 1 < pl.num_programs(0))
    def _():
        fetch(1 - slot, first_tile + i + 1)

    pltpu.make_async_copy(ys_hbm.at[pl.ds(0, tile_rows)], ybuf.at[slot], sems.at[slot]).wait()
    y = ybuf[slot]
    slab = slab_ref[...]
    r = lax.broadcasted_iota(I32, (tm, tile_rows), 1).astype(F32)
    ya = jnp.dot((r == slab[:, 6:7]).astype(BF16), y, preferred_element_type=F32)
    yb = jnp.dot((r == slab[:, 7:8]).astype(BF16), y, preferred_element_type=F32)
    out = x_ref[...] + (slab[:, 4:5] * ya + slab[:, 5:6] * yb)
    if final_norm:
        out = _rmsnorm(out, g_ref[...])
    o_ref[...] = out


def _moe_combine(local_src, x, slab, g, ys, *, ts, tile_rows, final_norm, first_tile, local_stride):
    n, d = x.shape
    return pl.pallas_call(
        functools.partial(_moe_combine_body, final_norm=final_norm, first_tile=first_tile,
                          local_stride=local_stride),
        grid=(n // ts,),
        in_specs=[pl.BlockSpec(memory_space=pltpu.SMEM),
                  pl.BlockSpec((ts, d), lambda i: (i, 0)), pl.BlockSpec((ts, LANES), lambda i: (i, 0)),
                  pl.BlockSpec((1, d), lambda i: (0, 0)), pl.BlockSpec(memory_space=pl.ANY)],
        out_specs=pl.BlockSpec((ts, d), lambda i: (i, 0)),
        out_shape=jax.ShapeDtypeStruct((n, d), F32),
        scratch_shapes=[pltpu.VMEM((2, tile_rows, d), BF16), pltpu.SemaphoreType.DMA((2,))],
        compiler_params=_params("arbitrary"),
        name="moe_combine",
    )(local_src, x, slab, g, ys)


def kernel(x_prompt, x_sample, cache_k, cache_v, cache_logf, state_pool, page_table, norm_mix, w_in, b_forget,
           w_pool, pool_scale, w_up_pool, w_up_att, w_out, norm_ffn, w_router_group, b_router_group,
           w_router_expert, b_router_expert, w_gate, w_up, w_down, norm_final):
    depth = norm_mix.shape[0]
    assert depth == 1, "single trunk layer"
    b, t, d = x_prompt.shape
    db, dt, _ = x_sample.shape
    assert dt == 1, "one sample token per sequence"
    _, _, page, n_heads, dh = cache_k.shape
    n_pages = page_table.shape[1]
    past = n_pages * page
    n_state, d_pool = state_pool.shape[2], state_pool.shape[3]
    d_att = n_heads * dh
    n_pool_groups = w_pool.shape[1]
    assert n_pool_groups == len(POOL_WINDOWS) and d_pool // n_pool_groups == LANES
    assert n_state == max(POOL_WINDOWS) - 1 and n_state < POOL_HALO
    n_groups, n_per_group = w_router_expert.shape[1], w_router_expert.shape[3]
    n_exp = n_groups * n_per_group
    assert n_groups + n_exp <= LANES and 2 * dh == LANES and n_heads % 2 == 0
    n = b * t
    q_scale = float(dh) ** -0.5
    tm = min(TOKEN_TILE, t)
    assert t % tm == 0 and t % ATTN_TILE == 0

    o_main = d_pool + 3 * d_att
    wi = w_in[0]
    wm_f, wf_f, wg_f = wi[:, :o_main], wi[:, o_main:o_main + n_heads], wi[:, o_main + n_heads:]
    wf_pad = jnp.pad(wf_f, ((0, 0), (0, LANES - n_heads)))
    wit = jnp.transpose(wi)
    wmt_f, wgt_f = wit[:o_main], wit[o_main + n_heads:]
    wft_pad = jnp.pad(wit[o_main:o_main + n_heads], ((0, LANES - n_heads), (0, 0)))
    bf_pad = jnp.pad(b_forget[0], (0, LANES - n_heads)).reshape(1, LANES)
    g_mix = norm_mix[0].reshape(1, d)
    g_ffn = norm_ffn[0].reshape(1, d)
    g_fin = norm_final.reshape(1, d)
    ps = pool_scale[0].reshape(1, d_pool)
    wr_f = jnp.concatenate([w_router_group[0], jnp.transpose(w_router_expert[0], (1, 0, 2)).reshape(d, n_exp)], axis=1)
    wr_pad = jnp.pad(wr_f, ((0, 0), (0, LANES - n_groups - n_exp)))
    br_pad = jnp.pad(jnp.concatenate([b_router_group[0], b_router_expert[0].reshape(n_exp)]),
                     (0, LANES - n_groups - n_exp)).reshape(1, LANES)
    bf = lambda a: a.astype(BF16)

    xp = x_prompt.reshape(n, d)
    u_p, q_p, kt_p, vt_p, kb_p, vb_p, lft_p, gate_p = _proj_prompt(
        xp, g_mix, bf(wm_f), bf(wf_pad), bf(wg_f), bf_pad, tm=tm, seq_len=t, d_pool=d_pool, d_att=d_att,
        n_heads=n_heads, q_scale=q_scale)
    c = _cumsum_lanes(lft_p.reshape(b * n_heads, t))
    nt = t // ATTN_TILE
    c_blk = jnp.transpose(c.reshape(b, n_heads // 2, 2, nt, ATTN_TILE), (0, 1, 3, 2, 4))
    n_ptiles = n // tm
    n_ttiles = n_ptiles + 1
    tile_rows = -(-(2 * tm + n_exp * (ROW_GROUP - 1)) // MOE_ROW_TILE) * MOE_ROW_TILE
    assert 2 * db + n_exp * (ROW_GROUP - 1) <= tile_rows - ROW_GROUP, "the sample tile must end in an unused row group"

    xs = x_sample.reshape(db, d)
    z_s, lf_s, gate_s = _proj_sample(xs, g_mix, wmt_f, wft_pad, wgt_f, bf_pad, tn=512)
    u_s = z_s[:, :d_pool]
    q_s = z_s[:, d_pool:d_pool + d_att] * q_scale
    k_s = z_s[:, d_pool + d_att:d_pool + 2 * d_att]
    v_s = z_s[:, d_pool + 2 * d_att:]
    att_p, att_s = _attention(q_p.reshape(b, t, d_att), kb_p.reshape(b, t, d_att), vb_p.reshape(b, t, d_att), c_blk,
                              page_table, q_s, k_s, v_s, lf_s[:, :n_heads],
                              jnp.transpose(cache_k[0], (0, 2, 3, 1)), jnp.transpose(cache_v[0], (0, 2, 3, 1)),
                              jnp.transpose(cache_logf[0], (0, 2, 1)), tile=ATTN_TILE, dh=dh)
    state_t = jnp.transpose(state_pool[0], (1, 0, 2))
    x2_s, xs_rows_s, slab_s, counts_s = _merge_sample(
        xs, u_s, state_t, att_s.reshape(db, d_att), gate_s, w_pool[0], ps, w_up_pool[0], w_up_att[0], w_out[0],
        g_ffn, wr_pad, br_pad, start_pos=past, n_groups=n_groups, n_per_group=n_per_group, tile_rows=tile_rows)

    x2_p, xs_rows, slab_p, counts_p = _merge_prompt(
        xp, u_p, att_p.reshape(n, d_att), gate_p, bf(w_pool[0]), ps, bf(w_up_pool[0]), bf(w_up_att[0]),
        bf(w_out[0]), g_ffn, bf(wr_pad), br_pad, xs_rows_s, tm=tm, seq_len=t, n_groups=n_groups,
        n_per_group=n_per_group)

    tmm = MOE_ROW_TILE
    gpt = tmm // ROW_GROUP
    n_groups_max = -(-(2 * (n + db)) // ROW_GROUP) + n_ttiles * n_exp + n_exp * (gpt - 1)
    cnt = jnp.concatenate([counts_p, counts_s], axis=0)[:, 0, :n_exp].astype(I32).reshape(-1)
    group_src, local_src, tile_expert, tile_weight, local_stride = _moe_plan(
        cnt, n_ttiles=n_ttiles, n_exp=n_exp, tile_rows=tile_rows, gpt=gpt, n_mm_tiles=-(-n_groups_max // gpt),
        seg_groups_max=tm // ROW_GROUP)
    ys_rows = _moe_mm(tile_expert, tile_weight, group_src, xs_rows, w_gate[0], w_up[0], w_down[0], tm=tmm)
    y_prompt = _moe_combine(local_src, x2_p, slab_p, g_fin, ys_rows, ts=tm, tile_rows=tile_rows,
                            final_norm=True, first_tile=0, local_stride=local_stride)
    y_sample = _moe_combine(local_src, x2_s, slab_s, g_fin, ys_rows, ts=db, tile_rows=tile_rows,
                            final_norm=True, first_tile=n_ptiles, local_stride=local_stride)

    new_pool_p = u_p.reshape(b, t, d_pool)[:, t - n_state:, :]
    new_pool_s = jnp.concatenate([state_pool[0][:, 1:, :], u_s[:, None, :]], axis=1)
    to_heads = lambda a: jnp.transpose(a.reshape(b, n_heads, dh, t), (0, 3, 1, 2))[None]
    return (y_prompt.reshape(b, t, d), y_sample.reshape(db, 1, d),
            to_heads(kt_p), to_heads(vt_p), jnp.transpose(lft_p, (0, 2, 1))[None],
            new_pool_p[None],
            k_s.reshape(1, db, 1, n_heads, dh), v_s.reshape(1, db, 1, n_heads, dh),
            lf_s[:, :n_heads].reshape(1, db, 1, n_heads), new_pool_s[None])
```
